```python
import math
import jax, jax.numpy as jnp
from jax import lax
import numpy as np

D_MODEL = 1024
BATCH = 8
SEQ = 2048
DEPTH = 2
DEC_BATCH = 128
DEC_SEQ = 8
PAST_LEN = 16384
PAGE_SIZE = 128

A_HEAD_DIM = 64
A_WIDTH = D_MODEL // 2
A_HEADS = A_WIDTH // A_HEAD_DIM
R_DECAY = 64
R_AAA = 64
R_GATE = 128
A_COLS = 3 * A_WIDTH + R_DECAY + R_AAA + R_GATE
B_WIDTH = D_MODEL // 4
B_BLOCKS = 4
B_BLOCK_DIM = B_WIDTH // B_BLOCKS
CONV_WIDTH = 4
LRU_C = 8.0
C_WIDTH = D_MODEL - A_WIDTH - B_WIDTH
S5_GROUP_CH = 16
S5_GROUPS = C_WIDTH // S5_GROUP_CH
S5_STATE = 64
MIX_WIDTH = A_WIDTH + B_WIDTH + C_WIDTH
IN_COLS = A_COLS + 2 * B_WIDTH + C_WIDTH
D_FF = ((8 * D_MODEL + 3 * 256 - 1) // (3 * 256)) * 256
PLE_DIM = 256
RMS_EPS = 1e-6
GN_EPS = 64e-5

kernel_name = 'hybrid_rwkv7_rglru_s5_decode_step'


def rmsnorm(x, g):
    x32 = x.astype(jnp.float32)
    inv = lax.rsqrt(jnp.mean(x32 * x32, axis=-1, keepdims=True) + RMS_EPS)
    return (x32 * inv).astype(x.dtype) * g


def _linear_op(e1, e2):
    a1, b1 = e1
    a2, b2 = e2
    return a1 * a2, a2 * b1 + b2


def _complex_linear_op(e1, e2):
    ar1, ai1, br1, bi1 = e1
    ar2, ai2, br2, bi2 = e2
    return (ar2 * ar1 - ai2 * ai1, ar2 * ai1 + ai2 * ar1,
            ar2 * br1 - ai2 * bi1 + br2, ar2 * bi1 + ai2 * br1 + bi2)


def rwkv7_mix(cols, shift0, wkv0, lp):
    bsz, t = cols.shape[0], cols.shape[1]
    prev = jnp.concatenate([shift0[:, None, :], cols[:, :-1]], axis=1)
    xs = cols + (prev - cols) * lp['mu_a']
    o = 3 * A_WIDTH
    r = xs[..., :A_WIDTH]
    k = xs[..., A_WIDTH:2 * A_WIDTH]
    v = xs[..., 2 * A_WIDTH:o]
    xw = xs[..., o:o + R_DECAY]
    xa = xs[..., o + R_DECAY:o + R_DECAY + R_AAA]
    xg = xs[..., o + R_DECAY + R_AAA:]
    w_log = -jax.nn.softplus(-(lp['w0'] + jnp.tanh(xw) @ lp['w_dec2'])) - 0.5
    decay = jnp.exp(-jnp.exp(w_log))
    a = jax.nn.sigmoid(lp['a0'] + xa @ lp['w_a2'])
    g = jax.nn.sigmoid(xg) @ lp['w_g2']

    def heads(z):
        return z.reshape(bsz, t, A_HEADS, A_HEAD_DIM)

    kk32 = heads(k * lp['k_k']).astype(jnp.float32)
    kk = (kk32 * lax.rsqrt(jnp.maximum(jnp.sum(kk32 * kk32, axis=-1, keepdims=True), 1e-24))).astype(k.dtype)
    k_mod = k * (1 + (a - 1) * lp['k_a'])
    r_h, k_h, v_h = heads(r), heads(k_mod), heads(v)
    seq = tuple(jnp.moveaxis(z, 1, 0) for z in (r_h, heads(decay), k_h, v_h, kk, heads(a)))

    def step(s, inp):
        r_t, w_t, k_t, v_t, kk_t, a_t = inp
        s_kk = jnp.einsum('bhij,bhj->bhi', s, kk_t)
        s = (s * w_t[:, :, None, :]
             - s_kk[..., None] * (kk_t * a_t)[:, :, None, :]
             + v_t[..., None] * k_t[:, :, None, :])
        return s, jnp.einsum('bhij,bhj->bhi', s, r_t)

    wkv1, y = lax.scan(step, wkv0, seq)
    y = jnp.moveaxis(y, 0, 1)
    y32 = y.astype(jnp.float32)
    mu = jnp.mean(y32, axis=-1, keepdims=True)
    var = jnp.mean((y32 - mu) ** 2, axis=-1, keepdims=True)
    yn = ((y32 - mu) * lax.rsqrt(var + GN_EPS)).astype(y.dtype)
    yn = yn * lp['lnx_w'].reshape(A_HEADS, A_HEAD_DIM) + lp['lnx_b'].reshape(A_HEADS, A_HEAD_DIM)
    bonus = jnp.sum(r_h * k_h * lp['r_k'], axis=-1, keepdims=True) * v_h
    out = (yn + bonus).reshape(bsz, t, A_WIDTH) * g
    return out, cols[:, -1], wkv1


def rglru_mix(cols, conv0, lru0, lp):
    bsz, t = cols.shape[0], cols.shape[1]
    gate_br = cols[..., :B_WIDTH]
    xb = cols[..., B_WIDTH:]
    xpad = jnp.concatenate([conv0, xb], axis=1)
    xc = lp['conv_b'] + sum(xpad[:, j:j + t] * lp['conv_w'][j] for j in range(CONV_WIDTH))
    conv1 = xpad[:, -(CONV_WIDTH - 1):]
    xblk = xc.reshape(bsz, t, B_BLOCKS, B_BLOCK_DIM)
    gate_r = jax.nn.sigmoid(jnp.einsum('btnd,nde->btne', xblk, lp['w_rg']).reshape(bsz, t, B_WIDTH) + lp['b_rg'])
    gate_i = jax.nn.sigmoid(jnp.einsum('btnd,nde->btne', xblk, lp['w_ig']).reshape(bsz, t, B_WIDTH) + lp['b_ig'])
    log_a = -LRU_C * gate_r * jax.nn.softplus(-lp['lru_lambda'])
    a = jnp.exp(log_a)
    mult = jnp.sqrt(-jnp.expm1(2 * log_a))
    b = mult * gate_i * xc
    b = b.at[:, 0].add(a[:, 0] * lru0)
    _, hs = lax.associative_scan(_linear_op, (a, b), axis=1)
    y = hs * jax.nn.gelu(gate_br)
    return y, conv1, hs[:, -1]


def s5_mix(u, h0r, h0i, lp):
    bsz, t = u.shape[0], u.shape[1]
    dt = jnp.exp(lp['s5_log_dt'])[:, None]
    lr, li = lp['s5_lam_re'], lp['s5_lam_im']
    mag = jnp.exp(lr * dt)
    ar = mag * jnp.cos(li * dt)
    ai = mag * jnp.sin(li * dt)
    den = lr * lr + li * li
    cr = ((ar - 1) * lr + ai * li) / den
    ci = (ai * lr - (ar - 1) * li) / den
    bb_re = cr[..., None] * lp['s5_b_re'] - ci[..., None] * lp['s5_b_im']
    bb_im = cr[..., None] * lp['s5_b_im'] + ci[..., None] * lp['s5_b_re']
    ug = u.reshape(bsz, t, S5_GROUPS, S5_GROUP_CH)
    bu_re = jnp.einsum('btgc,gpc->btgp', ug, bb_re)
    bu_im = jnp.einsum('btgc,gpc->btgp', ug, bb_im)
    bu_re = bu_re.at[:, 0].add(ar * h0r - ai * h0i)
    bu_im = bu_im.at[:, 0].add(ar * h0i + ai * h0r)
    a_re = jnp.broadcast_to(ar, bu_re.shape)
    a_im = jnp.broadcast_to(ai, bu_im.shape)
    _, _, h_re, h_im = lax.associative_scan(_complex_linear_op, (a_re, a_im, bu_re, bu_im), axis=1)
    y = (jnp.einsum('btgp,gcp->btgc', h_re, lp['s5_c_re'])
         - jnp.einsum('btgp,gcp->btgc', h_im, lp['s5_c_im']))
    y = y.reshape(bsz, t, C_WIDTH) + lp['s5_d'] * u
    z = jax.nn.gelu(y)
    out = z * jax.nn.sigmoid(z @ lp['w_glu'] + lp['b_glu'])
    return out, h_re[:, -1], h_im[:, -1]


def hybrid_layer(h, p_l, shift0, wkv0, conv0, lru0, s5r0, s5i0, lp):
    xn = rmsnorm(h, lp['g_mix'])
    cols = jnp.einsum('btd,dc->btc', xn, lp['w_in'])
    cols_a = cols[..., :A_COLS]
    cols_b = cols[..., A_COLS:A_COLS + 2 * B_WIDTH]
    cols_c = cols[..., A_COLS + 2 * B_WIDTH:]
    y_a, shift1, wkv1 = rwkv7_mix(cols_a, shift0, wkv0, lp)
    y_b, conv1, lru1 = rglru_mix(cols_b, conv0, lru0, lp)
    y_c, s5r1, s5i1 = s5_mix(cols_c, s5r0, s5i0, lp)
    mixed = jnp.concatenate([y_a, rmsnorm(y_b, lp['g_out_b']), rmsnorm(y_c, lp['g_out_c'])], axis=-1)
    h = h + jnp.einsum('btm,md->btd', mixed, lp['w_out'])
    xf = rmsnorm(h, lp['g_ffn'])
    gu = jnp.einsum('btd,df->btf', xf, lp['w_ffn_up'])
    h = h + jnp.einsum('btf,fd->btd', jax.nn.silu(gu[..., :D_FF]) * gu[..., D_FF:], lp['w_ffn_down'])
    ple = jnp.einsum('btq,qd->btd', p_l, lp['w_ple'])
    h = h + ple * jax.nn.sigmoid(jnp.einsum('btd,de->bte', rmsnorm(h, lp['g_ple']), lp['w_ple_gate']))
    return h, (shift1, wkv1, conv1, lru1, s5r1, s5i1)


def run_trunk(x, p, states, layer_params, g_final):
    h = x
    news = []
    for l in range(DEPTH):
        h, st = hybrid_layer(h, p[l], states[0][l], states[1][l], states[2][l], states[3][l],
                             states[4][l], states[5][l], layer_params[l])
        news.append(st)
    stacked = tuple(jnp.stack([n[j] for n in news], axis=0) for j in range(6))
    return rmsnorm(h, g_final), stacked


def setup_inputs(seed: int = 0) -> dict:
    key = jax.random.key(seed)
    ks = iter(jax.random.split(key, 64))
    f32 = jnp.float32
    L = DEPTH

    def nrm(shape, scale=1.0):
        return scale * jax.random.normal(next(ks), shape, f32)

    def unif(shape, lo, hi):
        return jax.random.uniform(next(ks), shape, f32, lo, hi)

    a_init = unif((L, B_WIDTH), 0.9, 0.999)
    s_init = a_init ** (1.0 / LRU_C)
    inp = {}
    inp['x_prompt'] = nrm((BATCH, SEQ, D_MODEL))
    inp['x_sample'] = nrm((DEC_BATCH, DEC_SEQ, D_MODEL))
    inp['p_prompt'] = nrm((DEPTH, BATCH, SEQ, PLE_DIM))
    inp['p_sample'] = nrm((DEPTH, DEC_BATCH, DEC_SEQ, PLE_DIM))
    inp['state_shift'] = nrm((L, DEC_BATCH, A_COLS))
    inp['state_wkv'] = nrm((L, DEC_BATCH, A_HEADS, A_HEAD_DIM, A_HEAD_DIM), 0.5)
    inp['state_conv'] = nrm((L, DEC_BATCH, CONV_WIDTH - 1, B_WIDTH))
    inp['state_lru'] = nrm((L, DEC_BATCH, B_WIDTH), 0.5)
    inp['state_s5_re'] = nrm((L, DEC_BATCH, S5_GROUPS, S5_STATE), 0.5)
    inp['state_s5_im'] = nrm((L, DEC_BATCH, S5_GROUPS, S5_STATE), 0.5)
    inp['g_mix'] = 1.0 + nrm((L, D_MODEL), 0.05)
    inp['w_in'] = nrm((L, D_MODEL, IN_COLS), D_MODEL ** -0.5)
    inp['mu_a'] = unif((L, A_COLS), 0.0, 1.0)
    inp['w0'] = jnp.linspace(-6.0, -1.0, A_WIDTH, dtype=f32)[None, :] + nrm((L, A_WIDTH), 0.1)
    inp['w_dec2'] = nrm((L, R_DECAY, A_WIDTH), 0.1)
    inp['a0'] = nrm((L, A_WIDTH), 0.1)
    inp['w_a2'] = nrm((L, R_AAA, A_WIDTH), R_AAA ** -0.5)
    inp['w_g2'] = nrm((L, R_GATE, A_WIDTH), R_GATE ** -0.5)
    inp['k_k'] = 0.85 + nrm((L, A_WIDTH), 0.05)
    inp['k_a'] = 1.0 + nrm((L, A_WIDTH), 0.05)
    inp['r_k'] = nrm((L, A_HEADS, A_HEAD_DIM), 0.1)
    inp['lnx_w'] = 1.0 + nrm((L, A_WIDTH), 0.05)
    inp['lnx_b'] = nrm((L, A_WIDTH), 0.01)
    inp['conv_w'] = nrm((L, CONV_WIDTH, B_WIDTH), CONV_WIDTH ** -0.5)
    inp['conv_b'] = nrm((L, B_WIDTH), 0.01)
    inp['w_rg'] = nrm((L, B_BLOCKS, B_BLOCK_DIM, B_BLOCK_DIM), B_BLOCK_DIM ** -0.5)
    inp['b_rg'] = nrm((L, B_WIDTH), 0.01)
    inp['w_ig'] = nrm((L, B_BLOCKS, B_BLOCK_DIM, B_BLOCK_DIM), B_BLOCK_DIM ** -0.5)
    inp['b_ig'] = nrm((L, B_WIDTH), 0.01)
    inp['lru_lambda'] = jnp.log(s_init) - jnp.log1p(-s_init)
    inp['g_out_b'] = 1.0 + nrm((L, B_WIDTH), 0.05)
    inp['s5_lam_re'] = -0.5 + nrm((L, S5_GROUPS, S5_STATE), 0.01)
    inp['s5_lam_im'] = math.pi * jnp.arange(S5_STATE, dtype=f32)[None, None, :] + nrm((L, S5_GROUPS, S5_STATE), 0.01)
    inp['s5_log_dt'] = unif((L, S5_GROUPS), math.log(1e-3), math.log(1e-1))
    inp['s5_b_re'] = nrm((L, S5_GROUPS, S5_STATE, S5_GROUP_CH), (2 * S5_GROUP_CH) ** -0.5)
    inp['s5_b_im'] = nrm((L, S5_GROUPS, S5_STATE, S5_GROUP_CH), (2 * S5_GROUP_CH) ** -0.5)
    inp['s5_c_re'] = nrm((L, S5_GROUPS, S5_GROUP_CH, S5_STATE), (2 * S5_STATE) ** -0.5)
    inp['s5_c_im'] = nrm((L, S5_GROUPS, S5_GROUP_CH, S5_STATE), (2 * S5_STATE) ** -0.5)
    inp['s5_d'] = nrm((L, C_WIDTH), 0.5)
    inp['w_glu'] = nrm((L, C_WIDTH, C_WIDTH), C_WIDTH ** -0.5)
    inp['b_glu'] = nrm((L, C_WIDTH), 0.01)
    inp['g_out_c'] = 1.0 + nrm((L, C_WIDTH), 0.05)
    inp['w_out'] = nrm((L, MIX_WIDTH, D_MODEL), MIX_WIDTH ** -0.5)
    inp['g_ffn'] = 1.0 + nrm((L, D_MODEL), 0.05)
    inp['w_ffn_up'] = nrm((L, D_MODEL, 2 * D_FF), D_MODEL ** -0.5)
    inp['w_ffn_down'] = nrm((L, D_FF, D_MODEL), D_FF ** -0.5)
    inp['g_ple'] = 1.0 + nrm((L, D_MODEL), 0.05)
    inp['w_ple'] = nrm((L, PLE_DIM, D_MODEL), PLE_DIM ** -0.5)
    inp['w_ple_gate'] = nrm((L, D_MODEL, D_MODEL), D_MODEL ** -0.5)
    inp['g_final'] = 1.0 + nrm((D_MODEL,), 0.05)
    return inp


def reference(x_prompt, x_sample, p_prompt, p_sample, state_shift, state_wkv, state_conv, state_lru,
              state_s5_re, state_s5_im, g_mix, w_in, mu_a, w0, w_dec2, a0, w_a2, w_g2, k_k, k_a, r_k,
              lnx_w, lnx_b, conv_w, conv_b, w_rg, b_rg, w_ig, b_ig, lru_lambda, g_out_b, s5_lam_re,
              s5_lam_im, s5_log_dt, s5_b_re, s5_b_im, s5_c_re, s5_c_im, s5_d, w_glu, b_glu, g_out_c,
              w_out, g_ffn, w_ffn_up, w_ffn_down, g_ple, w_ple, w_ple_gate, g_final):
    layer_params = [dict(g_mix=g_mix[l], w_in=w_in[l], mu_a=mu_a[l], w0=w0[l], w_dec2=w_dec2[l],
                         a0=a0[l], w_a2=w_a2[l], w_g2=w_g2[l], k_k=k_k[l], k_a=k_a[l], r_k=r_k[l],
                         lnx_w=lnx_w[l], lnx_b=lnx_b[l], conv_w=conv_w[l], conv_b=conv_b[l],
                         w_rg=w_rg[l], b_rg=b_rg[l], w_ig=w_ig[l], b_ig=b_ig[l],
                         lru_lambda=lru_lambda[l], g_out_b=g_out_b[l], s5_lam_re=s5_lam_re[l],
                         s5_lam_im=s5_lam_im[l], s5_log_dt=s5_log_dt[l], s5_b_re=s5_b_re[l],
                         s5_b_im=s5_b_im[l], s5_c_re=s5_c_re[l], s5_c_im=s5_c_im[l], s5_d=s5_d[l],
                         w_glu=w_glu[l], b_glu=b_glu[l], g_out_c=g_out_c[l], w_out=w_out[l],
                         g_ffn=g_ffn[l], w_ffn_up=w_ffn_up[l], w_ffn_down=w_ffn_down[l],
                         g_ple=g_ple[l], w_ple=w_ple[l], w_ple_gate=w_ple_gate[l])
                    for l in range(DEPTH)]
    bp = x_prompt.shape[0]
    dt_ = x_prompt.dtype
    zero_states = (jnp.zeros((DEPTH, bp, A_COLS), dt_),
                   jnp.zeros((DEPTH, bp, A_HEADS, A_HEAD_DIM, A_HEAD_DIM), dt_),
                   jnp.zeros((DEPTH, bp, CONV_WIDTH - 1, B_WIDTH), dt_),
                   jnp.zeros((DEPTH, bp, B_WIDTH), dt_),
                   jnp.zeros((DEPTH, bp, S5_GROUPS, S5_STATE), dt_),
                   jnp.zeros((DEPTH, bp, S5_GROUPS, S5_STATE), dt_))
    y_prompt, new_p = run_trunk(x_prompt, p_prompt, zero_states, layer_params, g_final)
    sample_states = (state_shift, state_wkv, state_conv, state_lru, state_s5_re, state_s5_im)
    y_sample, new_s = run_trunk(x_sample, p_sample, sample_states, layer_params, g_final)
    shift_p, wkv_p, conv_p, lru_p, s5re_p, s5im_p = new_p
    shift_s, wkv_s, conv_s, lru_s, s5re_s, s5im_s = new_s
    return (y_prompt, y_sample, shift_p, wkv_p, conv_p, lru_p, s5re_p, s5im_p,
            shift_s, wkv_s, conv_s, lru_s, s5re_s, s5im_s)
```

```python
import functools
import math

import jax
import jax.numpy as jnp
from jax import lax
from jax.experimental import pallas as pl
from jax.experimental.pallas import tpu as pltpu

F32 = jnp.float32
BF16 = jnp.bfloat16

D_MODEL = 1024
A_WIDTH = 512
A_HEADS = 8
A_HEAD_DIM = 64
A_COLS = 1792
B_WIDTH = 256
B_BLOCKS = 4
C_WIDTH = 256
S5_GROUPS = 16
S5_GROUP_CH = 16
S5_STATE = 64
S5_W = S5_GROUPS * S5_STATE
IN_COLS = A_COLS + 2 * B_WIDTH + C_WIDTH
D_FF = 2816
PLE_DIM = 256
LRU_C = 8.0
RMS_EPS = 1e-6
GN_EPS = 64e-5

LANES = 128
SUBLANES = 8
VMEM_LIMIT_BYTES = 56 * 1024 * 1024

TOKEN_TILE = 512
FFN_CHUNK = 256
RWKV_CHUNK = 64
SCAN_TILE = 64
SAMPLE_SEQS = 8


def _const_spec(shape):
    nd = len(shape)
    return pl.BlockSpec(shape, lambda *_: (0,) * nd, pipeline_mode=pl.Buffered(1))


def _params(n_axes):
    return pltpu.CompilerParams(dimension_semantics=("arbitrary",) * n_axes,
                                vmem_limit_bytes=VMEM_LIMIT_BYTES)


def _dot(a, b):
    return jnp.dot(a, b, preferred_element_type=F32)


def _dot_nt(a, b):
    return lax.dot_general(a, b, (((1,), (1,)), ((), ())), preferred_element_type=F32)


def _dot_tn(a, b):
    return lax.dot_general(a, b, (((0,), (0,)), ((), ())), preferred_element_type=F32)


def _bf(x):
    return x.astype(BF16)


def _rms(x, g):
    inv = lax.rsqrt(jnp.mean(x * x, axis=-1, keepdims=True) + RMS_EPS)
    return x * inv * g


def _sigmoid(x):
    return 1.0 / (1.0 + jnp.exp(-x))


def _softplus(x):
    return jnp.maximum(x, 0.0) + jnp.log(1.0 + jnp.exp(-jnp.abs(x)))


def _gelu_tanh(x):
    c = math.sqrt(2.0 / math.pi)
    return 0.5 * x * (1.0 + jnp.tanh(c * (x + 0.044715 * (x * x * x))))


def _row_iota(shape):
    return lax.broadcasted_iota(jnp.int32, shape, 0)


def _shift_rows(x, d):
    return pltpu.roll(x, d, axis=0)


def _proj_in_kernel(h_ref, g_ref, w_ref, a_ref, b_ref, c_ref):
    xn = _bf(_rms(h_ref[...], g_ref[...]))
    a_ref[...] = _dot(xn, w_ref[:, 0:A_COLS])
    b_ref[...] = _dot(xn, w_ref[:, A_COLS:A_COLS + 2 * B_WIDTH])
    c_ref[...] = _dot(xn, w_ref[:, A_COLS + 2 * B_WIDTH:IN_COLS])


def _proj_in(h, g, w_bf):
    n = h.shape[0]
    tm = TOKEN_TILE
    row = lambda i: (i, 0)
    return pl.pallas_call(
        _proj_in_kernel,
        grid=(n // tm,),
        in_specs=[pl.BlockSpec((tm, D_MODEL), row), _const_spec((1, D_MODEL)),
                  _const_spec((D_MODEL, IN_COLS))],
        out_specs=[pl.BlockSpec((tm, A_COLS), row), pl.BlockSpec((tm, 2 * B_WIDTH), row),
                   pl.BlockSpec((tm, C_WIDTH), row)],
        out_shape=[jax.ShapeDtypeStruct((n, A_COLS), F32), jax.ShapeDtypeStruct((n, 2 * B_WIDTH), F32),
                   jax.ShapeDtypeStruct((n, C_WIDTH), F32)],
        compiler_params=_params(1),
        name="proj_in",
    )(h, g, w_bf)


def _post_kernel(h_ref, ya_ref, yb_ref, yc_ref, p_ref, wo_ref, vec_ref, wup_ref, wdn_ref, wple_ref,
                 wgate_ref, o_ref, act_scr, *, final):
    g_ffn = vec_ref[0:1, :]
    g_ple = vec_ref[1:2, :]
    g_final = vec_ref[2:3, :]
    h1 = (h_ref[...] + _dot(_bf(ya_ref[...]), wo_ref[0:A_WIDTH, :])
          + _dot(_bf(yb_ref[...]), wo_ref[A_WIDTH:A_WIDTH + B_WIDTH, :])
          + _dot(_bf(yc_ref[...]), wo_ref[A_WIDTH + B_WIDTH:D_MODEL, :]))
    xf = _bf(_rms(h1, g_ffn))
    for c0 in range(0, D_FF, FFN_CHUNK):
        gate = _dot(xf, wup_ref[:, c0:c0 + FFN_CHUNK])
        up = _dot(xf, wup_ref[:, D_FF + c0:D_FF + c0 + FFN_CHUNK])
        act_scr[:, c0:c0 + FFN_CHUNK] = _bf(gate * _sigmoid(gate) * up)
    h2 = h1 + _dot(act_scr[...], wdn_ref[...])
    ple = _dot(_bf(p_ref[...]), wple_ref[...])
    gate = _sigmoid(_dot(_bf(_rms(h2, g_ple)), wgate_ref[...]))
    h3 = h2 + ple * gate
    if final:
        h3 = _rms(h3, g_final)
    o_ref[...] = h3


def _post(h, ya, yb, yc, p, wo, vec, wup, wdn, wple, wgate, final):
    n = h.shape[0]
    tm = TOKEN_TILE
    row = lambda i: (i, 0)
    return pl.pallas_call(
        functools.partial(_post_kernel, final=final),
        grid=(n // tm,),
        in_specs=[pl.BlockSpec((tm, D_MODEL), row), pl.BlockSpec((tm, A_WIDTH), row),
                  pl.BlockSpec((tm, B_WIDTH), row), pl.BlockSpec((tm, C_WIDTH), row),
                  pl.BlockSpec((tm, PLE_DIM), row),
                  _const_spec((D_MODEL, D_MODEL)), _const_spec((SUBLANES, D_MODEL)),
                  _const_spec((D_MODEL, 2 * D_FF)), _const_spec((D_FF, D_MODEL)),
                  _const_spec((PLE_DIM, D_MODEL)), _const_spec((D_MODEL, D_MODEL))],
        out_specs=pl.BlockSpec((tm, D_MODEL), row),
        out_shape=jax.ShapeDtypeStruct((n, D_MODEL), F32),
        scratch_shapes=[pltpu.VMEM((tm, D_FF), BF16)],
        compiler_params=_params(1),
        name="post",
    )(h, ya, yb, yc, p, wo, vec, wup, wdn, wple, wgate)


def _seg_sum(x, ones_bd):
    hi = _bf(x)
    lo = _bf(x - hi.astype(F32))
    return _dot(hi, ones_bd) + _dot(lo, ones_bd)


def _rwkv_prep(cols, prev, mu, vec, wdec, wa, wg, ones_bd, chunk):
    rows = cols.shape[0]
    xs = cols + (prev - cols) * mu
    r = xs[:, 0:A_WIDTH]
    k = xs[:, A_WIDTH:2 * A_WIDTH]
    v = xs[:, 2 * A_WIDTH:3 * A_WIDTH]
    xwa = xs[:, 3 * A_WIDTH:3 * A_WIDTH + LANES]
    xg = xs[:, 3 * A_WIDTH + LANES:A_COLS]
    w0, a0, k_k, k_a, r_k = (vec[i:i + 1, :] for i in range(5))
    z = w0 + _dot(_bf(jnp.tanh(xwa)), wdec)
    log_decay = -jnp.exp(-_softplus(-z) - 0.5)
    a = _sigmoid(a0 + _dot(_bf(xwa), wa))
    g = _dot(_bf(_sigmoid(xg)), wg)
    kk_raw = k * k_k
    kk = kk_raw * lax.rsqrt(jnp.maximum(_seg_sum(kk_raw * kk_raw, ones_bd), 1e-24))
    k_mod = k * (1.0 + (a - 1.0) * k_a)
    bonus = _seg_sum(r * k_mod * r_k, ones_bd) * v
    ri = _row_iota((rows, rows))
    ci = lax.broadcasted_iota(jnp.int32, (rows, rows), 1)
    same_chunk = (ci & (-chunk)) == (ri & (-chunk))
    tri = _bf(jnp.where((ci <= ri) & same_chunk, 1.0, 0.0))
    h1 = _bf(log_decay)
    r1 = log_decay - h1.astype(F32)
    h2 = _bf(r1)
    h3 = _bf(r1 - h2.astype(F32))
    cum = _dot(tri, h1) + _dot(tri, h2) + _dot(tri, h3)
    gam = jnp.exp(cum)
    ginv = jnp.exp(-cum)
    gprev = jnp.exp(cum - log_decay)
    return dict(rt=r * gam, kap=kk * gprev, bet=kk * a * ginv, kt=k_mod * ginv, v=v, gam=gam,
                bonus=bonus, g=g)


def _rwkv_chunk(kap, rt, bet, kt, v, s_pair, g_end):
    c = kap.shape[0]
    lane = lax.broadcasted_iota(jnp.int32, (c, LANES), 1)
    first = lane < A_HEAD_DIM

    def stack(x):
        return _bf(jnp.concatenate([jnp.where(first, x, 0.0), jnp.where(first, 0.0, x)], axis=0))

    kap2, r2, bet2, kt2, v2 = stack(kap), stack(rt), stack(bet), stack(kt), stack(v)
    ri = _row_iota((2 * c, 2 * c))
    ci = lax.broadcasted_iota(jnp.int32, (2 * c, 2 * c), 1)
    strict = ci < ri
    incl = ci <= ri
    a_b = jnp.where(strict, _dot_nt(kap2, bet2), 0.0)
    a_k = jnp.where(strict, _dot_nt(kap2, kt2), 0.0)
    a_rb = jnp.where(incl, _dot_nt(r2, bet2), 0.0)
    a_rk = jnp.where(incl, _dot_nt(r2, kt2), 0.0)
    t = jnp.where(ri == ci, 1.0, 0.0) - a_b
    lp = a_b
    n = 2
    while n < c:
        lpb = _bf(lp)
        lp = _dot(lpb, lpb)
        t = t + _dot(_bf(t), _bf(lp))
        n *= 2
    tb = _bf(t)
    akv = _dot(_bf(a_k), v2)
    w2 = -_dot(tb, kap2)
    u02 = -_dot(tb, _bf(akv))
    sb = _bf(s_pair)
    u2 = _bf(_dot_nt(_bf(w2), sb) + u02)
    y2 = _dot_nt(r2, sb) + _dot(_bf(a_rb), u2) + _dot(_bf(a_rk), v2)
    y = y2[0:c] + y2[c:2 * c]
    ds = _dot_tn(u2, bet2) + _dot_tn(v2, kt2)
    return y, (s_pair + ds) * g_end


def _rwkv_finish(y, bonus, g, vec, ones_bd):
    lnx_w = vec[5:6, :]
    lnx_b = vec[6:7, :]
    inv_n = 1.0 / A_HEAD_DIM
    mu = _seg_sum(y, ones_bd) * inv_n
    d = y - mu
    var = _seg_sum(d * d, ones_bd) * inv_n
    yn = d * lax.rsqrt(var + GN_EPS) * lnx_w + lnx_b
    return (yn + bonus) * g


def _pair_state(s_ref, idx, p):
    z = jnp.zeros((A_HEAD_DIM, A_HEAD_DIM), F32)
    top = jnp.concatenate([s_ref[idx, 2 * p], z], axis=1)
    bot = jnp.concatenate([z, s_ref[idx, 2 * p + 1]], axis=1)
    return jnp.concatenate([top, bot], axis=0)


def _rwkv_prompt_kernel(cols_ref, mu_ref, vec_ref, wdec_ref, wa_ref, wg_ref, ones_ref,
                        y_ref, wkv_ref, last_ref, s_scr, prev_scr):
    t = pl.program_id(1)
    rows = cols_ref.shape[0]
    chunk = RWKV_CHUNK

    @pl.when(t == 0)
    def _():
        s_scr[...] = jnp.zeros_like(s_scr)
        prev_scr[...] = jnp.zeros_like(prev_scr)

    cols = cols_ref[...]
    first_row = _row_iota(cols.shape) == 0
    prev = jnp.where(first_row, prev_scr[SUBLANES - 1:SUBLANES, :], _shift_rows(cols, 1))
    ones_bd = ones_ref[...]
    q = _rwkv_prep(cols, prev, mu_ref[...], vec_ref[...], wdec_ref[...], wa_ref[...], wg_ref[...],
                   ones_bd, chunk)
    ys = []
    for p in range(A_HEADS // 2):
        ln = slice(p * LANES, (p + 1) * LANES)
        s_pair = s_scr[p]
        yp = []
        for c0 in range(0, rows, chunk):
            rw = slice(c0, c0 + chunk)
            y, s_pair = _rwkv_chunk(q["kap"][rw, ln], q["rt"][rw, ln], q["bet"][rw, ln], q["kt"][rw, ln],
                                    q["v"][rw, ln], s_pair, q["gam"][c0 + chunk - 1:c0 + chunk, ln])
            yp.append(y)
        s_scr[p] = s_pair
        ys.append(jnp.concatenate(yp, axis=0) if len(yp) > 1 else yp[0])
        wkv_ref[0, 2 * p] = s_pair[0:A_HEAD_DIM, 0:A_HEAD_DIM]
        wkv_ref[0, 2 * p + 1] = s_pair[A_HEAD_DIM:LANES, A_HEAD_DIM:LANES]
    y = jnp.concatenate(ys, axis=1)
    y_ref[...] = _rwkv_finish(y, q["bonus"], q["g"], vec_ref[...], ones_bd)
    tail = cols[rows - SUBLANES:rows, :]
    prev_scr[...] = tail
    last_ref[0] = tail


def _rwkv_prompt(cols_a, nb, seq, mu, vec, wdec, wa, wg, ones_bd):
    tb = RWKV_CHUNK
    nt = seq // tb
    blk = lambda b, t: (b * nt + t, 0)
    per_b3 = lambda b, t: (b, 0, 0)
    return pl.pallas_call(
        _rwkv_prompt_kernel,
        grid=(nb, nt),
        in_specs=[pl.BlockSpec((tb, A_COLS), blk), _const_spec((1, A_COLS)), _const_spec((SUBLANES, A_WIDTH)),
                  _const_spec((LANES, A_WIDTH)), _const_spec((LANES, A_WIDTH)), _const_spec((LANES, A_WIDTH)),
                  _const_spec((A_WIDTH, A_WIDTH))],
        out_specs=[pl.BlockSpec((tb, A_WIDTH), blk),
                   pl.BlockSpec((1, A_HEADS, A_HEAD_DIM, A_HEAD_DIM), lambda b, t: (b, 0, 0, 0)),
                   pl.BlockSpec((1, SUBLANES, A_COLS), per_b3)],
        out_shape=[jax.ShapeDtypeStruct((nb * seq, A_WIDTH), F32),
                   jax.ShapeDtypeStruct((nb, A_HEADS, A_HEAD_DIM, A_HEAD_DIM), F32),
                   jax.ShapeDtypeStruct((nb, SUBLANES, A_COLS), F32)],
        scratch_shapes=[pltpu.VMEM((A_HEADS // 2, LANES, LANES), F32), pltpu.VMEM((SUBLANES, A_COLS), F32)],
        compiler_params=_params(2),
        name="rwkv_prompt",
    )(cols_a, mu, vec, wdec, wa, wg, ones_bd)


def _rwkv_sample_kernel(cols_ref, shift_ref, wkv0_ref, mu_ref, vec_ref, wdec_ref, wa_ref, wg_ref, ones_ref,
                        y_ref, wkv_ref, kap_s, rt_s, bet_s, kt_s, v_s, gam_s, y_s, *, steps):
    rows = cols_ref.shape[0]
    nseq = rows // steps
    cols = cols_ref[...]
    tloc = _row_iota(cols.shape) & (steps - 1)
    shift0 = jnp.broadcast_to(shift_ref[...], (nseq, steps, A_COLS)).reshape(rows, A_COLS)
    prev = jnp.where(tloc == 0, shift0, _shift_rows(cols, 1))
    ones_bd = ones_ref[...]
    q = _rwkv_prep(cols, prev, mu_ref[...], vec_ref[...], wdec_ref[...], wa_ref[...], wg_ref[...],
                   ones_bd, steps)
    kap_s[...] = q["kap"]
    rt_s[...] = q["rt"]
    bet_s[...] = q["bet"]
    kt_s[...] = q["kt"]
    v_s[...] = q["v"]
    gam_s[...] = q["gam"]

    def seq_body(s, carry):
        r0 = pl.multiple_of(s * steps, steps)
        rw = pl.ds(r0, steps)
        for p in range(A_HEADS // 2):
            ln = slice(p * LANES, (p + 1) * LANES)
            s_pair = _pair_state(wkv0_ref, s, p)
            g_end = gam_s[rw, ln][steps - 1:steps, :]
            y, s_new = _rwkv_chunk(kap_s[rw, ln], rt_s[rw, ln], bet_s[rw, ln], kt_s[rw, ln], v_s[rw, ln],
                                   s_pair, g_end)
            y_s[rw, ln] = y
            wkv_ref[s, 2 * p] = s_new[0:A_HEAD_DIM, 0:A_HEAD_DIM]
            wkv_ref[s, 2 * p + 1] = s_new[A_HEAD_DIM:LANES, A_HEAD_DIM:LANES]
        return carry

    lax.fori_loop(0, nseq, seq_body, 0)
    y_ref[...] = _rwkv_finish(y_s[...], q["bonus"], q["g"], vec_ref[...], ones_bd)


def _rwkv_sample(cols_a, shift0, wkv0, nb, steps, mu, vec, wdec, wa, wg, ones_bd):
    bb = SAMPLE_SEQS
    rows = bb * steps
    blk = lambda i: (i, 0)
    scr = pltpu.VMEM((rows, A_WIDTH), F32)
    return pl.pallas_call(
        functools.partial(_rwkv_sample_kernel, steps=steps),
        grid=(nb // bb,),
        in_specs=[pl.BlockSpec((rows, A_COLS), blk),
                  pl.BlockSpec((bb, 1, A_COLS), lambda i: (i, 0, 0)),
                  pl.BlockSpec((bb, A_HEADS, A_HEAD_DIM, A_HEAD_DIM), lambda i: (i, 0, 0, 0)),
                  _const_spec((1, A_COLS)), _const_spec((SUBLANES, A_WIDTH)),
                  _const_spec((LANES, A_WIDTH)), _const_spec((LANES, A_WIDTH)), _const_spec((LANES, A_WIDTH)),
                  _const_spec((A_WIDTH, A_WIDTH))],
        out_specs=[pl.BlockSpec((rows, A_WIDTH), blk),
                   pl.BlockSpec((bb, A_HEADS, A_HEAD_DIM, A_HEAD_DIM), lambda i: (i, 0, 0, 0))],
        out_shape=[jax.ShapeDtypeStruct((nb * steps, A_WIDTH), F32),
                   jax.ShapeDtypeStruct((nb, A_HEADS, A_HEAD_DIM, A_HEAD_DIM), F32)],
        scratch_shapes=[scr] * 7,
        compiler_params=_params(1),
        name="rwkv_sample",
    )(cols_a, shift0, wkv0, mu, vec, wdec, wa, wg, ones_bd)


def _lru_body(gate_br, xb, conv_prev, h0, tloc, tlen, vec, bias, wgates):
    rows = xb.shape[0]
    conv_b = vec[4:5, :]
    lam = vec[5:6, :]
    g_out = vec[6:7, :]
    xc = conv_b + vec[3:4, :] * xb
    for j in (1, 2, 3):
        tail = conv_prev if j == 3 else _shift_rows(conv_prev, rows - (3 - j))
        xc = xc + vec[3 - j:4 - j, :] * jnp.where(tloc >= j, _shift_rows(xb, j), tail)
    gates = _sigmoid(_dot(_bf(xc), wgates) + bias)
    gate_r = gates[:, 0:B_WIDTH]
    gate_i = gates[:, B_WIDTH:2 * B_WIDTH]
    log_a = (-LRU_C) * gate_r * _softplus(-lam)
    a = jnp.exp(log_a)
    mult = jnp.sqrt(-jnp.tanh(log_a) * (a * a + 1.0))
    b = mult * gate_i * xc
    b = b + jnp.where(tloc == 0, a * h0, 0.0)
    d = 1
    while d < tlen:
        keep = tloc >= d
        a_s = jnp.where(keep, _shift_rows(a, d), 1.0)
        b_s = jnp.where(keep, _shift_rows(b, d), 0.0)
        b = a * b_s + b
        a = a * a_s
        d *= 2
    hs = b
    y = hs * _gelu_tanh(gate_br)
    return _rms(y, g_out), hs


def _lru_prompt_kernel(cols_ref, vec_ref, bias_ref, wg_ref, y_ref, conv_ref, h_ref, x_scr, h_scr):
    t = pl.program_id(1)
    rows = cols_ref.shape[0]

    @pl.when(t == 0)
    def _():
        x_scr[...] = jnp.zeros_like(x_scr)
        h_scr[...] = jnp.zeros_like(h_scr)

    gate_br = cols_ref[:, 0:B_WIDTH]
    xb = cols_ref[:, B_WIDTH:2 * B_WIDTH]
    tloc = _row_iota(xb.shape)
    conv_prev = jnp.concatenate([pltpu.roll(x_scr[...], 3, axis=0), jnp.zeros((rows - SUBLANES, B_WIDTH), F32)],
                                axis=0)
    h0 = jnp.broadcast_to(h_scr[SUBLANES - 1:SUBLANES, :], xb.shape)
    y, hs = _lru_body(gate_br, xb, conv_prev, h0, tloc, rows, vec_ref[...], bias_ref[...], wg_ref[...])
    y_ref[...] = y
    x_tail = xb[rows - SUBLANES:rows, :]
    h_tail = hs[rows - SUBLANES:rows, :]
    x_scr[...] = x_tail
    h_scr[...] = h_tail
    conv_ref[0] = x_tail
    h_ref[0] = h_tail


def _lru_prompt(cols_b, nb, seq, vec, bias, wgates):
    tb = SCAN_TILE
    nt = seq // tb
    blk = lambda b, t: (b * nt + t, 0)
    per_b = lambda b, t: (b, 0, 0)
    return pl.pallas_call(
        _lru_prompt_kernel,
        grid=(nb, nt),
        in_specs=[pl.BlockSpec((tb, 2 * B_WIDTH), blk), _const_spec((SUBLANES, B_WIDTH)),
                  _const_spec((1, 2 * B_WIDTH)), _const_spec((B_WIDTH, 2 * B_WIDTH))],
        out_specs=[pl.BlockSpec((tb, B_WIDTH), blk), pl.BlockSpec((1, SUBLANES, B_WIDTH), per_b),
                   pl.BlockSpec((1, SUBLANES, B_WIDTH), per_b)],
        out_shape=[jax.ShapeDtypeStruct((nb * seq, B_WIDTH), F32),
                   jax.ShapeDtypeStruct((nb, SUBLANES, B_WIDTH), F32),
                   jax.ShapeDtypeStruct((nb, SUBLANES, B_WIDTH), F32)],
        scratch_shapes=[pltpu.VMEM((SUBLANES, B_WIDTH), F32), pltpu.VMEM((SUBLANES, B_WIDTH), F32)],
        compiler_params=_params(2),
        name="lru_prompt",
    )(cols_b, vec, bias, wgates)


def _lru_sample_kernel(cols_ref, conv0_ref, h0_ref, vec_ref, bias_ref, wg_ref, y_ref, conv_ref, h_ref, *, steps):
    rows = cols_ref.shape[0]
    nseq = rows // steps
    gate_br = cols_ref[:, 0:B_WIDTH]
    xb = cols_ref[:, B_WIDTH:2 * B_WIDTH]
    tloc = _row_iota(xb.shape) & (steps - 1)
    h0 = jnp.broadcast_to(h0_ref[...], (nseq, steps, B_WIDTH)).reshape(rows, B_WIDTH)
    y, hs = _lru_body(gate_br, xb, conv0_ref[...], h0, tloc, steps, vec_ref[...], bias_ref[...], wg_ref[...])
    y_ref[...] = y
    conv_ref[...] = xb
    h_ref[...] = hs


def _lru_sample(cols_b, conv0_rows, h0, nb, steps, vec, bias, wgates):
    bb = SAMPLE_SEQS
    rows = bb * steps
    blk = lambda i: (i, 0)
    return pl.pallas_call(
        functools.partial(_lru_sample_kernel, steps=steps),
        grid=(nb // bb,),
        in_specs=[pl.BlockSpec((rows, 2 * B_WIDTH), blk), pl.BlockSpec((rows, B_WIDTH), blk),
                  pl.BlockSpec((bb, 1, B_WIDTH), lambda i: (i, 0, 0)),
                  _const_spec((SUBLANES, B_WIDTH)), _const_spec((1, 2 * B_WIDTH)),
                  _const_spec((B_WIDTH, 2 * B_WIDTH))],
        out_specs=[pl.BlockSpec((rows, B_WIDTH), blk)] * 3,
        out_shape=[jax.ShapeDtypeStruct((nb * steps, B_WIDTH), F32)] * 3,
        compiler_params=_params(1),
        name="lru_sample",
    )(cols_b, conv0_rows, h0, vec, bias, wgates)


def _s5_body(u, h0r, h0i, tloc, tlen, lam, vec, bbd, cbd, wglu):
    lr = lam[0:1, :]
    li = lam[1:2, :]
    dt = jnp.exp(lam[2:3, :])
    mag = jnp.exp(lr * dt)
    ar = mag * jnp.cos(li * dt)
    ai = mag * jnp.sin(li * dt)
    den = lr * lr + li * li
    cr = ((ar - 1.0) * lr + ai * li) / den
    ci = (ai * lr - (ar - 1.0) * li) / den
    ub = _bf(u)
    pb = _dot(ub, bbd[:, 0:S5_W])
    qb = _dot(ub, bbd[:, S5_W:2 * S5_W])
    hr = pb * cr - qb * ci
    hi = pb * ci + qb * cr
    first = tloc == 0
    hr = hr + jnp.where(first, ar * h0r - ai * h0i, 0.0)
    hi = hi + jnp.where(first, ar * h0i + ai * h0r, 0.0)
    pr, pi = ar, ai
    d = 1
    while d < tlen:
        keep = tloc >= d
        sr = jnp.where(keep, _shift_rows(hr, d), 0.0)
        si = jnp.where(keep, _shift_rows(hi, d), 0.0)
        hr, hi = hr + pr * sr - pi * si, hi + pr * si + pi * sr
        pr, pi = pr * pr - pi * pi, 2.0 * pr * pi
        d *= 2
    s5_d = vec[0:1, :]
    b_glu = vec[1:2, :]
    g_out = vec[2:3, :]
    y = _dot(_bf(hr), cbd[0:S5_W, :]) + _dot(_bf(hi), cbd[S5_W:2 * S5_W, :]) + s5_d * u
    z = _gelu_tanh(y)
    out = z * _sigmoid(_dot(_bf(z), wglu) + b_glu)
    return _rms(out, g_out), hr, hi


def _s5_prompt_kernel(u_ref, lam_ref, vec_ref, bbd_ref, cbd_ref, wglu_ref, y_ref, hr_ref, hi_ref, r_scr, i_scr):
    t = pl.program_id(1)
    rows = u_ref.shape[0]

    @pl.when(t == 0)
    def _():
        r_scr[...] = jnp.zeros_like(r_scr)
        i_scr[...] = jnp.zeros_like(i_scr)

    u = u_ref[...]
    tloc = _row_iota((rows, S5_W))
    h0r = jnp.broadcast_to(r_scr[SUBLANES - 1:SUBLANES, :], (rows, S5_W))
    h0i = jnp.broadcast_to(i_scr[SUBLANES - 1:SUBLANES, :], (rows, S5_W))
    y, hr, hi = _s5_body(u, h0r, h0i, tloc, rows, lam_ref[...], vec_ref[...], bbd_ref[...], cbd_ref[...],
                         wglu_ref[...])
    y_ref[...] = y
    r_tail = hr[rows - SUBLANES:rows, :]
    i_tail = hi[rows - SUBLANES:rows, :]
    r_scr[...] = r_tail
    i_scr[...] = i_tail
    hr_ref[0] = r_tail
    hi_ref[0] = i_tail


def _s5_prompt(cols_c, nb, seq, lam, vec, bbd, cbd, wglu):
    tb = SCAN_TILE
    nt = seq // tb
    blk = lambda b, t: (b * nt + t, 0)
    per_b = lambda b, t: (b, 0, 0)
    return pl.pallas_call(
        _s5_prompt_kernel,
        grid=(nb, nt),
        in_specs=[pl.BlockSpec((tb, C_WIDTH), blk), _const_spec((SUBLANES, S5_W)), _const_spec((SUBLANES, C_WIDTH)),
                  _const_spec((C_WIDTH, 2 * S5_W)), _const_spec((2 * S5_W, C_WIDTH)),
                  _const_spec((C_WIDTH, C_WIDTH))],
        out_specs=[pl.BlockSpec((tb, C_WIDTH), blk), pl.BlockSpec((1, SUBLANES, S5_W), per_b),
                   pl.BlockSpec((1, SUBLANES, S5_W), per_b)],
        out_shape=[jax.ShapeDtypeStruct((nb * seq, C_WIDTH), F32),
                   jax.ShapeDtypeStruct((nb, SUBLANES, S5_W), F32),
                   jax.ShapeDtypeStruct((nb, SUBLANES, S5_W), F32)],
        scratch_shapes=[pltpu.VMEM((SUBLANES, S5_W), F32), pltpu.VMEM((SUBLANES, S5_W), F32)],
        compiler_params=_params(2),
        name="s5_prompt",
    )(cols_c, lam, vec, bbd, cbd, wglu)


def _s5_sample_kernel(u_ref, h0r_ref, h0i_ref, lam_ref, vec_ref, bbd_ref, cbd_ref, wglu_ref,
                      y_ref, hr_ref, hi_ref, *, steps):
    rows = u_ref.shape[0]
    nseq = rows // steps
    tloc = _row_iota((rows, S5_W)) & (steps - 1)
    h0r = jnp.broadcast_to(h0r_ref[...], (nseq, steps, S5_W)).reshape(rows, S5_W)
    h0i = jnp.broadcast_to(h0i_ref[...], (nseq, steps, S5_W)).reshape(rows, S5_W)
    y, hr, hi = _s5_body(u_ref[...], h0r, h0i, tloc, steps, lam_ref[...], vec_ref[...], bbd_ref[...],
                         cbd_ref[...], wglu_ref[...])
    y_ref[...] = y
    hr_ref[...] = hr
    hi_ref[...] = hi


def _s5_sample(cols_c, h0r, h0i, nb, steps, lam, vec, bbd, cbd, wglu):
    bb = SAMPLE_SEQS
    rows = bb * steps
    blk = lambda i: (i, 0)
    st = pl.BlockSpec((bb, 1, S5_W), lambda i: (i, 0, 0))
    return pl.pallas_call(
        functools.partial(_s5_sample_kernel, steps=steps),
        grid=(nb // bb,),
        in_specs=[pl.BlockSpec((rows, C_WIDTH), blk), st, st,
                  _const_spec((SUBLANES, S5_W)), _const_spec((SUBLANES, C_WIDTH)),
                  _const_spec((C_WIDTH, 2 * S5_W)), _const_spec((2 * S5_W, C_WIDTH)),
                  _const_spec((C_WIDTH, C_WIDTH))],
        out_specs=[pl.BlockSpec((rows, C_WIDTH), blk), pl.BlockSpec((rows, S5_W), blk),
                   pl.BlockSpec((rows, S5_W), blk)],
        out_shape=[jax.ShapeDtypeStruct((nb * steps, C_WIDTH), F32),
                   jax.ShapeDtypeStruct((nb * steps, S5_W), F32),
                   jax.ShapeDtypeStruct((nb * steps, S5_W), F32)],
        compiler_params=_params(1),
        name="s5_sample",
    )(cols_c, h0r, h0i, lam, vec, bbd, cbd, wglu)


def _pad_rows(rows, width):
    m = jnp.stack([r.reshape(width) for r in rows], axis=0)
    return jnp.pad(m, ((0, SUBLANES - m.shape[0]), (0, 0)))


def _layer_weights(l, w):
    eye_b = jnp.eye(B_BLOCKS, dtype=F32)
    eye_g = jnp.eye(S5_GROUPS, dtype=F32)
    bd4 = lambda m: jnp.einsum("nde,nm->ndme", m, eye_b).reshape(B_WIDTH, B_WIDTH)
    zeros_lora = jnp.zeros((LANES - 64, A_WIDTH), F32)
    b_in = lambda m: jnp.einsum("gpc,gh->gchp", m, eye_g).reshape(C_WIDTH, S5_W)
    c_out = lambda m: jnp.einsum("gcp,gh->gphc", m, eye_g).reshape(S5_W, C_WIDTH)
    return dict(
        g_mix=w["g_mix"][l][None],
        w_in=_bf(w["w_in"][l]),
        mu=w["mu_a"][l][None],
        a_vec=_pad_rows([w["w0"][l], w["a0"][l], w["k_k"][l], w["k_a"][l], w["r_k"][l], w["lnx_w"][l],
                         w["lnx_b"][l]], A_WIDTH),
        wdec=_bf(jnp.concatenate([w["w_dec2"][l], zeros_lora], axis=0)),
        wa=_bf(jnp.concatenate([zeros_lora, w["w_a2"][l]], axis=0)),
        wg=_bf(w["w_g2"][l]),
        b_vec=_pad_rows([w["conv_w"][l][0], w["conv_w"][l][1], w["conv_w"][l][2], w["conv_w"][l][3],
                         w["conv_b"][l], w["lru_lambda"][l], w["g_out_b"][l]], B_WIDTH),
        b_bias=jnp.concatenate([w["b_rg"][l], w["b_ig"][l]])[None],
        b_wgates=_bf(jnp.concatenate([bd4(w["w_rg"][l]), bd4(w["w_ig"][l])], axis=1)),
        c_lam=_pad_rows([w["s5_lam_re"][l], w["s5_lam_im"][l],
                         jnp.repeat(w["s5_log_dt"][l], S5_STATE)], S5_W),
        c_vec=_pad_rows([w["s5_d"][l], w["b_glu"][l], w["g_out_c"][l]], C_WIDTH),
        c_bbd=_bf(jnp.concatenate([b_in(w["s5_b_re"][l]), b_in(w["s5_b_im"][l])], axis=1)),
        c_cbd=_bf(jnp.concatenate([c_out(w["s5_c_re"][l]), -c_out(w["s5_c_im"][l])], axis=0)),
        c_wglu=_bf(w["w_glu"][l]),
        w_out=_bf(w["w_out"][l]),
        p_vec=_pad_rows([w["g_ffn"][l], w["g_ple"][l], w["g_final"]], D_MODEL),
        w_up=_bf(w["w_ffn_up"][l]),
        w_down=_bf(w["w_ffn_down"][l]),
        w_ple=_bf(w["w_ple"][l]),
        w_gate=_bf(w["w_ple_gate"][l]),
    )


def _post_layer(h, ya, yb, yc, p, lw, final):
    return _post(h, ya, yb, yc, p, lw["w_out"], lw["p_vec"], lw["w_up"], lw["w_down"], lw["w_ple"],
                 lw["w_gate"], final)


def _run_prompt(x, p, lws, ones_bd):
    nb, seq, _ = x.shape
    depth = len(lws)
    h = x.reshape(nb * seq, D_MODEL)
    outs = []
    for l, lw in enumerate(lws):
        cols_a, cols_b, cols_c = _proj_in(h, lw["g_mix"], lw["w_in"])
        ya, wkv, last = _rwkv_prompt(cols_a, nb, seq, lw["mu"], lw["a_vec"], lw["wdec"], lw["wa"], lw["wg"],
                                     ones_bd)
        yb, conv, lru = _lru_prompt(cols_b, nb, seq, lw["b_vec"], lw["b_bias"], lw["b_wgates"])
        yc, s5r, s5i = _s5_prompt(cols_c, nb, seq, lw["c_lam"], lw["c_vec"], lw["c_bbd"], lw["c_cbd"],
                                  lw["c_wglu"])
        h = _post_layer(h, ya, yb, yc, p[l].reshape(nb * seq, PLE_DIM), lw, l == depth - 1)
        outs.append((last[:, SUBLANES - 1], wkv, conv[:, SUBLANES - 3:], lru[:, SUBLANES - 1],
                     s5r[:, SUBLANES - 1].reshape(nb, S5_GROUPS, S5_STATE),
                     s5i[:, SUBLANES - 1].reshape(nb, S5_GROUPS, S5_STATE)))
    states = tuple(jnp.stack([o[j] for o in outs], axis=0) for j in range(6))
    return h.reshape(nb, seq, D_MODEL), states


def _run_sample(x, p, states, lws, ones_bd):
    nb, steps, _ = x.shape
    depth = len(lws)
    st_shift, st_wkv, st_conv, st_lru, st_s5r, st_s5i = states
    h = x.reshape(nb * steps, D_MODEL)
    outs = []
    for l, lw in enumerate(lws):
        cols_a, cols_b, cols_c = _proj_in(h, lw["g_mix"], lw["w_in"])
        ya, wkv = _rwkv_sample(cols_a, st_shift[l][:, None, :], st_wkv[l], nb, steps, lw["mu"], lw["a_vec"],
                               lw["wdec"], lw["wa"], lw["wg"], ones_bd)
        conv0_rows = jnp.pad(st_conv[l], ((0, 0), (0, steps - 3), (0, 0))).reshape(nb * steps, B_WIDTH)
        yb, conv, lru = _lru_sample(cols_b, conv0_rows, st_lru[l][:, None, :], nb, steps, lw["b_vec"],
                                    lw["b_bias"], lw["b_wgates"])
        yc, s5r, s5i = _s5_sample(cols_c, st_s5r[l].reshape(nb, 1, S5_W), st_s5i[l].reshape(nb, 1, S5_W),
                                  nb, steps, lw["c_lam"], lw["c_vec"], lw["c_bbd"], lw["c_cbd"], lw["c_wglu"])
        h = _post_layer(h, ya, yb, yc, p[l].reshape(nb * steps, PLE_DIM), lw, l == depth - 1)
        last = lambda z, width: z.reshape(nb, steps, width)[:, steps - 1]
        outs.append((last(cols_a, A_COLS), wkv, conv.reshape(nb, steps, B_WIDTH)[:, steps - 3:],
                     last(lru, B_WIDTH),
                     last(s5r, S5_W).reshape(nb, S5_GROUPS, S5_STATE),
                     last(s5i, S5_W).reshape(nb, S5_GROUPS, S5_STATE)))
    new_states = tuple(jnp.stack([o[j] for o in outs], axis=0) for j in range(6))
    return h.reshape(nb, steps, D_MODEL), new_states


def kernel(x_prompt, x_sample, p_prompt, p_sample, state_shift, state_wkv, state_conv, state_lru, state_s5_re, state_s5_im, g_mix, w_in, mu_a, w0, w_dec2, a0, w_a2, w_g2, k_k, k_a, r_k, lnx_w, lnx_b, conv_w, conv_b, w_rg, b_rg, w_ig, b_ig, lru_lambda, g_out_b, s5_lam_re, s5_lam_im, s5_log_dt, s5_b_re, s5_b_im, s5_c_re, s5_c_im, s5_d, w_glu, b_glu, g_out_c, w_out, g_ffn, w_ffn_up, w_ffn_down, g_ple, w_ple, w_ple_gate, g_final):
    w = dict(g_mix=g_mix, w_in=w_in, mu_a=mu_a, w0=w0, w_dec2=w_dec2, a0=a0, w_a2=w_a2, w_g2=w_g2, k_k=k_k,
             k_a=k_a, r_k=r_k, lnx_w=lnx_w, lnx_b=lnx_b, conv_w=conv_w, conv_b=conv_b, w_rg=w_rg, b_rg=b_rg,
             w_ig=w_ig, b_ig=b_ig, lru_lambda=lru_lambda, g_out_b=g_out_b, s5_lam_re=s5_lam_re,
             s5_lam_im=s5_lam_im, s5_log_dt=s5_log_dt, s5_b_re=s5_b_re, s5_b_im=s5_b_im, s5_c_re=s5_c_re,
             s5_c_im=s5_c_im, s5_d=s5_d, w_glu=w_glu, b_glu=b_glu, g_out_c=g_out_c, w_out=w_out, g_ffn=g_ffn,
             w_ffn_up=w_ffn_up, w_ffn_down=w_ffn_down, g_ple=g_ple, w_ple=w_ple, w_ple_gate=w_ple_gate,
             g_final=g_final)
    depth = g_mix.shape[0]
    lws = [_layer_weights(l, w) for l in range(depth)]
    ones_bd = _bf(jnp.kron(jnp.eye(A_HEADS, dtype=F32), jnp.ones((A_HEAD_DIM, A_HEAD_DIM), F32)))
    y_prompt, new_p = _run_prompt(x_prompt, p_prompt, lws, ones_bd)
    y_sample, new_s = _run_sample(x_sample, p_sample,
                                  (state_shift, state_wkv, state_conv, state_lru, state_s5_re, state_s5_im),
                                  lws, ones_bd)
    return (y_prompt, y_sample) + new_p + new_s
```

```python
import functools
import math

import jax
import jax.numpy as jnp
from jax import lax
from jax.experimental import pallas as pl
from jax.experimental.pallas import tpu as pltpu

F32 = jnp.float32
BF16 = jnp.bfloat16

D_MODEL = 1024
A_WIDTH = 512
A_HEADS = 8
A_HEAD_DIM = 64
A_COLS = 1792
B_WIDTH = 256
B_BLOCKS = 4
C_WIDTH = 256
S5_GROUPS = 16
S5_GROUP_CH = 16
S5_STATE = 64
S5_W = S5_GROUPS * S5_STATE
IN_COLS = A_COLS + 2 * B_WIDTH + C_WIDTH
D_FF = 2816
PLE_DIM = 256
LRU_C = 8.0
RMS_EPS = 1e-6
GN_EPS = 64e-5

LANES = 128
SUBLANES = 8
VMEM_LIMIT_BYTES = 56 * 1024 * 1024

TOKEN_TILE = 512
FFN_CHUNK = 256
RWKV_CHUNK = 64
RWKV_BLOCK = 128
SCAN_TILE = 64
SAMPLE_SEQS = 8


def _const_spec(shape):
    nd = len(shape)
    return pl.BlockSpec(shape, lambda *_: (0,) * nd, pipeline_mode=pl.Buffered(1))


def _params(n_axes):
    return pltpu.CompilerParams(dimension_semantics=("arbitrary",) * n_axes,
                                vmem_limit_bytes=VMEM_LIMIT_BYTES)


def _dot(a, b):
    return jnp.dot(a, b, preferred_element_type=F32)


def _dot_nt(a, b):
    return lax.dot_general(a, b, (((1,), (1,)), ((), ())), preferred_element_type=F32)


def _dot_tn(a, b):
    return lax.dot_general(a, b, (((0,), (0,)), ((), ())), preferred_element_type=F32)


def _bf(x):
    return x.astype(BF16)


def _rms(x, g):
    inv = lax.rsqrt(jnp.mean(x * x, axis=-1, keepdims=True) + RMS_EPS)
    return x * inv * g


def _sigmoid(x):
    return 1.0 / (1.0 + jnp.exp(-x))


def _softplus(x):
    return jnp.maximum(x, 0.0) + jnp.log(1.0 + jnp.exp(-jnp.abs(x)))


def _gelu_tanh(x):
    c = math.sqrt(2.0 / math.pi)
    return 0.5 * x * (1.0 + jnp.tanh(c * (x + 0.044715 * (x * x * x))))


def _row_iota(shape):
    return lax.broadcasted_iota(jnp.int32, shape, 0)


def _shift_rows(x, d):
    return pltpu.roll(x, d, axis=0)


def _proj_in_kernel(h_ref, g_ref, w_ref, a_ref, b_ref, c_ref):
    xn = _bf(_rms(h_ref[...], g_ref[...]))
    a_ref[...] = _dot(xn, w_ref[:, 0:A_COLS])
    b_ref[...] = _dot(xn, w_ref[:, A_COLS:A_COLS + 2 * B_WIDTH])
    c_ref[...] = _dot(xn, w_ref[:, A_COLS + 2 * B_WIDTH:IN_COLS])


def _proj_in(h, g, w_bf):
    n = h.shape[0]
    tm = TOKEN_TILE
    row = lambda i: (i, 0)
    return pl.pallas_call(
        _proj_in_kernel,
        grid=(n // tm,),
        in_specs=[pl.BlockSpec((tm, D_MODEL), row), _const_spec((1, D_MODEL)),
                  _const_spec((D_MODEL, IN_COLS))],
        out_specs=[pl.BlockSpec((tm, A_COLS), row), pl.BlockSpec((tm, 2 * B_WIDTH), row),
                   pl.BlockSpec((tm, C_WIDTH), row)],
        out_shape=[jax.ShapeDtypeStruct((n, A_COLS), F32), jax.ShapeDtypeStruct((n, 2 * B_WIDTH), F32),
                   jax.ShapeDtypeStruct((n, C_WIDTH), F32)],
        compiler_params=_params(1),
        name="proj_in",
    )(h, g, w_bf)


def _post_kernel(h_ref, ya_ref, yb_ref, yc_ref, p_ref, wo_ref, vec_ref, wup_ref, wdn_ref, wple_ref,
                 wgate_ref, o_ref, act_scr, *, final):
    g_ffn = vec_ref[0:1, :]
    g_ple = vec_ref[1:2, :]
    g_final = vec_ref[2:3, :]
    h1 = (h_ref[...] + _dot(_bf(ya_ref[...]), wo_ref[0:A_WIDTH, :])
          + _dot(_bf(yb_ref[...]), wo_ref[A_WIDTH:A_WIDTH + B_WIDTH, :])
          + _dot(_bf(yc_ref[...]), wo_ref[A_WIDTH + B_WIDTH:D_MODEL, :]))
    xf = _bf(_rms(h1, g_ffn))
    for c0 in range(0, D_FF, FFN_CHUNK):
        gate = _dot(xf, wup_ref[:, c0:c0 + FFN_CHUNK])
        up = _dot(xf, wup_ref[:, D_FF + c0:D_FF + c0 + FFN_CHUNK])
        act_scr[:, c0:c0 + FFN_CHUNK] = _bf(gate * _sigmoid(gate) * up)
    h2 = h1 + _dot(act_scr[...], wdn_ref[...])
    ple = _dot(_bf(p_ref[...]), wple_ref[...])
    gate = _sigmoid(_dot(_bf(_rms(h2, g_ple)), wgate_ref[...]))
    h3 = h2 + ple * gate
    if final:
        h3 = _rms(h3, g_final)
    o_ref[...] = h3


def _post(h, ya, yb, yc, p, wo, vec, wup, wdn, wple, wgate, final):
    n = h.shape[0]
    tm = TOKEN_TILE
    row = lambda i: (i, 0)
    return pl.pallas_call(
        functools.partial(_post_kernel, final=final),
        grid=(n // tm,),
        in_specs=[pl.BlockSpec((tm, D_MODEL), row), pl.BlockSpec((tm, A_WIDTH), row),
                  pl.BlockSpec((tm, B_WIDTH), row), pl.BlockSpec((tm, C_WIDTH), row),
                  pl.BlockSpec((tm, PLE_DIM), row),
                  _const_spec((D_MODEL, D_MODEL)), _const_spec((SUBLANES, D_MODEL)),
                  _const_spec((D_MODEL, 2 * D_FF)), _const_spec((D_FF, D_MODEL)),
                  _const_spec((PLE_DIM, D_MODEL)), _const_spec((D_MODEL, D_MODEL))],
        out_specs=pl.BlockSpec((tm, D_MODEL), row),
        out_shape=jax.ShapeDtypeStruct((n, D_MODEL), F32),
        scratch_shapes=[pltpu.VMEM((tm, D_FF), BF16)],
        compiler_params=_params(1),
        name="post",
    )(h, ya, yb, yc, p, wo, vec, wup, wdn, wple, wgate)


def _seg_sum(x, ones_bd):
    hi = _bf(x)
    lo = _bf(x - hi.astype(F32))
    return _dot(hi, ones_bd) + _dot(lo, ones_bd)


def _rwkv_prep(cols, prev, mu, vec, wdec, wa, wg, ones_bd, chunk):
    rows = cols.shape[0]
    xs = cols + (prev - cols) * mu
    r = xs[:, 0:A_WIDTH]
    k = xs[:, A_WIDTH:2 * A_WIDTH]
    v = xs[:, 2 * A_WIDTH:3 * A_WIDTH]
    xwa = xs[:, 3 * A_WIDTH:3 * A_WIDTH + LANES]
    xg = xs[:, 3 * A_WIDTH + LANES:A_COLS]
    w0, a0, k_k, k_a, r_k = (vec[i:i + 1, :] for i in range(5))
    z = w0 + _dot(_bf(jnp.tanh(xwa)), wdec)
    log_decay = -jnp.exp(-_softplus(-z) - 0.5)
    a = _sigmoid(a0 + _dot(_bf(xwa), wa))
    g = _dot(_bf(_sigmoid(xg)), wg)
    kk_raw = k * k_k
    kk = kk_raw * lax.rsqrt(jnp.maximum(_seg_sum(kk_raw * kk_raw, ones_bd), 1e-24))
    k_mod = k * (1.0 + (a - 1.0) * k_a)
    bonus = _seg_sum(r * k_mod * r_k, ones_bd) * v
    ri = _row_iota((rows, rows))
    ci = lax.broadcasted_iota(jnp.int32, (rows, rows), 1)
    same_chunk = (ci & (-chunk)) == (ri & (-chunk))
    tri = _bf(jnp.where((ci <= ri) & same_chunk, 1.0, 0.0))
    h1 = _bf(log_decay)
    r1 = log_decay - h1.astype(F32)
    h2 = _bf(r1)
    h3 = _bf(r1 - h2.astype(F32))
    cum = _dot(tri, h1) + _dot(tri, h2) + _dot(tri, h3)
    gam = jnp.exp(cum)
    ginv = jnp.exp(-cum)
    gprev = jnp.exp(cum - log_decay)
    return dict(rt=r * gam, kap=kk * gprev, bet=kk * a * ginv, kt=k_mod * ginv, v=v, gam=gam,
                bonus=bonus, g=g)


def _rwkv_local(ops):
    c = ops[0]["kap"].shape[0]
    c2 = 2 * c
    first = lax.broadcasted_iota(jnp.int32, (c, LANES), 1) < A_HEAD_DIM
    ri = _row_iota((c2, c2))
    ci = lax.broadcasted_iota(jnp.int32, (c2, c2), 1)
    strict = ci < ri
    incl = ci <= ri
    eye = jnp.where(ri == ci, 1.0, 0.0)
    merged = c2 % LANES == 0

    def stack(x):
        return _bf(jnp.concatenate([jnp.where(first, x, 0.0), jnp.where(first, 0.0, x)], axis=0))

    st = [{k: stack(o[k]) for k in ("kap", "rt", "bet", "kt", "v")} for o in ops]
    for s in st:
        if merged:
            g = _dot_nt(jnp.concatenate([s["kap"], s["rt"]], axis=0), jnp.concatenate([s["bet"], s["kt"]], axis=0))
            quad = lambda i, j: g[i * c2:(i + 1) * c2, j * c2:(j + 1) * c2]
            s["a_b"] = jnp.where(strict, quad(0, 0), 0.0)
            a_k, a_rb, a_rk = quad(0, 1), quad(1, 0), quad(1, 1)
        else:
            s["a_b"] = jnp.where(strict, _dot_nt(s["kap"], s["bet"]), 0.0)
            a_k, a_rb, a_rk = _dot_nt(s["kap"], s["kt"]), _dot_nt(s["rt"], s["bet"]), _dot_nt(s["rt"], s["kt"])
        s["a_k"] = _bf(jnp.where(strict, a_k, 0.0))
        s["a_rb"] = _bf(jnp.where(incl, a_rb, 0.0))
        s["a_rk"] = _bf(jnp.where(incl, a_rk, 0.0))
    for s in st:
        s["t"] = eye - s["a_b"]
        s["lp"] = _bf(s["a_b"])
        s["akv"] = _bf(_dot(s["a_k"], s["v"]))
    n = 2
    while n < c:
        for s in st:
            s["lp32"] = _dot(s["lp"], s["lp"])
        for s in st:
            s["lp"] = _bf(s["lp32"])
            s["t"] = s["t"] + _dot(_bf(s["t"]), s["lp"])
        n *= 2
    for s in st:
        wu = _dot(_bf(s["t"]), jnp.concatenate([s["kap"], s["akv"]], axis=1))
        s["w2"] = _bf(-wu[:, 0:LANES])
        s["u02"] = -wu[:, LANES:2 * LANES]
    return st


def _rwkv_state(st, states, g_ends):
    c2 = st[0]["kap"].shape[0]
    c = c2 // 2
    merged = c2 % LANES == 0
    xs = []
    for s, state in zip(st, states):
        xs.append(_dot_nt(jnp.concatenate([s["w2"], s["rt"]], axis=0), _bf(state)))
    ys, new_states = [], []
    u2s = [_bf(x[0:c2] + s["u02"]) for x, s in zip(xs, st)]
    for s, x, u2 in zip(st, xs, u2s):
        if merged:
            y2 = x[c2:2 * c2] + _dot(jnp.concatenate([s["a_rb"], s["a_rk"]], axis=1),
                                     jnp.concatenate([u2, s["v"]], axis=0))
        else:
            y2 = x[c2:2 * c2] + _dot(s["a_rb"], u2) + _dot(s["a_rk"], s["v"])
        ys.append(y2[0:c] + y2[c:c2])
    for s, u2, state, g_end in zip(st, u2s, states, g_ends):
        ds = _dot_tn(jnp.concatenate([u2, s["v"]], axis=0), jnp.concatenate([s["bet"], s["kt"]], axis=0))
        new_states.append((state + ds) * g_end)
    return ys, new_states


def _rwkv_finish(y, bonus, g, vec, ones_bd):
    lnx_w = vec[5:6, :]
    lnx_b = vec[6:7, :]
    inv_n = 1.0 / A_HEAD_DIM
    mu = _seg_sum(y, ones_bd) * inv_n
    d = y - mu
    var = _seg_sum(d * d, ones_bd) * inv_n
    yn = d * lax.rsqrt(var + GN_EPS) * lnx_w + lnx_b
    return (yn + bonus) * g


def _pair_state(s_ref, idx, p):
    z = jnp.zeros((A_HEAD_DIM, A_HEAD_DIM), F32)
    top = jnp.concatenate([s_ref[idx, 2 * p], z], axis=1)
    bot = jnp.concatenate([z, s_ref[idx, 2 * p + 1]], axis=1)
    return jnp.concatenate([top, bot], axis=0)


def _rwkv_prompt_kernel(cols_ref, mu_ref, vec_ref, wdec_ref, wa_ref, wg_ref, ones_ref,
                        y_ref, wkv_ref, last_ref, s_scr, prev_scr):
    t = pl.program_id(1)
    rows = cols_ref.shape[0]
    chunk = RWKV_CHUNK

    @pl.when(t == 0)
    def _():
        s_scr[...] = jnp.zeros_like(s_scr)
        prev_scr[...] = jnp.zeros_like(prev_scr)

    cols = cols_ref[...]
    first_row = _row_iota(cols.shape) == 0
    prev = jnp.where(first_row, prev_scr[SUBLANES - 1:SUBLANES, :], _shift_rows(cols, 1))
    ones_bd = ones_ref[...]
    q = _rwkv_prep(cols, prev, mu_ref[...], vec_ref[...], wdec_ref[...], wa_ref[...], wg_ref[...],
                   ones_bd, chunk)
    n_pairs = A_HEADS // 2
    lanes = [slice(p * LANES, (p + 1) * LANES) for p in range(n_pairs)]
    starts = list(range(0, rows, chunk))
    ops = [{k: q[k][c0:c0 + chunk, ln] for k in ("kap", "rt", "bet", "kt", "v")}
           for c0 in starts for ln in lanes]
    st = _rwkv_local(ops)
    states = [s_scr[p] for p in range(n_pairs)]
    y_rows = []
    for i, c0 in enumerate(starts):
        g_ends = [q["gam"][c0 + chunk - 1:c0 + chunk, ln] for ln in lanes]
        ys, states = _rwkv_state(st[i * n_pairs:(i + 1) * n_pairs], states, g_ends)
        y_rows.append(jnp.concatenate(ys, axis=1))
    for p in range(n_pairs):
        s_scr[p] = states[p]
        wkv_ref[0, 2 * p] = states[p][0:A_HEAD_DIM, 0:A_HEAD_DIM]
        wkv_ref[0, 2 * p + 1] = states[p][A_HEAD_DIM:LANES, A_HEAD_DIM:LANES]
    y = jnp.concatenate(y_rows, axis=0) if len(y_rows) > 1 else y_rows[0]
    y_ref[...] = _rwkv_finish(y, q["bonus"], q["g"], vec_ref[...], ones_bd)
    tail = cols[rows - SUBLANES:rows, :]
    prev_scr[...] = tail
    last_ref[0] = tail


def _rwkv_prompt(cols_a, nb, seq, mu, vec, wdec, wa, wg, ones_bd):
    tb = RWKV_BLOCK
    nt = seq // tb
    blk = lambda b, t: (b * nt + t, 0)
    per_b3 = lambda b, t: (b, 0, 0)
    return pl.pallas_call(
        _rwkv_prompt_kernel,
        grid=(nb, nt),
        in_specs=[pl.BlockSpec((tb, A_COLS), blk), _const_spec((1, A_COLS)), _const_spec((SUBLANES, A_WIDTH)),
                  _const_spec((LANES, A_WIDTH)), _const_spec((LANES, A_WIDTH)), _const_spec((LANES, A_WIDTH)),
                  _const_spec((A_WIDTH, A_WIDTH))],
        out_specs=[pl.BlockSpec((tb, A_WIDTH), blk),
                   pl.BlockSpec((1, A_HEADS, A_HEAD_DIM, A_HEAD_DIM), lambda b, t: (b, 0, 0, 0)),
                   pl.BlockSpec((1, SUBLANES, A_COLS), per_b3)],
        out_shape=[jax.ShapeDtypeStruct((nb * seq, A_WIDTH), F32),
                   jax.ShapeDtypeStruct((nb, A_HEADS, A_HEAD_DIM, A_HEAD_DIM), F32),
                   jax.ShapeDtypeStruct((nb, SUBLANES, A_COLS), F32)],
        scratch_shapes=[pltpu.VMEM((A_HEADS // 2, LANES, LANES), F32), pltpu.VMEM((SUBLANES, A_COLS), F32)],
        compiler_params=_params(2),
        name="rwkv_prompt",
    )(cols_a, mu, vec, wdec, wa, wg, ones_bd)


def _rwkv_sample_kernel(cols_ref, shift_ref, wkv0_ref, mu_ref, vec_ref, wdec_ref, wa_ref, wg_ref, ones_ref,
                        y_ref, wkv_ref, *, steps):
    rows = cols_ref.shape[0]
    nseq = rows // steps
    cols = cols_ref[...]
    tloc = _row_iota(cols.shape) & (steps - 1)
    shift0 = jnp.broadcast_to(shift_ref[...], (nseq, steps, A_COLS)).reshape(rows, A_COLS)
    prev = jnp.where(tloc == 0, shift0, _shift_rows(cols, 1))
    ones_bd = ones_ref[...]
    q = _rwkv_prep(cols, prev, mu_ref[...], vec_ref[...], wdec_ref[...], wa_ref[...], wg_ref[...],
                   ones_bd, steps)
    n_pairs = A_HEADS // 2
    lanes = [slice(p * LANES, (p + 1) * LANES) for p in range(n_pairs)]
    chains = [(s, p) for s in range(nseq) for p in range(n_pairs)]
    ops = [{k: q[k][s * steps:(s + 1) * steps, lanes[p]] for k in ("kap", "rt", "bet", "kt", "v")}
           for s, p in chains]
    states = [_pair_state(wkv0_ref, s, p) for s, p in chains]
    g_ends = [q["gam"][(s + 1) * steps - 1:(s + 1) * steps, lanes[p]] for s, p in chains]
    ys, new_states = _rwkv_state(_rwkv_local(ops), states, g_ends)
    for (s, p), s_new in zip(chains, new_states):
        wkv_ref[s, 2 * p] = s_new[0:A_HEAD_DIM, 0:A_HEAD_DIM]
        wkv_ref[s, 2 * p + 1] = s_new[A_HEAD_DIM:LANES, A_HEAD_DIM:LANES]
    y = jnp.concatenate([jnp.concatenate(ys[s * n_pairs:(s + 1) * n_pairs], axis=1) for s in range(nseq)],
                        axis=0)
    y_ref[...] = _rwkv_finish(y, q["bonus"], q["g"], vec_ref[...], ones_bd)


def _rwkv_sample(cols_a, shift0, wkv0, nb, steps, mu, vec, wdec, wa, wg, ones_bd):
    bb = SAMPLE_SEQS
    rows = bb * steps
    blk = lambda i: (i, 0)
    return pl.pallas_call(
        functools.partial(_rwkv_sample_kernel, steps=steps),
        grid=(nb // bb,),
        in_specs=[pl.BlockSpec((rows, A_COLS), blk),
                  pl.BlockSpec((bb, 1, A_COLS), lambda i: (i, 0, 0)),
                  pl.BlockSpec((bb, A_HEADS, A_HEAD_DIM, A_HEAD_DIM), lambda i: (i, 0, 0, 0)),
                  _const_spec((1, A_COLS)), _const_spec((SUBLANES, A_WIDTH)),
                  _const_spec((LANES, A_WIDTH)), _const_spec((LANES, A_WIDTH)), _const_spec((LANES, A_WIDTH)),
                  _const_spec((A_WIDTH, A_WIDTH))],
        out_specs=[pl.BlockSpec((rows, A_WIDTH), blk),
                   pl.BlockSpec((bb, A_HEADS, A_HEAD_DIM, A_HEAD_DIM), lambda i: (i, 0, 0, 0))],
        out_shape=[jax.ShapeDtypeStruct((nb * steps, A_WIDTH), F32),
                   jax.ShapeDtypeStruct((nb, A_HEADS, A_HEAD_DIM, A_HEAD_DIM), F32)],
        compiler_params=_params(1),
        name="rwkv_sample",
    )(cols_a, shift0, wkv0, mu, vec, wdec, wa, wg, ones_bd)


def _lru_body(gate_br, xb, conv_prev, h0, tloc, tlen, vec, bias, wgates):
    rows = xb.shape[0]
    conv_b = vec[4:5, :]
    lam = vec[5:6, :]
    g_out = vec[6:7, :]
    xc = conv_b + vec[3:4, :] * xb
    for j in (1, 2, 3):
        tail = conv_prev if j == 3 else _shift_rows(conv_prev, rows - (3 - j))
        xc = xc + vec[3 - j:4 - j, :] * jnp.where(tloc >= j, _shift_rows(xb, j), tail)
    gates = _sigmoid(_dot(_bf(xc), wgates) + bias)
    gate_r = gates[:, 0:B_WIDTH]
    gate_i = gates[:, B_WIDTH:2 * B_WIDTH]
    log_a = (-LRU_C) * gate_r * _softplus(-lam)
    a = jnp.exp(log_a)
    mult = jnp.sqrt(-jnp.tanh(log_a) * (a * a + 1.0))
    b = mult * gate_i * xc
    b = b + jnp.where(tloc == 0, a * h0, 0.0)
    d = 1
    while d < tlen:
        keep = tloc >= d
        a_s = jnp.where(keep, _shift_rows(a, d), 1.0)
        b_s = jnp.where(keep, _shift_rows(b, d), 0.0)
        b = a * b_s + b
        a = a * a_s
        d *= 2
    hs = b
    y = hs * _gelu_tanh(gate_br)
    return _rms(y, g_out), hs


def _lru_prompt_kernel(cols_ref, vec_ref, bias_ref, wg_ref, y_ref, conv_ref, h_ref, x_scr, h_scr):
    t = pl.program_id(1)
    rows = cols_ref.shape[0]

    @pl.when(t == 0)
    def _():
        x_scr[...] = jnp.zeros_like(x_scr)
        h_scr[...] = jnp.zeros_like(h_scr)

    gate_br = cols_ref[:, 0:B_WIDTH]
    xb = cols_ref[:, B_WIDTH:2 * B_WIDTH]
    tloc = _row_iota(xb.shape)
    conv_prev = jnp.concatenate([pltpu.roll(x_scr[...], 3, axis=0), jnp.zeros((rows - SUBLANES, B_WIDTH), F32)],
                                axis=0)
    h0 = jnp.broadcast_to(h_scr[SUBLANES - 1:SUBLANES, :], xb.shape)
    y, hs = _lru_body(gate_br, xb, conv_prev, h0, tloc, rows, vec_ref[...], bias_ref[...], wg_ref[...])
    y_ref[...] = y
    x_tail = xb[rows - SUBLANES:rows, :]
    h_tail = hs[rows - SUBLANES:rows, :]
    x_scr[...] = x_tail
    h_scr[...] = h_tail
    conv_ref[0] = x_tail
    h_ref[0] = h_tail


def _lru_prompt(cols_b, nb, seq, vec, bias, wgates):
    tb = SCAN_TILE
    nt = seq // tb
    blk = lambda b, t: (b * nt + t, 0)
    per_b = lambda b, t: (b, 0, 0)
    return pl.pallas_call(
        _lru_prompt_kernel,
        grid=(nb, nt),
        in_specs=[pl.BlockSpec((tb, 2 * B_WIDTH), blk), _const_spec((SUBLANES, B_WIDTH)),
                  _const_spec((1, 2 * B_WIDTH)), _const_spec((B_WIDTH, 2 * B_WIDTH))],
        out_specs=[pl.BlockSpec((tb, B_WIDTH), blk), pl.BlockSpec((1, SUBLANES, B_WIDTH), per_b),
                   pl.BlockSpec((1, SUBLANES, B_WIDTH), per_b)],
        out_shape=[jax.ShapeDtypeStruct((nb * seq, B_WIDTH), F32),
                   jax.ShapeDtypeStruct((nb, SUBLANES, B_WIDTH), F32),
                   jax.ShapeDtypeStruct((nb, SUBLANES, B_WIDTH), F32)],
        scratch_shapes=[pltpu.VMEM((SUBLANES, B_WIDTH), F32), pltpu.VMEM((SUBLANES, B_WIDTH), F32)],
        compiler_params=_params(2),
        name="lru_prompt",
    )(cols_b, vec, bias, wgates)


def _lru_sample_kernel(cols_ref, conv0_ref, h0_ref, vec_ref, bias_ref, wg_ref, y_ref, conv_ref, h_ref, *, steps):
    rows = cols_ref.shape[0]
    nseq = rows // steps
    gate_br = cols_ref[:, 0:B_WIDTH]
    xb = cols_ref[:, B_WIDTH:2 * B_WIDTH]
    tloc = _row_iota(xb.shape) & (steps - 1)
    h0 = jnp.broadcast_to(h0_ref[...], (nseq, steps, B_WIDTH)).reshape(rows, B_WIDTH)
    y, hs = _lru_body(gate_br, xb, conv0_ref[...], h0, tloc, steps, vec_ref[...], bias_ref[...], wg_ref[...])
    y_ref[...] = y
    conv_ref[...] = xb
    h_ref[...] = hs


def _lru_sample(cols_b, conv0_rows, h0, nb, steps, vec, bias, wgates):
    bb = SAMPLE_SEQS
    rows = bb * steps
    blk = lambda i: (i, 0)
    return pl.pallas_call(
        functools.partial(_lru_sample_kernel, steps=steps),
        grid=(nb // bb,),
        in_specs=[pl.BlockSpec((rows, 2 * B_WIDTH), blk), pl.BlockSpec((rows, B_WIDTH), blk),
                  pl.BlockSpec((bb, 1, B_WIDTH), lambda i: (i, 0, 0)),
                  _const_spec((SUBLANES, B_WIDTH)), _const_spec((1, 2 * B_WIDTH)),
                  _const_spec((B_WIDTH, 2 * B_WIDTH))],
        out_specs=[pl.BlockSpec((rows, B_WIDTH), blk)] * 3,
        out_shape=[jax.ShapeDtypeStruct((nb * steps, B_WIDTH), F32)] * 3,
        compiler_params=_params(1),
        name="lru_sample",
    )(cols_b, conv0_rows, h0, vec, bias, wgates)


def _s5_body(u, h0r, h0i, tloc, tlen, lam, vec, bbd, cbd, wglu):
    lr = lam[0:1, :]
    li = lam[1:2, :]
    dt = jnp.exp(lam[2:3, :])
    mag = jnp.exp(lr * dt)
    ar = mag * jnp.cos(li * dt)
    ai = mag * jnp.sin(li * dt)
    den = lr * lr + li * li
    cr = ((ar - 1.0) * lr + ai * li) / den
    ci = (ai * lr - (ar - 1.0) * li) / den
    ub = _bf(u)
    pb = _dot(ub, bbd[:, 0:S5_W])
    qb = _dot(ub, bbd[:, S5_W:2 * S5_W])
    hr = pb * cr - qb * ci
    hi = pb * ci + qb * cr
    first = tloc == 0
    hr = hr + jnp.where(first, ar * h0r - ai * h0i, 0.0)
    hi = hi + jnp.where(first, ar * h0i + ai * h0r, 0.0)
    pr, pi = ar, ai
    d = 1
    while d < tlen:
        keep = tloc >= d
        sr = jnp.where(keep, _shift_rows(hr, d), 0.0)
        si = jnp.where(keep, _shift_rows(hi, d), 0.0)
        hr, hi = hr + pr * sr - pi * si, hi + pr * si + pi * sr
        pr, pi = pr * pr - pi * pi, 2.0 * pr * pi
        d *= 2
    s5_d = vec[0:1, :]
    b_glu = vec[1:2, :]
    g_out = vec[2:3, :]
    y = _dot(_bf(hr), cbd[0:S5_W, :]) + _dot(_bf(hi), cbd[S5_W:2 * S5_W, :]) + s5_d * u
    z = _gelu_tanh(y)
    out = z * _sigmoid(_dot(_bf(z), wglu) + b_glu)
    return _rms(out, g_out), hr, hi


def _s5_prompt_kernel(u_ref, lam_ref, vec_ref, bbd_ref, cbd_ref, wglu_ref, y_ref, hr_ref, hi_ref, r_scr, i_scr):
    t = pl.program_id(1)
    rows = u_ref.shape[0]

    @pl.when(t == 0)
    def _():
        r_scr[...] = jnp.zeros_like(r_scr)
        i_scr[...] = jnp.zeros_like(i_scr)

    u = u_ref[...]
    tloc = _row_iota((rows, S5_W))
    h0r = jnp.broadcast_to(r_scr[SUBLANES - 1:SUBLANES, :], (rows, S5_W))
    h0i = jnp.broadcast_to(i_scr[SUBLANES - 1:SUBLANES, :], (rows, S5_W))
    y, hr, hi = _s5_body(u, h0r, h0i, tloc, rows, lam_ref[...], vec_ref[...], bbd_ref[...], cbd_ref[...],
                         wglu_ref[...])
    y_ref[...] = y
    r_tail = hr[rows - SUBLANES:rows, :]
    i_tail = hi[rows - SUBLANES:rows, :]
    r_scr[...] = r_tail
    i_scr[...] = i_tail
    hr_ref[0] = r_tail
    hi_ref[0] = i_tail


def _s5_prompt(cols_c, nb, seq, lam, vec, bbd, cbd, wglu):
    tb = SCAN_TILE
    nt = seq // tb
    blk = lambda b, t: (b * nt + t, 0)
    per_b = lambda b, t: (b, 0, 0)
    return pl.pallas_call(
        _s5_prompt_kernel,
        grid=(nb, nt),
        in_specs=[pl.BlockSpec((tb, C_WIDTH), blk), _const_spec((SUBLANES, S5_W)), _const_spec((SUBLANES, C_WIDTH)),
                  _const_spec((C_WIDTH, 2 * S5_W)), _const_spec((2 * S5_W, C_WIDTH)),
                  _const_spec((C_WIDTH, C_WIDTH))],
        out_specs=[pl.BlockSpec((tb, C_WIDTH), blk), pl.BlockSpec((1, SUBLANES, S5_W), per_b),
                   pl.BlockSpec((1, SUBLANES, S5_W), per_b)],
        out_shape=[jax.ShapeDtypeStruct((nb * seq, C_WIDTH), F32),
                   jax.ShapeDtypeStruct((nb, SUBLANES, S5_W), F32),
                   jax.ShapeDtypeStruct((nb, SUBLANES, S5_W), F32)],
        scratch_shapes=[pltpu.VMEM((SUBLANES, S5_W), F32), pltpu.VMEM((SUBLANES, S5_W), F32)],
        compiler_params=_params(2),
        name="s5_prompt",
    )(cols_c, lam, vec, bbd, cbd, wglu)


def _s5_sample_kernel(u_ref, h0r_ref, h0i_ref, lam_ref, vec_ref, bbd_ref, cbd_ref, wglu_ref,
                      y_ref, hr_ref, hi_ref, *, steps):
    rows = u_ref.shape[0]
    nseq = rows // steps
    tloc = _row_iota((rows, S5_W)) & (steps - 1)
    h0r = jnp.broadcast_to(h0r_ref[...], (nseq, steps, S5_W)).reshape(rows, S5_W)
    h0i = jnp.broadcast_to(h0i_ref[...], (nseq, steps, S5_W)).reshape(rows, S5_W)
    y, hr, hi = _s5_body(u_ref[...], h0r, h0i, tloc, steps, lam_ref[...], vec_ref[...], bbd_ref[...],
                         cbd_ref[...], wglu_ref[...])
    y_ref[...] = y
    hr_ref[...] = hr
    hi_ref[...] = hi


def _s5_sample(cols_c, h0r, h0i, nb, steps, lam, vec, bbd, cbd, wglu):
    bb = SAMPLE_SEQS
    rows = bb * steps
    blk = lambda i: (i, 0)
    st = pl.BlockSpec((bb, 1, S5_W), lambda i: (i, 0, 0))
    return pl.pallas_call(
        functools.partial(_s5_sample_kernel, steps=steps),
        grid=(nb // bb,),
        in_specs=[pl.BlockSpec((rows, C_WIDTH), blk), st, st,
                  _const_spec((SUBLANES, S5_W)), _const_spec((SUBLANES, C_WIDTH)),
                  _const_spec((C_WIDTH, 2 * S5_W)), _const_spec((2 * S5_W, C_WIDTH)),
                  _const_spec((C_WIDTH, C_WIDTH))],
        out_specs=[pl.BlockSpec((rows, C_WIDTH), blk), pl.BlockSpec((rows, S5_W), blk),
                   pl.BlockSpec((rows, S5_W), blk)],
        out_shape=[jax.ShapeDtypeStruct((nb * steps, C_WIDTH), F32),
                   jax.ShapeDtypeStruct((nb * steps, S5_W), F32),
                   jax.ShapeDtypeStruct((nb * steps, S5_W), F32)],
        compiler_params=_params(1),
        name="s5_sample",
    )(cols_c, h0r, h0i, lam, vec, bbd, cbd, wglu)


def _pad_rows(rows, width):
    m = jnp.stack([r.reshape(width) for r in rows], axis=0)
    return jnp.pad(m, ((0, SUBLANES - m.shape[0]), (0, 0)))


def _layer_weights(l, w):
    eye_b = jnp.eye(B_BLOCKS, dtype=F32)
    eye_g = jnp.eye(S5_GROUPS, dtype=F32)
    bd4 = lambda m: jnp.einsum("nde,nm->ndme", m, eye_b).reshape(B_WIDTH, B_WIDTH)
    zeros_lora = jnp.zeros((LANES - 64, A_WIDTH), F32)
    b_in = lambda m: jnp.einsum("gpc,gh->gchp", m, eye_g).reshape(C_WIDTH, S5_W)
    c_out = lambda m: jnp.einsum("gcp,gh->gphc", m, eye_g).reshape(S5_W, C_WIDTH)
    return dict(
        g_mix=w["g_mix"][l][None],
        w_in=_bf(w["w_in"][l]),
        mu=w["mu_a"][l][None],
        a_vec=_pad_rows([w["w0"][l], w["a0"][l], w["k_k"][l], w["k_a"][l], w["r_k"][l], w["lnx_w"][l],
                         w["lnx_b"][l]], A_WIDTH),
        wdec=_bf(jnp.concatenate([w["w_dec2"][l], zeros_lora], axis=0)),
        wa=_bf(jnp.concatenate([zeros_lora, w["w_a2"][l]], axis=0)),
        wg=_bf(w["w_g2"][l]),
        b_vec=_pad_rows([w["conv_w"][l][0], w["conv_w"][l][1], w["conv_w"][l][2], w["conv_w"][l][3],
                         w["conv_b"][l], w["lru_lambda"][l], w["g_out_b"][l]], B_WIDTH),
        b_bias=jnp.concatenate([w["b_rg"][l], w["b_ig"][l]])[None],
        b_wgates=_bf(jnp.concatenate([bd4(w["w_rg"][l]), bd4(w["w_ig"][l])], axis=1)),
        c_lam=_pad_rows([w["s5_lam_re"][l], w["s5_lam_im"][l],
                         jnp.repeat(w["s5_log_dt"][l], S5_STATE)], S5_W),
        c_vec=_pad_rows([w["s5_d"][l], w["b_glu"][l], w["g_out_c"][l]], C_WIDTH),
        c_bbd=_bf(jnp.concatenate([b_in(w["s5_b_re"][l]), b_in(w["s5_b_im"][l])], axis=1)),
        c_cbd=_bf(jnp.concatenate([c_out(w["s5_c_re"][l]), -c_out(w["s5_c_im"][l])], axis=0)),
        c_wglu=_bf(w["w_glu"][l]),
        w_out=_bf(w["w_out"][l]),
        p_vec=_pad_rows([w["g_ffn"][l], w["g_ple"][l], w["g_final"]], D_MODEL),
        w_up=_bf(w["w_ffn_up"][l]),
        w_down=_bf(w["w_ffn_down"][l]),
        w_ple=_bf(w["w_ple"][l]),
        w_gate=_bf(w["w_ple_gate"][l]),
    )


def _post_layer(h, ya, yb, yc, p, lw, final):
    return _post(h, ya, yb, yc, p, lw["w_out"], lw["p_vec"], lw["w_up"], lw["w_down"], lw["w_ple"],
                 lw["w_gate"], final)


def _run_prompt(x, p, lws, ones_bd):
    nb, seq, _ = x.shape
    depth = len(lws)
    h = x.reshape(nb * seq, D_MODEL)
    outs = []
    for l, lw in enumerate(lws):
        cols_a, cols_b, cols_c = _proj_in(h, lw["g_mix"], lw["w_in"])
        ya, wkv, last = _rwkv_prompt(cols_a, nb, seq, lw["mu"], lw["a_vec"], lw["wdec"], lw["wa"], lw["wg"],
                                     ones_bd)
        yb, conv, lru = _lru_prompt(cols_b, nb, seq, lw["b_vec"], lw["b_bias"], lw["b_wgates"])
        yc, s5r, s5i = _s5_prompt(cols_c, nb, seq, lw["c_lam"], lw["c_vec"], lw["c_bbd"], lw["c_cbd"],
                                  lw["c_wglu"])
        h = _post_layer(h, ya, yb, yc, p[l].reshape(nb * seq, PLE_DIM), lw, l == depth - 1)
        outs.append((last[:, SUBLANES - 1], wkv, conv[:, SUBLANES - 3:], lru[:, SUBLANES - 1],
                     s5r[:, SUBLANES - 1].reshape(nb, S5_GROUPS, S5_STATE),
                     s5i[:, SUBLANES - 1].reshape(nb, S5_GROUPS, S5_STATE)))
    states = tuple(jnp.stack([o[j] for o in outs], axis=0) for j in range(6))
    return h.reshape(nb, seq, D_MODEL), states


def _run_sample(x, p, states, lws, ones_bd):
    nb, steps, _ = x.shape
    depth = len(lws)
    st_shift, st_wkv, st_conv, st_lru, st_s5r, st_s5i = states
    h = x.reshape(nb * steps, D_MODEL)
    outs = []
    for l, lw in enumerate(lws):
        cols_a, cols_b, cols_c = _proj_in(h, lw["g_mix"], lw["w_in"])
        ya, wkv = _rwkv_sample(cols_a, st_shift[l][:, None, :], st_wkv[l], nb, steps, lw["mu"], lw["a_vec"],
                               lw["wdec"], lw["wa"], lw["wg"], ones_bd)
        conv0_rows = jnp.pad(st_conv[l], ((0, 0), (0, steps - 3), (0, 0))).reshape(nb * steps, B_WIDTH)
        yb, conv, lru = _lru_sample(cols_b, conv0_rows, st_lru[l][:, None, :], nb, steps, lw["b_vec"],
                                    lw["b_bias"], lw["b_wgates"])
        yc, s5r, s5i = _s5_sample(cols_c, st_s5r[l].reshape(nb, 1, S5_W), st_s5i[l].reshape(nb, 1, S5_W),
                                  nb, steps, lw["c_lam"], lw["c_vec"], lw["c_bbd"], lw["c_cbd"], lw["c_wglu"])
        h = _post_layer(h, ya, yb, yc, p[l].reshape(nb * steps, PLE_DIM), lw, l == depth - 1)
        last = lambda z, width: z.reshape(nb, steps, width)[:, steps - 1]
        outs.append((last(cols_a, A_COLS), wkv, conv.reshape(nb, steps, B_WIDTH)[:, steps - 3:],
                     last(lru, B_WIDTH),
                     last(s5r, S5_W).reshape(nb, S5_GROUPS, S5_STATE),
                     last(s5i, S5_W).reshape(nb, S5_GROUPS, S5_STATE)))
    new_states = tuple(jnp.stack([o[j] for o in outs], axis=0) for j in range(6))
    return h.reshape(nb, steps, D_MODEL), new_states


def kernel(x_prompt, x_sample, p_prompt, p_sample, state_shift, state_wkv, state_conv, state_lru, state_s5_re, state_s5_im, g_mix, w_in, mu_a, w0, w_dec2, a0, w_a2, w_g2, k_k, k_a, r_k, lnx_w, lnx_b, conv_w, conv_b, w_rg, b_rg, w_ig, b_ig, lru_lambda, g_out_b, s5_lam_re, s5_lam_im, s5_log_dt, s5_b_re, s5_b_im, s5_c_re, s5_c_im, s5_d, w_glu, b_glu, g_out_c, w_out, g_ffn, w_ffn_up, w_ffn_down, g_ple, w_ple, w_ple_gate, g_final):
    w = dict(g_mix=g_mix, w_in=w_in, mu_a=mu_a, w0=w0, w_dec2=w_dec2, a0=a0, w_a2=w_a2, w_g2=w_g2, k_k=k_k,
             k_a=k_a, r_k=r_k, lnx_w=lnx_w, lnx_b=lnx_b, conv_w=conv_w, conv_b=conv_b, w_rg=w_rg, b_rg=b_rg,
             w_ig=w_ig, b_ig=b_ig, lru_lambda=lru_lambda, g_out_b=g_out_b, s5_lam_re=s5_lam_re,
             s5_lam_im=s5_lam_im, s5_log_dt=s5_log_dt, s5_b_re=s5_b_re, s5_b_im=s5_b_im, s5_c_re=s5_c_re,
             s5_c_im=s5_c_im, s5_d=s5_d, w_glu=w_glu, b_glu=b_glu, g_out_c=g_out_c, w_out=w_out, g_ffn=g_ffn,
             w_ffn_up=w_ffn_up, w_ffn_down=w_ffn_down, g_ple=g_ple, w_ple=w_ple, w_ple_gate=w_ple_gate,
             g_final=g_final)
    depth = g_mix.shape[0]
    lws = [_layer_weights(l, w) for l in range(depth)]
    ones_bd = _bf(jnp.kron(jnp.eye(A_HEADS, dtype=F32), jnp.ones((A_HEAD_DIM, A_HEAD_DIM), F32)))
    y_prompt, new_p = _run_prompt(x_prompt, p_prompt, lws, ones_bd)
    y_sample, new_s = _run_sample(x_sample, p_sample,
                                  (state_shift, state_wkv, state_conv, state_lru, state_s5_re, state_s5_im),
                                  lws, ones_bd)
    return (y_prompt, y_sample) + new_p + new_s
```

```python
import functools
import math

import jax
import jax.numpy as jnp
from jax import lax
from jax.experimental import pallas as pl
from jax.experimental.pallas import tpu as pltpu

F32 = jnp.float32
BF16 = jnp.bfloat16

D_MODEL = 1024
A_WIDTH = 512
A_HEADS = 8
A_HEAD_DIM = 64
A_COLS = 1792
B_WIDTH = 256
B_BLOCKS = 4
CONV_WIDTH = 4
C_WIDTH = 256
S5_GROUPS = 16
S5_GROUP_CH = 16
S5_STATE = 64
S5_W = S5_GROUPS * S5_STATE
IN_COLS = A_COLS + 2 * B_WIDTH + C_WIDTH
D_FF = 2816
PLE_DIM = 256
LRU_C = 8.0
RMS_EPS = 1e-6
GN_EPS = 64e-5

LANES = 128
SUBLANES = 8
VMEM_LIMIT_BYTES = 56 * 1024 * 1024

TOKEN_TILE = 512
FFN_CHUNK = 256
RWKV_CHUNK = 64
RWKV_BLOCK = 128
SCAN_STEPS = 64
SAMPLE_SEQS = 8


def _const_spec(shape):
    nd = len(shape)
    return pl.BlockSpec(shape, lambda *_: (0,) * nd, pipeline_mode=pl.Buffered(1))


def _params(n_axes):
    return pltpu.CompilerParams(dimension_semantics=("arbitrary",) * n_axes,
                                vmem_limit_bytes=VMEM_LIMIT_BYTES)


def _dot(a, b):
    return jnp.dot(a, b, preferred_element_type=F32)


def _dot_nt(a, b):
    return lax.dot_general(a, b, (((1,), (1,)), ((), ())), preferred_element_type=F32)


def _dot_tn(a, b):
    return lax.dot_general(a, b, (((0,), (0,)), ((), ())), preferred_element_type=F32)


def _bf(x):
    return x.astype(BF16)


def _rms(x, g):
    inv = lax.rsqrt(jnp.mean(x * x, axis=-1, keepdims=True) + RMS_EPS)
    return x * inv * g


def _sigmoid(x):
    return 1.0 / (1.0 + jnp.exp(-x))


def _softplus(x):
    return jnp.maximum(x, 0.0) + jnp.log(1.0 + jnp.exp(-jnp.abs(x)))


def _gelu_tanh(x):
    c = math.sqrt(2.0 / math.pi)
    return 0.5 * x * (1.0 + jnp.tanh(c * (x + 0.044715 * (x * x * x))))


def _row_iota(shape):
    return lax.broadcasted_iota(jnp.int32, shape, 0)


def _shift_rows(x, d):
    return pltpu.roll(x, d, axis=0)


def _proj_in_kernel(h_ref, g_ref, w_ref, a_ref, b_ref, c_ref):
    xn = _bf(_rms(h_ref[...], g_ref[...]))
    a_ref[...] = _dot(xn, w_ref[:, 0:A_COLS])
    b_ref[...] = _dot(xn, w_ref[:, A_COLS:A_COLS + 2 * B_WIDTH])
    c_ref[...] = _dot(xn, w_ref[:, A_COLS + 2 * B_WIDTH:IN_COLS])


def _proj_in(h, g, w_bf):
    n = h.shape[0]
    tm = TOKEN_TILE
    row = lambda i: (i, 0)
    return pl.pallas_call(
        _proj_in_kernel,
        grid=(n // tm,),
        in_specs=[pl.BlockSpec((tm, D_MODEL), row), _const_spec((1, D_MODEL)),
                  _const_spec((D_MODEL, IN_COLS))],
        out_specs=[pl.BlockSpec((tm, A_COLS), row), pl.BlockSpec((tm, 2 * B_WIDTH), row),
                   pl.BlockSpec((tm, C_WIDTH), row)],
        out_shape=[jax.ShapeDtypeStruct((n, A_COLS), F32), jax.ShapeDtypeStruct((n, 2 * B_WIDTH), F32),
                   jax.ShapeDtypeStruct((n, C_WIDTH), F32)],
        compiler_params=_params(1),
        name="proj_in",
    )(h, g, w_bf)


def _lane_slabs(x):
    return [x[:, j * LANES:(j + 1) * LANES] for j in range(x.shape[1] // LANES)]


def _proj_in_tm_kernel(h_ref, g_ref, w_ref, a_ref, b_ref, c_ref, *, nb):
    b = pl.program_id(1)
    tm = h_ref.shape[0]
    xn = _bf(_rms(h_ref[...], g_ref[...]))
    a_ref[...] = _dot(xn, w_ref[:, 0:A_COLS])
    rows_of_b = pl.ds(b, tm, stride=nb)
    for j, slab in enumerate(_lane_slabs(_dot(xn, w_ref[:, A_COLS:A_COLS + 2 * B_WIDTH]))):
        b_ref[j, rows_of_b, :] = slab
    for j, slab in enumerate(_lane_slabs(_dot(xn, w_ref[:, A_COLS + 2 * B_WIDTH:IN_COLS]))):
        c_ref[j, rows_of_b, :] = slab


def _proj_in_tm(h, g, w_bf, nb, seq):
    tm = TOKEN_TILE
    nt = seq // tm
    row = lambda i, b: (b * nt + i, 0)
    slab = lambda i, b: (0, i, 0)
    nb_slabs, nc_slabs = 2 * B_WIDTH // LANES, C_WIDTH // LANES
    return pl.pallas_call(
        functools.partial(_proj_in_tm_kernel, nb=nb),
        grid=(nt, nb),
        in_specs=[pl.BlockSpec((tm, D_MODEL), row), _const_spec((1, D_MODEL)),
                  _const_spec((D_MODEL, IN_COLS))],
        out_specs=[pl.BlockSpec((tm, A_COLS), row), pl.BlockSpec((nb_slabs, tm * nb, LANES), slab),
                   pl.BlockSpec((nc_slabs, tm * nb, LANES), slab)],
        out_shape=[jax.ShapeDtypeStruct((nb * seq, A_COLS), F32),
                   jax.ShapeDtypeStruct((nb_slabs, nb * seq, LANES), F32),
                   jax.ShapeDtypeStruct((nc_slabs, nb * seq, LANES), F32)],
        compiler_params=_params(2),
        name="proj_in_tm",
    )(h, g, w_bf)


def _post_kernel(h_ref, ya_ref, yb_ref, yc_ref, p_ref, wo_ref, vec_ref, wup_ref, wdn_ref, wple_ref,
                 wgate_ref, o_ref, act_scr, *, final, nb_tm):
    g_ffn = vec_ref[0:1, :]
    g_ple = vec_ref[1:2, :]
    g_final = vec_ref[2:3, :]
    if nb_tm:
        rows_of_b = pl.ds(pl.program_id(1), h_ref.shape[0], stride=nb_tm)
        yb = jnp.concatenate([yb_ref[j, rows_of_b, :] for j in range(B_WIDTH // LANES)], axis=1)
        yc = jnp.concatenate([yc_ref[j, rows_of_b, :] for j in range(C_WIDTH // LANES)], axis=1)
    else:
        yb, yc = yb_ref[...], yc_ref[...]
    h1 = (h_ref[...] + _dot(_bf(ya_ref[...]), wo_ref[0:A_WIDTH, :])
          + _dot(_bf(yb), wo_ref[A_WIDTH:A_WIDTH + B_WIDTH, :])
          + _dot(_bf(yc), wo_ref[A_WIDTH + B_WIDTH:D_MODEL, :]))
    xf = _bf(_rms(h1, g_ffn))
    for c0 in range(0, D_FF, FFN_CHUNK):
        gate = _dot(xf, wup_ref[:, c0:c0 + FFN_CHUNK])
        up = _dot(xf, wup_ref[:, D_FF + c0:D_FF + c0 + FFN_CHUNK])
        act_scr[:, c0:c0 + FFN_CHUNK] = _bf(gate * _sigmoid(gate) * up)
    h2 = h1 + _dot(act_scr[...], wdn_ref[...])
    ple = _dot(_bf(p_ref[...]), wple_ref[...])
    gate = _sigmoid(_dot(_bf(_rms(h2, g_ple)), wgate_ref[...]))
    h3 = h2 + ple * gate
    if final:
        h3 = _rms(h3, g_final)
    o_ref[...] = h3


def _post(h, ya, yb, yc, p, wo, vec, wup, wdn, wple, wgate, final, nb_tm=0):
    n = h.shape[0]
    tm = TOKEN_TILE
    if nb_tm:
        nt = n // (nb_tm * tm)
        grid = (nt, nb_tm)
        row = lambda i, b: (b * nt + i, 0)
        slab = lambda i, b: (0, i, 0)
        yb_spec = pl.BlockSpec((B_WIDTH // LANES, tm * nb_tm, LANES), slab, pipeline_mode=pl.Buffered(1))
        yc_spec = pl.BlockSpec((C_WIDTH // LANES, tm * nb_tm, LANES), slab, pipeline_mode=pl.Buffered(1))
    else:
        grid = (n // tm,)
        row = lambda i: (i, 0)
        yb_spec = pl.BlockSpec((tm, B_WIDTH), row)
        yc_spec = pl.BlockSpec((tm, C_WIDTH), row)
    return pl.pallas_call(
        functools.partial(_post_kernel, final=final, nb_tm=nb_tm),
        grid=grid,
        in_specs=[pl.BlockSpec((tm, D_MODEL), row), pl.BlockSpec((tm, A_WIDTH), row),
                  yb_spec, yc_spec,
                  pl.BlockSpec((tm, PLE_DIM), row),
                  _const_spec((D_MODEL, D_MODEL)), _const_spec((SUBLANES, D_MODEL)),
                  _const_spec((D_MODEL, 2 * D_FF)), _const_spec((D_FF, D_MODEL)),
                  _const_spec((PLE_DIM, D_MODEL)), _const_spec((D_MODEL, D_MODEL))],
        out_specs=pl.BlockSpec((tm, D_MODEL), row),
        out_shape=jax.ShapeDtypeStruct((n, D_MODEL), F32),
        scratch_shapes=[pltpu.VMEM((tm, D_FF), BF16)],
        compiler_params=_params(len(grid)),
        name="post",
    )(h, ya, yb, yc, p, wo, vec, wup, wdn, wple, wgate)


def _seg_sum(x, ones_bd):
    hi = _bf(x)
    lo = _bf(x - hi.astype(F32))
    return _dot(hi, ones_bd) + _dot(lo, ones_bd)


def _rwkv_prep(cols, prev, mu, vec, wdec, wa, wg, ones_bd, chunk):
    rows = cols.shape[0]
    xs = cols + (prev - cols) * mu
    r = xs[:, 0:A_WIDTH]
    k = xs[:, A_WIDTH:2 * A_WIDTH]
    v = xs[:, 2 * A_WIDTH:3 * A_WIDTH]
    xwa = xs[:, 3 * A_WIDTH:3 * A_WIDTH + LANES]
    xg = xs[:, 3 * A_WIDTH + LANES:A_COLS]
    w0, a0, k_k, k_a, r_k = (vec[i:i + 1, :] for i in range(5))
    z = w0 + _dot(_bf(jnp.tanh(xwa)), wdec)
    log_decay = -jnp.exp(-_softplus(-z) - 0.5)
    a = _sigmoid(a0 + _dot(_bf(xwa), wa))
    g = _dot(_bf(_sigmoid(xg)), wg)
    kk_raw = k * k_k
    kk = kk_raw * lax.rsqrt(jnp.maximum(_seg_sum(kk_raw * kk_raw, ones_bd), 1e-24))
    k_mod = k * (1.0 + (a - 1.0) * k_a)
    bonus = _seg_sum(r * k_mod * r_k, ones_bd) * v
    ri = _row_iota((rows, rows))
    ci = lax.broadcasted_iota(jnp.int32, (rows, rows), 1)
    same_chunk = (ci & (-chunk)) == (ri & (-chunk))
    tri = _bf(jnp.where((ci <= ri) & same_chunk, 1.0, 0.0))
    h1 = _bf(log_decay)
    r1 = log_decay - h1.astype(F32)
    h2 = _bf(r1)
    h3 = _bf(r1 - h2.astype(F32))
    cum = _dot(tri, h1) + _dot(tri, h2) + _dot(tri, h3)
    gam = jnp.exp(cum)
    ginv = jnp.exp(-cum)
    gprev = jnp.exp(cum - log_decay)
    return dict(rt=r * gam, kap=kk * gprev, bet=kk * a * ginv, kt=k_mod * ginv, v=v, gam=gam,
                bonus=bonus, g=g)


def _rwkv_local(ops):
    c = ops[0]["kap"].shape[0]
    c2 = 2 * c
    first = lax.broadcasted_iota(jnp.int32, (c, LANES), 1) < A_HEAD_DIM
    ri = _row_iota((c2, c2))
    ci = lax.broadcasted_iota(jnp.int32, (c2, c2), 1)
    strict = ci < ri
    incl = ci <= ri
    eye = jnp.where(ri == ci, 1.0, 0.0)
    merged = c2 % LANES == 0

    def stack(x):
        return _bf(jnp.concatenate([jnp.where(first, x, 0.0), jnp.where(first, 0.0, x)], axis=0))

    st = [{k: stack(o[k]) for k in ("kap", "rt", "bet", "kt", "v")} for o in ops]
    for s in st:
        if merged:
            g = _dot_nt(jnp.concatenate([s["kap"], s["rt"]], axis=0), jnp.concatenate([s["bet"], s["kt"]], axis=0))
            quad = lambda i, j: g[i * c2:(i + 1) * c2, j * c2:(j + 1) * c2]
            s["a_b"] = jnp.where(strict, quad(0, 0), 0.0)
            a_k, a_rb, a_rk = quad(0, 1), quad(1, 0), quad(1, 1)
        else:
            s["a_b"] = jnp.where(strict, _dot_nt(s["kap"], s["bet"]), 0.0)
            a_k, a_rb, a_rk = _dot_nt(s["kap"], s["kt"]), _dot_nt(s["rt"], s["bet"]), _dot_nt(s["rt"], s["kt"])
        s["a_k"] = _bf(jnp.where(strict, a_k, 0.0))
        s["a_rb"] = _bf(jnp.where(incl, a_rb, 0.0))
        s["a_rk"] = _bf(jnp.where(incl, a_rk, 0.0))
    for s in st:
        s["t"] = eye - s["a_b"]
        s["lp"] = _bf(s["a_b"])
        s["akv"] = _bf(_dot(s["a_k"], s["v"]))
    n = 2
    while n < c:
        for s in st:
            s["lp32"] = _dot(s["lp"], s["lp"])
        for s in st:
            s["lp"] = _bf(s["lp32"])
            s["t"] = s["t"] + _dot(_bf(s["t"]), s["lp"])
        n *= 2
    for s in st:
        wu = _dot(_bf(s["t"]), jnp.concatenate([s["kap"], s["akv"]], axis=1))
        s["w2"] = _bf(-wu[:, 0:LANES])
        s["u02"] = -wu[:, LANES:2 * LANES]
    return st


def _rwkv_state(st, states, g_ends):
    c2 = st[0]["kap"].shape[0]
    c = c2 // 2
    merged = c2 % LANES == 0
    xs = []
    for s, state in zip(st, states):
        xs.append(_dot_nt(jnp.concatenate([s["w2"], s["rt"]], axis=0), _bf(state)))
    ys, new_states = [], []
    u2s = [_bf(x[0:c2] + s["u02"]) for x, s in zip(xs, st)]
    for s, x, u2 in zip(st, xs, u2s):
        if merged:
            y2 = x[c2:2 * c2] + _dot(jnp.concatenate([s["a_rb"], s["a_rk"]], axis=1),
                                     jnp.concatenate([u2, s["v"]], axis=0))
        else:
            y2 = x[c2:2 * c2] + _dot(s["a_rb"], u2) + _dot(s["a_rk"], s["v"])
        ys.append(y2[0:c] + y2[c:c2])
    for s, u2, state, g_end in zip(st, u2s, states, g_ends):
        ds = _dot_tn(jnp.concatenate([u2, s["v"]], axis=0), jnp.concatenate([s["bet"], s["kt"]], axis=0))
        new_states.append((state + ds) * g_end)
    return ys, new_states


def _rwkv_finish(y, bonus, g, vec, ones_bd):
    lnx_w = vec[5:6, :]
    lnx_b = vec[6:7, :]
    inv_n = 1.0 / A_HEAD_DIM
    mu = _seg_sum(y, ones_bd) * inv_n
    d = y - mu
    var = _seg_sum(d * d, ones_bd) * inv_n
    yn = d * lax.rsqrt(var + GN_EPS) * lnx_w + lnx_b
    return (yn + bonus) * g


def _pair_state(s_ref, idx, p):
    z = jnp.zeros((A_HEAD_DIM, A_HEAD_DIM), F32)
    top = jnp.concatenate([s_ref[idx, 2 * p], z], axis=1)
    bot = jnp.concatenate([z, s_ref[idx, 2 * p + 1]], axis=1)
    return jnp.concatenate([top, bot], axis=0)


def _rwkv_prompt_kernel(cols_ref, mu_ref, vec_ref, wdec_ref, wa_ref, wg_ref, ones_ref,
                        y_ref, wkv_ref, last_ref, s_scr, prev_scr):
    t = pl.program_id(1)
    rows = cols_ref.shape[0]
    chunk = RWKV_CHUNK

    @pl.when(t == 0)
    def _():
        s_scr[...] = jnp.zeros_like(s_scr)
        prev_scr[...] = jnp.zeros_like(prev_scr)

    cols = cols_ref[...]
    first_row = _row_iota(cols.shape) == 0
    prev = jnp.where(first_row, prev_scr[SUBLANES - 1:SUBLANES, :], _shift_rows(cols, 1))
    ones_bd = ones_ref[...]
    q = _rwkv_prep(cols, prev, mu_ref[...], vec_ref[...], wdec_ref[...], wa_ref[...], wg_ref[...],
                   ones_bd, chunk)
    n_pairs = A_HEADS // 2
    lanes = [slice(p * LANES, (p + 1) * LANES) for p in range(n_pairs)]
    starts = list(range(0, rows, chunk))
    ops = [{k: q[k][c0:c0 + chunk, ln] for k in ("kap", "rt", "bet", "kt", "v")}
           for c0 in starts for ln in lanes]
    st = _rwkv_local(ops)
    states = [s_scr[p] for p in range(n_pairs)]
    y_rows = []
    for i, c0 in enumerate(starts):
        g_ends = [q["gam"][c0 + chunk - 1:c0 + chunk, ln] for ln in lanes]
        ys, states = _rwkv_state(st[i * n_pairs:(i + 1) * n_pairs], states, g_ends)
        y_rows.append(jnp.concatenate(ys, axis=1))
    for p in range(n_pairs):
        s_scr[p] = states[p]
        wkv_ref[0, 2 * p] = states[p][0:A_HEAD_DIM, 0:A_HEAD_DIM]
        wkv_ref[0, 2 * p + 1] = states[p][A_HEAD_DIM:LANES, A_HEAD_DIM:LANES]
    y = jnp.concatenate(y_rows, axis=0) if len(y_rows) > 1 else y_rows[0]
    y_ref[...] = _rwkv_finish(y, q["bonus"], q["g"], vec_ref[...], ones_bd)
    tail = cols[rows - SUBLANES:rows, :]
    prev_scr[...] = tail
    last_ref[0] = tail


def _rwkv_prompt(cols_a, nb, seq, mu, vec, wdec, wa, wg, ones_bd):
    tb = RWKV_BLOCK
    nt = seq // tb
    blk = lambda b, t: (b * nt + t, 0)
    per_b3 = lambda b, t: (b, 0, 0)
    return pl.pallas_call(
        _rwkv_prompt_kernel,
        grid=(nb, nt),
        in_specs=[pl.BlockSpec((tb, A_COLS), blk), _const_spec((1, A_COLS)), _const_spec((SUBLANES, A_WIDTH)),
                  _const_spec((LANES, A_WIDTH)), _const_spec((LANES, A_WIDTH)), _const_spec((LANES, A_WIDTH)),
                  _const_spec((A_WIDTH, A_WIDTH))],
        out_specs=[pl.BlockSpec((tb, A_WIDTH), blk),
                   pl.BlockSpec((1, A_HEADS, A_HEAD_DIM, A_HEAD_DIM), lambda b, t: (b, 0, 0, 0)),
                   pl.BlockSpec((1, SUBLANES, A_COLS), per_b3)],
        out_shape=[jax.ShapeDtypeStruct((nb * seq, A_WIDTH), F32),
                   jax.ShapeDtypeStruct((nb, A_HEADS, A_HEAD_DIM, A_HEAD_DIM), F32),
                   jax.ShapeDtypeStruct((nb, SUBLANES, A_COLS), F32)],
        scratch_shapes=[pltpu.VMEM((A_HEADS // 2, LANES, LANES), F32), pltpu.VMEM((SUBLANES, A_COLS), F32)],
        compiler_params=_params(2),
        name="rwkv_prompt",
    )(cols_a, mu, vec, wdec, wa, wg, ones_bd)


def _rwkv_sample_kernel(cols_ref, shift_ref, wkv0_ref, mu_ref, vec_ref, wdec_ref, wa_ref, wg_ref, ones_ref,
                        y_ref, wkv_ref, *, steps):
    rows = cols_ref.shape[0]
    nseq = rows // steps
    cols = cols_ref[...]
    tloc = _row_iota(cols.shape) & (steps - 1)
    shift0 = jnp.broadcast_to(shift_ref[...], (nseq, steps, A_COLS)).reshape(rows, A_COLS)
    prev = jnp.where(tloc == 0, shift0, _shift_rows(cols, 1))
    ones_bd = ones_ref[...]
    q = _rwkv_prep(cols, prev, mu_ref[...], vec_ref[...], wdec_ref[...], wa_ref[...], wg_ref[...],
                   ones_bd, steps)
    n_pairs = A_HEADS // 2
    lanes = [slice(p * LANES, (p + 1) * LANES) for p in range(n_pairs)]
    chains = [(s, p) for s in range(nseq) for p in range(n_pairs)]
    ops = [{k: q[k][s * steps:(s + 1) * steps, lanes[p]] for k in ("kap", "rt", "bet", "kt", "v")}
           for s, p in chains]
    states = [_pair_state(wkv0_ref, s, p) for s, p in chains]
    g_ends = [q["gam"][(s + 1) * steps - 1:(s + 1) * steps, lanes[p]] for s, p in chains]
    ys, new_states = _rwkv_state(_rwkv_local(ops), states, g_ends)
    for (s, p), s_new in zip(chains, new_states):
        wkv_ref[s, 2 * p] = s_new[0:A_HEAD_DIM, 0:A_HEAD_DIM]
        wkv_ref[s, 2 * p + 1] = s_new[A_HEAD_DIM:LANES, A_HEAD_DIM:LANES]
    y = jnp.concatenate([jnp.concatenate(ys[s * n_pairs:(s + 1) * n_pairs], axis=1) for s in range(nseq)],
                        axis=0)
    y_ref[...] = _rwkv_finish(y, q["bonus"], q["g"], vec_ref[...], ones_bd)


def _rwkv_sample(cols_a, shift0, wkv0, nb, steps, mu, vec, wdec, wa, wg, ones_bd):
    bb = SAMPLE_SEQS
    rows = bb * steps
    blk = lambda i: (i, 0)
    return pl.pallas_call(
        functools.partial(_rwkv_sample_kernel, steps=steps),
        grid=(nb // bb,),
        in_specs=[pl.BlockSpec((rows, A_COLS), blk),
                  pl.BlockSpec((bb, 1, A_COLS), lambda i: (i, 0, 0)),
                  pl.BlockSpec((bb, A_HEADS, A_HEAD_DIM, A_HEAD_DIM), lambda i: (i, 0, 0, 0)),
                  _const_spec((1, A_COLS)), _const_spec((SUBLANES, A_WIDTH)),
                  _const_spec((LANES, A_WIDTH)), _const_spec((LANES, A_WIDTH)), _const_spec((LANES, A_WIDTH)),
                  _const_spec((A_WIDTH, A_WIDTH))],
        out_specs=[pl.BlockSpec((rows, A_WIDTH), blk),
                   pl.BlockSpec((bb, A_HEADS, A_HEAD_DIM, A_HEAD_DIM), lambda i: (i, 0, 0, 0))],
        out_shape=[jax.ShapeDtypeStruct((nb * steps, A_WIDTH), F32),
                   jax.ShapeDtypeStruct((nb, A_HEADS, A_HEAD_DIM, A_HEAD_DIM), F32)],
        compiler_params=_params(1),
        name="rwkv_sample",
    )(cols_a, shift0, wkv0, mu, vec, wdec, wa, wg, ones_bd)


def _lru_body(gate_br, xb, conv_prev, h0, tloc, tlen, vec, bias, wgates):
    rows = xb.shape[0]
    conv_b = vec[4:5, :]
    g_out = vec[6:7, :]
    xc = conv_b + vec[3:4, :] * xb
    for j in (1, 2, 3):
        tail = conv_prev if j == 3 else _shift_rows(conv_prev, rows - (3 - j))
        xc = xc + vec[3 - j:4 - j, :] * jnp.where(tloc >= j, _shift_rows(xb, j), tail)
    a, b = _lru_gates(xc, vec, bias, wgates)
    b = b + jnp.where(tloc == 0, a * h0, 0.0)
    d = 1
    while d < tlen:
        keep = tloc >= d
        a_s = jnp.where(keep, _shift_rows(a, d), 1.0)
        b_s = jnp.where(keep, _shift_rows(b, d), 0.0)
        b = a * b_s + b
        a = a * a_s
        d *= 2
    hs = b
    y = hs * _gelu_tanh(gate_br)
    return _rms(y, g_out), hs


def _lru_gates(xc, vec, bias, wgates):
    lam = vec[5:6, :]
    gates = _sigmoid(_dot(_bf(xc), wgates) + bias)
    gate_r = gates[:, 0:B_WIDTH]
    gate_i = gates[:, B_WIDTH:2 * B_WIDTH]
    log_a = (-LRU_C) * gate_r * _softplus(-lam)
    a = jnp.exp(log_a)
    mult = jnp.sqrt(-jnp.tanh(log_a) * (a * a + 1.0))
    return a, mult * gate_i * xc


def _lru_tm_kernel(cols_ref, vec_ref, bias_ref, wg_ref, y_ref, conv_ref, h_ref, x_scr, a_scr, b_scr, h_scr, *, nb):
    rows = a_scr.shape[0]
    hist = (CONV_WIDTH - 1) * nb

    @pl.when(pl.program_id(0) == 0)
    def _():
        x_scr[0:hist, :] = jnp.zeros((hist, B_WIDTH), F32)
        h_scr[...] = jnp.zeros_like(h_scr)

    vec = vec_ref[...]
    gate_br = jnp.concatenate([cols_ref[0], cols_ref[1]], axis=1)
    xb = jnp.concatenate([cols_ref[2], cols_ref[3]], axis=1)
    x_scr[hist:hist + rows, :] = xb
    xc = vec[4:5, :] + vec[3:4, :] * xb
    for j in range(1, CONV_WIDTH):
        xc = xc + vec[3 - j:4 - j, :] * x_scr[hist - j * nb:hist - j * nb + rows, :]
    a, b = _lru_gates(xc, vec, bias_ref[...], wg_ref[...])
    a_scr[...] = a
    b_scr[...] = b

    def step(i, h):
        rw = pl.ds(pl.multiple_of(i * nb, nb), nb)
        h = a_scr[rw, :] * h + b_scr[rw, :]
        b_scr[rw, :] = h
        return h

    h = lax.fori_loop(0, rows // nb, step, h_scr[...], unroll=8)
    h_scr[...] = h
    h_ref[...] = h
    tail = x_scr[rows:rows + hist, :]
    conv_ref[...] = tail
    x_scr[0:hist, :] = tail
    y = _rms(b_scr[...] * _gelu_tanh(gate_br), vec[6:7, :])
    for j, slab in enumerate(_lane_slabs(y)):
        y_ref[j] = slab


def _lru_tm(cols_b, nb, seq, vec, bias, wgates):
    rows = SCAN_STEPS * nb
    hist = (CONV_WIDTH - 1) * nb
    slab = lambda t: (0, t, 0)
    fixed = lambda t: (0, 0)
    return pl.pallas_call(
        functools.partial(_lru_tm_kernel, nb=nb),
        grid=(seq // SCAN_STEPS,),
        in_specs=[pl.BlockSpec((2 * B_WIDTH // LANES, rows, LANES), slab), _const_spec((SUBLANES, B_WIDTH)),
                  _const_spec((1, 2 * B_WIDTH)), _const_spec((B_WIDTH, 2 * B_WIDTH))],
        out_specs=[pl.BlockSpec((B_WIDTH // LANES, rows, LANES), slab), pl.BlockSpec((hist, B_WIDTH), fixed),
                   pl.BlockSpec((nb, B_WIDTH), fixed)],
        out_shape=[jax.ShapeDtypeStruct((B_WIDTH // LANES, nb * seq, LANES), F32),
                   jax.ShapeDtypeStruct((hist, B_WIDTH), F32),
                   jax.ShapeDtypeStruct((nb, B_WIDTH), F32)],
        scratch_shapes=[pltpu.VMEM((hist + rows, B_WIDTH), F32), pltpu.VMEM((rows, B_WIDTH), F32),
                        pltpu.VMEM((rows, B_WIDTH), F32), pltpu.VMEM((nb, B_WIDTH), F32)],
        compiler_params=_params(1),
        name="lru_tm",
    )(cols_b, vec, bias, wgates)


def _lru_sample_kernel(cols_ref, conv0_ref, h0_ref, vec_ref, bias_ref, wg_ref, y_ref, conv_ref, h_ref, *, steps):
    rows = cols_ref.shape[0]
    nseq = rows // steps
    gate_br = cols_ref[:, 0:B_WIDTH]
    xb = cols_ref[:, B_WIDTH:2 * B_WIDTH]
    tloc = _row_iota(xb.shape) & (steps - 1)
    h0 = jnp.broadcast_to(h0_ref[...], (nseq, steps, B_WIDTH)).reshape(rows, B_WIDTH)
    y, hs = _lru_body(gate_br, xb, conv0_ref[...], h0, tloc, steps, vec_ref[...], bias_ref[...], wg_ref[...])
    y_ref[...] = y
    conv_ref[...] = xb
    h_ref[...] = hs


def _lru_sample(cols_b, conv0_rows, h0, nb, steps, vec, bias, wgates):
    bb = SAMPLE_SEQS
    rows = bb * steps
    blk = lambda i: (i, 0)
    return pl.pallas_call(
        functools.partial(_lru_sample_kernel, steps=steps),
        grid=(nb // bb,),
        in_specs=[pl.BlockSpec((rows, 2 * B_WIDTH), blk), pl.BlockSpec((rows, B_WIDTH), blk),
                  pl.BlockSpec((bb, 1, B_WIDTH), lambda i: (i, 0, 0)),
                  _const_spec((SUBLANES, B_WIDTH)), _const_spec((1, 2 * B_WIDTH)),
                  _const_spec((B_WIDTH, 2 * B_WIDTH))],
        out_specs=[pl.BlockSpec((rows, B_WIDTH), blk)] * 3,
        out_shape=[jax.ShapeDtypeStruct((nb * steps, B_WIDTH), F32)] * 3,
        compiler_params=_params(1),
        name="lru_sample",
    )(cols_b, conv0_rows, h0, vec, bias, wgates)


def _s5_drive(u, lam, bbd):
    lr = lam[0:1, :]
    li = lam[1:2, :]
    dt = jnp.exp(lam[2:3, :])
    mag = jnp.exp(lr * dt)
    ar = mag * jnp.cos(li * dt)
    ai = mag * jnp.sin(li * dt)
    den = lr * lr + li * li
    cr = ((ar - 1.0) * lr + ai * li) / den
    ci = (ai * lr - (ar - 1.0) * li) / den
    ub = _bf(u)
    pb = _dot(ub, bbd[:, 0:S5_W])
    qb = _dot(ub, bbd[:, S5_W:2 * S5_W])
    return ar, ai, pb * cr - qb * ci, pb * ci + qb * cr


def _s5_readout(hr, hi, u, vec, cbd, wglu):
    s5_d = vec[0:1, :]
    b_glu = vec[1:2, :]
    g_out = vec[2:3, :]
    y = _dot(_bf(hr), cbd[0:S5_W, :]) + _dot(_bf(hi), cbd[S5_W:2 * S5_W, :]) + s5_d * u
    z = _gelu_tanh(y)
    out = z * _sigmoid(_dot(_bf(z), wglu) + b_glu)
    return _rms(out, g_out)


def _s5_body(u, h0r, h0i, tloc, tlen, lam, vec, bbd, cbd, wglu):
    ar, ai, hr, hi = _s5_drive(u, lam, bbd)
    first = tloc == 0
    hr = hr + jnp.where(first, ar * h0r - ai * h0i, 0.0)
    hi = hi + jnp.where(first, ar * h0i + ai * h0r, 0.0)
    pr, pi = ar, ai
    d = 1
    while d < tlen:
        keep = tloc >= d
        sr = jnp.where(keep, _shift_rows(hr, d), 0.0)
        si = jnp.where(keep, _shift_rows(hi, d), 0.0)
        hr, hi = hr + pr * sr - pi * si, hi + pr * si + pi * sr
        pr, pi = pr * pr - pi * pi, 2.0 * pr * pi
        d *= 2
    return _s5_readout(hr, hi, u, vec, cbd, wglu), hr, hi


def _s5_tm_kernel(u_ref, lam_ref, vec_ref, bbd_ref, cbd_ref, wglu_ref, y_ref, hr_ref, hi_ref,
                  re_scr, im_scr, sr_scr, si_scr, *, nb):
    rows = re_scr.shape[0]

    @pl.when(pl.program_id(0) == 0)
    def _():
        sr_scr[...] = jnp.zeros_like(sr_scr)
        si_scr[...] = jnp.zeros_like(si_scr)

    u = jnp.concatenate([u_ref[j] for j in range(C_WIDTH // LANES)], axis=1)
    ar, ai, dr, di = _s5_drive(u, lam_ref[...], bbd_ref[...])
    re_scr[...] = dr
    im_scr[...] = di
    ar = jnp.broadcast_to(ar, (nb, S5_W))
    ai = jnp.broadcast_to(ai, (nb, S5_W))

    def step(i, carry):
        hr, hi = carry
        rw = pl.ds(pl.multiple_of(i * nb, nb), nb)
        nr = ar * hr - ai * hi + re_scr[rw, :]
        ni = ar * hi + ai * hr + im_scr[rw, :]
        re_scr[rw, :] = nr
        im_scr[rw, :] = ni
        return nr, ni

    hr, hi = lax.fori_loop(0, rows // nb, step, (sr_scr[...], si_scr[...]), unroll=4)
    sr_scr[...] = hr
    si_scr[...] = hi
    hr_ref[...] = hr
    hi_ref[...] = hi
    y = _s5_readout(re_scr[...], im_scr[...], u, vec_ref[...], cbd_ref[...], wglu_ref[...])
    for j, slab in enumerate(_lane_slabs(y)):
        y_ref[j] = slab


def _s5_tm(cols_c, nb, seq, lam, vec, bbd, cbd, wglu):
    rows = SCAN_STEPS * nb
    slab = lambda t: (0, t, 0)
    fixed = lambda t: (0, 0)
    n_slabs = C_WIDTH // LANES
    return pl.pallas_call(
        functools.partial(_s5_tm_kernel, nb=nb),
        grid=(seq // SCAN_STEPS,),
        in_specs=[pl.BlockSpec((n_slabs, rows, LANES), slab), _const_spec((SUBLANES, S5_W)),
                  _const_spec((SUBLANES, C_WIDTH)), _const_spec((C_WIDTH, 2 * S5_W)),
                  _const_spec((2 * S5_W, C_WIDTH)), _const_spec((C_WIDTH, C_WIDTH))],
        out_specs=[pl.BlockSpec((n_slabs, rows, LANES), slab), pl.BlockSpec((nb, S5_W), fixed),
                   pl.BlockSpec((nb, S5_W), fixed)],
        out_shape=[jax.ShapeDtypeStruct((n_slabs, nb * seq, LANES), F32),
                   jax.ShapeDtypeStruct((nb, S5_W), F32), jax.ShapeDtypeStruct((nb, S5_W), F32)],
        scratch_shapes=[pltpu.VMEM((rows, S5_W), F32), pltpu.VMEM((rows, S5_W), F32),
                        pltpu.VMEM((nb, S5_W), F32), pltpu.VMEM((nb, S5_W), F32)],
        compiler_params=_params(1),
        name="s5_tm",
    )(cols_c, lam, vec, bbd, cbd, wglu)


def _s5_sample_kernel(u_ref, h0r_ref, h0i_ref, lam_ref, vec_ref, bbd_ref, cbd_ref, wglu_ref,
                      y_ref, hr_ref, hi_ref, *, steps):
    rows = u_ref.shape[0]
    nseq = rows // steps
    tloc = _row_iota((rows, S5_W)) & (steps - 1)
    h0r = jnp.broadcast_to(h0r_ref[...], (nseq, steps, S5_W)).reshape(rows, S5_W)
    h0i = jnp.broadcast_to(h0i_ref[...], (nseq, steps, S5_W)).reshape(rows, S5_W)
    y, hr, hi = _s5_body(u_ref[...], h0r, h0i, tloc, steps, lam_ref[...], vec_ref[...], bbd_ref[...],
                         cbd_ref[...], wglu_ref[...])
    y_ref[...] = y
    hr_ref[...] = hr
    hi_ref[...] = hi


def _s5_sample(cols_c, h0r, h0i, nb, steps, lam, vec, bbd, cbd, wglu):
    bb = SAMPLE_SEQS
    rows = bb * steps
    blk = lambda i: (i, 0)
    st = pl.BlockSpec((bb, 1, S5_W), lambda i: (i, 0, 0))
    return pl.pallas_call(
        functools.partial(_s5_sample_kernel, steps=steps),
        grid=(nb // bb,),
        in_specs=[pl.BlockSpec((rows, C_WIDTH), blk), st, st,
                  _const_spec((SUBLANES, S5_W)), _const_spec((SUBLANES, C_WIDTH)),
                  _const_spec((C_WIDTH, 2 * S5_W)), _const_spec((2 * S5_W, C_WIDTH)),
                  _const_spec((C_WIDTH, C_WIDTH))],
        out_specs=[pl.BlockSpec((rows, C_WIDTH), blk), pl.BlockSpec((rows, S5_W), blk),
                   pl.BlockSpec((rows, S5_W), blk)],
        out_shape=[jax.ShapeDtypeStruct((nb * steps, C_WIDTH), F32),
                   jax.ShapeDtypeStruct((nb * steps, S5_W), F32),
                   jax.ShapeDtypeStruct((nb * steps, S5_W), F32)],
        compiler_params=_params(1),
        name="s5_sample",
    )(cols_c, h0r, h0i, lam, vec, bbd, cbd, wglu)


def _pad_rows(rows, width):
    m = jnp.stack([r.reshape(width) for r in rows], axis=0)
    return jnp.pad(m, ((0, SUBLANES - m.shape[0]), (0, 0)))


def _layer_weights(l, w):
    eye_b = jnp.eye(B_BLOCKS, dtype=F32)
    eye_g = jnp.eye(S5_GROUPS, dtype=F32)
    bd4 = lambda m: jnp.einsum("nde,nm->ndme", m, eye_b).reshape(B_WIDTH, B_WIDTH)
    zeros_lora = jnp.zeros((LANES - 64, A_WIDTH), F32)
    b_in = lambda m: jnp.einsum("gpc,gh->gchp", m, eye_g).reshape(C_WIDTH, S5_W)
    c_out = lambda m: jnp.einsum("gcp,gh->gphc", m, eye_g).reshape(S5_W, C_WIDTH)
    return dict(
        g_mix=w["g_mix"][l][None],
        w_in=_bf(w["w_in"][l]),
        mu=w["mu_a"][l][None],
        a_vec=_pad_rows([w["w0"][l], w["a0"][l], w["k_k"][l], w["k_a"][l], w["r_k"][l], w["lnx_w"][l],
                         w["lnx_b"][l]], A_WIDTH),
        wdec=_bf(jnp.concatenate([w["w_dec2"][l], zeros_lora], axis=0)),
        wa=_bf(jnp.concatenate([zeros_lora, w["w_a2"][l]], axis=0)),
        wg=_bf(w["w_g2"][l]),
        b_vec=_pad_rows([w["conv_w"][l][0], w["conv_w"][l][1], w["conv_w"][l][2], w["conv_w"][l][3],
                         w["conv_b"][l], w["lru_lambda"][l], w["g_out_b"][l]], B_WIDTH),
        b_bias=jnp.concatenate([w["b_rg"][l], w["b_ig"][l]])[None],
        b_wgates=_bf(jnp.concatenate([bd4(w["w_rg"][l]), bd4(w["w_ig"][l])], axis=1)),
        c_lam=_pad_rows([w["s5_lam_re"][l], w["s5_lam_im"][l],
                         jnp.repeat(w["s5_log_dt"][l], S5_STATE)], S5_W),
        c_vec=_pad_rows([w["s5_d"][l], w["b_glu"][l], w["g_out_c"][l]], C_WIDTH),
        c_bbd=_bf(jnp.concatenate([b_in(w["s5_b_re"][l]), b_in(w["s5_b_im"][l])], axis=1)),
        c_cbd=_bf(jnp.concatenate([c_out(w["s5_c_re"][l]), -c_out(w["s5_c_im"][l])], axis=0)),
        c_wglu=_bf(w["w_glu"][l]),
        w_out=_bf(w["w_out"][l]),
        p_vec=_pad_rows([w["g_ffn"][l], w["g_ple"][l], w["g_final"]], D_MODEL),
        w_up=_bf(w["w_ffn_up"][l]),
        w_down=_bf(w["w_ffn_down"][l]),
        w_ple=_bf(w["w_ple"][l]),
        w_gate=_bf(w["w_ple_gate"][l]),
    )


def _post_layer(h, ya, yb, yc, p, lw, final, nb_tm=0):
    return _post(h, ya, yb, yc, p, lw["w_out"], lw["p_vec"], lw["w_up"], lw["w_down"], lw["w_ple"],
                 lw["w_gate"], final, nb_tm)


def _run_prompt(x, p, lws, ones_bd):
    nb, seq, _ = x.shape
    depth = len(lws)
    h = x.reshape(nb * seq, D_MODEL)
    outs = []
    for l, lw in enumerate(lws):
        cols_a, cols_b, cols_c = _proj_in_tm(h, lw["g_mix"], lw["w_in"], nb, seq)
        ya, wkv, last = _rwkv_prompt(cols_a, nb, seq, lw["mu"], lw["a_vec"], lw["wdec"], lw["wa"], lw["wg"],
                                     ones_bd)
        yb, conv, lru = _lru_tm(cols_b, nb, seq, lw["b_vec"], lw["b_bias"], lw["b_wgates"])
        yc, s5r, s5i = _s5_tm(cols_c, nb, seq, lw["c_lam"], lw["c_vec"], lw["c_bbd"], lw["c_cbd"],
                              lw["c_wglu"])
        h = _post_layer(h, ya, yb, yc, p[l].reshape(nb * seq, PLE_DIM), lw, l == depth - 1, nb_tm=nb)
        conv = jnp.swapaxes(conv.reshape(CONV_WIDTH - 1, nb, B_WIDTH), 0, 1)
        outs.append((last[:, SUBLANES - 1], wkv, conv, lru,
                     s5r.reshape(nb, S5_GROUPS, S5_STATE), s5i.reshape(nb, S5_GROUPS, S5_STATE)))
    states = tuple(jnp.stack([o[j] for o in outs], axis=0) for j in range(6))
    return h.reshape(nb, seq, D_MODEL), states


def _run_sample(x, p, states, lws, ones_bd):
    nb, steps, _ = x.shape
    depth = len(lws)
    st_shift, st_wkv, st_conv, st_lru, st_s5r, st_s5i = states
    h = x.reshape(nb * steps, D_MODEL)
    outs = []
    for l, lw in enumerate(lws):
        cols_a, cols_b, cols_c = _proj_in(h, lw["g_mix"], lw["w_in"])
        ya, wkv = _rwkv_sample(cols_a, st_shift[l][:, None, :], st_wkv[l], nb, steps, lw["mu"], lw["a_vec"],
                               lw["wdec"], lw["wa"], lw["wg"], ones_bd)
        conv0_rows = jnp.pad(st_conv[l], ((0, 0), (0, steps - 3), (0, 0))).reshape(nb * steps, B_WIDTH)
        yb, conv, lru = _lru_sample(cols_b, conv0_rows, st_lru[l][:, None, :], nb, steps, lw["b_vec"],
                                    lw["b_bias"], lw["b_wgates"])
        yc, s5r, s5i = _s5_sample(cols_c, st_s5r[l].reshape(nb, 1, S5_W), st_s5i[l].reshape(nb, 1, S5_W),
                                  nb, steps, lw["c_lam"], lw["c_vec"], lw["c_bbd"], lw["c_cbd"], lw["c_wglu"])
        h = _post_layer(h, ya, yb, yc, p[l].reshape(nb * steps, PLE_DIM), lw, l == depth - 1)
        last = lambda z, width: z.reshape(nb, steps, width)[:, steps - 1]
        outs.append((last(cols_a, A_COLS), wkv, conv.reshape(nb, steps, B_WIDTH)[:, steps - 3:],
                     last(lru, B_WIDTH),
                     last(s5r, S5_W).reshape(nb, S5_GROUPS, S5_STATE),
                     last(s5i, S5_W).reshape(nb, S5_GROUPS, S5_STATE)))
    new_states = tuple(jnp.stack([o[j] for o in outs], axis=0) for j in range(6))
    return h.reshape(nb, steps, D_MODEL), new_states


def kernel(x_prompt, x_sample, p_prompt, p_sample, state_shift, state_wkv, state_conv, state_lru, state_s5_re, state_s5_im, g_mix, w_in, mu_a, w0, w_dec2, a0, w_a2, w_g2, k_k, k_a, r_k, lnx_w, lnx_b, conv_w, conv_b, w_rg, b_rg, w_ig, b_ig, lru_lambda, g_out_b, s5_lam_re, s5_lam_im, s5_log_dt, s5_b_re, s5_b_im, s5_c_re, s5_c_im, s5_d, w_glu, b_glu, g_out_c, w_out, g_ffn, w_ffn_up, w_ffn_down, g_ple, w_ple, w_ple_gate, g_final):
    w = dict(g_mix=g_mix, w_in=w_in, mu_a=mu_a, w0=w0, w_dec2=w_dec2, a0=a0, w_a2=w_a2, w_g2=w_g2, k_k=k_k,
             k_a=k_a, r_k=r_k, lnx_w=lnx_w, lnx_b=lnx_b, conv_w=conv_w, conv_b=conv_b, w_rg=w_rg, b_rg=b_rg,
             w_ig=w_ig, b_ig=b_ig, lru_lambda=lru_lambda, g_out_b=g_out_b, s5_lam_re=s5_lam_re,
             s5_lam_im=s5_lam_im, s5_log_dt=s5_log_dt, s5_b_re=s5_b_re, s5_b_im=s5_b_im, s5_c_re=s5_c_re,
             s5_c_im=s5_c_im, s5_d=s5_d, w_glu=w_glu, b_glu=b_glu, g_out_c=g_out_c, w_out=w_out, g_ffn=g_ffn,
             w_ffn_up=w_ffn_up, w_ffn_down=w_ffn_down, g_ple=g_ple, w_ple=w_ple, w_ple_gate=w_ple_gate,
             g_final=g_final)
    depth = g_mix.shape[0]
    lws = [_layer_weights(l, w) for l in range(depth)]
    ones_bd = _bf(jnp.kron(jnp.eye(A_HEADS, dtype=F32), jnp.ones((A_HEAD_DIM, A_HEAD_DIM), F32)))
    y_prompt, new_p = _run_prompt(x_prompt, p_prompt, lws, ones_bd)
    y_sample, new_s = _run_sample(x_sample, p_sample,
                                  (state_shift, state_wkv, state_conv, state_lru, state_s5_re, state_s5_im),
                                  lws, ones_bd)
    return (y_prompt, y_sample) + new_p + new_s
```

```python
import functools
import math

import jax
import jax.numpy as jnp
from jax import lax
from jax.experimental import pallas as pl
from jax.experimental.pallas import tpu as pltpu

F32 = jnp.float32
BF16 = jnp.bfloat16

D_MODEL = 1024
A_WIDTH = 512
A_HEADS = 8
A_HEAD_DIM = 64
A_COLS = 1792
B_WIDTH = 256
B_BLOCKS = 4
CONV_WIDTH = 4
C_WIDTH = 256
S5_GROUPS = 16
S5_GROUP_CH = 16
S5_STATE = 64
S5_W = S5_GROUPS * S5_STATE
IN_COLS = A_COLS + 2 * B_WIDTH + C_WIDTH
D_FF = 2816
PLE_DIM = 256
LRU_C = 8.0
RMS_EPS = 1e-6
GN_EPS = 64e-5

LANES = 128
SUBLANES = 8
VMEM_LIMIT_BYTES = 56 * 1024 * 1024

TOKEN_TILE = 512
FFN_CHUNK = 256
RWKV_CHUNK = 64
RWKV_BLOCK = 256
SCAN_STEPS = 64
SAMPLE_SEQS = 8


def _const_spec(shape):
    nd = len(shape)
    return pl.BlockSpec(shape, lambda *_: (0,) * nd, pipeline_mode=pl.Buffered(1))


def _params(n_axes):
    return pltpu.CompilerParams(dimension_semantics=("arbitrary",) * n_axes,
                                vmem_limit_bytes=VMEM_LIMIT_BYTES)


def _dot(a, b):
    return jnp.dot(a, b, preferred_element_type=F32)


def _dot_nt(a, b):
    return lax.dot_general(a, b, (((1,), (1,)), ((), ())), preferred_element_type=F32)


def _dot_tn(a, b):
    return lax.dot_general(a, b, (((0,), (0,)), ((), ())), preferred_element_type=F32)


def _bf(x):
    return x.astype(BF16)


def _rms(x, g):
    inv = lax.rsqrt(jnp.mean(x * x, axis=-1, keepdims=True) + RMS_EPS)
    return x * inv * g


def _sigmoid(x):
    return 1.0 / (1.0 + jnp.exp(-x))


def _softplus(x):
    return jnp.maximum(x, 0.0) + jnp.log(1.0 + jnp.exp(-jnp.abs(x)))


def _gelu_tanh(x):
    c = math.sqrt(2.0 / math.pi)
    return 0.5 * x * (1.0 + jnp.tanh(c * (x + 0.044715 * (x * x * x))))


def _row_iota(shape):
    return lax.broadcasted_iota(jnp.int32, shape, 0)


def _shift_rows(x, d):
    return pltpu.roll(x, d, axis=0)


def _proj_in_kernel(h_ref, g_ref, w_ref, a_ref, b_ref, c_ref):
    xn = _bf(_rms(h_ref[...], g_ref[...]))
    a_ref[...] = _dot(xn, w_ref[:, 0:A_COLS])
    b_ref[...] = _dot(xn, w_ref[:, A_COLS:A_COLS + 2 * B_WIDTH])
    c_ref[...] = _dot(xn, w_ref[:, A_COLS + 2 * B_WIDTH:IN_COLS])


def _proj_in(h, g, w_bf):
    n = h.shape[0]
    tm = TOKEN_TILE
    row = lambda i: (i, 0)
    return pl.pallas_call(
        _proj_in_kernel,
        grid=(n // tm,),
        in_specs=[pl.BlockSpec((tm, D_MODEL), row), _const_spec((1, D_MODEL)),
                  _const_spec((D_MODEL, IN_COLS))],
        out_specs=[pl.BlockSpec((tm, A_COLS), row), pl.BlockSpec((tm, 2 * B_WIDTH), row),
                   pl.BlockSpec((tm, C_WIDTH), row)],
        out_shape=[jax.ShapeDtypeStruct((n, A_COLS), F32), jax.ShapeDtypeStruct((n, 2 * B_WIDTH), F32),
                   jax.ShapeDtypeStruct((n, C_WIDTH), F32)],
        compiler_params=_params(1),
        name="proj_in",
    )(h, g, w_bf)


def _lane_slabs(x):
    return [x[:, j * LANES:(j + 1) * LANES] for j in range(x.shape[1] // LANES)]


def _proj_in_tm_kernel(h_ref, g_ref, w_ref, a_ref, b_ref, c_ref, *, nb):
    b = pl.program_id(1)
    tm = h_ref.shape[0]
    xn = _bf(_rms(h_ref[...], g_ref[...]))
    a_ref[...] = _dot(xn, w_ref[:, 0:A_COLS])
    rows_of_b = pl.ds(b, tm, stride=nb)
    for j, slab in enumerate(_lane_slabs(_dot(xn, w_ref[:, A_COLS:A_COLS + 2 * B_WIDTH]))):
        b_ref[j, rows_of_b, :] = slab
    for j, slab in enumerate(_lane_slabs(_dot(xn, w_ref[:, A_COLS + 2 * B_WIDTH:IN_COLS]))):
        c_ref[j, rows_of_b, :] = slab


def _proj_in_tm(h, g, w_bf, nb, seq):
    tm = TOKEN_TILE
    nt = seq // tm
    row = lambda i, b: (b * nt + i, 0)
    slab = lambda i, b: (0, i, 0)
    nb_slabs, nc_slabs = 2 * B_WIDTH // LANES, C_WIDTH // LANES
    return pl.pallas_call(
        functools.partial(_proj_in_tm_kernel, nb=nb),
        grid=(nt, nb),
        in_specs=[pl.BlockSpec((tm, D_MODEL), row), _const_spec((1, D_MODEL)),
                  _const_spec((D_MODEL, IN_COLS))],
        out_specs=[pl.BlockSpec((tm, A_COLS), row), pl.BlockSpec((nb_slabs, tm * nb, LANES), slab),
                   pl.BlockSpec((nc_slabs, tm * nb, LANES), slab)],
        out_shape=[jax.ShapeDtypeStruct((nb * seq, A_COLS), F32),
                   jax.ShapeDtypeStruct((nb_slabs, nb * seq, LANES), F32),
                   jax.ShapeDtypeStruct((nc_slabs, nb * seq, LANES), F32)],
        compiler_params=_params(2),
        name="proj_in_tm",
    )(h, g, w_bf)


def _post_kernel(h_ref, ya_ref, yb_ref, yc_ref, p_ref, wo_ref, vec_ref, wup_ref, wdn_ref, wple_ref,
                 wgate_ref, o_ref, act_scr, *, final, nb_tm):
    g_ffn = vec_ref[0:1, :]
    g_ple = vec_ref[1:2, :]
    g_final = vec_ref[2:3, :]
    if nb_tm:
        rows_of_b = pl.ds(pl.program_id(1), h_ref.shape[0], stride=nb_tm)
        yb = jnp.concatenate([yb_ref[j, rows_of_b, :] for j in range(B_WIDTH // LANES)], axis=1)
        yc = jnp.concatenate([yc_ref[j, rows_of_b, :] for j in range(C_WIDTH // LANES)], axis=1)
    else:
        yb, yc = yb_ref[...], yc_ref[...]
    h1 = (h_ref[...] + _dot(_bf(ya_ref[...]), wo_ref[0:A_WIDTH, :])
          + _dot(_bf(yb), wo_ref[A_WIDTH:A_WIDTH + B_WIDTH, :])
          + _dot(_bf(yc), wo_ref[A_WIDTH + B_WIDTH:D_MODEL, :]))
    xf = _bf(_rms(h1, g_ffn))
    for c0 in range(0, D_FF, FFN_CHUNK):
        gate = _dot(xf, wup_ref[:, c0:c0 + FFN_CHUNK])
        up = _dot(xf, wup_ref[:, D_FF + c0:D_FF + c0 + FFN_CHUNK])
        act_scr[:, c0:c0 + FFN_CHUNK] = _bf(gate * _sigmoid(gate) * up)
    h2 = h1 + _dot(act_scr[...], wdn_ref[...])
    ple = _dot(_bf(p_ref[...]), wple_ref[...])
    gate = _sigmoid(_dot(_bf(_rms(h2, g_ple)), wgate_ref[...]))
    h3 = h2 + ple * gate
    if final:
        h3 = _rms(h3, g_final)
    o_ref[...] = h3


def _post(h, ya, yb, yc, p, wo, vec, wup, wdn, wple, wgate, final, nb_tm=0):
    n = h.shape[0]
    tm = TOKEN_TILE
    if nb_tm:
        nt = n // (nb_tm * tm)
        grid = (nt, nb_tm)
        row = lambda i, b: (b * nt + i, 0)
        slab = lambda i, b: (0, i, 0)
        yb_spec = pl.BlockSpec((B_WIDTH // LANES, tm * nb_tm, LANES), slab, pipeline_mode=pl.Buffered(1))
        yc_spec = pl.BlockSpec((C_WIDTH // LANES, tm * nb_tm, LANES), slab, pipeline_mode=pl.Buffered(1))
    else:
        grid = (n // tm,)
        row = lambda i: (i, 0)
        yb_spec = pl.BlockSpec((tm, B_WIDTH), row)
        yc_spec = pl.BlockSpec((tm, C_WIDTH), row)
    return pl.pallas_call(
        functools.partial(_post_kernel, final=final, nb_tm=nb_tm),
        grid=grid,
        in_specs=[pl.BlockSpec((tm, D_MODEL), row), pl.BlockSpec((tm, A_WIDTH), row),
                  yb_spec, yc_spec,
                  pl.BlockSpec((tm, PLE_DIM), row),
                  _const_spec((D_MODEL, D_MODEL)), _const_spec((SUBLANES, D_MODEL)),
                  _const_spec((D_MODEL, 2 * D_FF)), _const_spec((D_FF, D_MODEL)),
                  _const_spec((PLE_DIM, D_MODEL)), _const_spec((D_MODEL, D_MODEL))],
        out_specs=pl.BlockSpec((tm, D_MODEL), row),
        out_shape=jax.ShapeDtypeStruct((n, D_MODEL), F32),
        scratch_shapes=[pltpu.VMEM((tm, D_FF), BF16)],
        compiler_params=_params(len(grid)),
        name="post",
    )(h, ya, yb, yc, p, wo, vec, wup, wdn, wple, wgate)


def _head_sum(x):
    first = lax.broadcasted_iota(jnp.int32, (x.shape[0], LANES), 1) < A_HEAD_DIM
    outs = []
    for xp in _lane_slabs(x):
        s0 = jnp.sum(jnp.where(first, xp, 0.0), axis=-1, keepdims=True)
        s1 = jnp.sum(jnp.where(first, 0.0, xp), axis=-1, keepdims=True)
        outs.append(jnp.where(first, s0, s1))
    return jnp.concatenate(outs, axis=1)


def _rwkv_prep(cols, prev, mu, vec, wdec, wa, wg, chunk):
    rows = cols.shape[0]
    xs = cols + (prev - cols) * mu
    r = xs[:, 0:A_WIDTH]
    k = xs[:, A_WIDTH:2 * A_WIDTH]
    v = xs[:, 2 * A_WIDTH:3 * A_WIDTH]
    xwa = xs[:, 3 * A_WIDTH:3 * A_WIDTH + LANES]
    xg = xs[:, 3 * A_WIDTH + LANES:A_COLS]
    w0, a0, k_k, k_a, r_k = (vec[i:i + 1, :] for i in range(5))
    z = w0 + _dot(_bf(jnp.tanh(xwa)), wdec)
    log_decay = -jnp.exp(-_softplus(-z) - 0.5)
    a = _sigmoid(a0 + _dot(_bf(xwa), wa))
    g = _dot(_bf(_sigmoid(xg)), wg)
    kk_raw = k * k_k
    kk = kk_raw * lax.rsqrt(jnp.maximum(_head_sum(kk_raw * kk_raw), 1e-24))
    k_mod = k * (1.0 + (a - 1.0) * k_a)
    bonus = _head_sum(r * k_mod * r_k) * v
    ri = _row_iota((rows, rows))
    ci = lax.broadcasted_iota(jnp.int32, (rows, rows), 1)
    same_chunk = (ci & (-chunk)) == (ri & (-chunk))
    tri = _bf(jnp.where((ci <= ri) & same_chunk, 1.0, 0.0))
    h1 = _bf(log_decay)
    r1 = log_decay - h1.astype(F32)
    h2 = _bf(r1)
    h3 = _bf(r1 - h2.astype(F32))
    cum = _dot(tri, h1) + _dot(tri, h2) + _dot(tri, h3)
    gam = jnp.exp(cum)
    ginv = jnp.exp(-cum)
    gprev = jnp.exp(cum - log_decay)
    return dict(rt=r * gam, kap=kk * gprev, bet=kk * a * ginv, kt=k_mod * ginv, v=v, gam=gam,
                bonus=bonus, g=g)


def _rwkv_local(ops):
    c = ops[0]["kap"].shape[0]
    c2 = 2 * c
    ri = _row_iota((c, c2))
    ci = lax.broadcasted_iota(jnp.int32, (c, c2), 1) & (c - 1)
    strict = ci < ri
    incl = ci <= ri
    eye = jnp.where(ri == ci, 1.0, 0.0)
    same_head = (_row_iota((c2, c2)) & c) == (lax.broadcasted_iota(jnp.int32, (c2, c2), 1) & c)
    merged = c2 % LANES == 0

    def blockdiag(m):
        return _bf(jnp.where(same_head, jnp.concatenate([m, m], axis=0), 0.0))

    st = []
    for o in ops:
        s = {k + "2": _stack_heads(o[k]) for k in ("kap", "bet", "kt", "v")}
        s["rt"] = o["rt"]
        lhs = _bf(jnp.concatenate([o["kap"], o["rt"]], axis=0))
        if merged:
            g = _dot_nt(lhs, jnp.concatenate([s["bet2"], s["kt2"]], axis=0))
            a_b, a_k, a_rb, a_rk = g[0:c, 0:c2], g[0:c, c2:2 * c2], g[c:c2, 0:c2], g[c:c2, c2:2 * c2]
        else:
            gb, gk = _dot_nt(lhs, s["bet2"]), _dot_nt(lhs, s["kt2"])
            a_b, a_k, a_rb, a_rk = gb[0:c], gk[0:c], gb[c:c2], gk[c:c2]
        s["a_b"] = jnp.where(strict, a_b, 0.0)
        s["a_k"] = _bf(jnp.where(strict, a_k, 0.0))
        s["a_rb"] = _bf(jnp.where(incl, a_rb, 0.0))
        s["a_rk"] = _bf(jnp.where(incl, a_rk, 0.0))
        st.append(s)
    for s in st:
        s["t"] = eye - s["a_b"]
        s["lp"] = s["a_b"]
        s["akv"] = _dot(s["a_k"], s["v2"])
    n = 2
    while n < c:
        for s in st:
            s["lp"] = _dot(_bf(s["lp"]), blockdiag(s["lp"]))
        for s in st:
            s["t"] = s["t"] + _dot(_bf(s["t"]), blockdiag(s["lp"]))
        n *= 2
    for s in st:
        wu = _dot(_bf(s["t"]), jnp.concatenate([s["kap2"], _stack_heads(s["akv"])], axis=1))
        s["w"] = -wu[:, 0:LANES]
        s["u0"] = -wu[:, LANES:2 * LANES]
    return st


def _stack_heads(x):
    first = lax.broadcasted_iota(jnp.int32, x.shape, 1) < A_HEAD_DIM
    return _bf(jnp.concatenate([jnp.where(first, x, 0.0), jnp.where(first, 0.0, x)], axis=0))


def _rwkv_state(st, states, g_ends):
    c = st[0]["rt"].shape[0]
    merged = (2 * c) % LANES == 0
    xs = []
    for s, state in zip(st, states):
        xs.append(_dot_nt(_bf(jnp.concatenate([s["w"], s["rt"]], axis=0)), _bf(state)))
    u2s = [_stack_heads(x[0:c] + s["u0"]) for x, s in zip(xs, st)]
    ys, new_states = [], []
    for s, x, u2 in zip(st, xs, u2s):
        if merged:
            y = x[c:2 * c] + _dot(jnp.concatenate([s["a_rb"], s["a_rk"]], axis=1),
                                  jnp.concatenate([u2, s["v2"]], axis=0))
        else:
            y = x[c:2 * c] + _dot(s["a_rb"], u2) + _dot(s["a_rk"], s["v2"])
        ys.append(y)
    for s, u2, state, g_end in zip(st, u2s, states, g_ends):
        ds = _dot_tn(jnp.concatenate([u2, s["v2"]], axis=0), jnp.concatenate([s["bet2"], s["kt2"]], axis=0))
        new_states.append((state + ds) * g_end)
    return ys, new_states


def _rwkv_finish(y, bonus, g, vec):
    lnx_w = vec[5:6, :]
    lnx_b = vec[6:7, :]
    inv_n = 1.0 / A_HEAD_DIM
    mu = _head_sum(y) * inv_n
    d = y - mu
    var = _head_sum(d * d) * inv_n
    yn = d * lax.rsqrt(var + GN_EPS) * lnx_w + lnx_b
    return (yn + bonus) * g


def _pair_state(s_ref, idx, p):
    z = jnp.zeros((A_HEAD_DIM, A_HEAD_DIM), F32)
    top = jnp.concatenate([s_ref[idx, 2 * p], z], axis=1)
    bot = jnp.concatenate([z, s_ref[idx, 2 * p + 1]], axis=1)
    return jnp.concatenate([top, bot], axis=0)


def _rwkv_prompt_kernel(cols_ref, mu_ref, vec_ref, wdec_ref, wa_ref, wg_ref,
                        y_ref, wkv_ref, last_ref, s_scr, prev_scr):
    t = pl.program_id(1)
    rows = cols_ref.shape[0]
    chunk = RWKV_CHUNK

    @pl.when(t == 0)
    def _():
        s_scr[...] = jnp.zeros_like(s_scr)
        prev_scr[...] = jnp.zeros_like(prev_scr)

    cols = cols_ref[...]
    first_row = _row_iota(cols.shape) == 0
    prev = jnp.where(first_row, prev_scr[SUBLANES - 1:SUBLANES, :], _shift_rows(cols, 1))
    q = _rwkv_prep(cols, prev, mu_ref[...], vec_ref[...], wdec_ref[...], wa_ref[...], wg_ref[...], chunk)
    n_pairs = A_HEADS // 2
    lanes = [slice(p * LANES, (p + 1) * LANES) for p in range(n_pairs)]
    starts = list(range(0, rows, chunk))
    ops = [{k: q[k][c0:c0 + chunk, ln] for k in ("kap", "rt", "bet", "kt", "v")}
           for c0 in starts for ln in lanes]
    st = _rwkv_local(ops)
    states = [s_scr[p] for p in range(n_pairs)]
    y_rows = []
    for i, c0 in enumerate(starts):
        g_ends = [q["gam"][c0 + chunk - 1:c0 + chunk, ln] for ln in lanes]
        ys, states = _rwkv_state(st[i * n_pairs:(i + 1) * n_pairs], states, g_ends)
        y_rows.append(jnp.concatenate(ys, axis=1))
    for p in range(n_pairs):
        s_scr[p] = states[p]
        wkv_ref[0, 2 * p] = states[p][0:A_HEAD_DIM, 0:A_HEAD_DIM]
        wkv_ref[0, 2 * p + 1] = states[p][A_HEAD_DIM:LANES, A_HEAD_DIM:LANES]
    y = jnp.concatenate(y_rows, axis=0) if len(y_rows) > 1 else y_rows[0]
    y_ref[...] = _rwkv_finish(y, q["bonus"], q["g"], vec_ref[...])
    tail = cols[rows - SUBLANES:rows, :]
    prev_scr[...] = tail
    last_ref[0] = tail


def _rwkv_prompt(cols_a, nb, seq, mu, vec, wdec, wa, wg):
    tb = RWKV_BLOCK
    nt = seq // tb
    blk = lambda b, t: (b * nt + t, 0)
    per_b3 = lambda b, t: (b, 0, 0)
    return pl.pallas_call(
        _rwkv_prompt_kernel,
        grid=(nb, nt),
        in_specs=[pl.BlockSpec((tb, A_COLS), blk), _const_spec((1, A_COLS)), _const_spec((SUBLANES, A_WIDTH)),
                  _const_spec((LANES, A_WIDTH)), _const_spec((LANES, A_WIDTH)), _const_spec((LANES, A_WIDTH))],
        out_specs=[pl.BlockSpec((tb, A_WIDTH), blk),
                   pl.BlockSpec((1, A_HEADS, A_HEAD_DIM, A_HEAD_DIM), lambda b, t: (b, 0, 0, 0)),
                   pl.BlockSpec((1, SUBLANES, A_COLS), per_b3)],
        out_shape=[jax.ShapeDtypeStruct((nb * seq, A_WIDTH), F32),
                   jax.ShapeDtypeStruct((nb, A_HEADS, A_HEAD_DIM, A_HEAD_DIM), F32),
                   jax.ShapeDtypeStruct((nb, SUBLANES, A_COLS), F32)],
        scratch_shapes=[pltpu.VMEM((A_HEADS // 2, LANES, LANES), F32), pltpu.VMEM((SUBLANES, A_COLS), F32)],
        compiler_params=_params(2),
        name="rwkv_prompt",
    )(cols_a, mu, vec, wdec, wa, wg)


def _rwkv_sample_kernel(cols_ref, shift_ref, wkv0_ref, mu_ref, vec_ref, wdec_ref, wa_ref, wg_ref,
                        y_ref, wkv_ref, *, steps):
    rows = cols_ref.shape[0]
    nseq = rows // steps
    cols = cols_ref[...]
    tloc = _row_iota(cols.shape) & (steps - 1)
    shift0 = jnp.broadcast_to(shift_ref[...], (nseq, steps, A_COLS)).reshape(rows, A_COLS)
    prev = jnp.where(tloc == 0, shift0, _shift_rows(cols, 1))
    q = _rwkv_prep(cols, prev, mu_ref[...], vec_ref[...], wdec_ref[...], wa_ref[...], wg_ref[...], steps)
    n_pairs = A_HEADS // 2
    lanes = [slice(p * LANES, (p + 1) * LANES) for p in range(n_pairs)]
    chains = [(s, p) for s in range(nseq) for p in range(n_pairs)]
    ops = [{k: q[k][s * steps:(s + 1) * steps, lanes[p]] for k in ("kap", "rt", "bet", "kt", "v")}
           for s, p in chains]
    states = [_pair_state(wkv0_ref, s, p) for s, p in chains]
    g_ends = [q["gam"][(s + 1) * steps - 1:(s + 1) * steps, lanes[p]] for s, p in chains]
    ys, new_states = _rwkv_state(_rwkv_local(ops), states, g_ends)
    for (s, p), s_new in zip(chains, new_states):
        wkv_ref[s, 2 * p] = s_new[0:A_HEAD_DIM, 0:A_HEAD_DIM]
        wkv_ref[s, 2 * p + 1] = s_new[A_HEAD_DIM:LANES, A_HEAD_DIM:LANES]
    y = jnp.concatenate([jnp.concatenate(ys[s * n_pairs:(s + 1) * n_pairs], axis=1) for s in range(nseq)],
                        axis=0)
    y_ref[...] = _rwkv_finish(y, q["bonus"], q["g"], vec_ref[...])


def _rwkv_sample(cols_a, shift0, wkv0, nb, steps, mu, vec, wdec, wa, wg):
    bb = SAMPLE_SEQS
    rows = bb * steps
    blk = lambda i: (i, 0)
    return pl.pallas_call(
        functools.partial(_rwkv_sample_kernel, steps=steps),
        grid=(nb // bb,),
        in_specs=[pl.BlockSpec((rows, A_COLS), blk),
                  pl.BlockSpec((bb, 1, A_COLS), lambda i: (i, 0, 0)),
                  pl.BlockSpec((bb, A_HEADS, A_HEAD_DIM, A_HEAD_DIM), lambda i: (i, 0, 0, 0)),
                  _const_spec((1, A_COLS)), _const_spec((SUBLANES, A_WIDTH)),
                  _const_spec((LANES, A_WIDTH)), _const_spec((LANES, A_WIDTH)), _const_spec((LANES, A_WIDTH))],
        out_specs=[pl.BlockSpec((rows, A_WIDTH), blk),
                   pl.BlockSpec((bb, A_HEADS, A_HEAD_DIM, A_HEAD_DIM), lambda i: (i, 0, 0, 0))],
        out_shape=[jax.ShapeDtypeStruct((nb * steps, A_WIDTH), F32),
                   jax.ShapeDtypeStruct((nb, A_HEADS, A_HEAD_DIM, A_HEAD_DIM), F32)],
        compiler_params=_params(1),
        name="rwkv_sample",
    )(cols_a, shift0, wkv0, mu, vec, wdec, wa, wg)


def _lru_body(gate_br, xb, conv_prev, h0, tloc, tlen, vec, bias, wgates):
    rows = xb.shape[0]
    conv_b = vec[4:5, :]
    g_out = vec[6:7, :]
    xc = conv_b + vec[3:4, :] * xb
    for j in (1, 2, 3):
        tail = conv_prev if j == 3 else _shift_rows(conv_prev, rows - (3 - j))
        xc = xc + vec[3 - j:4 - j, :] * jnp.where(tloc >= j, _shift_rows(xb, j), tail)
    a, b = _lru_gates(xc, vec, bias, wgates)
    b = b + jnp.where(tloc == 0, a * h0, 0.0)
    d = 1
    while d < tlen:
        keep = tloc >= d
        a_s = jnp.where(keep, _shift_rows(a, d), 1.0)
        b_s = jnp.where(keep, _shift_rows(b, d), 0.0)
        b = a * b_s + b
        a = a * a_s
        d *= 2
    hs = b
    y = hs * _gelu_tanh(gate_br)
    return _rms(y, g_out), hs


def _lru_gates(xc, vec, bias, wgates):
    lam = vec[5:6, :]
    gates = _sigmoid(_dot(_bf(xc), wgates) + bias)
    gate_r = gates[:, 0:B_WIDTH]
    gate_i = gates[:, B_WIDTH:2 * B_WIDTH]
    log_a = (-LRU_C) * gate_r * _softplus(-lam)
    a = jnp.exp(log_a)
    mult = jnp.sqrt(-jnp.tanh(log_a) * (a * a + 1.0))
    return a, mult * gate_i * xc


def _lru_tm_kernel(cols_ref, vec_ref, bias_ref, wg_ref, y_ref, conv_ref, h_ref, x_scr, a_scr, b_scr, h_scr, *, nb):
    rows = a_scr.shape[0]
    hist = (CONV_WIDTH - 1) * nb

    @pl.when(pl.program_id(0) == 0)
    def _():
        x_scr[0:hist, :] = jnp.zeros((hist, B_WIDTH), F32)
        h_scr[...] = jnp.zeros_like(h_scr)

    vec = vec_ref[...]
    gate_br = jnp.concatenate([cols_ref[0], cols_ref[1]], axis=1)
    xb = jnp.concatenate([cols_ref[2], cols_ref[3]], axis=1)
    x_scr[hist:hist + rows, :] = xb
    xc = vec[4:5, :] + vec[3:4, :] * xb
    for j in range(1, CONV_WIDTH):
        xc = xc + vec[3 - j:4 - j, :] * x_scr[hist - j * nb:hist - j * nb + rows, :]
    a, b = _lru_gates(xc, vec, bias_ref[...], wg_ref[...])
    a_scr[...] = a
    b_scr[...] = b

    def step(i, h):
        rw = pl.ds(pl.multiple_of(i * nb, nb), nb)
        h = a_scr[rw, :] * h + b_scr[rw, :]
        b_scr[rw, :] = h
        return h

    h = lax.fori_loop(0, rows // nb, step, h_scr[...], unroll=8)
    h_scr[...] = h
    h_ref[...] = h
    tail = x_scr[rows:rows + hist, :]
    conv_ref[...] = tail
    x_scr[0:hist, :] = tail
    y = _rms(b_scr[...] * _gelu_tanh(gate_br), vec[6:7, :])
    for j, slab in enumerate(_lane_slabs(y)):
        y_ref[j] = slab


def _lru_tm(cols_b, nb, seq, vec, bias, wgates):
    rows = SCAN_STEPS * nb
    hist = (CONV_WIDTH - 1) * nb
    slab = lambda t: (0, t, 0)
    fixed = lambda t: (0, 0)
    return pl.pallas_call(
        functools.partial(_lru_tm_kernel, nb=nb),
        grid=(seq // SCAN_STEPS,),
        in_specs=[pl.BlockSpec((2 * B_WIDTH // LANES, rows, LANES), slab), _const_spec((SUBLANES, B_WIDTH)),
                  _const_spec((1, 2 * B_WIDTH)), _const_spec((B_WIDTH, 2 * B_WIDTH))],
        out_specs=[pl.BlockSpec((B_WIDTH // LANES, rows, LANES), slab), pl.BlockSpec((hist, B_WIDTH), fixed),
                   pl.BlockSpec((nb, B_WIDTH), fixed)],
        out_shape=[jax.ShapeDtypeStruct((B_WIDTH // LANES, nb * seq, LANES), F32),
                   jax.ShapeDtypeStruct((hist, B_WIDTH), F32),
                   jax.ShapeDtypeStruct((nb, B_WIDTH), F32)],
        scratch_shapes=[pltpu.VMEM((hist + rows, B_WIDTH), F32), pltpu.VMEM((rows, B_WIDTH), F32),
                        pltpu.VMEM((rows, B_WIDTH), F32), pltpu.VMEM((nb, B_WIDTH), F32)],
        compiler_params=_params(1),
        name="lru_tm",
    )(cols_b, vec, bias, wgates)


def _lru_sample_kernel(cols_ref, conv0_ref, h0_ref, vec_ref, bias_ref, wg_ref, y_ref, conv_ref, h_ref, *, steps):
    rows = cols_ref.shape[0]
    nseq = rows // steps
    gate_br = cols_ref[:, 0:B_WIDTH]
    xb = cols_ref[:, B_WIDTH:2 * B_WIDTH]
    tloc = _row_iota(xb.shape) & (steps - 1)
    h0 = jnp.broadcast_to(h0_ref[...], (nseq, steps, B_WIDTH)).reshape(rows, B_WIDTH)
    y, hs = _lru_body(gate_br, xb, conv0_ref[...], h0, tloc, steps, vec_ref[...], bias_ref[...], wg_ref[...])
    y_ref[...] = y
    conv_ref[...] = xb
    h_ref[...] = hs


def _lru_sample(cols_b, conv0_rows, h0, nb, steps, vec, bias, wgates):
    bb = SAMPLE_SEQS
    rows = bb * steps
    blk = lambda i: (i, 0)
    return pl.pallas_call(
        functools.partial(_lru_sample_kernel, steps=steps),
        grid=(nb // bb,),
        in_specs=[pl.BlockSpec((rows, 2 * B_WIDTH), blk), pl.BlockSpec((rows, B_WIDTH), blk),
                  pl.BlockSpec((bb, 1, B_WIDTH), lambda i: (i, 0, 0)),
                  _const_spec((SUBLANES, B_WIDTH)), _const_spec((1, 2 * B_WIDTH)),
                  _const_spec((B_WIDTH, 2 * B_WIDTH))],
        out_specs=[pl.BlockSpec((rows, B_WIDTH), blk)] * 3,
        out_shape=[jax.ShapeDtypeStruct((nb * steps, B_WIDTH), F32)] * 3,
        compiler_params=_params(1),
        name="lru_sample",
    )(cols_b, conv0_rows, h0, vec, bias, wgates)


def _s5_drive(u, lam, bbd):
    lr = lam[0:1, :]
    li = lam[1:2, :]
    dt = jnp.exp(lam[2:3, :])
    mag = jnp.exp(lr * dt)
    ar = mag * jnp.cos(li * dt)
    ai = mag * jnp.sin(li * dt)
    den = lr * lr + li * li
    cr = ((ar - 1.0) * lr + ai * li) / den
    ci = (ai * lr - (ar - 1.0) * li) / den
    ub = _bf(u)
    pb = _dot(ub, bbd[:, 0:S5_W])
    qb = _dot(ub, bbd[:, S5_W:2 * S5_W])
    return ar, ai, pb * cr - qb * ci, pb * ci + qb * cr


def _s5_readout(hr, hi, u, vec, cbd, wglu):
    s5_d = vec[0:1, :]
    b_glu = vec[1:2, :]
    g_out = vec[2:3, :]
    y = _dot(_bf(hr), cbd[0:S5_W, :]) + _dot(_bf(hi), cbd[S5_W:2 * S5_W, :]) + s5_d * u
    z = _gelu_tanh(y)
    out = z * _sigmoid(_dot(_bf(z), wglu) + b_glu)
    return _rms(out, g_out)


def _s5_body(u, h0r, h0i, tloc, tlen, lam, vec, bbd, cbd, wglu):
    ar, ai, hr, hi = _s5_drive(u, lam, bbd)
    first = tloc == 0
    hr = hr + jnp.where(first, ar * h0r - ai * h0i, 0.0)
    hi = hi + jnp.where(first, ar * h0i + ai * h0r, 0.0)
    pr, pi = ar, ai
    d = 1
    while d < tlen:
        keep = tloc >= d
        sr = jnp.where(keep, _shift_rows(hr, d), 0.0)
        si = jnp.where(keep, _shift_rows(hi, d), 0.0)
        hr, hi = hr + pr * sr - pi * si, hi + pr * si + pi * sr
        pr, pi = pr * pr - pi * pi, 2.0 * pr * pi
        d *= 2
    return _s5_readout(hr, hi, u, vec, cbd, wglu), hr, hi


def _s5_tm_kernel(u_ref, lam_ref, vec_ref, bbd_ref, cbd_ref, wglu_ref, y_ref, hr_ref, hi_ref,
                  re_scr, im_scr, sr_scr, si_scr, *, nb):
    rows = re_scr.shape[0]

    @pl.when(pl.program_id(0) == 0)
    def _():
        sr_scr[...] = jnp.zeros_like(sr_scr)
        si_scr[...] = jnp.zeros_like(si_scr)

    u = jnp.concatenate([u_ref[j] for j in range(C_WIDTH // LANES)], axis=1)
    ar, ai, dr, di = _s5_drive(u, lam_ref[...], bbd_ref[...])
    re_scr[...] = dr
    im_scr[...] = di
    ar = jnp.broadcast_to(ar, (nb, S5_W))
    ai = jnp.broadcast_to(ai, (nb, S5_W))

    def step(i, carry):
        hr, hi = carry
        rw = pl.ds(pl.multiple_of(i * nb, nb), nb)
        nr = ar * hr - ai * hi + re_scr[rw, :]
        ni = ar * hi + ai * hr + im_scr[rw, :]
        re_scr[rw, :] = nr
        im_scr[rw, :] = ni
        return nr, ni

    hr, hi = lax.fori_loop(0, rows // nb, step, (sr_scr[...], si_scr[...]), unroll=4)
    sr_scr[...] = hr
    si_scr[...] = hi
    hr_ref[...] = hr
    hi_ref[...] = hi
    y = _s5_readout(re_scr[...], im_scr[...], u, vec_ref[...], cbd_ref[...], wglu_ref[...])
    for j, slab in enumerate(_lane_slabs(y)):
        y_ref[j] = slab


def _s5_tm(cols_c, nb, seq, lam, vec, bbd, cbd, wglu):
    rows = SCAN_STEPS * nb
    slab = lambda t: (0, t, 0)
    fixed = lambda t: (0, 0)
    n_slabs = C_WIDTH // LANES
    return pl.pallas_call(
        functools.partial(_s5_tm_kernel, nb=nb),
        grid=(seq // SCAN_STEPS,),
        in_specs=[pl.BlockSpec((n_slabs, rows, LANES), slab), _const_spec((SUBLANES, S5_W)),
                  _const_spec((SUBLANES, C_WIDTH)), _const_spec((C_WIDTH, 2 * S5_W)),
                  _const_spec((2 * S5_W, C_WIDTH)), _const_spec((C_WIDTH, C_WIDTH))],
        out_specs=[pl.BlockSpec((n_slabs, rows, LANES), slab), pl.BlockSpec((nb, S5_W), fixed),
                   pl.BlockSpec((nb, S5_W), fixed)],
        out_shape=[jax.ShapeDtypeStruct((n_slabs, nb * seq, LANES), F32),
                   jax.ShapeDtypeStruct((nb, S5_W), F32), jax.ShapeDtypeStruct((nb, S5_W), F32)],
        scratch_shapes=[pltpu.VMEM((rows, S5_W), F32), pltpu.VMEM((rows, S5_W), F32),
                        pltpu.VMEM((nb, S5_W), F32), pltpu.VMEM((nb, S5_W), F32)],
        compiler_params=_params(1),
        name="s5_tm",
    )(cols_c, lam, vec, bbd, cbd, wglu)


def _s5_sample_kernel(u_ref, h0r_ref, h0i_ref, lam_ref, vec_ref, bbd_ref, cbd_ref, wglu_ref,
                      y_ref, hr_ref, hi_ref, *, steps):
    rows = u_ref.shape[0]
    nseq = rows // steps
    tloc = _row_iota((rows, S5_W)) & (steps - 1)
    h0r = jnp.broadcast_to(h0r_ref[...], (nseq, steps, S5_W)).reshape(rows, S5_W)
    h0i = jnp.broadcast_to(h0i_ref[...], (nseq, steps, S5_W)).reshape(rows, S5_W)
    y, hr, hi = _s5_body(u_ref[...], h0r, h0i, tloc, steps, lam_ref[...], vec_ref[...], bbd_ref[...],
                         cbd_ref[...], wglu_ref[...])
    y_ref[...] = y
    hr_ref[...] = hr
    hi_ref[...] = hi


def _s5_sample(cols_c, h0r, h0i, nb, steps, lam, vec, bbd, cbd, wglu):
    bb = SAMPLE_SEQS
    rows = bb * steps
    blk = lambda i: (i, 0)
    st = pl.BlockSpec((bb, 1, S5_W), lambda i: (i, 0, 0))
    return pl.pallas_call(
        functools.partial(_s5_sample_kernel, steps=steps),
        grid=(nb // bb,),
        in_specs=[pl.BlockSpec((rows, C_WIDTH), blk), st, st,
                  _const_spec((SUBLANES, S5_W)), _const_spec((SUBLANES, C_WIDTH)),
                  _const_spec((C_WIDTH, 2 * S5_W)), _const_spec((2 * S5_W, C_WIDTH)),
                  _const_spec((C_WIDTH, C_WIDTH))],
        out_specs=[pl.BlockSpec((rows, C_WIDTH), blk), pl.BlockSpec((rows, S5_W), blk),
                   pl.BlockSpec((rows, S5_W), blk)],
        out_shape=[jax.ShapeDtypeStruct((nb * steps, C_WIDTH), F32),
                   jax.ShapeDtypeStruct((nb * steps, S5_W), F32),
                   jax.ShapeDtypeStruct((nb * steps, S5_W), F32)],
        compiler_params=_params(1),
        name="s5_sample",
    )(cols_c, h0r, h0i, lam, vec, bbd, cbd, wglu)


def _pad_rows(rows, width):
    m = jnp.stack([r.reshape(width) for r in rows], axis=0)
    return jnp.pad(m, ((0, SUBLANES - m.shape[0]), (0, 0)))


def _layer_weights(l, w):
    eye_b = jnp.eye(B_BLOCKS, dtype=F32)
    eye_g = jnp.eye(S5_GROUPS, dtype=F32)
    bd4 = lambda m: jnp.einsum("nde,nm->ndme", m, eye_b).reshape(B_WIDTH, B_WIDTH)
    zeros_lora = jnp.zeros((LANES - 64, A_WIDTH), F32)
    b_in = lambda m: jnp.einsum("gpc,gh->gchp", m, eye_g).reshape(C_WIDTH, S5_W)
    c_out = lambda m: jnp.einsum("gcp,gh->gphc", m, eye_g).reshape(S5_W, C_WIDTH)
    return dict(
        g_mix=w["g_mix"][l][None],
        w_in=_bf(w["w_in"][l]),
        mu=w["mu_a"][l][None],
        a_vec=_pad_rows([w["w0"][l], w["a0"][l], w["k_k"][l], w["k_a"][l], w["r_k"][l], w["lnx_w"][l],
                         w["lnx_b"][l]], A_WIDTH),
        wdec=_bf(jnp.concatenate([w["w_dec2"][l], zeros_lora], axis=0)),
        wa=_bf(jnp.concatenate([zeros_lora, w["w_a2"][l]], axis=0)),
        wg=_bf(w["w_g2"][l]),
        b_vec=_pad_rows([w["conv_w"][l][0], w["conv_w"][l][1], w["conv_w"][l][2], w["conv_w"][l][3],
                         w["conv_b"][l], w["lru_lambda"][l], w["g_out_b"][l]], B_WIDTH),
        b_bias=jnp.concatenate([w["b_rg"][l], w["b_ig"][l]])[None],
        b_wgates=_bf(jnp.concatenate([bd4(w["w_rg"][l]), bd4(w["w_ig"][l])], axis=1)),
        c_lam=_pad_rows([w["s5_lam_re"][l], w["s5_lam_im"][l],
                         jnp.repeat(w["s5_log_dt"][l], S5_STATE)], S5_W),
        c_vec=_pad_rows([w["s5_d"][l], w["b_glu"][l], w["g_out_c"][l]], C_WIDTH),
        c_bbd=_bf(jnp.concatenate([b_in(w["s5_b_re"][l]), b_in(w["s5_b_im"][l])], axis=1)),
        c_cbd=_bf(jnp.concatenate([c_out(w["s5_c_re"][l]), -c_out(w["s5_c_im"][l])], axis=0)),
        c_wglu=_bf(w["w_glu"][l]),
        w_out=_bf(w["w_out"][l]),
        p_vec=_pad_rows([w["g_ffn"][l], w["g_ple"][l], w["g_final"]], D_MODEL),
        w_up=_bf(w["w_ffn_up"][l]),
        w_down=_bf(w["w_ffn_down"][l]),
        w_ple=_bf(w["w_ple"][l]),
        w_gate=_bf(w["w_ple_gate"][l]),
    )


def _post_layer(h, ya, yb, yc, p, lw, final, nb_tm=0):
    return _post(h, ya, yb, yc, p, lw["w_out"], lw["p_vec"], lw["w_up"], lw["w_down"], lw["w_ple"],
                 lw["w_gate"], final, nb_tm)


def _run_prompt(x, p, lws):
    nb, seq, _ = x.shape
    depth = len(lws)
    h = x.reshape(nb * seq, D_MODEL)
    outs = []
    for l, lw in enumerate(lws):
        cols_a, cols_b, cols_c = _proj_in_tm(h, lw["g_mix"], lw["w_in"], nb, seq)
        ya, wkv, last = _rwkv_prompt(cols_a, nb, seq, lw["mu"], lw["a_vec"], lw["wdec"], lw["wa"], lw["wg"])
        yb, conv, lru = _lru_tm(cols_b, nb, seq, lw["b_vec"], lw["b_bias"], lw["b_wgates"])
        yc, s5r, s5i = _s5_tm(cols_c, nb, seq, lw["c_lam"], lw["c_vec"], lw["c_bbd"], lw["c_cbd"],
                              lw["c_wglu"])
        h = _post_layer(h, ya, yb, yc, p[l].reshape(nb * seq, PLE_DIM), lw, l == depth - 1, nb_tm=nb)
        conv = jnp.swapaxes(conv.reshape(CONV_WIDTH - 1, nb, B_WIDTH), 0, 1)
        outs.append((last[:, SUBLANES - 1], wkv, conv, lru,
                     s5r.reshape(nb, S5_GROUPS, S5_STATE), s5i.reshape(nb, S5_GROUPS, S5_STATE)))
    states = tuple(jnp.stack([o[j] for o in outs], axis=0) for j in range(6))
    return h.reshape(nb, seq, D_MODEL), states


def _run_sample(x, p, states, lws):
    nb, steps, _ = x.shape
    depth = len(lws)
    st_shift, st_wkv, st_conv, st_lru, st_s5r, st_s5i = states
    h = x.reshape(nb * steps, D_MODEL)
    outs = []
    for l, lw in enumerate(lws):
        cols_a, cols_b, cols_c = _proj_in(h, lw["g_mix"], lw["w_in"])
        ya, wkv = _rwkv_sample(cols_a, st_shift[l][:, None, :], st_wkv[l], nb, steps, lw["mu"], lw["a_vec"],
                               lw["wdec"], lw["wa"], lw["wg"])
        conv0_rows = jnp.pad(st_conv[l], ((0, 0), (0, steps - 3), (0, 0))).reshape(nb * steps, B_WIDTH)
        yb, conv, lru = _lru_sample(cols_b, conv0_rows, st_lru[l][:, None, :], nb, steps, lw["b_vec"],
                                    lw["b_bias"], lw["b_wgates"])
        yc, s5r, s5i = _s5_sample(cols_c, st_s5r[l].reshape(nb, 1, S5_W), st_s5i[l].reshape(nb, 1, S5_W),
                                  nb, steps, lw["c_lam"], lw["c_vec"], lw["c_bbd"], lw["c_cbd"], lw["c_wglu"])
        h = _post_layer(h, ya, yb, yc, p[l].reshape(nb * steps, PLE_DIM), lw, l == depth - 1)
        last = lambda z, width: z.reshape(nb, steps, width)[:, steps - 1]
        outs.append((last(cols_a, A_COLS), wkv, conv.reshape(nb, steps, B_WIDTH)[:, steps - 3:],
                     last(lru, B_WIDTH),
                     last(s5r, S5_W).reshape(nb, S5_GROUPS, S5_STATE),
                     last(s5i, S5_W).reshape(nb, S5_GROUPS, S5_STATE)))
    new_states = tuple(jnp.stack([o[j] for o in outs], axis=0) for j in range(6))
    return h.reshape(nb, steps, D_MODEL), new_states


def kernel(x_prompt, x_sample, p_prompt, p_sample, state_shift, state_wkv, state_conv, state_lru, state_s5_re, state_s5_im, g_mix, w_in, mu_a, w0, w_dec2, a0, w_a2, w_g2, k_k, k_a, r_k, lnx_w, lnx_b, conv_w, conv_b, w_rg, b_rg, w_ig, b_ig, lru_lambda, g_out_b, s5_lam_re, s5_lam_im, s5_log_dt, s5_b_re, s5_b_im, s5_c_re, s5_c_im, s5_d, w_glu, b_glu, g_out_c, w_out, g_ffn, w_ffn_up, w_ffn_down, g_ple, w_ple, w_ple_gate, g_final):
    w = dict(g_mix=g_mix, w_in=w_in, mu_a=mu_a, w0=w0, w_dec2=w_dec2, a0=a0, w_a2=w_a2, w_g2=w_g2, k_k=k_k,
             k_a=k_a, r_k=r_k, lnx_w=lnx_w, lnx_b=lnx_b, conv_w=conv_w, conv_b=conv_b, w_rg=w_rg, b_rg=b_rg,
             w_ig=w_ig, b_ig=b_ig, lru_lambda=lru_lambda, g_out_b=g_out_b, s5_lam_re=s5_lam_re,
             s5_lam_im=s5_lam_im, s5_log_dt=s5_log_dt, s5_b_re=s5_b_re, s5_b_im=s5_b_im, s5_c_re=s5_c_re,
             s5_c_im=s5_c_im, s5_d=s5_d, w_glu=w_glu, b_glu=b_glu, g_out_c=g_out_c, w_out=w_out, g_ffn=g_ffn,
             w_ffn_up=w_ffn_up, w_ffn_down=w_ffn_down, g_ple=g_ple, w_ple=w_ple, w_ple_gate=w_ple_gate,
             g_final=g_final)
    depth = g_mix.shape[0]
    lws = [_layer_weights(l, w) for l in range(depth)]
    y_prompt, new_p = _run_prompt(x_prompt, p_prompt, lws)
    y_sample, new_s = _run_sample(x_sample, p_sample,
                                  (state_shift, state_wkv, state_conv, state_lru, state_s5_re, state_s5_im), lws)
    return (y_prompt, y_sample) + new_p + new_s
```

```python
import functools
import math

import jax
import jax.numpy as jnp
from jax import lax
from jax.experimental import pallas as pl
from jax.experimental.pallas import tpu as pltpu

F32 = jnp.float32
BF16 = jnp.bfloat16

D_MODEL = 1024
A_WIDTH = 512
A_HEADS = 8
A_HEAD_DIM = 64
A_COLS = 1792
B_WIDTH = 256
B_BLOCKS = 4
CONV_WIDTH = 4
C_WIDTH = 256
S5_GROUPS = 16
S5_GROUP_CH = 16
S5_STATE = 64
S5_W = S5_GROUPS * S5_STATE
IN_COLS = A_COLS + 2 * B_WIDTH + C_WIDTH
D_FF = 2816
PLE_DIM = 256
LRU_C = 8.0
RMS_EPS = 1e-6
GN_EPS = 64e-5

LANES = 128
SUBLANES = 8
VMEM_LIMIT_BYTES = 56 * 1024 * 1024

TOKEN_TILE = 512
FFN_CHUNK = 256
RWKV_CHUNK = 64
RWKV_BLOCK = 256
SCAN_STEPS = 64
SAMPLE_SEQS = 8


def _layer_spec(l, shape):
    nd = len(shape)
    return pl.BlockSpec((None,) + tuple(shape), lambda *_: (l,) + (0,) * nd, pipeline_mode=pl.Buffered(1))


def _params(n_axes):
    return pltpu.CompilerParams(dimension_semantics=("arbitrary",) * n_axes,
                                vmem_limit_bytes=VMEM_LIMIT_BYTES)


def _dot(a, b):
    return jnp.dot(a, b, preferred_element_type=F32)


def _dot_nt(a, b):
    return lax.dot_general(a, b, (((1,), (1,)), ((), ())), preferred_element_type=F32)


def _dot_tn(a, b):
    return lax.dot_general(a, b, (((0,), (0,)), ((), ())), preferred_element_type=F32)


def _bf(x):
    return x.astype(BF16)


def _rms(x, g):
    inv = lax.rsqrt(jnp.mean(x * x, axis=-1, keepdims=True) + RMS_EPS)
    return x * inv * g


def _sigmoid(x):
    return 1.0 / (1.0 + jnp.exp(-x))


def _softplus(x):
    return jnp.maximum(x, 0.0) + jnp.log(1.0 + jnp.exp(-jnp.abs(x)))


def _gelu_tanh(x):
    c = math.sqrt(2.0 / math.pi)
    return 0.5 * x * (1.0 + jnp.tanh(c * (x + 0.044715 * (x * x * x))))


def _row_iota(shape):
    return lax.broadcasted_iota(jnp.int32, shape, 0)


def _shift_rows(x, d):
    return pltpu.roll(x, d, axis=0)


def _lane_slabs(x):
    return [x[:, j * LANES:(j + 1) * LANES] for j in range(x.shape[1] // LANES)]


def _last_steps(x, nseq, steps, n):
    return x.reshape(nseq, steps, x.shape[1])[:, steps - n:steps, :]


def _stack_previous(prev_refs, out_ref):
    for j, ref in enumerate(prev_refs):
        out_ref[j] = ref[...]


def _proj_in_kernel(h_ref, g_ref, w_ref, a_ref, b_ref, c_ref):
    xn = _bf(_rms(h_ref[...], g_ref[...]))
    a_ref[...] = _dot(xn, w_ref[:, 0:A_COLS])
    b_ref[...] = _dot(xn, w_ref[:, A_COLS:A_COLS + 2 * B_WIDTH])
    c_ref[...] = _dot(xn, w_ref[:, A_COLS + 2 * B_WIDTH:IN_COLS])


def _proj_in(h, g, w_bf, l):
    n = h.shape[0]
    tm = TOKEN_TILE
    row = lambda i: (i, 0)
    return pl.pallas_call(
        _proj_in_kernel,
        grid=(n // tm,),
        in_specs=[pl.BlockSpec((tm, D_MODEL), row), _layer_spec(l, (1, D_MODEL)),
                  _layer_spec(l, (D_MODEL, IN_COLS))],
        out_specs=[pl.BlockSpec((tm, A_COLS), row), pl.BlockSpec((tm, 2 * B_WIDTH), row),
                   pl.BlockSpec((tm, C_WIDTH), row)],
        out_shape=[jax.ShapeDtypeStruct((n, A_COLS), F32), jax.ShapeDtypeStruct((n, 2 * B_WIDTH), F32),
                   jax.ShapeDtypeStruct((n, C_WIDTH), F32)],
        compiler_params=_params(1),
        name="proj_in",
    )(h, g, w_bf)


def _proj_in_tm_kernel(h_ref, g_ref, w_ref, a_ref, b_ref, c_ref, *, nb):
    b = pl.program_id(1)
    tm = h_ref.shape[0]
    xn = _bf(_rms(h_ref[...], g_ref[...]))
    a_ref[...] = _dot(xn, w_ref[:, 0:A_COLS])
    rows_of_b = pl.ds(b, tm, stride=nb)
    for j, slab in enumerate(_lane_slabs(_dot(xn, w_ref[:, A_COLS:A_COLS + 2 * B_WIDTH]))):
        b_ref[j, rows_of_b, :] = slab
    for j, slab in enumerate(_lane_slabs(_dot(xn, w_ref[:, A_COLS + 2 * B_WIDTH:IN_COLS]))):
        c_ref[j, rows_of_b, :] = slab


def _proj_in_tm(h, g, w_bf, l, nb, seq):
    tm = TOKEN_TILE
    nt = seq // tm
    row = lambda i, b: (b * nt + i, 0)
    slab = lambda i, b: (0, i, 0)
    nb_slabs, nc_slabs = 2 * B_WIDTH // LANES, C_WIDTH // LANES
    return pl.pallas_call(
        functools.partial(_proj_in_tm_kernel, nb=nb),
        grid=(nt, nb),
        in_specs=[pl.BlockSpec((tm, D_MODEL), row), _layer_spec(l, (1, D_MODEL)),
                  _layer_spec(l, (D_MODEL, IN_COLS))],
        out_specs=[pl.BlockSpec((tm, A_COLS), row), pl.BlockSpec((nb_slabs, tm * nb, LANES), slab),
                   pl.BlockSpec((nc_slabs, tm * nb, LANES), slab)],
        out_shape=[jax.ShapeDtypeStruct((nb * seq, A_COLS), F32),
                   jax.ShapeDtypeStruct((nb_slabs, nb * seq, LANES), F32),
                   jax.ShapeDtypeStruct((nc_slabs, nb * seq, LANES), F32)],
        compiler_params=_params(2),
        name="proj_in_tm",
    )(h, g, w_bf)


def _post_kernel(h_ref, ya_ref, yb_ref, yc_ref, p_ref, wo_ref, vec_ref, wup_ref, wdn_ref, wple_ref,
                 wgate_ref, o_ref, act_scr, *, final, nb_tm):
    g_ffn = vec_ref[0:1, :]
    g_ple = vec_ref[1:2, :]
    g_final = vec_ref[2:3, :]
    if nb_tm:
        rows_of_b = pl.ds(pl.program_id(1), h_ref.shape[0], stride=nb_tm)
        yb = jnp.concatenate([yb_ref[j, rows_of_b, :] for j in range(B_WIDTH // LANES)], axis=1)
        yc = jnp.concatenate([yc_ref[j, rows_of_b, :] for j in range(C_WIDTH // LANES)], axis=1)
    else:
        yb, yc = yb_ref[...], yc_ref[...]
    h1 = (h_ref[...] + _dot(_bf(ya_ref[...]), wo_ref[0:A_WIDTH, :])
          + _dot(_bf(yb), wo_ref[A_WIDTH:A_WIDTH + B_WIDTH, :])
          + _dot(_bf(yc), wo_ref[A_WIDTH + B_WIDTH:D_MODEL, :]))
    xf = _bf(_rms(h1, g_ffn))
    for c0 in range(0, D_FF, FFN_CHUNK):
        gate = _dot(xf, wup_ref[:, c0:c0 + FFN_CHUNK])
        up = _dot(xf, wup_ref[:, D_FF + c0:D_FF + c0 + FFN_CHUNK])
        act_scr[:, c0:c0 + FFN_CHUNK] = _bf(gate * _sigmoid(gate) * up)
    h2 = h1 + _dot(act_scr[...], wdn_ref[...])
    ple = _dot(_bf(p_ref[...]), wple_ref[...])
    gate = _sigmoid(_dot(_bf(_rms(h2, g_ple)), wgate_ref[...]))
    h3 = h2 + ple * gate
    if final:
        h3 = _rms(h3, g_final)
    o_ref[...] = h3


def _post(h, ya, yb, yc, p, wts, l, final, nb_tm=0):
    n = h.shape[0]
    tm = TOKEN_TILE
    if nb_tm:
        nt = n // (nb_tm * tm)
        grid = (nt, nb_tm)
        row = lambda i, b: (b * nt + i, 0)
        prow = lambda i, b: (l, b * nt + i, 0)
        slab = lambda i, b: (0, i, 0)
        yb_spec = pl.BlockSpec((B_WIDTH // LANES, tm * nb_tm, LANES), slab, pipeline_mode=pl.Buffered(1))
        yc_spec = pl.BlockSpec((C_WIDTH // LANES, tm * nb_tm, LANES), slab, pipeline_mode=pl.Buffered(1))
    else:
        grid = (n // tm,)
        row = lambda i: (i, 0)
        prow = lambda i: (l, i, 0)
        yb_spec = pl.BlockSpec((tm, B_WIDTH), row)
        yc_spec = pl.BlockSpec((tm, C_WIDTH), row)
    return pl.pallas_call(
        functools.partial(_post_kernel, final=final, nb_tm=nb_tm),
        grid=grid,
        in_specs=[pl.BlockSpec((tm, D_MODEL), row), pl.BlockSpec((tm, A_WIDTH), row),
                  yb_spec, yc_spec,
                  pl.BlockSpec((None, tm, PLE_DIM), prow),
                  _layer_spec(l, (D_MODEL, D_MODEL)), _layer_spec(l, (SUBLANES, D_MODEL)),
                  _layer_spec(l, (D_MODEL, 2 * D_FF)), _layer_spec(l, (D_FF, D_MODEL)),
                  _layer_spec(l, (PLE_DIM, D_MODEL)), _layer_spec(l, (D_MODEL, D_MODEL))],
        out_specs=pl.BlockSpec((tm, D_MODEL), row),
        out_shape=jax.ShapeDtypeStruct((n, D_MODEL), F32),
        scratch_shapes=[pltpu.VMEM((tm, D_FF), BF16)],
        compiler_params=_params(len(grid)),
        name="post",
    )(h, ya, yb, yc, p, wts["w_out"], wts["p_vec"], wts["w_up"], wts["w_down"], wts["w_ple"], wts["w_gate"])


def _head_sum(x):
    first = lax.broadcasted_iota(jnp.int32, (x.shape[0], LANES), 1) < A_HEAD_DIM
    outs = []
    for xp in _lane_slabs(x):
        s0 = jnp.sum(jnp.where(first, xp, 0.0), axis=-1, keepdims=True)
        s1 = jnp.sum(jnp.where(first, 0.0, xp), axis=-1, keepdims=True)
        outs.append(jnp.where(first, s0, s1))
    return jnp.concatenate(outs, axis=1)


def _rwkv_prep(cols, prev, mu, vec, wdec, wa, wg, chunk):
    rows = cols.shape[0]
    xs = cols + (prev - cols) * mu
    r = xs[:, 0:A_WIDTH]
    k = xs[:, A_WIDTH:2 * A_WIDTH]
    v = xs[:, 2 * A_WIDTH:3 * A_WIDTH]
    xwa = xs[:, 3 * A_WIDTH:3 * A_WIDTH + LANES]
    xg = xs[:, 3 * A_WIDTH + LANES:A_COLS]
    w0, a0, k_k, k_a, r_k = (vec[i:i + 1, :] for i in range(5))
    z = w0 + _dot(_bf(jnp.tanh(xwa)), wdec)
    log_decay = -jnp.exp(-_softplus(-z) - 0.5)
    a = _sigmoid(a0 + _dot(_bf(xwa), wa))
    g = _dot(_bf(_sigmoid(xg)), wg)
    kk_raw = k * k_k
    kk = kk_raw * lax.rsqrt(jnp.maximum(_head_sum(kk_raw * kk_raw), 1e-24))
    k_mod = k * (1.0 + (a - 1.0) * k_a)
    bonus = _head_sum(r * k_mod * r_k) * v
    ri = _row_iota((rows, rows))
    ci = lax.broadcasted_iota(jnp.int32, (rows, rows), 1)
    same_chunk = (ci & (-chunk)) == (ri & (-chunk))
    tri = _bf(jnp.where((ci <= ri) & same_chunk, 1.0, 0.0))
    h1 = _bf(log_decay)
    r1 = log_decay - h1.astype(F32)
    h2 = _bf(r1)
    h3 = _bf(r1 - h2.astype(F32))
    cum = _dot(tri, h1) + _dot(tri, h2) + _dot(tri, h3)
    gam = jnp.exp(cum)
    ginv = jnp.exp(-cum)
    gprev = jnp.exp(cum - log_decay)
    return dict(rt=r * gam, kap=kk * gprev, bet=kk * a * ginv, kt=k_mod * ginv, v=v, gam=gam,
                bonus=bonus, g=g)


def _stack_heads(x):
    first = lax.broadcasted_iota(jnp.int32, x.shape, 1) < A_HEAD_DIM
    return _bf(jnp.concatenate([jnp.where(first, x, 0.0), jnp.where(first, 0.0, x)], axis=0))


def _rwkv_local(ops):
    c = ops[0]["kap"].shape[0]
    c2 = 2 * c
    ri = _row_iota((c, c2))
    ci = lax.broadcasted_iota(jnp.int32, (c, c2), 1) & (c - 1)
    strict = ci < ri
    incl = ci <= ri
    eye = jnp.where(ri == ci, 1.0, 0.0)
    same_head = (_row_iota((c2, c2)) & c) == (lax.broadcasted_iota(jnp.int32, (c2, c2), 1) & c)
    merged = c2 % LANES == 0

    def blockdiag(m):
        return _bf(jnp.where(same_head, jnp.concatenate([m, m], axis=0), 0.0))

    st = []
    for o in ops:
        s = {k + "2": _stack_heads(o[k]) for k in ("kap", "bet", "kt", "v")}
        s["rt"] = o["rt"]
        lhs = _bf(jnp.concatenate([o["kap"], o["rt"]], axis=0))
        if merged:
            g = _dot_nt(lhs, jnp.concatenate([s["bet2"], s["kt2"]], axis=0))
            a_b, a_k, a_rb, a_rk = g[0:c, 0:c2], g[0:c, c2:2 * c2], g[c:c2, 0:c2], g[c:c2, c2:2 * c2]
        else:
            gb, gk = _dot_nt(lhs, s["bet2"]), _dot_nt(lhs, s["kt2"])
            a_b, a_k, a_rb, a_rk = gb[0:c], gk[0:c], gb[c:c2], gk[c:c2]
        s["a_b"] = jnp.where(strict, a_b, 0.0)
        s["a_k"] = _bf(jnp.where(strict, a_k, 0.0))
        s["a_rb"] = _bf(jnp.where(incl, a_rb, 0.0))
        s["a_rk"] = _bf(jnp.where(incl, a_rk, 0.0))
        st.append(s)
    for s in st:
        s["t"] = eye - s["a_b"]
        s["lp"] = s["a_b"]
        s["akv"] = _dot(s["a_k"], s["v2"])
    n = 2
    while n < c:
        for s in st:
            s["lp"] = _dot(_bf(s["lp"]), blockdiag(s["lp"]))
        for s in st:
            s["t"] = s["t"] + _dot(_bf(s["t"]), blockdiag(s["lp"]))
        n *= 2
    for s in st:
        wu = _dot(_bf(s["t"]), jnp.concatenate([s["kap2"], _stack_heads(s["akv"])], axis=1))
        s["w"] = -wu[:, 0:LANES]
        s["u0"] = -wu[:, LANES:2 * LANES]
    return st


def _rwkv_state(st, states, g_ends):
    c = st[0]["rt"].shape[0]
    merged = (2 * c) % LANES == 0
    xs = []
    for s, state in zip(st, states):
        xs.append(_dot_nt(_bf(jnp.concatenate([s["w"], s["rt"]], axis=0)), _bf(state)))
    u2s = [_stack_heads(x[0:c] + s["u0"]) for x, s in zip(xs, st)]
    ys, new_states = [], []
    for s, x, u2 in zip(st, xs, u2s):
        if merged:
            y = x[c:2 * c] + _dot(jnp.concatenate([s["a_rb"], s["a_rk"]], axis=1),
                                  jnp.concatenate([u2, s["v2"]], axis=0))
        else:
            y = x[c:2 * c] + _dot(s["a_rb"], u2) + _dot(s["a_rk"], s["v2"])
        ys.append(y)
    for s, u2, state, g_end in zip(st, u2s, states, g_ends):
        ds = _dot_tn(jnp.concatenate([u2, s["v2"]], axis=0), jnp.concatenate([s["bet2"], s["kt2"]], axis=0))
        new_states.append((state + ds) * g_end)
    return ys, new_states


def _rwkv_finish(y, bonus, g, vec):
    lnx_w = vec[5:6, :]
    lnx_b = vec[6:7, :]
    inv_n = 1.0 / A_HEAD_DIM
    mu = _head_sum(y) * inv_n
    d = y - mu
    var = _head_sum(d * d) * inv_n
    yn = d * lax.rsqrt(var + GN_EPS) * lnx_w + lnx_b
    return (yn + bonus) * g


def _pair_state(s_ref, idx, p):
    z = jnp.zeros((A_HEAD_DIM, A_HEAD_DIM), F32)
    top = jnp.concatenate([s_ref[idx, 2 * p], z], axis=1)
    bot = jnp.concatenate([z, s_ref[idx, 2 * p + 1]], axis=1)
    return jnp.concatenate([top, bot], axis=0)


def _rwkv_prompt_kernel(cols_ref, mu_ref, vec_ref, wdec_ref, wa_ref, wg_ref,
                        y_ref, wkv_ref, last_ref, s_scr, prev_scr):
    t = pl.program_id(1)
    rows = cols_ref.shape[0]
    chunk = RWKV_CHUNK

    @pl.when(t == 0)
    def _():
        s_scr[...] = jnp.zeros_like(s_scr)
        prev_scr[...] = jnp.zeros_like(prev_scr)

    cols = cols_ref[...]
    first_row = _row_iota(cols.shape) == 0
    prev = jnp.where(first_row, prev_scr[SUBLANES - 1:SUBLANES, :], _shift_rows(cols, 1))
    q = _rwkv_prep(cols, prev, mu_ref[...], vec_ref[...], wdec_ref[...], wa_ref[...], wg_ref[...], chunk)
    n_pairs = A_HEADS // 2
    lanes = [slice(p * LANES, (p + 1) * LANES) for p in range(n_pairs)]
    starts = list(range(0, rows, chunk))
    ops = [{k: q[k][c0:c0 + chunk, ln] for k in ("kap", "rt", "bet", "kt", "v")}
           for c0 in starts for ln in lanes]
    st = _rwkv_local(ops)
    states = [s_scr[p] for p in range(n_pairs)]
    y_rows = []
    for i, c0 in enumerate(starts):
        g_ends = [q["gam"][c0 + chunk - 1:c0 + chunk, ln] for ln in lanes]
        ys, states = _rwkv_state(st[i * n_pairs:(i + 1) * n_pairs], states, g_ends)
        y_rows.append(jnp.concatenate(ys, axis=1))
    for p in range(n_pairs):
        s_scr[p] = states[p]
        wkv_ref[0, 2 * p] = states[p][0:A_HEAD_DIM, 0:A_HEAD_DIM]
        wkv_ref[0, 2 * p + 1] = states[p][A_HEAD_DIM:LANES, A_HEAD_DIM:LANES]
    y = jnp.concatenate(y_rows, axis=0) if len(y_rows) > 1 else y_rows[0]
    y_ref[...] = _rwkv_finish(y, q["bonus"], q["g"], vec_ref[...])
    tail = cols[rows - SUBLANES:rows, :]
    prev_scr[...] = tail
    last_ref[0] = tail


def _rwkv_param_specs(l):
    return [_layer_spec(l, (1, A_COLS)), _layer_spec(l, (SUBLANES, A_WIDTH)), _layer_spec(l, (LANES, A_WIDTH)),
            _layer_spec(l, (LANES, A_WIDTH)), _layer_spec(l, (LANES, A_WIDTH))]


def _rwkv_param_args(wts):
    return (wts["mu"], wts["a_vec"], wts["wdec"], wts["wa"], wts["wg"])


def _rwkv_prompt(cols_a, nb, seq, wts, l):
    tb = RWKV_BLOCK
    nt = seq // tb
    blk = lambda b, t: (b * nt + t, 0)
    per_b3 = lambda b, t: (b, 0, 0)
    return pl.pallas_call(
        _rwkv_prompt_kernel,
        grid=(nb, nt),
        in_specs=[pl.BlockSpec((tb, A_COLS), blk)] + _rwkv_param_specs(l),
        out_specs=[pl.BlockSpec((tb, A_WIDTH), blk),
                   pl.BlockSpec((1, A_HEADS, A_HEAD_DIM, A_HEAD_DIM), lambda b, t: (b, 0, 0, 0)),
                   pl.BlockSpec((1, SUBLANES, A_COLS), per_b3)],
        out_shape=[jax.ShapeDtypeStruct((nb * seq, A_WIDTH), F32),
                   jax.ShapeDtypeStruct((nb, A_HEADS, A_HEAD_DIM, A_HEAD_DIM), F32),
                   jax.ShapeDtypeStruct((nb, SUBLANES, A_COLS), F32)],
        scratch_shapes=[pltpu.VMEM((A_HEADS // 2, LANES, LANES), F32), pltpu.VMEM((SUBLANES, A_COLS), F32)],
        compiler_params=_params(2),
        name="rwkv_prompt",
    )(cols_a, *_rwkv_param_args(wts))


def _rwkv_sample_kernel(*refs, steps, n_prev):
    cols_ref, shift_ref, wkv0_ref, mu_ref, vec_ref, wdec_ref, wa_ref, wg_ref = refs[0:8]
    prev_refs = refs[8:8 + 2 * n_prev]
    y_ref, shift_out_ref, wkv_out_ref = refs[8 + 2 * n_prev:]
    rows = cols_ref.shape[0]
    nseq = rows // steps
    cols = cols_ref[...]
    tloc = _row_iota(cols.shape) & (steps - 1)
    shift0 = jnp.broadcast_to(shift_ref[...], (nseq, steps, A_COLS)).reshape(rows, A_COLS)
    prev = jnp.where(tloc == 0, shift0, _shift_rows(cols, 1))
    q = _rwkv_prep(cols, prev, mu_ref[...], vec_ref[...], wdec_ref[...], wa_ref[...], wg_ref[...], steps)
    n_pairs = A_HEADS // 2
    lanes = [slice(p * LANES, (p + 1) * LANES) for p in range(n_pairs)]
    chains = [(s, p) for s in range(nseq) for p in range(n_pairs)]
    ops = [{k: q[k][s * steps:(s + 1) * steps, lanes[p]] for k in ("kap", "rt", "bet", "kt", "v")}
           for s, p in chains]
    states = [_pair_state(wkv0_ref, s, p) for s, p in chains]
    g_ends = [q["gam"][(s + 1) * steps - 1:(s + 1) * steps, lanes[p]] for s, p in chains]
    ys, new_states = _rwkv_state(_rwkv_local(ops), states, g_ends)
    _stack_previous(prev_refs[0::2], shift_out_ref)
    _stack_previous(prev_refs[1::2], wkv_out_ref)
    for (s, p), s_new in zip(chains, new_states):
        wkv_out_ref[n_prev, s, 2 * p] = s_new[0:A_HEAD_DIM, 0:A_HEAD_DIM]
        wkv_out_ref[n_prev, s, 2 * p + 1] = s_new[A_HEAD_DIM:LANES, A_HEAD_DIM:LANES]
    shift_out_ref[n_prev] = _last_steps(cols, nseq, steps, 1)
    y = jnp.concatenate([jnp.concatenate(ys[s * n_pairs:(s + 1) * n_pairs], axis=1) for s in range(nseq)],
                        axis=0)
    y_ref[...] = _rwkv_finish(y, q["bonus"], q["g"], vec_ref[...])


def _rwkv_sample(cols_a, shift0, wkv0, prevs, nb, steps, wts, l):
    bb = SAMPLE_SEQS
    rows = bb * steps
    n_prev = len(prevs)
    blk = lambda i: (i, 0)
    seq3 = lambda i: (0, i, 0, 0)
    seq4 = lambda i: (0, i, 0, 0, 0)
    prev_specs, prev_args = [], []
    for sh, wk in prevs:
        prev_specs += [pl.BlockSpec((None, bb, 1, A_COLS), seq3),
                       pl.BlockSpec((None, bb, A_HEADS, A_HEAD_DIM, A_HEAD_DIM), seq4)]
        prev_args += [sh, wk]
    return pl.pallas_call(
        functools.partial(_rwkv_sample_kernel, steps=steps, n_prev=n_prev),
        grid=(nb // bb,),
        in_specs=[pl.BlockSpec((rows, A_COLS), blk),
                  pl.BlockSpec((None, bb, 1, A_COLS), lambda i: (l, i, 0, 0)),
                  pl.BlockSpec((None, bb, A_HEADS, A_HEAD_DIM, A_HEAD_DIM), lambda i: (l, i, 0, 0, 0))]
                 + _rwkv_param_specs(l) + prev_specs,
        out_specs=[pl.BlockSpec((rows, A_WIDTH), blk),
                   pl.BlockSpec((n_prev + 1, bb, 1, A_COLS), seq3),
                   pl.BlockSpec((n_prev + 1, bb, A_HEADS, A_HEAD_DIM, A_HEAD_DIM), seq4)],
        out_shape=[jax.ShapeDtypeStruct((nb * steps, A_WIDTH), F32),
                   jax.ShapeDtypeStruct((n_prev + 1, nb, 1, A_COLS), F32),
                   jax.ShapeDtypeStruct((n_prev + 1, nb, A_HEADS, A_HEAD_DIM, A_HEAD_DIM), F32)],
        compiler_params=_params(1),
        name="rwkv_sample",
    )(cols_a, shift0, wkv0, *_rwkv_param_args(wts), *prev_args)


def _lru_gates(xc, vec, bias, wgates):
    lam = vec[5:6, :]
    gates = _sigmoid(_dot(_bf(xc), wgates) + bias)
    gate_r = gates[:, 0:B_WIDTH]
    gate_i = gates[:, B_WIDTH:2 * B_WIDTH]
    log_a = (-LRU_C) * gate_r * _softplus(-lam)
    a = jnp.exp(log_a)
    mult = jnp.sqrt(-jnp.tanh(log_a) * (a * a + 1.0))
    return a, mult * gate_i * xc


def _lru_body(gate_br, xb, conv_prev, h0, tloc, tlen, vec, bias, wgates):
    rows = xb.shape[0]
    conv_b = vec[4:5, :]
    g_out = vec[6:7, :]
    xc = conv_b + vec[3:4, :] * xb
    for j in (1, 2, 3):
        tail = conv_prev if j == 3 else _shift_rows(conv_prev, rows - (3 - j))
        xc = xc + vec[3 - j:4 - j, :] * jnp.where(tloc >= j, _shift_rows(xb, j), tail)
    a, b = _lru_gates(xc, vec, bias, wgates)
    b = b + jnp.where(tloc == 0, a * h0, 0.0)
    d = 1
    while d < tlen:
        keep = tloc >= d
        a_s = jnp.where(keep, _shift_rows(a, d), 1.0)
        b_s = jnp.where(keep, _shift_rows(b, d), 0.0)
        b = a * b_s + b
        a = a * a_s
        d *= 2
    hs = b
    y = hs * _gelu_tanh(gate_br)
    return _rms(y, g_out), hs


def _lru_tm_kernel(cols_ref, vec_ref, bias_ref, wg_ref, y_ref, conv_ref, h_ref, x_scr, a_scr, b_scr, h_scr, *, nb):
    rows = a_scr.shape[0]
    hist = (CONV_WIDTH - 1) * nb

    @pl.when(pl.program_id(0) == 0)
    def _():
        x_scr[0:hist, :] = jnp.zeros((hist, B_WIDTH), F32)
        h_scr[...] = jnp.zeros_like(h_scr)

    vec = vec_ref[...]
    gate_br = jnp.concatenate([cols_ref[0], cols_ref[1]], axis=1)
    xb = jnp.concatenate([cols_ref[2], cols_ref[3]], axis=1)
    x_scr[hist:hist + rows, :] = xb
    xc = vec[4:5, :] + vec[3:4, :] * xb
    for j in range(1, CONV_WIDTH):
        xc = xc + vec[3 - j:4 - j, :] * x_scr[hist - j * nb:hist - j * nb + rows, :]
    a, b = _lru_gates(xc, vec, bias_ref[...], wg_ref[...])
    a_scr[...] = a
    b_scr[...] = b

    def step(i, h):
        rw = pl.ds(pl.multiple_of(i * nb, nb), nb)
        h = a_scr[rw, :] * h + b_scr[rw, :]
        b_scr[rw, :] = h
        return h

    h = lax.fori_loop(0, rows // nb, step, h_scr[...], unroll=8)
    h_scr[...] = h
    h_ref[...] = h
    tail = x_scr[rows:rows + hist, :]
    conv_ref[...] = tail
    x_scr[0:hist, :] = tail
    y = _rms(b_scr[...] * _gelu_tanh(gate_br), vec[6:7, :])
    for j, slab in enumerate(_lane_slabs(y)):
        y_ref[j] = slab


def _lru_param_specs(l):
    return [_layer_spec(l, (SUBLANES, B_WIDTH)), _layer_spec(l, (1, 2 * B_WIDTH)),
            _layer_spec(l, (B_WIDTH, 2 * B_WIDTH))]


def _lru_param_args(wts):
    return (wts["b_vec"], wts["b_bias"], wts["b_wgates"])


def _lru_tm(cols_b, nb, seq, wts, l):
    rows = SCAN_STEPS * nb
    hist = (CONV_WIDTH - 1) * nb
    slab = lambda t: (0, t, 0)
    fixed = lambda t: (0, 0)
    return pl.pallas_call(
        functools.partial(_lru_tm_kernel, nb=nb),
        grid=(seq // SCAN_STEPS,),
        in_specs=[pl.BlockSpec((2 * B_WIDTH // LANES, rows, LANES), slab)] + _lru_param_specs(l),
        out_specs=[pl.BlockSpec((B_WIDTH // LANES, rows, LANES), slab), pl.BlockSpec((hist, B_WIDTH), fixed),
                   pl.BlockSpec((nb, B_WIDTH), fixed)],
        out_shape=[jax.ShapeDtypeStruct((B_WIDTH // LANES, nb * seq, LANES), F32),
                   jax.ShapeDtypeStruct((hist, B_WIDTH), F32),
                   jax.ShapeDtypeStruct((nb, B_WIDTH), F32)],
        scratch_shapes=[pltpu.VMEM((hist + rows, B_WIDTH), F32), pltpu.VMEM((rows, B_WIDTH), F32),
                        pltpu.VMEM((rows, B_WIDTH), F32), pltpu.VMEM((nb, B_WIDTH), F32)],
        compiler_params=_params(1),
        name="lru_tm",
    )(cols_b, *_lru_param_args(wts))


def _lru_sample_kernel(*refs, steps, n_prev):
    cols_ref, conv0_ref, h0_ref, vec_ref, bias_ref, wg_ref = refs[0:6]
    prev_refs = refs[6:6 + 2 * n_prev]
    y_ref, conv_out_ref, h_out_ref = refs[6 + 2 * n_prev:]
    rows = cols_ref.shape[0]
    nseq = rows // steps
    gate_br = cols_ref[:, 0:B_WIDTH]
    xb = cols_ref[:, B_WIDTH:2 * B_WIDTH]
    tloc = _row_iota(xb.shape) & (steps - 1)
    h0 = jnp.broadcast_to(h0_ref[...], (nseq, steps, B_WIDTH)).reshape(rows, B_WIDTH)
    y, hs = _lru_body(gate_br, xb, conv0_ref[...], h0, tloc, steps, vec_ref[...], bias_ref[...], wg_ref[...])
    y_ref[...] = y
    _stack_previous(prev_refs[0::2], conv_out_ref)
    _stack_previous(prev_refs[1::2], h_out_ref)
    conv_out_ref[n_prev] = _last_steps(xb, nseq, steps, CONV_WIDTH - 1)
    h_out_ref[n_prev] = _last_steps(hs, nseq, steps, 1)


def _lru_sample(cols_b, conv0_rows, h0, prevs, nb, steps, wts, l):
    bb = SAMPLE_SEQS
    rows = bb * steps
    n_prev = len(prevs)
    blk = lambda i: (i, 0)
    seq3 = lambda i: (0, i, 0, 0)
    prev_specs, prev_args = [], []
    for cv, hh in prevs:
        prev_specs += [pl.BlockSpec((None, bb, CONV_WIDTH - 1, B_WIDTH), seq3),
                       pl.BlockSpec((None, bb, 1, B_WIDTH), seq3)]
        prev_args += [cv, hh]
    return pl.pallas_call(
        functools.partial(_lru_sample_kernel, steps=steps, n_prev=n_prev),
        grid=(nb // bb,),
        in_specs=[pl.BlockSpec((rows, 2 * B_WIDTH), blk),
                  pl.BlockSpec((None, rows, B_WIDTH), lambda i: (l, i, 0)),
                  pl.BlockSpec((None, bb, 1, B_WIDTH), lambda i: (l, i, 0, 0))]
                 + _lru_param_specs(l) + prev_specs,
        out_specs=[pl.BlockSpec((rows, B_WIDTH), blk),
                   pl.BlockSpec((n_prev + 1, bb, CONV_WIDTH - 1, B_WIDTH), seq3),
                   pl.BlockSpec((n_prev + 1, bb, 1, B_WIDTH), seq3)],
        out_shape=[jax.ShapeDtypeStruct((nb * steps, B_WIDTH), F32),
                   jax.ShapeDtypeStruct((n_prev + 1, nb, CONV_WIDTH - 1, B_WIDTH), F32),
                   jax.ShapeDtypeStruct((n_prev + 1, nb, 1, B_WIDTH), F32)],
        compiler_params=_params(1),
        name="lru_sample",
    )(cols_b, conv0_rows, h0, *_lru_param_args(wts), *prev_args)


def _s5_drive(u, lam, bbd):
    lr = lam[0:1, :]
    li = lam[1:2, :]
    dt = jnp.exp(lam[2:3, :])
    mag = jnp.exp(lr * dt)
    ar = mag * jnp.cos(li * dt)
    ai = mag * jnp.sin(li * dt)
    den = lr * lr + li * li
    cr = ((ar - 1.0) * lr + ai * li) / den
    ci = (ai * lr - (ar - 1.0) * li) / den
    ub = _bf(u)
    pb = _dot(ub, bbd[:, 0:S5_W])
    qb = _dot(ub, bbd[:, S5_W:2 * S5_W])
    return ar, ai, pb * cr - qb * ci, pb * ci + qb * cr


def _s5_readout(hr, hi, u, vec, cbd, wglu):
    s5_d = vec[0:1, :]
    b_glu = vec[1:2, :]
    g_out = vec[2:3, :]
    y = _dot(_bf(hr), cbd[0:S5_W, :]) + _dot(_bf(hi), cbd[S5_W:2 * S5_W, :]) + s5_d * u
    z = _gelu_tanh(y)
    out = z * _sigmoid(_dot(_bf(z), wglu) + b_glu)
    return _rms(out, g_out)


def _s5_body(u, h0r, h0i, tloc, tlen, lam, vec, bbd, cbd, wglu):
    ar, ai, hr, hi = _s5_drive(u, lam, bbd)
    first = tloc == 0
    hr = hr + jnp.where(first, ar * h0r - ai * h0i, 0.0)
    hi = hi + jnp.where(first, ar * h0i + ai * h0r, 0.0)
    pr, pi = ar, ai
    d = 1
    while d < tlen:
        keep = tloc >= d
        sr = jnp.where(keep, _shift_rows(hr, d), 0.0)
        si = jnp.where(keep, _shift_rows(hi, d), 0.0)
        hr, hi = hr + pr * sr - pi * si, hi + pr * si + pi * sr
        pr, pi = pr * pr - pi * pi, 2.0 * pr * pi
        d *= 2
    return _s5_readout(hr, hi, u, vec, cbd, wglu), hr, hi


def _s5_tm_kernel(u_ref, lam_ref, vec_ref, bbd_ref, cbd_ref, wglu_ref, y_ref, hr_ref, hi_ref,
                  re_scr, im_scr, sr_scr, si_scr, *, nb):
    rows = re_scr.shape[0]

    @pl.when(pl.program_id(0) == 0)
    def _():
        sr_scr[...] = jnp.zeros_like(sr_scr)
        si_scr[...] = jnp.zeros_like(si_scr)

    u = jnp.concatenate([u_ref[j] for j in range(C_WIDTH // LANES)], axis=1)
    ar, ai, dr, di = _s5_drive(u, lam_ref[...], bbd_ref[...])
    re_scr[...] = dr
    im_scr[...] = di
    ar = jnp.broadcast_to(ar, (nb, S5_W))
    ai = jnp.broadcast_to(ai, (nb, S5_W))

    def step(i, carry):
        hr, hi = carry
        rw = pl.ds(pl.multiple_of(i * nb, nb), nb)
        nr = ar * hr - ai * hi + re_scr[rw, :]
        ni = ar * hi + ai * hr + im_scr[rw, :]
        re_scr[rw, :] = nr
        im_scr[rw, :] = ni
        return nr, ni

    hr, hi = lax.fori_loop(0, rows // nb, step, (sr_scr[...], si_scr[...]), unroll=4)
    sr_scr[...] = hr
    si_scr[...] = hi
    hr_ref[...] = hr
    hi_ref[...] = hi
    y = _s5_readout(re_scr[...], im_scr[...], u, vec_ref[...], cbd_ref[...], wglu_ref[...])
    for j, slab in enumerate(_lane_slabs(y)):
        y_ref[j] = slab


def _s5_param_specs(l):
    return [_layer_spec(l, (SUBLANES, S5_W)), _layer_spec(l, (SUBLANES, C_WIDTH)),
            _layer_spec(l, (C_WIDTH, 2 * S5_W)), _layer_spec(l, (2 * S5_W, C_WIDTH)),
            _layer_spec(l, (C_WIDTH, C_WIDTH))]


def _s5_param_args(wts):
    return (wts["c_lam"], wts["c_vec"], wts["c_bbd"], wts["c_cbd"], wts["c_wglu"])


def _s5_tm(cols_c, nb, seq, wts, l):
    rows = SCAN_STEPS * nb
    slab = lambda t: (0, t, 0)
    fixed = lambda t: (0, 0)
    n_slabs = C_WIDTH // LANES
    return pl.pallas_call(
        functools.partial(_s5_tm_kernel, nb=nb),
        grid=(seq // SCAN_STEPS,),
        in_specs=[pl.BlockSpec((n_slabs, rows, LANES), slab)] + _s5_param_specs(l),
        out_specs=[pl.BlockSpec((n_slabs, rows, LANES), slab), pl.BlockSpec((nb, S5_W), fixed),
                   pl.BlockSpec((nb, S5_W), fixed)],
        out_shape=[jax.ShapeDtypeStruct((n_slabs, nb * seq, LANES), F32),
                   jax.ShapeDtypeStruct((nb, S5_W), F32), jax.ShapeDtypeStruct((nb, S5_W), F32)],
        scratch_shapes=[pltpu.VMEM((rows, S5_W), F32), pltpu.VMEM((rows, S5_W), F32),
                        pltpu.VMEM((nb, S5_W), F32), pltpu.VMEM((nb, S5_W), F32)],
        compiler_params=_params(1),
        name="s5_tm",
    )(cols_c, *_s5_param_args(wts))


def _s5_sample_kernel(*refs, steps, n_prev):
    u_ref, h0r_ref, h0i_ref, lam_ref, vec_ref, bbd_ref, cbd_ref, wglu_ref = refs[0:8]
    prev_refs = refs[8:8 + 2 * n_prev]
    y_ref, hr_out_ref, hi_out_ref = refs[8 + 2 * n_prev:]
    rows = u_ref.shape[0]
    nseq = rows // steps
    tloc = _row_iota((rows, S5_W)) & (steps - 1)
    h0r = jnp.broadcast_to(h0r_ref[...], (nseq, steps, S5_W)).reshape(rows, S5_W)
    h0i = jnp.broadcast_to(h0i_ref[...], (nseq, steps, S5_W)).reshape(rows, S5_W)
    y, hr, hi = _s5_body(u_ref[...], h0r, h0i, tloc, steps, lam_ref[...], vec_ref[...], bbd_ref[...],
                         cbd_ref[...], wglu_ref[...])
    y_ref[...] = y
    _stack_previous(prev_refs[0::2], hr_out_ref)
    _stack_previous(prev_refs[1::2], hi_out_ref)
    hr_out_ref[n_prev] = _last_steps(hr, nseq, steps, 1)
    hi_out_ref[n_prev] = _last_steps(hi, nseq, steps, 1)


def _s5_sample(cols_c, h0r, h0i, prevs, nb, steps, wts, l):
    bb = SAMPLE_SEQS
    rows = bb * steps
    n_prev = len(prevs)
    blk = lambda i: (i, 0)
    seq3 = lambda i: (0, i, 0, 0)
    st = pl.BlockSpec((None, bb, 1, S5_W), lambda i: (l, i, 0, 0))
    prev_specs, prev_args = [], []
    for pr, pi in prevs:
        prev_specs += [pl.BlockSpec((None, bb, 1, S5_W), seq3)] * 2
        prev_args += [pr, pi]
    stacked = pl.BlockSpec((n_prev + 1, bb, 1, S5_W), seq3)
    return pl.pallas_call(
        functools.partial(_s5_sample_kernel, steps=steps, n_prev=n_prev),
        grid=(nb // bb,),
        in_specs=[pl.BlockSpec((rows, C_WIDTH), blk), st, st] + _s5_param_specs(l) + prev_specs,
        out_specs=[pl.BlockSpec((rows, C_WIDTH), blk), stacked, stacked],
        out_shape=[jax.ShapeDtypeStruct((nb * steps, C_WIDTH), F32),
                   jax.ShapeDtypeStruct((n_prev + 1, nb, 1, S5_W), F32),
                   jax.ShapeDtypeStruct((n_prev + 1, nb, 1, S5_W), F32)],
        compiler_params=_params(1),
        name="s5_sample",
    )(cols_c, h0r, h0i, *_s5_param_args(wts), *prev_args)


def _rows8(rows, width):
    m = jnp.stack([r.reshape(r.shape[0], width) for r in rows], axis=1)
    return jnp.pad(m, ((0, 0), (0, SUBLANES - m.shape[1]), (0, 0)))


def _stacked_weights(w):
    depth = w["g_mix"].shape[0]
    eye_b = jnp.eye(B_BLOCKS, dtype=F32)
    eye_g = jnp.eye(S5_GROUPS, dtype=F32)
    bd4 = lambda m: jnp.einsum("lnde,nm->lndme", m, eye_b).reshape(depth, B_WIDTH, B_WIDTH)
    zeros_lora = jnp.zeros((depth, LANES - 64, A_WIDTH), F32)
    b_in = lambda m: jnp.einsum("lgpc,gh->lgchp", m, eye_g).reshape(depth, C_WIDTH, S5_W)
    c_out = lambda m: jnp.einsum("lgcp,gh->lgphc", m, eye_g).reshape(depth, S5_W, C_WIDTH)
    g_final = jnp.broadcast_to(w["g_final"][None], (depth, D_MODEL))
    return dict(
        g_mix=w["g_mix"][:, None, :],
        w_in=_bf(w["w_in"]),
        mu=w["mu_a"][:, None, :],
        a_vec=_rows8([w["w0"], w["a0"], w["k_k"], w["k_a"], w["r_k"], w["lnx_w"], w["lnx_b"]], A_WIDTH),
        wdec=_bf(jnp.concatenate([w["w_dec2"], zeros_lora], axis=1)),
        wa=_bf(jnp.concatenate([zeros_lora, w["w_a2"]], axis=1)),
        wg=_bf(w["w_g2"]),
        b_vec=_rows8([w["conv_w"][:, 0], w["conv_w"][:, 1], w["conv_w"][:, 2], w["conv_w"][:, 3],
                      w["conv_b"], w["lru_lambda"], w["g_out_b"]], B_WIDTH),
        b_bias=jnp.concatenate([w["b_rg"], w["b_ig"]], axis=1)[:, None, :],
        b_wgates=_bf(jnp.concatenate([bd4(w["w_rg"]), bd4(w["w_ig"])], axis=2)),
        c_lam=_rows8([w["s5_lam_re"], w["s5_lam_im"], jnp.repeat(w["s5_log_dt"], S5_STATE, axis=1)], S5_W),
        c_vec=_rows8([w["s5_d"], w["b_glu"], w["g_out_c"]], C_WIDTH),
        c_bbd=_bf(jnp.concatenate([b_in(w["s5_b_re"]), b_in(w["s5_b_im"])], axis=2)),
        c_cbd=_bf(jnp.concatenate([c_out(w["s5_c_re"]), -c_out(w["s5_c_im"])], axis=1)),
        c_wglu=_bf(w["w_glu"]),
        w_out=_bf(w["w_out"]),
        p_vec=_rows8([w["g_ffn"], w["g_ple"], g_final], D_MODEL),
        w_up=_bf(w["w_ffn_up"]),
        w_down=_bf(w["w_ffn_down"]),
        w_ple=_bf(w["w_ple"]),
        w_gate=_bf(w["w_ple_gate"]),
    )


def _run_prompt(x, p, wts, depth):
    nb, seq, _ = x.shape
    h = x.reshape(nb * seq, D_MODEL)
    p = p.reshape(depth, nb * seq, PLE_DIM)
    outs = []
    for l in range(depth):
        cols_a, cols_b, cols_c = _proj_in_tm(h, wts["g_mix"], wts["w_in"], l, nb, seq)
        ya, wkv, last = _rwkv_prompt(cols_a, nb, seq, wts, l)
        yb, conv, lru = _lru_tm(cols_b, nb, seq, wts, l)
        yc, s5r, s5i = _s5_tm(cols_c, nb, seq, wts, l)
        h = _post(h, ya, yb, yc, p, wts, l, l == depth - 1, nb_tm=nb)
        outs.append((last[:, SUBLANES - 1], wkv, conv, lru, s5r, s5i))
    shift, wkv, conv, lru, s5r, s5i = (jnp.stack([o[j] for o in outs], axis=0) for j in range(6))
    conv = jnp.swapaxes(conv.reshape(depth, CONV_WIDTH - 1, nb, B_WIDTH), 1, 2)
    s5_shape = (depth, nb, S5_GROUPS, S5_STATE)
    return h.reshape(nb, seq, D_MODEL), (shift, wkv, conv, lru, s5r.reshape(s5_shape), s5i.reshape(s5_shape))


def _run_sample(x, p, states, wts, depth):
    nb, steps, _ = x.shape
    st_shift, st_wkv, st_conv, st_lru, st_s5r, st_s5i = states
    h = x.reshape(nb * steps, D_MODEL)
    p = p.reshape(depth, nb * steps, PLE_DIM)
    shift0 = st_shift[:, :, None, :]
    conv0_rows = jnp.pad(st_conv, ((0, 0), (0, 0), (0, steps - (CONV_WIDTH - 1)), (0, 0))
                         ).reshape(depth, nb * steps, B_WIDTH)
    lru0 = st_lru[:, :, None, :]
    s5r0 = st_s5r.reshape(depth, nb, 1, S5_W)
    s5i0 = st_s5i.reshape(depth, nb, 1, S5_W)
    prev_a, prev_b, prev_c = [], [], []
    for l in range(depth):
        last = l == depth - 1
        cols_a, cols_b, cols_c = _proj_in(h, wts["g_mix"], wts["w_in"], l)
        ya, shift, wkv = _rwkv_sample(cols_a, shift0, st_wkv, prev_a if last else [], nb, steps, wts, l)
        yb, conv, lru = _lru_sample(cols_b, conv0_rows, lru0, prev_b if last else [], nb, steps, wts, l)
        yc, s5r, s5i = _s5_sample(cols_c, s5r0, s5i0, prev_c if last else [], nb, steps, wts, l)
        h = _post(h, ya, yb, yc, p, wts, l, last)
        prev_a.append((shift, wkv))
        prev_b.append((conv, lru))
        prev_c.append((s5r, s5i))
    s5_shape = (depth, nb, S5_GROUPS, S5_STATE)
    new_states = (shift.reshape(depth, nb, A_COLS), wkv, conv, lru.reshape(depth, nb, B_WIDTH),
                  s5r.reshape(s5_shape), s5i.reshape(s5_shape))
    return h.reshape(nb, steps, D_MODEL), new_states


def kernel(x_prompt, x_sample, p_prompt, p_sample, state_shift, state_wkv, state_conv, state_lru, state_s5_re, state_s5_im, g_mix, w_in, mu_a, w0, w_dec2, a0, w_a2, w_g2, k_k, k_a, r_k, lnx_w, lnx_b, conv_w, conv_b, w_rg, b_rg, w_ig, b_ig, lru_lambda, g_out_b, s5_lam_re, s5_lam_im, s5_log_dt, s5_b_re, s5_b_im, s5_c_re, s5_c_im, s5_d, w_glu, b_glu, g_out_c, w_out, g_ffn, w_ffn_up, w_ffn_down, g_ple, w_ple, w_ple_gate, g_final):
    w = dict(g_mix=g_mix, w_in=w_in, mu_a=mu_a, w0=w0, w_dec2=w_dec2, a0=a0, w_a2=w_a2, w_g2=w_g2, k_k=k_k,
             k_a=k_a, r_k=r_k, lnx_w=lnx_w, lnx_b=lnx_b, conv_w=conv_w, conv_b=conv_b, w_rg=w_rg, b_rg=b_rg,
             w_ig=w_ig, b_ig=b_ig, lru_lambda=lru_lambda, g_out_b=g_out_b, s5_lam_re=s5_lam_re,
             s5_lam_im=s5_lam_im, s5_log_dt=s5_log_dt, s5_b_re=s5_b_re, s5_b_im=s5_b_im, s5_c_re=s5_c_re,
             s5_c_im=s5_c_im, s5_d=s5_d, w_glu=w_glu, b_glu=b_glu, g_out_c=g_out_c, w_out=w_out, g_ffn=g_ffn,
             w_ffn_up=w_ffn_up, w_ffn_down=w_ffn_down, g_ple=g_ple, w_ple=w_ple, w_ple_gate=w_ple_gate,
             g_final=g_final)
    depth = g_mix.shape[0]
    wts = _stacked_weights(w)
    y_prompt, new_p = _run_prompt(x_prompt, p_prompt, wts, depth)
    y_sample, new_s = _run_sample(x_sample, p_sample,
                                  (state_shift, state_wkv, state_conv, state_lru, state_s5_re, state_s5_im),
                                  wts, depth)
    return (y_prompt, y_sample) + new_p + new_s
```

```python
import functools
import math

import jax
import jax.numpy as jnp
from jax import lax
from jax.experimental import pallas as pl
from jax.experimental.pallas import tpu as pltpu

F32 = jnp.float32
BF16 = jnp.bfloat16

D_MODEL = 1024
A_WIDTH = 512
A_HEADS = 8
A_HEAD_DIM = 64
A_COLS = 1792
B_WIDTH = 256
B_BLOCKS = 4
CONV_WIDTH = 4
C_WIDTH = 256
S5_GROUPS = 16
S5_GROUP_CH = 16
S5_STATE = 64
S5_W = S5_GROUPS * S5_STATE
IN_COLS = A_COLS + 2 * B_WIDTH + C_WIDTH
D_FF = 2816
PLE_DIM = 256
LRU_C = 8.0
RMS_EPS = 1e-6
GN_EPS = 64e-5

LANES = 128
SUBLANES = 8
VMEM_LIMIT_BYTES = 56 * 1024 * 1024

TOKEN_TILE = 512
FFN_CHUNK = 256
RWKV_CHUNK = 64
RWKV_BLOCK = 256
SCAN_STEPS = 128
SAMPLE_SEQS = 8


def _layer_spec(l, shape):
    nd = len(shape)
    return pl.BlockSpec((None,) + tuple(shape), lambda *_: (l,) + (0,) * nd, pipeline_mode=pl.Buffered(1))


def _params(n_axes):
    return pltpu.CompilerParams(dimension_semantics=("arbitrary",) * n_axes,
                                vmem_limit_bytes=VMEM_LIMIT_BYTES)


def _dot(a, b):
    return jnp.dot(a, b, preferred_element_type=F32)


def _dot_nt(a, b):
    return lax.dot_general(a, b, (((1,), (1,)), ((), ())), preferred_element_type=F32)


def _dot_tn(a, b):
    return lax.dot_general(a, b, (((0,), (0,)), ((), ())), preferred_element_type=F32)


def _bf(x):
    return x.astype(BF16)


def _rms(x, g):
    inv = lax.rsqrt(jnp.mean(x * x, axis=-1, keepdims=True) + RMS_EPS)
    return x * inv * g


def _sigmoid(x):
    return 1.0 / (1.0 + jnp.exp(-x))


def _softplus(x):
    return jnp.maximum(x, 0.0) + jnp.log(1.0 + jnp.exp(-jnp.abs(x)))


def _gelu_tanh(x):
    c = math.sqrt(2.0 / math.pi)
    return 0.5 * x * (1.0 + jnp.tanh(c * (x + 0.044715 * (x * x * x))))


def _row_iota(shape):
    return lax.broadcasted_iota(jnp.int32, shape, 0)


def _shift_rows(x, d):
    return pltpu.roll(x, d, axis=0)


def _lane_slabs(x):
    return [x[:, j * LANES:(j + 1) * LANES] for j in range(x.shape[1] // LANES)]


def _last_steps(x, nseq, steps, n):
    return x.reshape(nseq, steps, x.shape[1])[:, steps - n:steps, :]


def _stack_previous(prev_refs, out_ref):
    for j, ref in enumerate(prev_refs):
        out_ref[j] = ref[...]


def _proj_in_kernel(h_ref, g_ref, w_ref, a_ref, b_ref, c_ref):
    xn = _bf(_rms(h_ref[...], g_ref[...]))
    a_ref[...] = _dot(xn, w_ref[:, 0:A_COLS])
    b_ref[...] = _dot(xn, w_ref[:, A_COLS:A_COLS + 2 * B_WIDTH])
    c_ref[...] = _dot(xn, w_ref[:, A_COLS + 2 * B_WIDTH:IN_COLS])


def _proj_in(h, g, w_bf, l):
    n = h.shape[0]
    tm = TOKEN_TILE
    row = lambda i: (i, 0)
    return pl.pallas_call(
        _proj_in_kernel,
        grid=(n // tm,),
        in_specs=[pl.BlockSpec((tm, D_MODEL), row), _layer_spec(l, (1, D_MODEL)),
                  _layer_spec(l, (D_MODEL, IN_COLS))],
        out_specs=[pl.BlockSpec((tm, A_COLS), row), pl.BlockSpec((tm, 2 * B_WIDTH), row),
                   pl.BlockSpec((tm, C_WIDTH), row)],
        out_shape=[jax.ShapeDtypeStruct((n, A_COLS), F32), jax.ShapeDtypeStruct((n, 2 * B_WIDTH), F32),
                   jax.ShapeDtypeStruct((n, C_WIDTH), F32)],
        compiler_params=_params(1),
        name="proj_in",
    )(h, g, w_bf)


def _proj_in_tm_kernel(h_ref, g_ref, w_ref, a_ref, b_ref, c_ref, *, nb):
    b = pl.program_id(1)
    tm = h_ref.shape[0]
    xn = _bf(_rms(h_ref[...], g_ref[...]))
    a_ref[...] = _dot(xn, w_ref[:, 0:A_COLS])
    rows_of_b = pl.ds(b, tm, stride=nb)
    for j, slab in enumerate(_lane_slabs(_dot(xn, w_ref[:, A_COLS:A_COLS + 2 * B_WIDTH]))):
        b_ref[j, rows_of_b, :] = slab
    for j, slab in enumerate(_lane_slabs(_dot(xn, w_ref[:, A_COLS + 2 * B_WIDTH:IN_COLS]))):
        c_ref[j, rows_of_b, :] = slab


def _proj_in_tm(h, g, w_bf, l, nb, seq):
    tm = TOKEN_TILE
    nt = seq // tm
    row = lambda i, b: (b * nt + i, 0)
    slab = lambda i, b: (0, i, 0)
    nb_slabs, nc_slabs = 2 * B_WIDTH // LANES, C_WIDTH // LANES
    return pl.pallas_call(
        functools.partial(_proj_in_tm_kernel, nb=nb),
        grid=(nt, nb),
        in_specs=[pl.BlockSpec((tm, D_MODEL), row), _layer_spec(l, (1, D_MODEL)),
                  _layer_spec(l, (D_MODEL, IN_COLS))],
        out_specs=[pl.BlockSpec((tm, A_COLS), row), pl.BlockSpec((nb_slabs, tm * nb, LANES), slab),
                   pl.BlockSpec((nc_slabs, tm * nb, LANES), slab)],
        out_shape=[jax.ShapeDtypeStruct((nb * seq, A_COLS), F32),
                   jax.ShapeDtypeStruct((nb_slabs, nb * seq, LANES), F32),
                   jax.ShapeDtypeStruct((nc_slabs, nb * seq, LANES), F32)],
        compiler_params=_params(2),
        name="proj_in_tm",
    )(h, g, w_bf)


def _post_kernel(h_ref, ya_ref, yb_ref, yc_ref, p_ref, wo_ref, vec_ref, wup_ref, wdn_ref, wple_ref,
                 wgate_ref, o_ref, act_scr, *, final, nb_tm):
    g_ffn = vec_ref[0:1, :]
    g_ple = vec_ref[1:2, :]
    g_final = vec_ref[2:3, :]
    if nb_tm:
        rows_of_b = pl.ds(pl.program_id(1), h_ref.shape[0], stride=nb_tm)
        yb = jnp.concatenate([yb_ref[j, rows_of_b, :] for j in range(B_WIDTH // LANES)], axis=1)
        yc = jnp.concatenate([yc_ref[j, rows_of_b, :] for j in range(C_WIDTH // LANES)], axis=1)
    else:
        yb, yc = yb_ref[...], yc_ref[...]
    h1 = (h_ref[...] + _dot(_bf(ya_ref[...]), wo_ref[0:A_WIDTH, :])
          + _dot(_bf(yb), wo_ref[A_WIDTH:A_WIDTH + B_WIDTH, :])
          + _dot(_bf(yc), wo_ref[A_WIDTH + B_WIDTH:D_MODEL, :]))
    xf = _bf(_rms(h1, g_ffn))
    for c0 in range(0, D_FF, FFN_CHUNK):
        gate = _dot(xf, wup_ref[:, c0:c0 + FFN_CHUNK])
        up = _dot(xf, wup_ref[:, D_FF + c0:D_FF + c0 + FFN_CHUNK])
        act_scr[:, c0:c0 + FFN_CHUNK] = _bf(gate * _sigmoid(gate) * up)
    h2 = h1 + _dot(act_scr[...], wdn_ref[...])
    ple = _dot(_bf(p_ref[...]), wple_ref[...])
    gate = _sigmoid(_dot(_bf(_rms(h2, g_ple)), wgate_ref[...]))
    h3 = h2 + ple * gate
    if final:
        h3 = _rms(h3, g_final)
    o_ref[...] = h3


def _post(h, ya, yb, yc, p, wts, l, final, nb_tm=0):
    n = h.shape[0]
    tm = TOKEN_TILE
    if nb_tm:
        nt = n // (nb_tm * tm)
        grid = (nt, nb_tm)
        row = lambda i, b: (b * nt + i, 0)
        prow = lambda i, b: (l, b * nt + i, 0)
        slab = lambda i, b: (0, i, 0)
        yb_spec = pl.BlockSpec((B_WIDTH // LANES, tm * nb_tm, LANES), slab, pipeline_mode=pl.Buffered(1))
        yc_spec = pl.BlockSpec((C_WIDTH // LANES, tm * nb_tm, LANES), slab, pipeline_mode=pl.Buffered(1))
    else:
        grid = (n // tm,)
        row = lambda i: (i, 0)
        prow = lambda i: (l, i, 0)
        yb_spec = pl.BlockSpec((tm, B_WIDTH), row)
        yc_spec = pl.BlockSpec((tm, C_WIDTH), row)
    return pl.pallas_call(
        functools.partial(_post_kernel, final=final, nb_tm=nb_tm),
        grid=grid,
        in_specs=[pl.BlockSpec((tm, D_MODEL), row), pl.BlockSpec((tm, A_WIDTH), row),
                  yb_spec, yc_spec,
                  pl.BlockSpec((None, tm, PLE_DIM), prow),
                  _layer_spec(l, (D_MODEL, D_MODEL)), _layer_spec(l, (SUBLANES, D_MODEL)),
                  _layer_spec(l, (D_MODEL, 2 * D_FF)), _layer_spec(l, (D_FF, D_MODEL)),
                  _layer_spec(l, (PLE_DIM, D_MODEL)), _layer_spec(l, (D_MODEL, D_MODEL))],
        out_specs=pl.BlockSpec((tm, D_MODEL), row),
        out_shape=jax.ShapeDtypeStruct((n, D_MODEL), F32),
        scratch_shapes=[pltpu.VMEM((tm, D_FF), BF16)],
        compiler_params=_params(len(grid)),
        name="post",
    )(h, ya, yb, yc, p, wts["w_out"], wts["p_vec"], wts["w_up"], wts["w_down"], wts["w_ple"], wts["w_gate"])


def _head_sum(x):
    first = lax.broadcasted_iota(jnp.int32, (x.shape[0], LANES), 1) < A_HEAD_DIM
    outs = []
    for xp in _lane_slabs(x):
        s0 = jnp.sum(jnp.where(first, xp, 0.0), axis=-1, keepdims=True)
        s1 = jnp.sum(jnp.where(first, 0.0, xp), axis=-1, keepdims=True)
        outs.append(jnp.where(first, s0, s1))
    return jnp.concatenate(outs, axis=1)


def _rwkv_prep(cols, prev, mu, vec, wdec, wa, wg, chunk):
    rows = cols.shape[0]
    xs = cols + (prev - cols) * mu
    r = xs[:, 0:A_WIDTH]
    k = xs[:, A_WIDTH:2 * A_WIDTH]
    v = xs[:, 2 * A_WIDTH:3 * A_WIDTH]
    xwa = xs[:, 3 * A_WIDTH:3 * A_WIDTH + LANES]
    xg = xs[:, 3 * A_WIDTH + LANES:A_COLS]
    w0, a0, k_k, k_a, r_k = (vec[i:i + 1, :] for i in range(5))
    z = w0 + _dot(_bf(jnp.tanh(xwa)), wdec)
    log_decay = -jnp.exp(-_softplus(-z) - 0.5)
    a = _sigmoid(a0 + _dot(_bf(xwa), wa))
    g = _dot(_bf(_sigmoid(xg)), wg)
    kk_raw = k * k_k
    kk = kk_raw * lax.rsqrt(jnp.maximum(_head_sum(kk_raw * kk_raw), 1e-24))
    k_mod = k * (1.0 + (a - 1.0) * k_a)
    bonus = _head_sum(r * k_mod * r_k) * v
    ri = _row_iota((rows, rows))
    ci = lax.broadcasted_iota(jnp.int32, (rows, rows), 1)
    same_chunk = (ci & (-chunk)) == (ri & (-chunk))
    tri = _bf(jnp.where((ci <= ri) & same_chunk, 1.0, 0.0))
    h1 = _bf(log_decay)
    r1 = log_decay - h1.astype(F32)
    h2 = _bf(r1)
    h3 = _bf(r1 - h2.astype(F32))
    cum = _dot(tri, h1) + _dot(tri, h2) + _dot(tri, h3)
    gam = jnp.exp(cum)
    ginv = jnp.exp(-cum)
    gprev = jnp.exp(cum - log_decay)
    return dict(rt=r * gam, kap=kk * gprev, bet=kk * a * ginv, kt=k_mod * ginv, v=v, gam=gam,
                bonus=bonus, g=g)


def _stack_heads(x):
    first = lax.broadcasted_iota(jnp.int32, x.shape, 1) < A_HEAD_DIM
    return _bf(jnp.concatenate([jnp.where(first, x, 0.0), jnp.where(first, 0.0, x)], axis=0))


def _rwkv_local(ops):
    c = ops[0]["kap"].shape[0]
    c2 = 2 * c
    ri = _row_iota((c, c2))
    ci = lax.broadcasted_iota(jnp.int32, (c, c2), 1) & (c - 1)
    strict = ci < ri
    incl = ci <= ri
    eye = jnp.where(ri == ci, 1.0, 0.0)
    same_head = (_row_iota((c2, c2)) & c) == (lax.broadcasted_iota(jnp.int32, (c2, c2), 1) & c)
    merged = c2 % LANES == 0

    def blockdiag(m):
        return _bf(jnp.where(same_head, jnp.concatenate([m, m], axis=0), 0.0))

    st = []
    for o in ops:
        s = {k + "2": _stack_heads(o[k]) for k in ("kap", "bet", "kt", "v")}
        s["rt"] = o["rt"]
        lhs = _bf(jnp.concatenate([o["kap"], o["rt"]], axis=0))
        if merged:
            g = _dot_nt(lhs, jnp.concatenate([s["bet2"], s["kt2"]], axis=0))
            a_b, a_k, a_rb, a_rk = g[0:c, 0:c2], g[0:c, c2:2 * c2], g[c:c2, 0:c2], g[c:c2, c2:2 * c2]
        else:
            gb, gk = _dot_nt(lhs, s["bet2"]), _dot_nt(lhs, s["kt2"])
            a_b, a_k, a_rb, a_rk = gb[0:c], gk[0:c], gb[c:c2], gk[c:c2]
        s["a_b"] = jnp.where(strict, a_b, 0.0)
        s["a_k"] = _bf(jnp.where(strict, a_k, 0.0))
        s["a_rb"] = _bf(jnp.where(incl, a_rb, 0.0))
        s["a_rk"] = _bf(jnp.where(incl, a_rk, 0.0))
        st.append(s)
    for s in st:
        s["t"] = eye - s["a_b"]
        s["lp"] = s["a_b"]
        s["akv"] = _dot(s["a_k"], s["v2"])
    n = 2
    while n < c:
        for s in st:
            s["lp"] = _dot(_bf(s["lp"]), blockdiag(s["lp"]))
        for s in st:
            s["t"] = s["t"] + _dot(_bf(s["t"]), blockdiag(s["lp"]))
        n *= 2
    for s in st:
        wu = _dot(_bf(s["t"]), jnp.concatenate([s["kap2"], _stack_heads(s["akv"])], axis=1))
        s["w"] = -wu[:, 0:LANES]
        s["u0"] = -wu[:, LANES:2 * LANES]
    return st


def _rwkv_state(st, states, g_ends):
    c = st[0]["rt"].shape[0]
    merged = (2 * c) % LANES == 0
    xs = []
    for s, state in zip(st, states):
        xs.append(_dot_nt(_bf(jnp.concatenate([s["w"], s["rt"]], axis=0)), _bf(state)))
    u2s = [_stack_heads(x[0:c] + s["u0"]) for x, s in zip(xs, st)]
    ys, new_states = [], []
    for s, x, u2 in zip(st, xs, u2s):
        if merged:
            y = x[c:2 * c] + _dot(jnp.concatenate([s["a_rb"], s["a_rk"]], axis=1),
                                  jnp.concatenate([u2, s["v2"]], axis=0))
        else:
            y = x[c:2 * c] + _dot(s["a_rb"], u2) + _dot(s["a_rk"], s["v2"])
        ys.append(y)
    for s, u2, state, g_end in zip(st, u2s, states, g_ends):
        ds = _dot_tn(jnp.concatenate([u2, s["v2"]], axis=0), jnp.concatenate([s["bet2"], s["kt2"]], axis=0))
        new_states.append((state + ds) * g_end)
    return ys, new_states


def _rwkv_finish(y, bonus, g, vec):
    lnx_w = vec[5:6, :]
    lnx_b = vec[6:7, :]
    inv_n = 1.0 / A_HEAD_DIM
    mu = _head_sum(y) * inv_n
    d = y - mu
    var = _head_sum(d * d) * inv_n
    yn = d * lax.rsqrt(var + GN_EPS) * lnx_w + lnx_b
    return (yn + bonus) * g


def _pair_state(s_ref, idx, p):
    z = jnp.zeros((A_HEAD_DIM, A_HEAD_DIM), F32)
    top = jnp.concatenate([s_ref[idx, 2 * p], z], axis=1)
    bot = jnp.concatenate([z, s_ref[idx, 2 * p + 1]], axis=1)
    return jnp.concatenate([top, bot], axis=0)


def _rwkv_prompt_kernel(cols_ref, mu_ref, vec_ref, wdec_ref, wa_ref, wg_ref,
                        y_ref, wkv_ref, last_ref, s_scr, prev_scr):
    t = pl.program_id(1)
    rows = cols_ref.shape[0]
    chunk = RWKV_CHUNK

    @pl.when(t == 0)
    def _():
        s_scr[...] = jnp.zeros_like(s_scr)
        prev_scr[...] = jnp.zeros_like(prev_scr)

    cols = cols_ref[...]
    first_row = _row_iota(cols.shape) == 0
    prev = jnp.where(first_row, prev_scr[SUBLANES - 1:SUBLANES, :], _shift_rows(cols, 1))
    q = _rwkv_prep(cols, prev, mu_ref[...], vec_ref[...], wdec_ref[...], wa_ref[...], wg_ref[...], chunk)
    n_pairs = A_HEADS // 2
    lanes = [slice(p * LANES, (p + 1) * LANES) for p in range(n_pairs)]
    starts = list(range(0, rows, chunk))
    ops = [{k: q[k][c0:c0 + chunk, ln] for k in ("kap", "rt", "bet", "kt", "v")}
           for c0 in starts for ln in lanes]
    st = _rwkv_local(ops)
    states = [s_scr[p] for p in range(n_pairs)]
    y_rows = []
    for i, c0 in enumerate(starts):
        g_ends = [q["gam"][c0 + chunk - 1:c0 + chunk, ln] for ln in lanes]
        ys, states = _rwkv_state(st[i * n_pairs:(i + 1) * n_pairs], states, g_ends)
        y_rows.append(jnp.concatenate(ys, axis=1))
    for p in range(n_pairs):
        s_scr[p] = states[p]
        wkv_ref[0, 2 * p] = states[p][0:A_HEAD_DIM, 0:A_HEAD_DIM]
        wkv_ref[0, 2 * p + 1] = states[p][A_HEAD_DIM:LANES, A_HEAD_DIM:LANES]
    y = jnp.concatenate(y_rows, axis=0) if len(y_rows) > 1 else y_rows[0]
    y_ref[...] = _rwkv_finish(y, q["bonus"], q["g"], vec_ref[...])
    tail = cols[rows - SUBLANES:rows, :]
    prev_scr[...] = tail
    last_ref[0] = tail


def _rwkv_param_specs(l):
    return [_layer_spec(l, (1, A_COLS)), _layer_spec(l, (SUBLANES, A_WIDTH)), _layer_spec(l, (LANES, A_WIDTH)),
            _layer_spec(l, (LANES, A_WIDTH)), _layer_spec(l, (LANES, A_WIDTH))]


def _rwkv_param_args(wts):
    return (wts["mu"], wts["a_vec"], wts["wdec"], wts["wa"], wts["wg"])


def _rwkv_prompt(cols_a, nb, seq, wts, l):
    tb = RWKV_BLOCK
    nt = seq // tb
    blk = lambda b, t: (b * nt + t, 0)
    per_b3 = lambda b, t: (b, 0, 0)
    return pl.pallas_call(
        _rwkv_prompt_kernel,
        grid=(nb, nt),
        in_specs=[pl.BlockSpec((tb, A_COLS), blk)] + _rwkv_param_specs(l),
        out_specs=[pl.BlockSpec((tb, A_WIDTH), blk),
                   pl.BlockSpec((1, A_HEADS, A_HEAD_DIM, A_HEAD_DIM), lambda b, t: (b, 0, 0, 0)),
                   pl.BlockSpec((1, SUBLANES, A_COLS), per_b3)],
        out_shape=[jax.ShapeDtypeStruct((nb * seq, A_WIDTH), F32),
                   jax.ShapeDtypeStruct((nb, A_HEADS, A_HEAD_DIM, A_HEAD_DIM), F32),
                   jax.ShapeDtypeStruct((nb, SUBLANES, A_COLS), F32)],
        scratch_shapes=[pltpu.VMEM((A_HEADS // 2, LANES, LANES), F32), pltpu.VMEM((SUBLANES, A_COLS), F32)],
        compiler_params=_params(2),
        name="rwkv_prompt",
    )(cols_a, *_rwkv_param_args(wts))


def _rwkv_sample_kernel(*refs, steps, n_prev):
    cols_ref, shift_ref, wkv0_ref, mu_ref, vec_ref, wdec_ref, wa_ref, wg_ref = refs[0:8]
    prev_refs = refs[8:8 + 2 * n_prev]
    y_ref, shift_out_ref, wkv_out_ref = refs[8 + 2 * n_prev:]
    rows = cols_ref.shape[0]
    nseq = rows // steps
    cols = cols_ref[...]
    tloc = _row_iota(cols.shape) & (steps - 1)
    shift0 = jnp.broadcast_to(shift_ref[...], (nseq, steps, A_COLS)).reshape(rows, A_COLS)
    prev = jnp.where(tloc == 0, shift0, _shift_rows(cols, 1))
    q = _rwkv_prep(cols, prev, mu_ref[...], vec_ref[...], wdec_ref[...], wa_ref[...], wg_ref[...], steps)
    n_pairs = A_HEADS // 2
    lanes = [slice(p * LANES, (p + 1) * LANES) for p in range(n_pairs)]
    chains = [(s, p) for s in range(nseq) for p in range(n_pairs)]
    ops = [{k: q[k][s * steps:(s + 1) * steps, lanes[p]] for k in ("kap", "rt", "bet", "kt", "v")}
           for s, p in chains]
    states = [_pair_state(wkv0_ref, s, p) for s, p in chains]
    g_ends = [q["gam"][(s + 1) * steps - 1:(s + 1) * steps, lanes[p]] for s, p in chains]
    ys, new_states = _rwkv_state(_rwkv_local(ops), states, g_ends)
    _stack_previous(prev_refs[0::2], shift_out_ref)
    _stack_previous(prev_refs[1::2], wkv_out_ref)
    for (s, p), s_new in zip(chains, new_states):
        wkv_out_ref[n_prev, s, 2 * p] = s_new[0:A_HEAD_DIM, 0:A_HEAD_DIM]
        wkv_out_ref[n_prev, s, 2 * p + 1] = s_new[A_HEAD_DIM:LANES, A_HEAD_DIM:LANES]
    shift_out_ref[n_prev] = _last_steps(cols, nseq, steps, 1)
    y = jnp.concatenate([jnp.concatenate(ys[s * n_pairs:(s + 1) * n_pairs], axis=1) for s in range(nseq)],
                        axis=0)
    y_ref[...] = _rwkv_finish(y, q["bonus"], q["g"], vec_ref[...])


def _rwkv_sample(cols_a, shift0, wkv0, prevs, nb, steps, wts, l):
    bb = SAMPLE_SEQS
    rows = bb * steps
    n_prev = len(prevs)
    blk = lambda i: (i, 0)
    seq3 = lambda i: (0, i, 0, 0)
    seq4 = lambda i: (0, i, 0, 0, 0)
    prev_specs, prev_args = [], []
    for sh, wk in prevs:
        prev_specs += [pl.BlockSpec((None, bb, 1, A_COLS), seq3),
                       pl.BlockSpec((None, bb, A_HEADS, A_HEAD_DIM, A_HEAD_DIM), seq4)]
        prev_args += [sh, wk]
    return pl.pallas_call(
        functools.partial(_rwkv_sample_kernel, steps=steps, n_prev=n_prev),
        grid=(nb // bb,),
        in_specs=[pl.BlockSpec((rows, A_COLS), blk),
                  pl.BlockSpec((None, bb, 1, A_COLS), lambda i: (l, i, 0, 0)),
                  pl.BlockSpec((None, bb, A_HEADS, A_HEAD_DIM, A_HEAD_DIM), lambda i: (l, i, 0, 0, 0))]
                 + _rwkv_param_specs(l) + prev_specs,
        out_specs=[pl.BlockSpec((rows, A_WIDTH), blk),
                   pl.BlockSpec((n_prev + 1, bb, 1, A_COLS), seq3),
                   pl.BlockSpec((n_prev + 1, bb, A_HEADS, A_HEAD_DIM, A_HEAD_DIM), seq4)],
        out_shape=[jax.ShapeDtypeStruct((nb * steps, A_WIDTH), F32),
                   jax.ShapeDtypeStruct((n_prev + 1, nb, 1, A_COLS), F32),
                   jax.ShapeDtypeStruct((n_prev + 1, nb, A_HEADS, A_HEAD_DIM, A_HEAD_DIM), F32)],
        compiler_params=_params(1),
        name="rwkv_sample",
    )(cols_a, shift0, wkv0, *_rwkv_param_args(wts), *prev_args)


def _lru_gates(xc, vec, bias, wgates):
    lam = vec[5:6, :]
    gates = _sigmoid(_dot(_bf(xc), wgates) + bias)
    gate_r = gates[:, 0:B_WIDTH]
    gate_i = gates[:, B_WIDTH:2 * B_WIDTH]
    log_a = (-LRU_C) * gate_r * _softplus(-lam)
    a = jnp.exp(log_a)
    mult = jnp.sqrt(-jnp.tanh(log_a) * (a * a + 1.0))
    return a, mult * gate_i * xc


def _lru_body(gate_br, xb, conv_prev, h0, tloc, tlen, vec, bias, wgates):
    rows = xb.shape[0]
    conv_b = vec[4:5, :]
    g_out = vec[6:7, :]
    xc = conv_b + vec[3:4, :] * xb
    for j in (1, 2, 3):
        tail = conv_prev if j == 3 else _shift_rows(conv_prev, rows - (3 - j))
        xc = xc + vec[3 - j:4 - j, :] * jnp.where(tloc >= j, _shift_rows(xb, j), tail)
    a, b = _lru_gates(xc, vec, bias, wgates)
    b = b + jnp.where(tloc == 0, a * h0, 0.0)
    d = 1
    while d < tlen:
        keep = tloc >= d
        a_s = jnp.where(keep, _shift_rows(a, d), 1.0)
        b_s = jnp.where(keep, _shift_rows(b, d), 0.0)
        b = a * b_s + b
        a = a * a_s
        d *= 2
    hs = b
    y = hs * _gelu_tanh(gate_br)
    return _rms(y, g_out), hs


def _lru_tm_kernel(cols_ref, vec_ref, bias_ref, wg_ref, y_ref, conv_ref, h_ref, x_scr, a_scr, b_scr, h_scr, *, nb):
    rows = a_scr.shape[0]
    hist = (CONV_WIDTH - 1) * nb

    @pl.when(pl.program_id(0) == 0)
    def _():
        x_scr[0:hist, :] = jnp.zeros((hist, B_WIDTH), F32)
        h_scr[...] = jnp.zeros_like(h_scr)

    vec = vec_ref[...]
    gate_br = jnp.concatenate([cols_ref[0], cols_ref[1]], axis=1)
    xb = jnp.concatenate([cols_ref[2], cols_ref[3]], axis=1)
    x_scr[hist:hist + rows, :] = xb
    xc = vec[4:5, :] + vec[3:4, :] * xb
    for j in range(1, CONV_WIDTH):
        xc = xc + vec[3 - j:4 - j, :] * x_scr[hist - j * nb:hist - j * nb + rows, :]
    a, b = _lru_gates(xc, vec, bias_ref[...], wg_ref[...])
    a_scr[...] = a
    b_scr[...] = b

    def step(i, h):
        rw = pl.ds(pl.multiple_of(i * nb, nb), nb)
        h = a_scr[rw, :] * h + b_scr[rw, :]
        b_scr[rw, :] = h
        return h

    h = lax.fori_loop(0, rows // nb, step, h_scr[...], unroll=8)
    h_scr[...] = h
    h_ref[...] = h
    tail = x_scr[rows:rows + hist, :]
    conv_ref[...] = tail
    x_scr[0:hist, :] = tail
    y = _rms(b_scr[...] * _gelu_tanh(gate_br), vec[6:7, :])
    for j, slab in enumerate(_lane_slabs(y)):
        y_ref[j] = slab


def _lru_param_specs(l):
    return [_layer_spec(l, (SUBLANES, B_WIDTH)), _layer_spec(l, (1, 2 * B_WIDTH)),
            _layer_spec(l, (B_WIDTH, 2 * B_WIDTH))]


def _lru_param_args(wts):
    return (wts["b_vec"], wts["b_bias"], wts["b_wgates"])


def _lru_tm(cols_b, nb, seq, wts, l):
    rows = SCAN_STEPS * nb
    hist = (CONV_WIDTH - 1) * nb
    slab = lambda t: (0, t, 0)
    fixed = lambda t: (0, 0)
    return pl.pallas_call(
        functools.partial(_lru_tm_kernel, nb=nb),
        grid=(seq // SCAN_STEPS,),
        in_specs=[pl.BlockSpec((2 * B_WIDTH // LANES, rows, LANES), slab)] + _lru_param_specs(l),
        out_specs=[pl.BlockSpec((B_WIDTH // LANES, rows, LANES), slab), pl.BlockSpec((hist, B_WIDTH), fixed),
                   pl.BlockSpec((nb, B_WIDTH), fixed)],
        out_shape=[jax.ShapeDtypeStruct((B_WIDTH // LANES, nb * seq, LANES), F32),
                   jax.ShapeDtypeStruct((hist, B_WIDTH), F32),
                   jax.ShapeDtypeStruct((nb, B_WIDTH), F32)],
        scratch_shapes=[pltpu.VMEM((hist + rows, B_WIDTH), F32), pltpu.VMEM((rows, B_WIDTH), F32),
                        pltpu.VMEM((rows, B_WIDTH), F32), pltpu.VMEM((nb, B_WIDTH), F32)],
        compiler_params=_params(1),
        name="lru_tm",
    )(cols_b, *_lru_param_args(wts))


def _lru_sample_kernel(*refs, steps, n_prev):
    cols_ref, conv0_ref, h0_ref, vec_ref, bias_ref, wg_ref = refs[0:6]
    prev_refs = refs[6:6 + 2 * n_prev]
    y_ref, conv_out_ref, h_out_ref = refs[6 + 2 * n_prev:]
    rows = cols_ref.shape[0]
    nseq = rows // steps
    gate_br = cols_ref[:, 0:B_WIDTH]
    xb = cols_ref[:, B_WIDTH:2 * B_WIDTH]
    tloc = _row_iota(xb.shape) & (steps - 1)
    h0 = jnp.broadcast_to(h0_ref[...], (nseq, steps, B_WIDTH)).reshape(rows, B_WIDTH)
    y, hs = _lru_body(gate_br, xb, conv0_ref[...], h0, tloc, steps, vec_ref[...], bias_ref[...], wg_ref[...])
    y_ref[...] = y
    _stack_previous(prev_refs[0::2], conv_out_ref)
    _stack_previous(prev_refs[1::2], h_out_ref)
    conv_out_ref[n_prev] = _last_steps(xb, nseq, steps, CONV_WIDTH - 1)
    h_out_ref[n_prev] = _last_steps(hs, nseq, steps, 1)


def _lru_sample(cols_b, conv0_rows, h0, prevs, nb, steps, wts, l):
    bb = SAMPLE_SEQS
    rows = bb * steps
    n_prev = len(prevs)
    blk = lambda i: (i, 0)
    seq3 = lambda i: (0, i, 0, 0)
    prev_specs, prev_args = [], []
    for cv, hh in prevs:
        prev_specs += [pl.BlockSpec((None, bb, CONV_WIDTH - 1, B_WIDTH), seq3),
                       pl.BlockSpec((None, bb, 1, B_WIDTH), seq3)]
        prev_args += [cv, hh]
    return pl.pallas_call(
        functools.partial(_lru_sample_kernel, steps=steps, n_prev=n_prev),
        grid=(nb // bb,),
        in_specs=[pl.BlockSpec((rows, 2 * B_WIDTH), blk),
                  pl.BlockSpec((None, rows, B_WIDTH), lambda i: (l, i, 0)),
                  pl.BlockSpec((None, bb, 1, B_WIDTH), lambda i: (l, i, 0, 0))]
                 + _lru_param_specs(l) + prev_specs,
        out_specs=[pl.BlockSpec((rows, B_WIDTH), blk),
                   pl.BlockSpec((n_prev + 1, bb, CONV_WIDTH - 1, B_WIDTH), seq3),
                   pl.BlockSpec((n_prev + 1, bb, 1, B_WIDTH), seq3)],
        out_shape=[jax.ShapeDtypeStruct((nb * steps, B_WIDTH), F32),
                   jax.ShapeDtypeStruct((n_prev + 1, nb, CONV_WIDTH - 1, B_WIDTH), F32),
                   jax.ShapeDtypeStruct((n_prev + 1, nb, 1, B_WIDTH), F32)],
        compiler_params=_params(1),
        name="lru_sample",
    )(cols_b, conv0_rows, h0, *_lru_param_args(wts), *prev_args)


def _s5_drive(u, lam, bbd):
    lr = lam[0:1, :]
    li = lam[1:2, :]
    dt = jnp.exp(lam[2:3, :])
    mag = jnp.exp(lr * dt)
    ar = mag * jnp.cos(li * dt)
    ai = mag * jnp.sin(li * dt)
    den = lr * lr + li * li
    cr = ((ar - 1.0) * lr + ai * li) / den
    ci = (ai * lr - (ar - 1.0) * li) / den
    ub = _bf(u)
    pb = _dot(ub, bbd[:, 0:S5_W])
    qb = _dot(ub, bbd[:, S5_W:2 * S5_W])
    return ar, ai, pb * cr - qb * ci, pb * ci + qb * cr


def _s5_readout(hr, hi, u, vec, cbd, wglu):
    s5_d = vec[0:1, :]
    b_glu = vec[1:2, :]
    g_out = vec[2:3, :]
    y = _dot(_bf(hr), cbd[0:S5_W, :]) + _dot(_bf(hi), cbd[S5_W:2 * S5_W, :]) + s5_d * u
    z = _gelu_tanh(y)
    out = z * _sigmoid(_dot(_bf(z), wglu) + b_glu)
    return _rms(out, g_out)


def _s5_body(u, h0r, h0i, tloc, tlen, lam, vec, bbd, cbd, wglu):
    ar, ai, hr, hi = _s5_drive(u, lam, bbd)
    first = tloc == 0
    hr = hr + jnp.where(first, ar * h0r - ai * h0i, 0.0)
    hi = hi + jnp.where(first, ar * h0i + ai * h0r, 0.0)
    pr, pi = ar, ai
    d = 1
    while d < tlen:
        keep = tloc >= d
        sr = jnp.where(keep, _shift_rows(hr, d), 0.0)
        si = jnp.where(keep, _shift_rows(hi, d), 0.0)
        hr, hi = hr + pr * sr - pi * si, hi + pr * si + pi * sr
        pr, pi = pr * pr - pi * pi, 2.0 * pr * pi
        d *= 2
    return _s5_readout(hr, hi, u, vec, cbd, wglu), hr, hi


def _s5_tm_kernel(u_ref, lam_ref, vec_ref, bbd_ref, cbd_ref, wglu_ref, y_ref, hr_ref, hi_ref,
                  re_scr, im_scr, sr_scr, si_scr, *, nb):
    rows = re_scr.shape[0]

    @pl.when(pl.program_id(0) == 0)
    def _():
        sr_scr[...] = jnp.zeros_like(sr_scr)
        si_scr[...] = jnp.zeros_like(si_scr)

    u = jnp.concatenate([u_ref[j] for j in range(C_WIDTH // LANES)], axis=1)
    ar, ai, dr, di = _s5_drive(u, lam_ref[...], bbd_ref[...])
    re_scr[...] = dr
    im_scr[...] = di
    ar = jnp.broadcast_to(ar, (nb, S5_W))
    ai = jnp.broadcast_to(ai, (nb, S5_W))

    def step(i, carry):
        hr, hi = carry
        rw = pl.ds(pl.multiple_of(i * nb, nb), nb)
        nr = ar * hr - ai * hi + re_scr[rw, :]
        ni = ar * hi + ai * hr + im_scr[rw, :]
        re_scr[rw, :] = nr
        im_scr[rw, :] = ni
        return nr, ni

    hr, hi = lax.fori_loop(0, rows // nb, step, (sr_scr[...], si_scr[...]), unroll=4)
    sr_scr[...] = hr
    si_scr[...] = hi
    hr_ref[...] = hr
    hi_ref[...] = hi
    y = _s5_readout(re_scr[...], im_scr[...], u, vec_ref[...], cbd_ref[...], wglu_ref[...])
    for j, slab in enumerate(_lane_slabs(y)):
        y_ref[j] = slab


def _s5_param_specs(l):
    return [_layer_spec(l, (SUBLANES, S5_W)), _layer_spec(l, (SUBLANES, C_WIDTH)),
            _layer_spec(l, (C_WIDTH, 2 * S5_W)), _layer_spec(l, (2 * S5_W, C_WIDTH)),
            _layer_spec(l, (C_WIDTH, C_WIDTH))]


def _s5_param_args(wts):
    return (wts["c_lam"], wts["c_vec"], wts["c_bbd"], wts["c_cbd"], wts["c_wglu"])


def _s5_tm(cols_c, nb, seq, wts, l):
    rows = SCAN_STEPS * nb
    slab = lambda t: (0, t, 0)
    fixed = lambda t: (0, 0)
    n_slabs = C_WIDTH // LANES
    return pl.pallas_call(
        functools.partial(_s5_tm_kernel, nb=nb),
        grid=(seq // SCAN_STEPS,),
        in_specs=[pl.BlockSpec((n_slabs, rows, LANES), slab)] + _s5_param_specs(l),
        out_specs=[pl.BlockSpec((n_slabs, rows, LANES), slab), pl.BlockSpec((nb, S5_W), fixed),
                   pl.BlockSpec((nb, S5_W), fixed)],
        out_shape=[jax.ShapeDtypeStruct((n_slabs, nb * seq, LANES), F32),
                   jax.ShapeDtypeStruct((nb, S5_W), F32), jax.ShapeDtypeStruct((nb, S5_W), F32)],
        scratch_shapes=[pltpu.VMEM((rows, S5_W), F32), pltpu.VMEM((rows, S5_W), F32),
                        pltpu.VMEM((nb, S5_W), F32), pltpu.VMEM((nb, S5_W), F32)],
        compiler_params=_params(1),
        name="s5_tm",
    )(cols_c, *_s5_param_args(wts))


def _s5_sample_kernel(*refs, steps, n_prev):
    u_ref, h0r_ref, h0i_ref, lam_ref, vec_ref, bbd_ref, cbd_ref, wglu_ref = refs[0:8]
    prev_refs = refs[8:8 + 2 * n_prev]
    y_ref, hr_out_ref, hi_out_ref = refs[8 + 2 * n_prev:]
    rows = u_ref.shape[0]
    nseq = rows // steps
    tloc = _row_iota((rows, S5_W)) & (steps - 1)
    h0r = jnp.broadcast_to(h0r_ref[...], (nseq, steps, S5_W)).reshape(rows, S5_W)
    h0i = jnp.broadcast_to(h0i_ref[...], (nseq, steps, S5_W)).reshape(rows, S5_W)
    y, hr, hi = _s5_body(u_ref[...], h0r, h0i, tloc, steps, lam_ref[...], vec_ref[...], bbd_ref[...],
                         cbd_ref[...], wglu_ref[...])
    y_ref[...] = y
    _stack_previous(prev_refs[0::2], hr_out_ref)
    _stack_previous(prev_refs[1::2], hi_out_ref)
    hr_out_ref[n_prev] = _last_steps(hr, nseq, steps, 1)
    hi_out_ref[n_prev] = _last_steps(hi, nseq, steps, 1)


def _s5_sample(cols_c, h0r, h0i, prevs, nb, steps, wts, l):
    bb = SAMPLE_SEQS
    rows = bb * steps
    n_prev = len(prevs)
    blk = lambda i: (i, 0)
    seq3 = lambda i: (0, i, 0, 0)
    st = pl.BlockSpec((None, bb, 1, S5_W), lambda i: (l, i, 0, 0))
    prev_specs, prev_args = [], []
    for pr, pi in prevs:
        prev_specs += [pl.BlockSpec((None, bb, 1, S5_W), seq3)] * 2
        prev_args += [pr, pi]
    stacked = pl.BlockSpec((n_prev + 1, bb, 1, S5_W), seq3)
    return pl.pallas_call(
        functools.partial(_s5_sample_kernel, steps=steps, n_prev=n_prev),
        grid=(nb // bb,),
        in_specs=[pl.BlockSpec((rows, C_WIDTH), blk), st, st] + _s5_param_specs(l) + prev_specs,
        out_specs=[pl.BlockSpec((rows, C_WIDTH), blk), stacked, stacked],
        out_shape=[jax.ShapeDtypeStruct((nb * steps, C_WIDTH), F32),
                   jax.ShapeDtypeStruct((n_prev + 1, nb, 1, S5_W), F32),
                   jax.ShapeDtypeStruct((n_prev + 1, nb, 1, S5_W), F32)],
        compiler_params=_params(1),
        name="s5_sample",
    )(cols_c, h0r, h0i, *_s5_param_args(wts), *prev_args)


def _rows8(rows, width):
    m = jnp.stack([r.reshape(r.shape[0], width) for r in rows], axis=1)
    return jnp.pad(m, ((0, 0), (0, SUBLANES - m.shape[1]), (0, 0)))


def _stacked_weights(w):
    depth = w["g_mix"].shape[0]
    eye_b = jnp.eye(B_BLOCKS, dtype=F32)
    eye_g = jnp.eye(S5_GROUPS, dtype=F32)
    bd4 = lambda m: jnp.einsum("lnde,nm->lndme", m, eye_b).reshape(depth, B_WIDTH, B_WIDTH)
    zeros_lora = jnp.zeros((depth, LANES - 64, A_WIDTH), F32)
    b_in = lambda m: jnp.einsum("lgpc,gh->lgchp", m, eye_g).reshape(depth, C_WIDTH, S5_W)
    c_out = lambda m: jnp.einsum("lgcp,gh->lgphc", m, eye_g).reshape(depth, S5_W, C_WIDTH)
    g_final = jnp.broadcast_to(w["g_final"][None], (depth, D_MODEL))
    return dict(
        g_mix=w["g_mix"][:, None, :],
        w_in=_bf(w["w_in"]),
        mu=w["mu_a"][:, None, :],
        a_vec=_rows8([w["w0"], w["a0"], w["k_k"], w["k_a"], w["r_k"], w["lnx_w"], w["lnx_b"]], A_WIDTH),
        wdec=_bf(jnp.concatenate([w["w_dec2"], zeros_lora], axis=1)),
        wa=_bf(jnp.concatenate([zeros_lora, w["w_a2"]], axis=1)),
        wg=_bf(w["w_g2"]),
        b_vec=_rows8([w["conv_w"][:, 0], w["conv_w"][:, 1], w["conv_w"][:, 2], w["conv_w"][:, 3],
                      w["conv_b"], w["lru_lambda"], w["g_out_b"]], B_WIDTH),
        b_bias=jnp.concatenate([w["b_rg"], w["b_ig"]], axis=1)[:, None, :],
        b_wgates=_bf(jnp.concatenate([bd4(w["w_rg"]), bd4(w["w_ig"])], axis=2)),
        c_lam=_rows8([w["s5_lam_re"], w["s5_lam_im"], jnp.repeat(w["s5_log_dt"], S5_STATE, axis=1)], S5_W),
        c_vec=_rows8([w["s5_d"], w["b_glu"], w["g_out_c"]], C_WIDTH),
        c_bbd=_bf(jnp.concatenate([b_in(w["s5_b_re"]), b_in(w["s5_b_im"])], axis=2)),
        c_cbd=_bf(jnp.concatenate([c_out(w["s5_c_re"]), -c_out(w["s5_c_im"])], axis=1)),
        c_wglu=_bf(w["w_glu"]),
        w_out=_bf(w["w_out"]),
        p_vec=_rows8([w["g_ffn"], w["g_ple"], g_final], D_MODEL),
        w_up=_bf(w["w_ffn_up"]),
        w_down=_bf(w["w_ffn_down"]),
        w_ple=_bf(w["w_ple"]),
        w_gate=_bf(w["w_ple_gate"]),
    )


def _run_prompt(x, p, wts, depth):
    nb, seq, _ = x.shape
    h = x.reshape(nb * seq, D_MODEL)
    p = p.reshape(depth, nb * seq, PLE_DIM)
    outs = []
    for l in range(depth):
        cols_a, cols_b, cols_c = _proj_in_tm(h, wts["g_mix"], wts["w_in"], l, nb, seq)
        ya, wkv, last = _rwkv_prompt(cols_a, nb, seq, wts, l)
        yb, conv, lru = _lru_tm(cols_b, nb, seq, wts, l)
        yc, s5r, s5i = _s5_tm(cols_c, nb, seq, wts, l)
        h = _post(h, ya, yb, yc, p, wts, l, l == depth - 1, nb_tm=nb)
        outs.append((last[:, SUBLANES - 1], wkv, conv, lru, s5r, s5i))
    shift, wkv, conv, lru, s5r, s5i = (jnp.stack([o[j] for o in outs], axis=0) for j in range(6))
    conv = jnp.swapaxes(conv.reshape(depth, CONV_WIDTH - 1, nb, B_WIDTH), 1, 2)
    s5_shape = (depth, nb, S5_GROUPS, S5_STATE)
    return h.reshape(nb, seq, D_MODEL), (shift, wkv, conv, lru, s5r.reshape(s5_shape), s5i.reshape(s5_shape))


def _run_sample(x, p, states, wts, depth):
    nb, steps, _ = x.shape
    st_shift, st_wkv, st_conv, st_lru, st_s5r, st_s5i = states
    h = x.reshape(nb * steps, D_MODEL)
    p = p.reshape(depth, nb * steps, PLE_DIM)
    shift0 = st_shift[:, :, None, :]
    conv0_rows = jnp.pad(st_conv, ((0, 0), (0, 0), (0, steps - (CONV_WIDTH - 1)), (0, 0))
                         ).reshape(depth, nb * steps, B_WIDTH)
    lru0 = st_lru[:, :, None, :]
    s5r0 = st_s5r.reshape(depth, nb, 1, S5_W)
    s5i0 = st_s5i.reshape(depth, nb, 1, S5_W)
    prev_a, prev_b, prev_c = [], [], []
    for l in range(depth):
        last = l == depth - 1
        cols_a, cols_b, cols_c = _proj_in(h, wts["g_mix"], wts["w_in"], l)
        ya, shift, wkv = _rwkv_sample(cols_a, shift0, st_wkv, prev_a if last else [], nb, steps, wts, l)
        yb, conv, lru = _lru_sample(cols_b, conv0_rows, lru0, prev_b if last else [], nb, steps, wts, l)
        yc, s5r, s5i = _s5_sample(cols_c, s5r0, s5i0, prev_c if last else [], nb, steps, wts, l)
        h = _post(h, ya, yb, yc, p, wts, l, last)
        prev_a.append((shift, wkv))
        prev_b.append((conv, lru))
        prev_c.append((s5r, s5i))
    s5_shape = (depth, nb, S5_GROUPS, S5_STATE)
    new_states = (shift.reshape(depth, nb, A_COLS), wkv, conv, lru.reshape(depth, nb, B_WIDTH),
                  s5r.reshape(s5_shape), s5i.reshape(s5_shape))
    return h.reshape(nb, steps, D_MODEL), new_states


def kernel(x_prompt, x_sample, p_prompt, p_sample, state_shift, state_wkv, state_conv, state_lru, state_s5_re, state_s5_im, g_mix, w_in, mu_a, w0, w_dec2, a0, w_a2, w_g2, k_k, k_a, r_k, lnx_w, lnx_b, conv_w, conv_b, w_rg, b_rg, w_ig, b_ig, lru_lambda, g_out_b, s5_lam_re, s5_lam_im, s5_log_dt, s5_b_re, s5_b_im, s5_c_re, s5_c_im, s5_d, w_glu, b_glu, g_out_c, w_out, g_ffn, w_ffn_up, w_ffn_down, g_ple, w_ple, w_ple_gate, g_final):
    w = dict(g_mix=g_mix, w_in=w_in, mu_a=mu_a, w0=w0, w_dec2=w_dec2, a0=a0, w_a2=w_a2, w_g2=w_g2, k_k=k_k,
             k_a=k_a, r_k=r_k, lnx_w=lnx_w, lnx_b=lnx_b, conv_w=conv_w, conv_b=conv_b, w_rg=w_rg, b_rg=b_rg,
             w_ig=w_ig, b_ig=b_ig, lru_lambda=lru_lambda, g_out_b=g_out_b, s5_lam_re=s5_lam_re,
             s5_lam_im=s5_lam_im, s5_log_dt=s5_log_dt, s5_b_re=s5_b_re, s5_b_im=s5_b_im, s5_c_re=s5_c_re,
             s5_c_im=s5_c_im, s5_d=s5_d, w_glu=w_glu, b_glu=b_glu, g_out_c=g_out_c, w_out=w_out, g_ffn=g_ffn,
             w_ffn_up=w_ffn_up, w_ffn_down=w_ffn_down, g_ple=g_ple, w_ple=w_ple, w_ple_gate=w_ple_gate,
             g_final=g_final)
    depth = g_mix.shape[0]
    wts = _stacked_weights(w)
    y_prompt, new_p = _run_prompt(x_prompt, p_prompt, wts, depth)
    y_sample, new_s = _run_sample(x_sample, p_sample,
                                  (state_shift, state_wkv, state_conv, state_lru, state_s5_re, state_s5_im),
                                  wts, depth)
    return (y_prompt, y_sample) + new_p + new_s
```

```python
import functools
import math

import jax
import jax.numpy as jnp
from jax import lax
from jax.experimental import pallas as pl
from jax.experimental.pallas import tpu as pltpu

F32 = jnp.float32
BF16 = jnp.bfloat16

D_MODEL = 1024
A_WIDTH = 512
A_HEADS = 8
A_HEAD_DIM = 64
A_COLS = 1792
B_WIDTH = 256
B_BLOCKS = 4
CONV_WIDTH = 4
C_WIDTH = 256
S5_GROUPS = 16
S5_GROUP_CH = 16
S5_STATE = 64
S5_W = S5_GROUPS * S5_STATE
IN_COLS = A_COLS + 2 * B_WIDTH + C_WIDTH
D_FF = 2816
PLE_DIM = 256
LRU_C = 8.0
RMS_EPS = 1e-6
GN_EPS = 64e-5

LANES = 128
SUBLANES = 8
VMEM_LIMIT_BYTES = 56 * 1024 * 1024

TOKEN_TILE = 512
FFN_CHUNK = 256
RWKV_CHUNK = 64
RWKV_BLOCK = 256
SCAN_STEPS = 128
SAMPLE_SEQS = 8


def _layer_spec(l, shape):
    nd = len(shape)
    return pl.BlockSpec((None,) + tuple(shape), lambda *_: (l,) + (0,) * nd, pipeline_mode=pl.Buffered(1))


def _params(n_axes):
    return pltpu.CompilerParams(dimension_semantics=("arbitrary",) * n_axes,
                                vmem_limit_bytes=VMEM_LIMIT_BYTES)


def _dot(a, b):
    return jnp.dot(a, b, preferred_element_type=F32)


def _dot_nt(a, b):
    return lax.dot_general(a, b, (((1,), (1,)), ((), ())), preferred_element_type=F32)


def _dot_tn(a, b):
    return lax.dot_general(a, b, (((0,), (0,)), ((), ())), preferred_element_type=F32)


def _bf(x):
    return x.astype(BF16)


def _rms(x, g):
    inv = lax.rsqrt(jnp.mean(x * x, axis=-1, keepdims=True) + RMS_EPS)
    return x * inv * g


def _sigmoid(x):
    return 1.0 / (1.0 + jnp.exp(-x))


def _softplus(x):
    return jnp.maximum(x, 0.0) + jnp.log(1.0 + jnp.exp(-jnp.abs(x)))


def _gelu_tanh(x):
    c = math.sqrt(2.0 / math.pi)
    return 0.5 * x * (1.0 + jnp.tanh(c * (x + 0.044715 * (x * x * x))))


def _row_iota(shape):
    return lax.broadcasted_iota(jnp.int32, shape, 0)


def _shift_rows(x, d):
    return pltpu.roll(x, d, axis=0)


def _lane_slabs(x):
    return [x[:, j * LANES:(j + 1) * LANES] for j in range(x.shape[1] // LANES)]


def _last_steps(x, nseq, steps, n):
    return x.reshape(nseq, steps, x.shape[1])[:, steps - n:steps, :]


def _stack_previous(prev_refs, out_ref):
    for j, ref in enumerate(prev_refs):
        out_ref[j] = ref[...]


def _proj_in_kernel(h_ref, g_ref, w_ref, a_ref, b_ref, c_ref):
    xn = _bf(_rms(h_ref[...], g_ref[...]))
    a_ref[...] = _dot(xn, w_ref[:, 0:A_COLS])
    b_ref[...] = _dot(xn, w_ref[:, A_COLS:A_COLS + 2 * B_WIDTH])
    c_ref[...] = _dot(xn, w_ref[:, A_COLS + 2 * B_WIDTH:IN_COLS])


def _proj_in(h, g, w_bf, l):
    n = h.shape[0]
    tm = TOKEN_TILE
    row = lambda i: (i, 0)
    return pl.pallas_call(
        _proj_in_kernel,
        grid=(n // tm,),
        in_specs=[pl.BlockSpec((tm, D_MODEL), row), _layer_spec(l, (1, D_MODEL)),
                  _layer_spec(l, (D_MODEL, IN_COLS))],
        out_specs=[pl.BlockSpec((tm, A_COLS), row), pl.BlockSpec((tm, 2 * B_WIDTH), row),
                   pl.BlockSpec((tm, C_WIDTH), row)],
        out_shape=[jax.ShapeDtypeStruct((n, A_COLS), F32), jax.ShapeDtypeStruct((n, 2 * B_WIDTH), F32),
                   jax.ShapeDtypeStruct((n, C_WIDTH), F32)],
        compiler_params=_params(1),
        name="proj_in",
    )(h, g, w_bf)


def _proj_in_tm_kernel(h_ref, g_ref, w_ref, a_ref, b_ref, c_ref, *, nb):
    b = pl.program_id(1)
    tm = h_ref.shape[0]
    xn = _bf(_rms(h_ref[...], g_ref[...]))
    a_ref[...] = _dot(xn, w_ref[:, 0:A_COLS])
    rows_of_b = pl.ds(b, tm, stride=nb)
    for j, slab in enumerate(_lane_slabs(_dot(xn, w_ref[:, A_COLS:A_COLS + 2 * B_WIDTH]))):
        b_ref[j, rows_of_b, :] = slab
    for j, slab in enumerate(_lane_slabs(_dot(xn, w_ref[:, A_COLS + 2 * B_WIDTH:IN_COLS]))):
        c_ref[j, rows_of_b, :] = slab


def _proj_in_tm(h, g, w_bf, l, nb, seq):
    tm = TOKEN_TILE
    nt = seq // tm
    row = lambda i, b: (b * nt + i, 0)
    slab = lambda i, b: (0, i, 0)
    nb_slabs, nc_slabs = 2 * B_WIDTH // LANES, C_WIDTH // LANES
    return pl.pallas_call(
        functools.partial(_proj_in_tm_kernel, nb=nb),
        grid=(nt, nb),
        in_specs=[pl.BlockSpec((tm, D_MODEL), row), _layer_spec(l, (1, D_MODEL)),
                  _layer_spec(l, (D_MODEL, IN_COLS))],
        out_specs=[pl.BlockSpec((tm, A_COLS), row), pl.BlockSpec((nb_slabs, tm * nb, LANES), slab),
                   pl.BlockSpec((nc_slabs, tm * nb, LANES), slab)],
        out_shape=[jax.ShapeDtypeStruct((nb * seq, A_COLS), F32),
                   jax.ShapeDtypeStruct((nb_slabs, nb * seq, LANES), F32),
                   jax.ShapeDtypeStruct((nc_slabs, nb * seq, LANES), F32)],
        compiler_params=_params(2),
        name="proj_in_tm",
    )(h, g, w_bf)


def _post_kernel(h_ref, ya_ref, yb_ref, yc_ref, p_ref, wo_ref, vec_ref, wup_ref, wdn_ref, wple_ref,
                 wgate_ref, o_ref, act_scr, *, final, nb_tm):
    g_ffn = vec_ref[0:1, :]
    g_ple = vec_ref[1:2, :]
    g_final = vec_ref[2:3, :]
    if nb_tm:
        rows_of_b = pl.ds(pl.program_id(1), h_ref.shape[0], stride=nb_tm)
        yb = jnp.concatenate([yb_ref[j, rows_of_b, :] for j in range(B_WIDTH // LANES)], axis=1)
        yc = jnp.concatenate([yc_ref[j, rows_of_b, :] for j in range(C_WIDTH // LANES)], axis=1)
    else:
        yb, yc = yb_ref[...], yc_ref[...]
    h1 = (h_ref[...] + _dot(_bf(ya_ref[...]), wo_ref[0:A_WIDTH, :])
          + _dot(_bf(yb), wo_ref[A_WIDTH:A_WIDTH + B_WIDTH, :])
          + _dot(_bf(yc), wo_ref[A_WIDTH + B_WIDTH:D_MODEL, :]))
    xf = _bf(_rms(h1, g_ffn))
    for c0 in range(0, D_FF, FFN_CHUNK):
        gate = _dot(xf, wup_ref[:, c0:c0 + FFN_CHUNK])
        up = _dot(xf, wup_ref[:, D_FF + c0:D_FF + c0 + FFN_CHUNK])
        act_scr[:, c0:c0 + FFN_CHUNK] = _bf(gate * _sigmoid(gate) * up)
    h2 = h1 + _dot(act_scr[...], wdn_ref[...])
    ple = _dot(_bf(p_ref[...]), wple_ref[...])
    gate = _sigmoid(_dot(_bf(_rms(h2, g_ple)), wgate_ref[...]))
    h3 = h2 + ple * gate
    if final:
        h3 = _rms(h3, g_final)
    o_ref[...] = h3


def _post(h, ya, yb, yc, p, wts, l, final, nb_tm=0):
    n = h.shape[0]
    tm = TOKEN_TILE
    if nb_tm:
        nt = n // (nb_tm * tm)
        grid = (nt, nb_tm)
        row = lambda i, b: (b * nt + i, 0)
        prow = lambda i, b: (l, b * nt + i, 0)
        slab = lambda i, b: (0, i, 0)
        yb_spec = pl.BlockSpec((B_WIDTH // LANES, tm * nb_tm, LANES), slab, pipeline_mode=pl.Buffered(1))
        yc_spec = pl.BlockSpec((C_WIDTH // LANES, tm * nb_tm, LANES), slab, pipeline_mode=pl.Buffered(1))
    else:
        grid = (n // tm,)
        row = lambda i: (i, 0)
        prow = lambda i: (l, i, 0)
        yb_spec = pl.BlockSpec((tm, B_WIDTH), row)
        yc_spec = pl.BlockSpec((tm, C_WIDTH), row)
    return pl.pallas_call(
        functools.partial(_post_kernel, final=final, nb_tm=nb_tm),
        grid=grid,
        in_specs=[pl.BlockSpec((tm, D_MODEL), row), pl.BlockSpec((tm, A_WIDTH), row),
                  yb_spec, yc_spec,
                  pl.BlockSpec((None, tm, PLE_DIM), prow),
                  _layer_spec(l, (D_MODEL, D_MODEL)), _layer_spec(l, (SUBLANES, D_MODEL)),
                  _layer_spec(l, (D_MODEL, 2 * D_FF)), _layer_spec(l, (D_FF, D_MODEL)),
                  _layer_spec(l, (PLE_DIM, D_MODEL)), _layer_spec(l, (D_MODEL, D_MODEL))],
        out_specs=pl.BlockSpec((tm, D_MODEL), row),
        out_shape=jax.ShapeDtypeStruct((n, D_MODEL), F32),
        scratch_shapes=[pltpu.VMEM((tm, D_FF), BF16)],
        compiler_params=_params(len(grid)),
        name="post",
    )(h, ya, yb, yc, p, wts["w_out"], wts["p_vec"], wts["w_up"], wts["w_down"], wts["w_ple"], wts["w_gate"])


def _head_sum(x):
    first = lax.broadcasted_iota(jnp.int32, (x.shape[0], LANES), 1) < A_HEAD_DIM
    outs = []
    for xp in _lane_slabs(x):
        s0 = jnp.sum(jnp.where(first, xp, 0.0), axis=-1, keepdims=True)
        s1 = jnp.sum(jnp.where(first, 0.0, xp), axis=-1, keepdims=True)
        outs.append(jnp.where(first, s0, s1))
    return jnp.concatenate(outs, axis=1)


def _rwkv_prep(cols, prev, mu, vec, wdec, wa, wg, chunk):
    rows = cols.shape[0]
    xs = cols + (prev - cols) * mu
    r = xs[:, 0:A_WIDTH]
    k = xs[:, A_WIDTH:2 * A_WIDTH]
    v = xs[:, 2 * A_WIDTH:3 * A_WIDTH]
    xwa = xs[:, 3 * A_WIDTH:3 * A_WIDTH + LANES]
    xg = xs[:, 3 * A_WIDTH + LANES:A_COLS]
    w0, a0, k_k, k_a, r_k = (vec[i:i + 1, :] for i in range(5))
    z = w0 + _dot(_bf(jnp.tanh(xwa)), wdec)
    log_decay = -math.exp(-0.5) * _sigmoid(z)
    a = _sigmoid(a0 + _dot(_bf(xwa), wa))
    g = _dot(_bf(_sigmoid(xg)), wg)
    kk_raw = k * k_k
    kk = kk_raw * lax.rsqrt(jnp.maximum(_head_sum(kk_raw * kk_raw), 1e-24))
    k_mod = k * (1.0 + (a - 1.0) * k_a)
    bonus = _head_sum(r * k_mod * r_k) * v
    ri = _row_iota((rows, rows))
    ci = lax.broadcasted_iota(jnp.int32, (rows, rows), 1)
    same_chunk = (ci & (-chunk)) == (ri & (-chunk))
    tri = _bf(jnp.where((ci <= ri) & same_chunk, 1.0, 0.0))
    h1 = _bf(log_decay)
    h2 = _bf(log_decay - h1.astype(F32))
    cum = _dot(tri, h1) + _dot(tri, h2)
    gam = jnp.exp(cum)
    ginv = jnp.exp(-cum)
    gprev = jnp.exp(cum - log_decay)
    return dict(rt=r * gam, kap=kk * gprev, bet=kk * a * ginv, kt=k_mod * ginv, v=v, gam=gam,
                bonus=bonus, g=g)


def _stack_heads(x):
    first = lax.broadcasted_iota(jnp.int32, x.shape, 1) < A_HEAD_DIM
    return _bf(jnp.concatenate([jnp.where(first, x, 0.0), jnp.where(first, 0.0, x)], axis=0))


def _rwkv_local(ops):
    c = ops[0]["kap"].shape[0]
    c2 = 2 * c
    ri = _row_iota((c, c2))
    ci = lax.broadcasted_iota(jnp.int32, (c, c2), 1) & (c - 1)
    strict = ci < ri
    incl = ci <= ri
    eye = jnp.where(ri == ci, 1.0, 0.0)
    same_head = (_row_iota((c2, c2)) & c) == (lax.broadcasted_iota(jnp.int32, (c2, c2), 1) & c)
    merged = c2 % LANES == 0

    def blockdiag(m):
        return _bf(jnp.where(same_head, jnp.concatenate([m, m], axis=0), 0.0))

    st = []
    for o in ops:
        s = {k + "2": _stack_heads(o[k]) for k in ("kap", "bet", "kt", "v")}
        s["rt"] = o["rt"]
        lhs = _bf(jnp.concatenate([o["kap"], o["rt"]], axis=0))
        if merged:
            g = _dot_nt(lhs, jnp.concatenate([s["bet2"], s["kt2"]], axis=0))
            a_b, a_k, a_rb, a_rk = g[0:c, 0:c2], g[0:c, c2:2 * c2], g[c:c2, 0:c2], g[c:c2, c2:2 * c2]
        else:
            gb, gk = _dot_nt(lhs, s["bet2"]), _dot_nt(lhs, s["kt2"])
            a_b, a_k, a_rb, a_rk = gb[0:c], gk[0:c], gb[c:c2], gk[c:c2]
        s["a_b"] = jnp.where(strict, a_b, 0.0)
        s["a_k"] = _bf(jnp.where(strict, a_k, 0.0))
        s["a_rb"] = _bf(jnp.where(incl, a_rb, 0.0))
        s["a_rk"] = _bf(jnp.where(incl, a_rk, 0.0))
        st.append(s)
    for s in st:
        s["t"] = eye - s["a_b"]
        s["lp"] = s["a_b"]
        s["akv"] = _dot(s["a_k"], s["v2"])
    n = 2
    while n < c:
        for s in st:
            s["lp"] = _dot(_bf(s["lp"]), blockdiag(s["lp"]))
        for s in st:
            s["t"] = s["t"] + _dot(_bf(s["t"]), blockdiag(s["lp"]))
        n *= 2
    for s in st:
        wu = _dot(_bf(s["t"]), jnp.concatenate([s["kap2"], _stack_heads(s["akv"])], axis=1))
        s["w"] = -wu[:, 0:LANES]
        s["u0"] = -wu[:, LANES:2 * LANES]
    return st


def _rwkv_state(st, states, g_ends):
    c = st[0]["rt"].shape[0]
    merged = (2 * c) % LANES == 0
    xs = []
    for s, state in zip(st, states):
        xs.append(_dot_nt(_bf(jnp.concatenate([s["w"], s["rt"]], axis=0)), _bf(state)))
    u2s = [_stack_heads(x[0:c] + s["u0"]) for x, s in zip(xs, st)]
    ys, new_states = [], []
    for s, x, u2 in zip(st, xs, u2s):
        if merged:
            y = x[c:2 * c] + _dot(jnp.concatenate([s["a_rb"], s["a_rk"]], axis=1),
                                  jnp.concatenate([u2, s["v2"]], axis=0))
        else:
            y = x[c:2 * c] + _dot(s["a_rb"], u2) + _dot(s["a_rk"], s["v2"])
        ys.append(y)
    for s, u2, state, g_end in zip(st, u2s, states, g_ends):
        ds = _dot_tn(jnp.concatenate([u2, s["v2"]], axis=0), jnp.concatenate([s["bet2"], s["kt2"]], axis=0))
        new_states.append((state + ds) * g_end)
    return ys, new_states


def _rwkv_finish(y, bonus, g, vec):
    lnx_w = vec[5:6, :]
    lnx_b = vec[6:7, :]
    inv_n = 1.0 / A_HEAD_DIM
    mu = _head_sum(y) * inv_n
    d = y - mu
    var = _head_sum(d * d) * inv_n
    yn = d * lax.rsqrt(var + GN_EPS) * lnx_w + lnx_b
    return (yn + bonus) * g


def _pair_state(s_ref, idx, p):
    z = jnp.zeros((A_HEAD_DIM, A_HEAD_DIM), F32)
    top = jnp.concatenate([s_ref[idx, 2 * p], z], axis=1)
    bot = jnp.concatenate([z, s_ref[idx, 2 * p + 1]], axis=1)
    return jnp.concatenate([top, bot], axis=0)


def _rwkv_prompt_kernel(cols_ref, mu_ref, vec_ref, wdec_ref, wa_ref, wg_ref,
                        y_ref, wkv_ref, last_ref, s_scr, prev_scr):
    t = pl.program_id(1)
    rows = cols_ref.shape[0]
    chunk = RWKV_CHUNK

    @pl.when(t == 0)
    def _():
        s_scr[...] = jnp.zeros_like(s_scr)
        prev_scr[...] = jnp.zeros_like(prev_scr)

    cols = cols_ref[...]
    first_row = _row_iota(cols.shape) == 0
    prev = jnp.where(first_row, prev_scr[SUBLANES - 1:SUBLANES, :], _shift_rows(cols, 1))
    q = _rwkv_prep(cols, prev, mu_ref[...], vec_ref[...], wdec_ref[...], wa_ref[...], wg_ref[...], chunk)
    n_pairs = A_HEADS // 2
    lanes = [slice(p * LANES, (p + 1) * LANES) for p in range(n_pairs)]
    starts = list(range(0, rows, chunk))
    ops = [{k: q[k][c0:c0 + chunk, ln] for k in ("kap", "rt", "bet", "kt", "v")}
           for c0 in starts for ln in lanes]
    st = _rwkv_local(ops)
    states = [s_scr[p] for p in range(n_pairs)]
    y_rows = []
    for i, c0 in enumerate(starts):
        g_ends = [q["gam"][c0 + chunk - 1:c0 + chunk, ln] for ln in lanes]
        ys, states = _rwkv_state(st[i * n_pairs:(i + 1) * n_pairs], states, g_ends)
        y_rows.append(jnp.concatenate(ys, axis=1))
    for p in range(n_pairs):
        s_scr[p] = states[p]
        wkv_ref[0, 2 * p] = states[p][0:A_HEAD_DIM, 0:A_HEAD_DIM]
        wkv_ref[0, 2 * p + 1] = states[p][A_HEAD_DIM:LANES, A_HEAD_DIM:LANES]
    y = jnp.concatenate(y_rows, axis=0) if len(y_rows) > 1 else y_rows[0]
    y_ref[...] = _rwkv_finish(y, q["bonus"], q["g"], vec_ref[...])
    tail = cols[rows - SUBLANES:rows, :]
    prev_scr[...] = tail
    last_ref[0] = tail


def _rwkv_param_specs(l):
    return [_layer_spec(l, (1, A_COLS)), _layer_spec(l, (SUBLANES, A_WIDTH)), _layer_spec(l, (LANES, A_WIDTH)),
            _layer_spec(l, (LANES, A_WIDTH)), _layer_spec(l, (LANES, A_WIDTH))]


def _rwkv_param_args(wts):
    return (wts["mu"], wts["a_vec"], wts["wdec"], wts["wa"], wts["wg"])


def _rwkv_prompt(cols_a, nb, seq, wts, l):
    tb = RWKV_BLOCK
    nt = seq // tb
    blk = lambda b, t: (b * nt + t, 0)
    per_b3 = lambda b, t: (b, 0, 0)
    return pl.pallas_call(
        _rwkv_prompt_kernel,
        grid=(nb, nt),
        in_specs=[pl.BlockSpec((tb, A_COLS), blk)] + _rwkv_param_specs(l),
        out_specs=[pl.BlockSpec((tb, A_WIDTH), blk),
                   pl.BlockSpec((1, A_HEADS, A_HEAD_DIM, A_HEAD_DIM), lambda b, t: (b, 0, 0, 0)),
                   pl.BlockSpec((1, SUBLANES, A_COLS), per_b3)],
        out_shape=[jax.ShapeDtypeStruct((nb * seq, A_WIDTH), F32),
                   jax.ShapeDtypeStruct((nb, A_HEADS, A_HEAD_DIM, A_HEAD_DIM), F32),
                   jax.ShapeDtypeStruct((nb, SUBLANES, A_COLS), F32)],
        scratch_shapes=[pltpu.VMEM((A_HEADS // 2, LANES, LANES), F32), pltpu.VMEM((SUBLANES, A_COLS), F32)],
        compiler_params=_params(2),
        name="rwkv_prompt",
    )(cols_a, *_rwkv_param_args(wts))


def _rwkv_sample_kernel(*refs, steps, n_prev):
    cols_ref, shift_ref, wkv0_ref, mu_ref, vec_ref, wdec_ref, wa_ref, wg_ref = refs[0:8]
    prev_refs = refs[8:8 + 2 * n_prev]
    y_ref, shift_out_ref, wkv_out_ref = refs[8 + 2 * n_prev:]
    rows = cols_ref.shape[0]
    nseq = rows // steps
    cols = cols_ref[...]
    tloc = _row_iota(cols.shape) & (steps - 1)
    shift0 = jnp.broadcast_to(shift_ref[...], (nseq, steps, A_COLS)).reshape(rows, A_COLS)
    prev = jnp.where(tloc == 0, shift0, _shift_rows(cols, 1))
    q = _rwkv_prep(cols, prev, mu_ref[...], vec_ref[...], wdec_ref[...], wa_ref[...], wg_ref[...], steps)
    n_pairs = A_HEADS // 2
    lanes = [slice(p * LANES, (p + 1) * LANES) for p in range(n_pairs)]
    chains = [(s, p) for s in range(nseq) for p in range(n_pairs)]
    ops = [{k: q[k][s * steps:(s + 1) * steps, lanes[p]] for k in ("kap", "rt", "bet", "kt", "v")}
           for s, p in chains]
    states = [_pair_state(wkv0_ref, s, p) for s, p in chains]
    g_ends = [q["gam"][(s + 1) * steps - 1:(s + 1) * steps, lanes[p]] for s, p in chains]
    ys, new_states = _rwkv_state(_rwkv_local(ops), states, g_ends)
    _stack_previous(prev_refs[0::2], shift_out_ref)
    _stack_previous(prev_refs[1::2], wkv_out_ref)
    for (s, p), s_new in zip(chains, new_states):
        wkv_out_ref[n_prev, s, 2 * p] = s_new[0:A_HEAD_DIM, 0:A_HEAD_DIM]
        wkv_out_ref[n_prev, s, 2 * p + 1] = s_new[A_HEAD_DIM:LANES, A_HEAD_DIM:LANES]
    shift_out_ref[n_prev] = _last_steps(cols, nseq, steps, 1)
    y = jnp.concatenate([jnp.concatenate(ys[s * n_pairs:(s + 1) * n_pairs], axis=1) for s in range(nseq)],
                        axis=0)
    y_ref[...] = _rwkv_finish(y, q["bonus"], q["g"], vec_ref[...])


def _rwkv_sample(cols_a, shift0, wkv0, prevs, nb, steps, wts, l):
    bb = SAMPLE_SEQS
    rows = bb * steps
    n_prev = len(prevs)
    blk = lambda i: (i, 0)
    seq3 = lambda i: (0, i, 0, 0)
    seq4 = lambda i: (0, i, 0, 0, 0)
    prev_specs, prev_args = [], []
    for sh, wk in prevs:
        prev_specs += [pl.BlockSpec((None, bb, 1, A_COLS), seq3),
                       pl.BlockSpec((None, bb, A_HEADS, A_HEAD_DIM, A_HEAD_DIM), seq4)]
        prev_args += [sh, wk]
    return pl.pallas_call(
        functools.partial(_rwkv_sample_kernel, steps=steps, n_prev=n_prev),
        grid=(nb // bb,),
        in_specs=[pl.BlockSpec((rows, A_COLS), blk),
                  pl.BlockSpec((None, bb, 1, A_COLS), lambda i: (l, i, 0, 0)),
                  pl.BlockSpec((None, bb, A_HEADS, A_HEAD_DIM, A_HEAD_DIM), lambda i: (l, i, 0, 0, 0))]
                 + _rwkv_param_specs(l) + prev_specs,
        out_specs=[pl.BlockSpec((rows, A_WIDTH), blk),
                   pl.BlockSpec((n_prev + 1, bb, 1, A_COLS), seq3),
                   pl.BlockSpec((n_prev + 1, bb, A_HEADS, A_HEAD_DIM, A_HEAD_DIM), seq4)],
        out_shape=[jax.ShapeDtypeStruct((nb * steps, A_WIDTH), F32),
                   jax.ShapeDtypeStruct((n_prev + 1, nb, 1, A_COLS), F32),
                   jax.ShapeDtypeStruct((n_prev + 1, nb, A_HEADS, A_HEAD_DIM, A_HEAD_DIM), F32)],
        compiler_params=_params(1),
        name="rwkv_sample",
    )(cols_a, shift0, wkv0, *_rwkv_param_args(wts), *prev_args)


def _lru_gates(xc, vec, bias, wgates):
    lam = vec[5:6, :]
    gates = _sigmoid(_dot(_bf(xc), wgates) + bias)
    gate_r = gates[:, 0:B_WIDTH]
    gate_i = gates[:, B_WIDTH:2 * B_WIDTH]
    log_a = (-LRU_C) * gate_r * _softplus(-lam)
    a = jnp.exp(log_a)
    mult = jnp.sqrt(-jnp.tanh(log_a) * (a * a + 1.0))
    return a, mult * gate_i * xc


def _lru_body(gate_br, xb, conv_prev, h0, tloc, tlen, vec, bias, wgates):
    rows = xb.shape[0]
    conv_b = vec[4:5, :]
    g_out = vec[6:7, :]
    xc = conv_b + vec[3:4, :] * xb
    for j in (1, 2, 3):
        tail = conv_prev if j == 3 else _shift_rows(conv_prev, rows - (3 - j))
        xc = xc + vec[3 - j:4 - j, :] * jnp.where(tloc >= j, _shift_rows(xb, j), tail)
    a, b = _lru_gates(xc, vec, bias, wgates)
    b = b + jnp.where(tloc == 0, a * h0, 0.0)
    d = 1
    while d < tlen:
        keep = tloc >= d
        a_s = jnp.where(keep, _shift_rows(a, d), 1.0)
        b_s = jnp.where(keep, _shift_rows(b, d), 0.0)
        b = a * b_s + b
        a = a * a_s
        d *= 2
    hs = b
    y = hs * _gelu_tanh(gate_br)
    return _rms(y, g_out), hs


def _lru_tm_kernel(cols_ref, vec_ref, bias_ref, wg_ref, y_ref, conv_ref, h_ref, x_scr, a_scr, b_scr, h_scr, *, nb):
    rows = a_scr.shape[0]
    hist = (CONV_WIDTH - 1) * nb

    @pl.when(pl.program_id(0) == 0)
    def _():
        x_scr[0:hist, :] = jnp.zeros((hist, B_WIDTH), F32)
        h_scr[...] = jnp.zeros_like(h_scr)

    vec = vec_ref[...]
    gate_br = jnp.concatenate([cols_ref[0], cols_ref[1]], axis=1)
    xb = jnp.concatenate([cols_ref[2], cols_ref[3]], axis=1)
    x_scr[hist:hist + rows, :] = xb
    xc = vec[4:5, :] + vec[3:4, :] * xb
    for j in range(1, CONV_WIDTH):
        xc = xc + vec[3 - j:4 - j, :] * x_scr[hist - j * nb:hist - j * nb + rows, :]
    a, b = _lru_gates(xc, vec, bias_ref[...], wg_ref[...])
    a_scr[...] = a
    b_scr[...] = b

    def step(i, h):
        rw = pl.ds(pl.multiple_of(i * nb, nb), nb)
        h = a_scr[rw, :] * h + b_scr[rw, :]
        b_scr[rw, :] = h
        return h

    h = lax.fori_loop(0, rows // nb, step, h_scr[...], unroll=8)
    h_scr[...] = h
    h_ref[...] = h
    tail = x_scr[rows:rows + hist, :]
    conv_ref[...] = tail
    x_scr[0:hist, :] = tail
    y = _rms(b_scr[...] * _gelu_tanh(gate_br), vec[6:7, :])
    for j, slab in enumerate(_lane_slabs(y)):
        y_ref[j] = slab


def _lru_param_specs(l):
    return [_layer_spec(l, (SUBLANES, B_WIDTH)), _layer_spec(l, (1, 2 * B_WIDTH)),
            _layer_spec(l, (B_WIDTH, 2 * B_WIDTH))]


def _lru_param_args(wts):
    return (wts["b_vec"], wts["b_bias"], wts["b_wgates"])


def _lru_tm(cols_b, nb, seq, wts, l):
    rows = SCAN_STEPS * nb
    hist = (CONV_WIDTH - 1) * nb
    slab = lambda t: (0, t, 0)
    fixed = lambda t: (0, 0)
    return pl.pallas_call(
        functools.partial(_lru_tm_kernel, nb=nb),
        grid=(seq // SCAN_STEPS,),
        in_specs=[pl.BlockSpec((2 * B_WIDTH // LANES, rows, LANES), slab)] + _lru_param_specs(l),
        out_specs=[pl.BlockSpec((B_WIDTH // LANES, rows, LANES), slab), pl.BlockSpec((hist, B_WIDTH), fixed),
                   pl.BlockSpec((nb, B_WIDTH), fixed)],
        out_shape=[jax.ShapeDtypeStruct((B_WIDTH // LANES, nb * seq, LANES), F32),
                   jax.ShapeDtypeStruct((hist, B_WIDTH), F32),
                   jax.ShapeDtypeStruct((nb, B_WIDTH), F32)],
        scratch_shapes=[pltpu.VMEM((hist + rows, B_WIDTH), F32), pltpu.VMEM((rows, B_WIDTH), F32),
                        pltpu.VMEM((rows, B_WIDTH), F32), pltpu.VMEM((nb, B_WIDTH), F32)],
        compiler_params=_params(1),
        name="lru_tm",
    )(cols_b, *_lru_param_args(wts))


def _lru_sample_kernel(*refs, steps, n_prev):
    cols_ref, conv0_ref, h0_ref, vec_ref, bias_ref, wg_ref = refs[0:6]
    prev_refs = refs[6:6 + 2 * n_prev]
    y_ref, conv_out_ref, h_out_ref = refs[6 + 2 * n_prev:]
    rows = cols_ref.shape[0]
    nseq = rows // steps
    gate_br = cols_ref[:, 0:B_WIDTH]
    xb = cols_ref[:, B_WIDTH:2 * B_WIDTH]
    tloc = _row_iota(xb.shape) & (steps - 1)
    h0 = jnp.broadcast_to(h0_ref[...], (nseq, steps, B_WIDTH)).reshape(rows, B_WIDTH)
    y, hs = _lru_body(gate_br, xb, conv0_ref[...], h0, tloc, steps, vec_ref[...], bias_ref[...], wg_ref[...])
    y_ref[...] = y
    _stack_previous(prev_refs[0::2], conv_out_ref)
    _stack_previous(prev_refs[1::2], h_out_ref)
    conv_out_ref[n_prev] = _last_steps(xb, nseq, steps, CONV_WIDTH - 1)
    h_out_ref[n_prev] = _last_steps(hs, nseq, steps, 1)


def _lru_sample(cols_b, conv0_rows, h0, prevs, nb, steps, wts, l):
    bb = SAMPLE_SEQS
    rows = bb * steps
    n_prev = len(prevs)
    blk = lambda i: (i, 0)
    seq3 = lambda i: (0, i, 0, 0)
    prev_specs, prev_args = [], []
    for cv, hh in prevs:
        prev_specs += [pl.BlockSpec((None, bb, CONV_WIDTH - 1, B_WIDTH), seq3),
                       pl.BlockSpec((None, bb, 1, B_WIDTH), seq3)]
        prev_args += [cv, hh]
    return pl.pallas_call(
        functools.partial(_lru_sample_kernel, steps=steps, n_prev=n_prev),
        grid=(nb // bb,),
        in_specs=[pl.BlockSpec((rows, 2 * B_WIDTH), blk),
                  pl.BlockSpec((None, rows, B_WIDTH), lambda i: (l, i, 0)),
                  pl.BlockSpec((None, bb, 1, B_WIDTH), lambda i: (l, i, 0, 0))]
                 + _lru_param_specs(l) + prev_specs,
        out_specs=[pl.BlockSpec((rows, B_WIDTH), blk),
                   pl.BlockSpec((n_prev + 1, bb, CONV_WIDTH - 1, B_WIDTH), seq3),
                   pl.BlockSpec((n_prev + 1, bb, 1, B_WIDTH), seq3)],
        out_shape=[jax.ShapeDtypeStruct((nb * steps, B_WIDTH), F32),
                   jax.ShapeDtypeStruct((n_prev + 1, nb, CONV_WIDTH - 1, B_WIDTH), F32),
                   jax.ShapeDtypeStruct((n_prev + 1, nb, 1, B_WIDTH), F32)],
        compiler_params=_params(1),
        name="lru_sample",
    )(cols_b, conv0_rows, h0, *_lru_param_args(wts), *prev_args)


def _s5_drive(u, lam, bbd):
    lr = lam[0:1, :]
    li = lam[1:2, :]
    dt = jnp.exp(lam[2:3, :])
    mag = jnp.exp(lr * dt)
    ar = mag * jnp.cos(li * dt)
    ai = mag * jnp.sin(li * dt)
    den = lr * lr + li * li
    cr = ((ar - 1.0) * lr + ai * li) / den
    ci = (ai * lr - (ar - 1.0) * li) / den
    ub = _bf(u)
    pb = _dot(ub, bbd[:, 0:S5_W])
    qb = _dot(ub, bbd[:, S5_W:2 * S5_W])
    return ar, ai, pb * cr - qb * ci, pb * ci + qb * cr


def _s5_readout(hr, hi, u, vec, cbd, wglu):
    s5_d = vec[0:1, :]
    b_glu = vec[1:2, :]
    g_out = vec[2:3, :]
    y = _dot(_bf(hr), cbd[0:S5_W, :]) + _dot(_bf(hi), cbd[S5_W:2 * S5_W, :]) + s5_d * u
    z = _gelu_tanh(y)
    out = z * _sigmoid(_dot(_bf(z), wglu) + b_glu)
    return _rms(out, g_out)


def _s5_body(u, h0r, h0i, tloc, tlen, lam, vec, bbd, cbd, wglu):
    ar, ai, hr, hi = _s5_drive(u, lam, bbd)
    first = tloc == 0
    hr = hr + jnp.where(first, ar * h0r - ai * h0i, 0.0)
    hi = hi + jnp.where(first, ar * h0i + ai * h0r, 0.0)
    pr, pi = ar, ai
    d = 1
    while d < tlen:
        keep = tloc >= d
        sr = jnp.where(keep, _shift_rows(hr, d), 0.0)
        si = jnp.where(keep, _shift_rows(hi, d), 0.0)
        hr, hi = hr + pr * sr - pi * si, hi + pr * si + pi * sr
        pr, pi = pr * pr - pi * pi, 2.0 * pr * pi
        d *= 2
    return _s5_readout(hr, hi, u, vec, cbd, wglu), hr, hi


def _s5_tm_kernel(u_ref, lam_ref, vec_ref, bbd_ref, cbd_ref, wglu_ref, y_ref, hr_ref, hi_ref,
                  re_scr, im_scr, sr_scr, si_scr, *, nb):
    rows = re_scr.shape[0]

    @pl.when(pl.program_id(0) == 0)
    def _():
        sr_scr[...] = jnp.zeros_like(sr_scr)
        si_scr[...] = jnp.zeros_like(si_scr)

    u = jnp.concatenate([u_ref[j] for j in range(C_WIDTH // LANES)], axis=1)
    ar, ai, dr, di = _s5_drive(u, lam_ref[...], bbd_ref[...])
    re_scr[...] = dr
    im_scr[...] = di
    ar = jnp.broadcast_to(ar, (nb, S5_W))
    ai = jnp.broadcast_to(ai, (nb, S5_W))

    def step(i, carry):
        hr, hi = carry
        rw = pl.ds(pl.multiple_of(i * nb, nb), nb)
        nr = ar * hr - ai * hi + re_scr[rw, :]
        ni = ar * hi + ai * hr + im_scr[rw, :]
        re_scr[rw, :] = nr
        im_scr[rw, :] = ni
        return nr, ni

    hr, hi = lax.fori_loop(0, rows // nb, step, (sr_scr[...], si_scr[...]), unroll=4)
    sr_scr[...] = hr
    si_scr[...] = hi
    hr_ref[...] = hr
    hi_ref[...] = hi
    y = _s5_readout(re_scr[...], im_scr[...], u, vec_ref[...], cbd_ref[...], wglu_ref[...])
    for j, slab in enumerate(_lane_slabs(y)):
        y_ref[j] = slab


def _s5_param_specs(l):
    return [_layer_spec(l, (SUBLANES, S5_W)), _layer_spec(l, (SUBLANES, C_WIDTH)),
            _layer_spec(l, (C_WIDTH, 2 * S5_W)), _layer_spec(l, (2 * S5_W, C_WIDTH)),
            _layer_spec(l, (C_WIDTH, C_WIDTH))]


def _s5_param_args(wts):
    return (wts["c_lam"], wts["c_vec"], wts["c_bbd"], wts["c_cbd"], wts["c_wglu"])


def _s5_tm(cols_c, nb, seq, wts, l):
    rows = SCAN_STEPS * nb
    slab = lambda t: (0, t, 0)
    fixed = lambda t: (0, 0)
    n_slabs = C_WIDTH // LANES
    return pl.pallas_call(
        functools.partial(_s5_tm_kernel, nb=nb),
        grid=(seq // SCAN_STEPS,),
        in_specs=[pl.BlockSpec((n_slabs, rows, LANES), slab)] + _s5_param_specs(l),
        out_specs=[pl.BlockSpec((n_slabs, rows, LANES), slab), pl.BlockSpec((nb, S5_W), fixed),
                   pl.BlockSpec((nb, S5_W), fixed)],
        out_shape=[jax.ShapeDtypeStruct((n_slabs, nb * seq, LANES), F32),
                   jax.ShapeDtypeStruct((nb, S5_W), F32), jax.ShapeDtypeStruct((nb, S5_W), F32)],
        scratch_shapes=[pltpu.VMEM((rows, S5_W), F32), pltpu.VMEM((rows, S5_W), F32),
                        pltpu.VMEM((nb, S5_W), F32), pltpu.VMEM((nb, S5_W), F32)],
        compiler_params=_params(1),
        name="s5_tm",
    )(cols_c, *_s5_param_args(wts))


def _s5_sample_kernel(*refs, steps, n_prev):
    u_ref, h0r_ref, h0i_ref, lam_ref, vec_ref, bbd_ref, cbd_ref, wglu_ref = refs[0:8]
    prev_refs = refs[8:8 + 2 * n_prev]
    y_ref, hr_out_ref, hi_out_ref = refs[8 + 2 * n_prev:]
    rows = u_ref.shape[0]
    nseq = rows // steps
    tloc = _row_iota((rows, S5_W)) & (steps - 1)
    h0r = jnp.broadcast_to(h0r_ref[...], (nseq, steps, S5_W)).reshape(rows, S5_W)
    h0i = jnp.broadcast_to(h0i_ref[...], (nseq, steps, S5_W)).reshape(rows, S5_W)
    y, hr, hi = _s5_body(u_ref[...], h0r, h0i, tloc, steps, lam_ref[...], vec_ref[...], bbd_ref[...],
                         cbd_ref[...], wglu_ref[...])
    y_ref[...] = y
    _stack_previous(prev_refs[0::2], hr_out_ref)
    _stack_previous(prev_refs[1::2], hi_out_ref)
    hr_out_ref[n_prev] = _last_steps(hr, nseq, steps, 1)
    hi_out_ref[n_prev] = _last_steps(hi, nseq, steps, 1)


def _s5_sample(cols_c, h0r, h0i, prevs, nb, steps, wts, l):
    bb = SAMPLE_SEQS
    rows = bb * steps
    n_prev = len(prevs)
    blk = lambda i: (i, 0)
    seq3 = lambda i: (0, i, 0, 0)
    st = pl.BlockSpec((None, bb, 1, S5_W), lambda i: (l, i, 0, 0))
    prev_specs, prev_args = [], []
    for pr, pi in prevs:
        prev_specs += [pl.BlockSpec((None, bb, 1, S5_W), seq3)] * 2
        prev_args += [pr, pi]
    stacked = pl.BlockSpec((n_prev + 1, bb, 1, S5_W), seq3)
    return pl.pallas_call(
        functools.partial(_s5_sample_kernel, steps=steps, n_prev=n_prev),
        grid=(nb // bb,),
        in_specs=[pl.BlockSpec((rows, C_WIDTH), blk), st, st] + _s5_param_specs(l) + prev_specs,
        out_specs=[pl.BlockSpec((rows, C_WIDTH), blk), stacked, stacked],
        out_shape=[jax.ShapeDtypeStruct((nb * steps, C_WIDTH), F32),
                   jax.ShapeDtypeStruct((n_prev + 1, nb, 1, S5_W), F32),
                   jax.ShapeDtypeStruct((n_prev + 1, nb, 1, S5_W), F32)],
        compiler_params=_params(1),
        name="s5_sample",
    )(cols_c, h0r, h0i, *_s5_param_args(wts), *prev_args)


def _rows8(rows, width):
    m = jnp.stack([r.reshape(r.shape[0], width) for r in rows], axis=1)
    return jnp.pad(m, ((0, 0), (0, SUBLANES - m.shape[1]), (0, 0)))


def _stacked_weights(w):
    depth = w["g_mix"].shape[0]
    eye_b = jnp.eye(B_BLOCKS, dtype=F32)
    eye_g = jnp.eye(S5_GROUPS, dtype=F32)
    bd4 = lambda m: jnp.einsum("lnde,nm->lndme", m, eye_b).reshape(depth, B_WIDTH, B_WIDTH)
    zeros_lora = jnp.zeros((depth, LANES - 64, A_WIDTH), F32)
    b_in = lambda m: jnp.einsum("lgpc,gh->lgchp", m, eye_g).reshape(depth, C_WIDTH, S5_W)
    c_out = lambda m: jnp.einsum("lgcp,gh->lgphc", m, eye_g).reshape(depth, S5_W, C_WIDTH)
    g_final = jnp.broadcast_to(w["g_final"][None], (depth, D_MODEL))
    return dict(
        g_mix=w["g_mix"][:, None, :],
        w_in=_bf(w["w_in"]),
        mu=w["mu_a"][:, None, :],
        a_vec=_rows8([w["w0"], w["a0"], w["k_k"], w["k_a"], w["r_k"], w["lnx_w"], w["lnx_b"]], A_WIDTH),
        wdec=_bf(jnp.concatenate([w["w_dec2"], zeros_lora], axis=1)),
        wa=_bf(jnp.concatenate([zeros_lora, w["w_a2"]], axis=1)),
        wg=_bf(w["w_g2"]),
        b_vec=_rows8([w["conv_w"][:, 0], w["conv_w"][:, 1], w["conv_w"][:, 2], w["conv_w"][:, 3],
                      w["conv_b"], w["lru_lambda"], w["g_out_b"]], B_WIDTH),
        b_bias=jnp.concatenate([w["b_rg"], w["b_ig"]], axis=1)[:, None, :],
        b_wgates=_bf(jnp.concatenate([bd4(w["w_rg"]), bd4(w["w_ig"])], axis=2)),
        c_lam=_rows8([w["s5_lam_re"], w["s5_lam_im"], jnp.repeat(w["s5_log_dt"], S5_STATE, axis=1)], S5_W),
        c_vec=_rows8([w["s5_d"], w["b_glu"], w["g_out_c"]], C_WIDTH),
        c_bbd=_bf(jnp.concatenate([b_in(w["s5_b_re"]), b_in(w["s5_b_im"])], axis=2)),
        c_cbd=_bf(jnp.concatenate([c_out(w["s5_c_re"]), -c_out(w["s5_c_im"])], axis=1)),
        c_wglu=_bf(w["w_glu"]),
        w_out=_bf(w["w_out"]),
        p_vec=_rows8([w["g_ffn"], w["g_ple"], g_final], D_MODEL),
        w_up=_bf(w["w_ffn_up"]),
        w_down=_bf(w["w_ffn_down"]),
        w_ple=_bf(w["w_ple"]),
        w_gate=_bf(w["w_ple_gate"]),
    )


def _run_prompt(x, p, wts, depth):
    nb, seq, _ = x.shape
    h = x.reshape(nb * seq, D_MODEL)
    p = p.reshape(depth, nb * seq, PLE_DIM)
    outs = []
    for l in range(depth):
        cols_a, cols_b, cols_c = _proj_in_tm(h, wts["g_mix"], wts["w_in"], l, nb, seq)
        ya, wkv, last = _rwkv_prompt(cols_a, nb, seq, wts, l)
        yb, conv, lru = _lru_tm(cols_b, nb, seq, wts, l)
        yc, s5r, s5i = _s5_tm(cols_c, nb, seq, wts, l)
        h = _post(h, ya, yb, yc, p, wts, l, l == depth - 1, nb_tm=nb)
        outs.append((last[:, SUBLANES - 1], wkv, conv, lru, s5r, s5i))
    shift, wkv, conv, lru, s5r, s5i = (jnp.stack([o[j] for o in outs], axis=0) for j in range(6))
    conv = jnp.swapaxes(conv.reshape(depth, CONV_WIDTH - 1, nb, B_WIDTH), 1, 2)
    s5_shape = (depth, nb, S5_GROUPS, S5_STATE)
    return h.reshape(nb, seq, D_MODEL), (shift, wkv, conv, lru, s5r.reshape(s5_shape), s5i.reshape(s5_shape))


def _run_sample(x, p, states, wts, depth):
    nb, steps, _ = x.shape
    st_shift, st_wkv, st_conv, st_lru, st_s5r, st_s5i = states
    h = x.reshape(nb * steps, D_MODEL)
    p = p.reshape(depth, nb * steps, PLE_DIM)
    shift0 = st_shift[:, :, None, :]
    conv0_rows = jnp.pad(st_conv, ((0, 0), (0, 0), (0, steps - (CONV_WIDTH - 1)), (0, 0))
                         ).reshape(depth, nb * steps, B_WIDTH)
    lru0 = st_lru[:, :, None, :]
    s5r0 = st_s5r.reshape(depth, nb, 1, S5_W)
    s5i0 = st_s5i.reshape(depth, nb, 1, S5_W)
    prev_a, prev_b, prev_c = [], [], []
    for l in range(depth):
        last = l == depth - 1
        cols_a, cols_b, cols_c = _proj_in(h, wts["g_mix"], wts["w_in"], l)
        ya, shift, wkv = _rwkv_sample(cols_a, shift0, st_wkv, prev_a if last else [], nb, steps, wts, l)
        yb, conv, lru = _lru_sample(cols_b, conv0_rows, lru0, prev_b if last else [], nb, steps, wts, l)
        yc, s5r, s5i = _s5_sample(cols_c, s5r0, s5i0, prev_c if last else [], nb, steps, wts, l)
        h = _post(h, ya, yb, yc, p, wts, l, last)
        prev_a.append((shift, wkv))
        prev_b.append((conv, lru))
        prev_c.append((s5r, s5i))
    s5_shape = (depth, nb, S5_GROUPS, S5_STATE)
    new_states = (shift.reshape(depth, nb, A_COLS), wkv, conv, lru.reshape(depth, nb, B_WIDTH),
                  s5r.reshape(s5_shape), s5i.reshape(s5_shape))
    return h.reshape(nb, steps, D_MODEL), new_states


def kernel(x_prompt, x_sample, p_prompt, p_sample, state_shift, state_wkv, state_conv, state_lru, state_s5_re, state_s5_im, g_mix, w_in, mu_a, w0, w_dec2, a0, w_a2, w_g2, k_k, k_a, r_k, lnx_w, lnx_b, conv_w, conv_b, w_rg, b_rg, w_ig, b_ig, lru_lambda, g_out_b, s5_lam_re, s5_lam_im, s5_log_dt, s5_b_re, s5_b_im, s5_c_re, s5_c_im, s5_d, w_glu, b_glu, g_out_c, w_out, g_ffn, w_ffn_up, w_ffn_down, g_ple, w_ple, w_ple_gate, g_final):
    w = dict(g_mix=g_mix, w_in=w_in, mu_a=mu_a, w0=w0, w_dec2=w_dec2, a0=a0, w_a2=w_a2, w_g2=w_g2, k_k=k_k,
             k_a=k_a, r_k=r_k, lnx_w=lnx_w, lnx_b=lnx_b, conv_w=conv_w, conv_b=conv_b, w_rg=w_rg, b_rg=b_rg,
             w_ig=w_ig, b_ig=b_ig, lru_lambda=lru_lambda, g_out_b=g_out_b, s5_lam_re=s5_lam_re,
             s5_lam_im=s5_lam_im, s5_log_dt=s5_log_dt, s5_b_re=s5_b_re, s5_b_im=s5_b_im, s5_c_re=s5_c_re,
             s5_c_im=s5_c_im, s5_d=s5_d, w_glu=w_glu, b_glu=b_glu, g_out_c=g_out_c, w_out=w_out, g_ffn=g_ffn,
             w_ffn_up=w_ffn_up, w_ffn_down=w_ffn_down, g_ple=g_ple, w_ple=w_ple, w_ple_gate=w_ple_gate,
             g_final=g_final)
    depth = g_mix.shape[0]
    wts = _stacked_weights(w)
    y_prompt, new_p = _run_prompt(x_prompt, p_prompt, wts, depth)
    y_sample, new_s = _run_sample(x_sample, p_sample,
                                  (state_shift, state_wkv, state_conv, state_lru, state_s5_re, state_s5_im),
                                  wts, depth)
    return (y_prompt, y_sample) + new_p + new_s
```

```python
import functools
import math

import jax
import jax.numpy as jnp
from jax import lax
from jax.experimental import pallas as pl
from jax.experimental.pallas import tpu as pltpu

F32 = jnp.float32
BF16 = jnp.bfloat16

D_MODEL = 1024
A_WIDTH = 512
A_HEADS = 8
A_HEAD_DIM = 64
A_COLS = 1792
B_WIDTH = 256
B_BLOCKS = 4
CONV_WIDTH = 4
C_WIDTH = 256
S5_GROUPS = 16
S5_GROUP_CH = 16
S5_STATE = 64
S5_W = S5_GROUPS * S5_STATE
IN_COLS = A_COLS + 2 * B_WIDTH + C_WIDTH
D_FF = 2816
PLE_DIM = 256
LRU_C = 8.0
RMS_EPS = 1e-6
GN_EPS = 64e-5

LANES = 128
SUBLANES = 8
VMEM_LIMIT_BYTES = 56 * 1024 * 1024

TOKEN_TILE = 512
FFN_CHUNK = 256
RWKV_CHUNK = 64
RWKV_BLOCK = 256
SCAN_STEPS = 128
SAMPLE_SEQS = 8


def _layer_spec(l, shape):
    nd = len(shape)
    return pl.BlockSpec((None,) + tuple(shape), lambda *_: (l,) + (0,) * nd, pipeline_mode=pl.Buffered(1))


def _params(n_axes):
    return pltpu.CompilerParams(dimension_semantics=("arbitrary",) * n_axes,
                                vmem_limit_bytes=VMEM_LIMIT_BYTES)


def _dot(a, b):
    return jnp.dot(a, b, preferred_element_type=F32)


def _dot_nt(a, b):
    return lax.dot_general(a, b, (((1,), (1,)), ((), ())), preferred_element_type=F32)


def _dot_tn(a, b):
    return lax.dot_general(a, b, (((0,), (0,)), ((), ())), preferred_element_type=F32)


def _bf(x):
    return x.astype(BF16)


def _rms(x, g):
    inv = lax.rsqrt(jnp.mean(x * x, axis=-1, keepdims=True) + RMS_EPS)
    return x * inv * g


def _sigmoid(x):
    return 1.0 / (1.0 + jnp.exp(-x))


def _softplus(x):
    return jnp.maximum(x, 0.0) + jnp.log(1.0 + jnp.exp(-jnp.abs(x)))


def _gelu_tanh(x):
    c = math.sqrt(2.0 / math.pi)
    return 0.5 * x * (1.0 + jnp.tanh(c * (x + 0.044715 * (x * x * x))))


def _row_iota(shape):
    return lax.broadcasted_iota(jnp.int32, shape, 0)


def _shift_rows(x, d):
    return pltpu.roll(x, d, axis=0)


def _lane_slabs(x):
    return [x[:, j * LANES:(j + 1) * LANES] for j in range(x.shape[1] // LANES)]


def _last_steps(x, nseq, steps, n):
    return x.reshape(nseq, steps, x.shape[1])[:, steps - n:steps, :]


def _stack_previous(prev_refs, out_ref):
    for j, ref in enumerate(prev_refs):
        out_ref[j] = ref[...]


def _proj_in_kernel(h_ref, g_ref, w_ref, a_ref, b_ref, c_ref):
    xn = _bf(_rms(h_ref[...], g_ref[...]))
    a_ref[...] = _dot(xn, w_ref[:, 0:A_COLS])
    b_ref[...] = _dot(xn, w_ref[:, A_COLS:A_COLS + 2 * B_WIDTH])
    c_ref[...] = _dot(xn, w_ref[:, A_COLS + 2 * B_WIDTH:IN_COLS])


def _proj_in(h, g, w_bf, l):
    n = h.shape[0]
    tm = TOKEN_TILE
    row = lambda i: (i, 0)
    return pl.pallas_call(
        _proj_in_kernel,
        grid=(n // tm,),
        in_specs=[pl.BlockSpec((tm, D_MODEL), row), _layer_spec(l, (1, D_MODEL)),
                  _layer_spec(l, (D_MODEL, IN_COLS))],
        out_specs=[pl.BlockSpec((tm, A_COLS), row), pl.BlockSpec((tm, 2 * B_WIDTH), row),
                   pl.BlockSpec((tm, C_WIDTH), row)],
        out_shape=[jax.ShapeDtypeStruct((n, A_COLS), F32), jax.ShapeDtypeStruct((n, 2 * B_WIDTH), F32),
                   jax.ShapeDtypeStruct((n, C_WIDTH), F32)],
        compiler_params=_params(1),
        name="proj_in",
    )(h, g, w_bf)


def _proj_in_tm_kernel(h_ref, g_ref, w_ref, mu_ref, a_ref, last_ref, b_ref, c_ref, prev_scr, *, nb):
    b = pl.program_id(1)
    tm = h_ref.shape[0]

    @pl.when((pl.program_id(0) == 0) & (b == 0))
    def _():
        prev_scr[...] = jnp.zeros_like(prev_scr)

    xn = _bf(_rms(h_ref[...], g_ref[...]))
    cols = _dot(xn, w_ref[:, 0:A_COLS])
    prev = jnp.where(_row_iota(cols.shape) == 0, prev_scr[b, SUBLANES - 1:SUBLANES, :], _shift_rows(cols, 1))
    a_ref[...] = cols + (prev - cols) * mu_ref[...]
    tail = cols[tm - SUBLANES:tm, :]
    prev_scr[b] = tail
    last_ref[b] = tail
    rows_of_b = pl.ds(b, tm, stride=nb)
    for j, slab in enumerate(_lane_slabs(_dot(xn, w_ref[:, A_COLS:A_COLS + 2 * B_WIDTH]))):
        b_ref[j, rows_of_b, :] = slab
    for j, slab in enumerate(_lane_slabs(_dot(xn, w_ref[:, A_COLS + 2 * B_WIDTH:IN_COLS]))):
        c_ref[j, rows_of_b, :] = slab


def _proj_in_tm(h, wts, l, nb, seq):
    tm = TOKEN_TILE
    nt = seq // tm
    row = lambda i, b: (b * nt + i, 0)
    slab = lambda i, b: (0, i, 0)
    nb_slabs, nc_slabs = 2 * B_WIDTH // LANES, C_WIDTH // LANES
    return pl.pallas_call(
        functools.partial(_proj_in_tm_kernel, nb=nb),
        grid=(nt, nb),
        in_specs=[pl.BlockSpec((tm, D_MODEL), row), _layer_spec(l, (1, D_MODEL)),
                  _layer_spec(l, (D_MODEL, IN_COLS)), _layer_spec(l, (1, A_COLS))],
        out_specs=[pl.BlockSpec((tm, A_COLS), row), pl.BlockSpec((nb, SUBLANES, A_COLS), lambda i, b: (0, 0, 0)),
                   pl.BlockSpec((nb_slabs, tm * nb, LANES), slab),
                   pl.BlockSpec((nc_slabs, tm * nb, LANES), slab)],
        out_shape=[jax.ShapeDtypeStruct((nb * seq, A_COLS), F32),
                   jax.ShapeDtypeStruct((nb, SUBLANES, A_COLS), F32),
                   jax.ShapeDtypeStruct((nb_slabs, nb * seq, LANES), F32),
                   jax.ShapeDtypeStruct((nc_slabs, nb * seq, LANES), F32)],
        scratch_shapes=[pltpu.VMEM((nb, SUBLANES, A_COLS), F32)],
        compiler_params=_params(2),
        name="proj_in_tm",
    )(h, wts["g_mix"], wts["w_in"], wts["mu"])


def _post_kernel(h_ref, ya_ref, yb_ref, yc_ref, p_ref, wo_ref, vec_ref, wup_ref, wdn_ref, wple_ref,
                 wgate_ref, o_ref, act_scr, *, final, nb_tm):
    g_ffn = vec_ref[0:1, :]
    g_ple = vec_ref[1:2, :]
    g_final = vec_ref[2:3, :]
    if nb_tm:
        rows_of_b = pl.ds(pl.program_id(1), h_ref.shape[0], stride=nb_tm)
        yb = jnp.concatenate([yb_ref[j, rows_of_b, :] for j in range(B_WIDTH // LANES)], axis=1)
        yc = jnp.concatenate([yc_ref[j, rows_of_b, :] for j in range(C_WIDTH // LANES)], axis=1)
    else:
        yb, yc = yb_ref[...], yc_ref[...]
    h1 = (h_ref[...] + _dot(_bf(ya_ref[...]), wo_ref[0:A_WIDTH, :])
          + _dot(_bf(yb), wo_ref[A_WIDTH:A_WIDTH + B_WIDTH, :])
          + _dot(_bf(yc), wo_ref[A_WIDTH + B_WIDTH:D_MODEL, :]))
    xf = _bf(_rms(h1, g_ffn))
    for c0 in range(0, D_FF, FFN_CHUNK):
        gate = _dot(xf, wup_ref[:, c0:c0 + FFN_CHUNK])
        up = _dot(xf, wup_ref[:, D_FF + c0:D_FF + c0 + FFN_CHUNK])
        act_scr[:, c0:c0 + FFN_CHUNK] = _bf(gate * _sigmoid(gate) * up)
    h2 = h1 + _dot(act_scr[...], wdn_ref[...])
    ple = _dot(_bf(p_ref[...]), wple_ref[...])
    gate = _sigmoid(_dot(_bf(_rms(h2, g_ple)), wgate_ref[...]))
    h3 = h2 + ple * gate
    if final:
        h3 = _rms(h3, g_final)
    o_ref[...] = h3


def _post(h, ya, yb, yc, p, wts, l, final, nb_tm=0):
    n = h.shape[0]
    tm = TOKEN_TILE
    if nb_tm:
        nt = n // (nb_tm * tm)
        grid = (nt, nb_tm)
        row = lambda i, b: (b * nt + i, 0)
        prow = lambda i, b: (l, b * nt + i, 0)
        slab = lambda i, b: (0, i, 0)
        yb_spec = pl.BlockSpec((B_WIDTH // LANES, tm * nb_tm, LANES), slab, pipeline_mode=pl.Buffered(1))
        yc_spec = pl.BlockSpec((C_WIDTH // LANES, tm * nb_tm, LANES), slab, pipeline_mode=pl.Buffered(1))
    else:
        grid = (n // tm,)
        row = lambda i: (i, 0)
        prow = lambda i: (l, i, 0)
        yb_spec = pl.BlockSpec((tm, B_WIDTH), row)
        yc_spec = pl.BlockSpec((tm, C_WIDTH), row)
    return pl.pallas_call(
        functools.partial(_post_kernel, final=final, nb_tm=nb_tm),
        grid=grid,
        in_specs=[pl.BlockSpec((tm, D_MODEL), row), pl.BlockSpec((tm, A_WIDTH), row),
                  yb_spec, yc_spec,
                  pl.BlockSpec((None, tm, PLE_DIM), prow),
                  _layer_spec(l, (D_MODEL, D_MODEL)), _layer_spec(l, (SUBLANES, D_MODEL)),
                  _layer_spec(l, (D_MODEL, 2 * D_FF)), _layer_spec(l, (D_FF, D_MODEL)),
                  _layer_spec(l, (PLE_DIM, D_MODEL)), _layer_spec(l, (D_MODEL, D_MODEL))],
        out_specs=pl.BlockSpec((tm, D_MODEL), row),
        out_shape=jax.ShapeDtypeStruct((n, D_MODEL), F32),
        scratch_shapes=[pltpu.VMEM((tm, D_FF), BF16)],
        compiler_params=_params(len(grid)),
        name="post",
    )(h, ya, yb, yc, p, wts["w_out"], wts["p_vec"], wts["w_up"], wts["w_down"], wts["w_ple"], wts["w_gate"])


def _head_sum(x):
    first = lax.broadcasted_iota(jnp.int32, (x.shape[0], LANES), 1) < A_HEAD_DIM
    outs = []
    for xp in _lane_slabs(x):
        s0 = jnp.sum(jnp.where(first, xp, 0.0), axis=-1, keepdims=True)
        s1 = jnp.sum(jnp.where(first, 0.0, xp), axis=-1, keepdims=True)
        outs.append(jnp.where(first, s0, s1))
    return jnp.concatenate(outs, axis=1)


def _rwkv_prep(xs, vec, wdec, wa, wg, chunk):
    rows = xs.shape[0]
    r = xs[:, 0:A_WIDTH]
    k = xs[:, A_WIDTH:2 * A_WIDTH]
    v = xs[:, 2 * A_WIDTH:3 * A_WIDTH]
    xwa = xs[:, 3 * A_WIDTH:3 * A_WIDTH + LANES]
    xg = xs[:, 3 * A_WIDTH + LANES:A_COLS]
    w0, a0, k_k, k_a, r_k = (vec[i:i + 1, :] for i in range(5))
    z = w0 + _dot(_bf(jnp.tanh(xwa)), wdec)
    log_decay = -math.exp(-0.5) * _sigmoid(z)
    a = _sigmoid(a0 + _dot(_bf(xwa), wa))
    g = _dot(_bf(_sigmoid(xg)), wg)
    kk_raw = k * k_k
    kk = kk_raw * lax.rsqrt(jnp.maximum(_head_sum(kk_raw * kk_raw), 1e-24))
    k_mod = k * (1.0 + (a - 1.0) * k_a)
    bonus = _head_sum(r * k_mod * r_k) * v
    ri = _row_iota((rows, rows))
    ci = lax.broadcasted_iota(jnp.int32, (rows, rows), 1)
    same_chunk = (ci & (-chunk)) == (ri & (-chunk))
    tri = _bf(jnp.where((ci <= ri) & same_chunk, 1.0, 0.0))
    h1 = _bf(log_decay)
    h2 = _bf(log_decay - h1.astype(F32))
    cum = _dot(tri, h1) + _dot(tri, h2)
    gam = jnp.exp(cum)
    ginv = jnp.exp(-cum)
    gprev = jnp.exp(cum - log_decay)
    return dict(rt=r * gam, kap=kk * gprev, bet=kk * a * ginv, kt=k_mod * ginv, v=v, gam=gam,
                bonus=bonus, g=g)


def _stack_heads(x):
    first = lax.broadcasted_iota(jnp.int32, x.shape, 1) < A_HEAD_DIM
    return _bf(jnp.concatenate([jnp.where(first, x, 0.0), jnp.where(first, 0.0, x)], axis=0))


def _rwkv_local(ops):
    c = ops[0]["kap"].shape[0]
    c2 = 2 * c
    ri = _row_iota((c, c2))
    ci = lax.broadcasted_iota(jnp.int32, (c, c2), 1) & (c - 1)
    strict = ci < ri
    incl = ci <= ri
    eye = jnp.where(ri == ci, 1.0, 0.0)
    same_head = (_row_iota((c2, c2)) & c) == (lax.broadcasted_iota(jnp.int32, (c2, c2), 1) & c)
    merged = c2 % LANES == 0

    def blockdiag(m):
        return _bf(jnp.where(same_head, jnp.concatenate([m, m], axis=0), 0.0))

    st = []
    for o in ops:
        s = {k + "2": _stack_heads(o[k]) for k in ("kap", "bet", "kt", "v")}
        s["rt"] = o["rt"]
        lhs = _bf(jnp.concatenate([o["kap"], o["rt"]], axis=0))
        if merged:
            g = _dot_nt(lhs, jnp.concatenate([s["bet2"], s["kt2"]], axis=0))
            a_b, a_k, a_rb, a_rk = g[0:c, 0:c2], g[0:c, c2:2 * c2], g[c:c2, 0:c2], g[c:c2, c2:2 * c2]
        else:
            gb, gk = _dot_nt(lhs, s["bet2"]), _dot_nt(lhs, s["kt2"])
            a_b, a_k, a_rb, a_rk = gb[0:c], gk[0:c], gb[c:c2], gk[c:c2]
        s["a_b"] = jnp.where(strict, a_b, 0.0)
        s["a_k"] = _bf(jnp.where(strict, a_k, 0.0))
        s["a_rb"] = _bf(jnp.where(incl, a_rb, 0.0))
        s["a_rk"] = _bf(jnp.where(incl, a_rk, 0.0))
        st.append(s)
    for s in st:
        s["t"] = eye - s["a_b"]
        s["lp"] = s["a_b"]
        s["akv"] = _dot(s["a_k"], s["v2"])
    n = 2
    while n < c:
        for s in st:
            s["lp"] = _dot(_bf(s["lp"]), blockdiag(s["lp"]))
        for s in st:
            s["t"] = s["t"] + _dot(_bf(s["t"]), blockdiag(s["lp"]))
        n *= 2
    for s in st:
        wu = _dot(_bf(s["t"]), jnp.concatenate([s["kap2"], _stack_heads(s["akv"])], axis=1))
        s["w"] = -wu[:, 0:LANES]
        s["u0"] = -wu[:, LANES:2 * LANES]
    return st


def _rwkv_state(st, states, g_ends):
    c = st[0]["rt"].shape[0]
    merged = (2 * c) % LANES == 0
    xs = []
    for s, state in zip(st, states):
        xs.append(_dot_nt(_bf(jnp.concatenate([s["w"], s["rt"]], axis=0)), _bf(state)))
    u2s = [_stack_heads(x[0:c] + s["u0"]) for x, s in zip(xs, st)]
    ys, new_states = [], []
    for s, x, u2 in zip(st, xs, u2s):
        if merged:
            y = x[c:2 * c] + _dot(jnp.concatenate([s["a_rb"], s["a_rk"]], axis=1),
                                  jnp.concatenate([u2, s["v2"]], axis=0))
        else:
            y = x[c:2 * c] + _dot(s["a_rb"], u2) + _dot(s["a_rk"], s["v2"])
        ys.append(y)
    for s, u2, state, g_end in zip(st, u2s, states, g_ends):
        ds = _dot_tn(jnp.concatenate([u2, s["v2"]], axis=0), jnp.concatenate([s["bet2"], s["kt2"]], axis=0))
        new_states.append((state + ds) * g_end)
    return ys, new_states


def _rwkv_finish(y, bonus, g, vec):
    lnx_w = vec[5:6, :]
    lnx_b = vec[6:7, :]
    inv_n = 1.0 / A_HEAD_DIM
    mu = _head_sum(y) * inv_n
    d = y - mu
    var = _head_sum(d * d) * inv_n
    yn = d * lax.rsqrt(var + GN_EPS) * lnx_w + lnx_b
    return (yn + bonus) * g


def _pair_state(s_ref, idx, p):
    z = jnp.zeros((A_HEAD_DIM, A_HEAD_DIM), F32)
    top = jnp.concatenate([s_ref[idx, 2 * p], z], axis=1)
    bot = jnp.concatenate([z, s_ref[idx, 2 * p + 1]], axis=1)
    return jnp.concatenate([top, bot], axis=0)


def _rwkv_prompt_kernel(xs_ref, vec_ref, wdec_ref, wa_ref, wg_ref, y_ref, wkv_ref, s_scr):
    t = pl.program_id(1)
    rows = xs_ref.shape[0]
    chunk = RWKV_CHUNK

    @pl.when(t == 0)
    def _():
        s_scr[...] = jnp.zeros_like(s_scr)

    q = _rwkv_prep(xs_ref[...], vec_ref[...], wdec_ref[...], wa_ref[...], wg_ref[...], chunk)
    n_pairs = A_HEADS // 2
    lanes = [slice(p * LANES, (p + 1) * LANES) for p in range(n_pairs)]
    starts = list(range(0, rows, chunk))
    ops = [{k: q[k][c0:c0 + chunk, ln] for k in ("kap", "rt", "bet", "kt", "v")}
           for c0 in starts for ln in lanes]
    st = _rwkv_local(ops)
    states = [s_scr[p] for p in range(n_pairs)]
    y_rows = []
    for i, c0 in enumerate(starts):
        g_ends = [q["gam"][c0 + chunk - 1:c0 + chunk, ln] for ln in lanes]
        ys, states = _rwkv_state(st[i * n_pairs:(i + 1) * n_pairs], states, g_ends)
        y_rows.append(jnp.concatenate(ys, axis=1))
    for p in range(n_pairs):
        s_scr[p] = states[p]
        wkv_ref[0, 2 * p] = states[p][0:A_HEAD_DIM, 0:A_HEAD_DIM]
        wkv_ref[0, 2 * p + 1] = states[p][A_HEAD_DIM:LANES, A_HEAD_DIM:LANES]
    y = jnp.concatenate(y_rows, axis=0) if len(y_rows) > 1 else y_rows[0]
    y_ref[...] = _rwkv_finish(y, q["bonus"], q["g"], vec_ref[...])


def _rwkv_param_specs(l):
    return [_layer_spec(l, (SUBLANES, A_WIDTH)), _layer_spec(l, (LANES, A_WIDTH)),
            _layer_spec(l, (LANES, A_WIDTH)), _layer_spec(l, (LANES, A_WIDTH))]


def _rwkv_param_args(wts):
    return (wts["a_vec"], wts["wdec"], wts["wa"], wts["wg"])


def _rwkv_prompt(xs_a, nb, seq, wts, l):
    tb = RWKV_BLOCK
    nt = seq // tb
    blk = lambda b, t: (b * nt + t, 0)
    return pl.pallas_call(
        _rwkv_prompt_kernel,
        grid=(nb, nt),
        in_specs=[pl.BlockSpec((tb, A_COLS), blk)] + _rwkv_param_specs(l),
        out_specs=[pl.BlockSpec((tb, A_WIDTH), blk),
                   pl.BlockSpec((1, A_HEADS, A_HEAD_DIM, A_HEAD_DIM), lambda b, t: (b, 0, 0, 0))],
        out_shape=[jax.ShapeDtypeStruct((nb * seq, A_WIDTH), F32),
                   jax.ShapeDtypeStruct((nb, A_HEADS, A_HEAD_DIM, A_HEAD_DIM), F32)],
        scratch_shapes=[pltpu.VMEM((A_HEADS // 2, LANES, LANES), F32)],
        compiler_params=_params(2),
        name="rwkv_prompt",
    )(xs_a, *_rwkv_param_args(wts))


def _rwkv_sample_kernel(*refs, steps, n_prev):
    cols_ref, shift_ref, wkv0_ref, mu_ref, vec_ref, wdec_ref, wa_ref, wg_ref = refs[0:8]
    prev_refs = refs[8:8 + 2 * n_prev]
    y_ref, shift_out_ref, wkv_out_ref = refs[8 + 2 * n_prev:]
    rows = cols_ref.shape[0]
    nseq = rows // steps
    cols = cols_ref[...]
    tloc = _row_iota(cols.shape) & (steps - 1)
    shift0 = jnp.broadcast_to(shift_ref[...], (nseq, steps, A_COLS)).reshape(rows, A_COLS)
    prev = jnp.where(tloc == 0, shift0, _shift_rows(cols, 1))
    xs = cols + (prev - cols) * mu_ref[...]
    q = _rwkv_prep(xs, vec_ref[...], wdec_ref[...], wa_ref[...], wg_ref[...], steps)
    n_pairs = A_HEADS // 2
    lanes = [slice(p * LANES, (p + 1) * LANES) for p in range(n_pairs)]
    chains = [(s, p) for s in range(nseq) for p in range(n_pairs)]
    ops = [{k: q[k][s * steps:(s + 1) * steps, lanes[p]] for k in ("kap", "rt", "bet", "kt", "v")}
           for s, p in chains]
    states = [_pair_state(wkv0_ref, s, p) for s, p in chains]
    g_ends = [q["gam"][(s + 1) * steps - 1:(s + 1) * steps, lanes[p]] for s, p in chains]
    ys, new_states = _rwkv_state(_rwkv_local(ops), states, g_ends)
    _stack_previous(prev_refs[0::2], shift_out_ref)
    _stack_previous(prev_refs[1::2], wkv_out_ref)
    for (s, p), s_new in zip(chains, new_states):
        wkv_out_ref[n_prev, s, 2 * p] = s_new[0:A_HEAD_DIM, 0:A_HEAD_DIM]
        wkv_out_ref[n_prev, s, 2 * p + 1] = s_new[A_HEAD_DIM:LANES, A_HEAD_DIM:LANES]
    shift_out_ref[n_prev] = _last_steps(cols, nseq, steps, 1)
    y = jnp.concatenate([jnp.concatenate(ys[s * n_pairs:(s + 1) * n_pairs], axis=1) for s in range(nseq)],
                        axis=0)
    y_ref[...] = _rwkv_finish(y, q["bonus"], q["g"], vec_ref[...])


def _rwkv_sample(cols_a, shift0, wkv0, prevs, nb, steps, wts, l):
    bb = SAMPLE_SEQS
    rows = bb * steps
    n_prev = len(prevs)
    blk = lambda i: (i, 0)
    seq3 = lambda i: (0, i, 0, 0)
    seq4 = lambda i: (0, i, 0, 0, 0)
    prev_specs, prev_args = [], []
    for sh, wk in prevs:
        prev_specs += [pl.BlockSpec((None, bb, 1, A_COLS), seq3),
                       pl.BlockSpec((None, bb, A_HEADS, A_HEAD_DIM, A_HEAD_DIM), seq4)]
        prev_args += [sh, wk]
    return pl.pallas_call(
        functools.partial(_rwkv_sample_kernel, steps=steps, n_prev=n_prev),
        grid=(nb // bb,),
        in_specs=[pl.BlockSpec((rows, A_COLS), blk),
                  pl.BlockSpec((None, bb, 1, A_COLS), lambda i: (l, i, 0, 0)),
                  pl.BlockSpec((None, bb, A_HEADS, A_HEAD_DIM, A_HEAD_DIM), lambda i: (l, i, 0, 0, 0)),
                  _layer_spec(l, (1, A_COLS))]
                 + _rwkv_param_specs(l) + prev_specs,
        out_specs=[pl.BlockSpec((rows, A_WIDTH), blk),
                   pl.BlockSpec((n_prev + 1, bb, 1, A_COLS), seq3),
                   pl.BlockSpec((n_prev + 1, bb, A_HEADS, A_HEAD_DIM, A_HEAD_DIM), seq4)],
        out_shape=[jax.ShapeDtypeStruct((nb * steps, A_WIDTH), F32),
                   jax.ShapeDtypeStruct((n_prev + 1, nb, 1, A_COLS), F32),
                   jax.ShapeDtypeStruct((n_prev + 1, nb, A_HEADS, A_HEAD_DIM, A_HEAD_DIM), F32)],
        compiler_params=_params(1),
        name="rwkv_sample",
    )(cols_a, shift0, wkv0, wts["mu"], *_rwkv_param_args(wts), *prev_args)


def _lru_gates(xc, vec, bias, wgates):
    lam = vec[5:6, :]
    gates = _sigmoid(_dot(_bf(xc), wgates) + bias)
    gate_r = gates[:, 0:B_WIDTH]
    gate_i = gates[:, B_WIDTH:2 * B_WIDTH]
    log_a = (-LRU_C) * gate_r * _softplus(-lam)
    a = jnp.exp(log_a)
    mult = jnp.sqrt(-jnp.tanh(log_a) * (a * a + 1.0))
    return a, mult * gate_i * xc


def _lru_body(gate_br, xb, conv_prev, h0, tloc, tlen, vec, bias, wgates):
    rows = xb.shape[0]
    conv_b = vec[4:5, :]
    g_out = vec[6:7, :]
    xc = conv_b + vec[3:4, :] * xb
    for j in (1, 2, 3):
        tail = conv_prev if j == 3 else _shift_rows(conv_prev, rows - (3 - j))
        xc = xc + vec[3 - j:4 - j, :] * jnp.where(tloc >= j, _shift_rows(xb, j), tail)
    a, b = _lru_gates(xc, vec, bias, wgates)
    b = b + jnp.where(tloc == 0, a * h0, 0.0)
    d = 1
    while d < tlen:
        keep = tloc >= d
        a_s = jnp.where(keep, _shift_rows(a, d), 1.0)
        b_s = jnp.where(keep, _shift_rows(b, d), 0.0)
        b = a * b_s + b
        a = a * a_s
        d *= 2
    hs = b
    y = hs * _gelu_tanh(gate_br)
    return _rms(y, g_out), hs


def _lru_param_specs(l):
    return [_layer_spec(l, (SUBLANES, B_WIDTH)), _layer_spec(l, (1, 2 * B_WIDTH)),
            _layer_spec(l, (B_WIDTH, 2 * B_WIDTH))]


def _lru_param_args(wts):
    return (wts["b_vec"], wts["b_bias"], wts["b_wgates"])


def _lru_sample_kernel(*refs, steps, n_prev):
    cols_ref, conv0_ref, h0_ref, vec_ref, bias_ref, wg_ref = refs[0:6]
    prev_refs = refs[6:6 + 2 * n_prev]
    y_ref, conv_out_ref, h_out_ref = refs[6 + 2 * n_prev:]
    rows = cols_ref.shape[0]
    nseq = rows // steps
    gate_br = cols_ref[:, 0:B_WIDTH]
    xb = cols_ref[:, B_WIDTH:2 * B_WIDTH]
    tloc = _row_iota(xb.shape) & (steps - 1)
    h0 = jnp.broadcast_to(h0_ref[...], (nseq, steps, B_WIDTH)).reshape(rows, B_WIDTH)
    y, hs = _lru_body(gate_br, xb, conv0_ref[...], h0, tloc, steps, vec_ref[...], bias_ref[...], wg_ref[...])
    y_ref[...] = y
    _stack_previous(prev_refs[0::2], conv_out_ref)
    _stack_previous(prev_refs[1::2], h_out_ref)
    conv_out_ref[n_prev] = _last_steps(xb, nseq, steps, CONV_WIDTH - 1)
    h_out_ref[n_prev] = _last_steps(hs, nseq, steps, 1)


def _lru_sample(cols_b, conv0_rows, h0, prevs, nb, steps, wts, l):
    bb = SAMPLE_SEQS
    rows = bb * steps
    n_prev = len(prevs)
    blk = lambda i: (i, 0)
    seq3 = lambda i: (0, i, 0, 0)
    prev_specs, prev_args = [], []
    for cv, hh in prevs:
        prev_specs += [pl.BlockSpec((None, bb, CONV_WIDTH - 1, B_WIDTH), seq3),
                       pl.BlockSpec((None, bb, 1, B_WIDTH), seq3)]
        prev_args += [cv, hh]
    return pl.pallas_call(
        functools.partial(_lru_sample_kernel, steps=steps, n_prev=n_prev),
        grid=(nb // bb,),
        in_specs=[pl.BlockSpec((rows, 2 * B_WIDTH), blk),
                  pl.BlockSpec((None, rows, B_WIDTH), lambda i: (l, i, 0)),
                  pl.BlockSpec((None, bb, 1, B_WIDTH), lambda i: (l, i, 0, 0))]
                 + _lru_param_specs(l) + prev_specs,
        out_specs=[pl.BlockSpec((rows, B_WIDTH), blk),
                   pl.BlockSpec((n_prev + 1, bb, CONV_WIDTH - 1, B_WIDTH), seq3),
                   pl.BlockSpec((n_prev + 1, bb, 1, B_WIDTH), seq3)],
        out_shape=[jax.ShapeDtypeStruct((nb * steps, B_WIDTH), F32),
                   jax.ShapeDtypeStruct((n_prev + 1, nb, CONV_WIDTH - 1, B_WIDTH), F32),
                   jax.ShapeDtypeStruct((n_prev + 1, nb, 1, B_WIDTH), F32)],
        compiler_params=_params(1),
        name="lru_sample",
    )(cols_b, conv0_rows, h0, *_lru_param_args(wts), *prev_args)


def _s5_drive(u, lam, bbd):
    lr = lam[0:1, :]
    li = lam[1:2, :]
    dt = jnp.exp(lam[2:3, :])
    mag = jnp.exp(lr * dt)
    ar = mag * jnp.cos(li * dt)
    ai = mag * jnp.sin(li * dt)
    den = lr * lr + li * li
    cr = ((ar - 1.0) * lr + ai * li) / den
    ci = (ai * lr - (ar - 1.0) * li) / den
    ub = _bf(u)
    pb = _dot(ub, bbd[:, 0:S5_W])
    qb = _dot(ub, bbd[:, S5_W:2 * S5_W])
    return ar, ai, pb * cr - qb * ci, pb * ci + qb * cr


def _s5_readout(hr, hi, u, vec, cbd, wglu):
    s5_d = vec[0:1, :]
    b_glu = vec[1:2, :]
    g_out = vec[2:3, :]
    y = _dot(_bf(hr), cbd[0:S5_W, :]) + _dot(_bf(hi), cbd[S5_W:2 * S5_W, :]) + s5_d * u
    z = _gelu_tanh(y)
    out = z * _sigmoid(_dot(_bf(z), wglu) + b_glu)
    return _rms(out, g_out)


def _s5_body(u, h0r, h0i, tloc, tlen, lam, vec, bbd, cbd, wglu):
    ar, ai, hr, hi = _s5_drive(u, lam, bbd)
    first = tloc == 0
    hr = hr + jnp.where(first, ar * h0r - ai * h0i, 0.0)
    hi = hi + jnp.where(first, ar * h0i + ai * h0r, 0.0)
    pr, pi = ar, ai
    d = 1
    while d < tlen:
        keep = tloc >= d
        sr = jnp.where(keep, _shift_rows(hr, d), 0.0)
        si = jnp.where(keep, _shift_rows(hi, d), 0.0)
        hr, hi = hr + pr * sr - pi * si, hi + pr * si + pi * sr
        pr, pi = pr * pr - pi * pi, 2.0 * pr * pi
        d *= 2
    return _s5_readout(hr, hi, u, vec, cbd, wglu), hr, hi


def _scan_tm_kernel(colsb_ref, u_ref, bvec_ref, bias_ref, wg_ref, lam_ref, cvec_ref, bbd_ref, cbd_ref, wglu_ref,
                    yb_ref, conv_ref, h_ref, yc_ref, hr_ref, hi_ref,
                    x_scr, a_scr, b_scr, h_scr, re_scr, im_scr, sr_scr, si_scr, *, nb):
    rows = a_scr.shape[0]
    hist = (CONV_WIDTH - 1) * nb

    @pl.when(pl.program_id(0) == 0)
    def _():
        x_scr[0:hist, :] = jnp.zeros((hist, B_WIDTH), F32)
        h_scr[...] = jnp.zeros_like(h_scr)
        sr_scr[...] = jnp.zeros_like(sr_scr)
        si_scr[...] = jnp.zeros_like(si_scr)

    u = jnp.concatenate([u_ref[j] for j in range(C_WIDTH // LANES)], axis=1)
    ar, ai, dr, di = _s5_drive(u, lam_ref[...], bbd_ref[...])
    re_scr[...] = dr
    im_scr[...] = di
    ar = jnp.broadcast_to(ar, (nb, S5_W))
    ai = jnp.broadcast_to(ai, (nb, S5_W))

    bvec = bvec_ref[...]
    gate_br = jnp.concatenate([colsb_ref[0], colsb_ref[1]], axis=1)
    xb = jnp.concatenate([colsb_ref[2], colsb_ref[3]], axis=1)
    x_scr[hist:hist + rows, :] = xb
    xc = bvec[4:5, :] + bvec[3:4, :] * xb
    for j in range(1, CONV_WIDTH):
        xc = xc + bvec[3 - j:4 - j, :] * x_scr[hist - j * nb:hist - j * nb + rows, :]
    a, b = _lru_gates(xc, bvec, bias_ref[...], wg_ref[...])
    a_scr[...] = a
    b_scr[...] = b

    def step(i, carry):
        h, hr, hi = carry
        rw = pl.ds(pl.multiple_of(i * nb, nb), nb)
        nr = ar * hr - ai * hi + re_scr[rw, :]
        ni = ar * hi + ai * hr + im_scr[rw, :]
        h = a_scr[rw, :] * h + b_scr[rw, :]
        re_scr[rw, :] = nr
        im_scr[rw, :] = ni
        b_scr[rw, :] = h
        return h, nr, ni

    h, hr, hi = lax.fori_loop(0, rows // nb, step, (h_scr[...], sr_scr[...], si_scr[...]), unroll=4)
    h_scr[...] = h
    sr_scr[...] = hr
    si_scr[...] = hi
    h_ref[...] = h
    hr_ref[...] = hr
    hi_ref[...] = hi
    tail = x_scr[rows:rows + hist, :]
    conv_ref[...] = tail
    x_scr[0:hist, :] = tail
    yc = _s5_readout(re_scr[...], im_scr[...], u, cvec_ref[...], cbd_ref[...], wglu_ref[...])
    for j, slab in enumerate(_lane_slabs(yc)):
        yc_ref[j] = slab
    yb = _rms(b_scr[...] * _gelu_tanh(gate_br), bvec[6:7, :])
    for j, slab in enumerate(_lane_slabs(yb)):
        yb_ref[j] = slab


def _s5_param_specs(l):
    return [_layer_spec(l, (SUBLANES, S5_W)), _layer_spec(l, (SUBLANES, C_WIDTH)),
            _layer_spec(l, (C_WIDTH, 2 * S5_W)), _layer_spec(l, (2 * S5_W, C_WIDTH)),
            _layer_spec(l, (C_WIDTH, C_WIDTH))]


def _s5_param_args(wts):
    return (wts["c_lam"], wts["c_vec"], wts["c_bbd"], wts["c_cbd"], wts["c_wglu"])


def _scan_tm(cols_b, cols_c, nb, seq, wts, l):
    rows = SCAN_STEPS * nb
    hist = (CONV_WIDTH - 1) * nb
    slab = lambda t: (0, t, 0)
    fixed = lambda t: (0, 0)
    nb_slabs, nc_slabs = B_WIDTH // LANES, C_WIDTH // LANES
    return pl.pallas_call(
        functools.partial(_scan_tm_kernel, nb=nb),
        grid=(seq // SCAN_STEPS,),
        in_specs=[pl.BlockSpec((2 * nb_slabs, rows, LANES), slab), pl.BlockSpec((nc_slabs, rows, LANES), slab)]
                 + _lru_param_specs(l) + _s5_param_specs(l),
        out_specs=[pl.BlockSpec((nb_slabs, rows, LANES), slab), pl.BlockSpec((hist, B_WIDTH), fixed),
                   pl.BlockSpec((nb, B_WIDTH), fixed),
                   pl.BlockSpec((nc_slabs, rows, LANES), slab), pl.BlockSpec((nb, S5_W), fixed),
                   pl.BlockSpec((nb, S5_W), fixed)],
        out_shape=[jax.ShapeDtypeStruct((nb_slabs, nb * seq, LANES), F32),
                   jax.ShapeDtypeStruct((hist, B_WIDTH), F32), jax.ShapeDtypeStruct((nb, B_WIDTH), F32),
                   jax.ShapeDtypeStruct((nc_slabs, nb * seq, LANES), F32),
                   jax.ShapeDtypeStruct((nb, S5_W), F32), jax.ShapeDtypeStruct((nb, S5_W), F32)],
        scratch_shapes=[pltpu.VMEM((hist + rows, B_WIDTH), F32), pltpu.VMEM((rows, B_WIDTH), F32),
                        pltpu.VMEM((rows, B_WIDTH), F32), pltpu.VMEM((nb, B_WIDTH), F32),
                        pltpu.VMEM((rows, S5_W), F32), pltpu.VMEM((rows, S5_W), F32),
                        pltpu.VMEM((nb, S5_W), F32), pltpu.VMEM((nb, S5_W), F32)],
        compiler_params=_params(1),
        name="scan_tm",
    )(cols_b, cols_c, *_lru_param_args(wts), *_s5_param_args(wts))


def _s5_sample_kernel(*refs, steps, n_prev):
    u_ref, h0r_ref, h0i_ref, lam_ref, vec_ref, bbd_ref, cbd_ref, wglu_ref = refs[0:8]
    prev_refs = refs[8:8 + 2 * n_prev]
    y_ref, hr_out_ref, hi_out_ref = refs[8 + 2 * n_prev:]
    rows = u_ref.shape[0]
    nseq = rows // steps
    tloc = _row_iota((rows, S5_W)) & (steps - 1)
    h0r = jnp.broadcast_to(h0r_ref[...], (nseq, steps, S5_W)).reshape(rows, S5_W)
    h0i = jnp.broadcast_to(h0i_ref[...], (nseq, steps, S5_W)).reshape(rows, S5_W)
    y, hr, hi = _s5_body(u_ref[...], h0r, h0i, tloc, steps, lam_ref[...], vec_ref[...], bbd_ref[...],
                         cbd_ref[...], wglu_ref[...])
    y_ref[...] = y
    _stack_previous(prev_refs[0::2], hr_out_ref)
    _stack_previous(prev_refs[1::2], hi_out_ref)
    hr_out_ref[n_prev] = _last_steps(hr, nseq, steps, 1)
    hi_out_ref[n_prev] = _last_steps(hi, nseq, steps, 1)


def _s5_sample(cols_c, h0r, h0i, prevs, nb, steps, wts, l):
    bb = SAMPLE_SEQS
    rows = bb * steps
    n_prev = len(prevs)
    blk = lambda i: (i, 0)
    seq3 = lambda i: (0, i, 0, 0)
    st = pl.BlockSpec((None, bb, 1, S5_W), lambda i: (l, i, 0, 0))
    prev_specs, prev_args = [], []
    for pr, pi in prevs:
        prev_specs += [pl.BlockSpec((None, bb, 1, S5_W), seq3)] * 2
        prev_args += [pr, pi]
    stacked = pl.BlockSpec((n_prev + 1, bb, 1, S5_W), seq3)
    return pl.pallas_call(
        functools.partial(_s5_sample_kernel, steps=steps, n_prev=n_prev),
        grid=(nb // bb,),
        in_specs=[pl.BlockSpec((rows, C_WIDTH), blk), st, st] + _s5_param_specs(l) + prev_specs,
        out_specs=[pl.BlockSpec((rows, C_WIDTH), blk), stacked, stacked],
        out_shape=[jax.ShapeDtypeStruct((nb * steps, C_WIDTH), F32),
                   jax.ShapeDtypeStruct((n_prev + 1, nb, 1, S5_W), F32),
                   jax.ShapeDtypeStruct((n_prev + 1, nb, 1, S5_W), F32)],
        compiler_params=_params(1),
        name="s5_sample",
    )(cols_c, h0r, h0i, *_s5_param_args(wts), *prev_args)


def _rows8(rows, width):
    m = jnp.stack([r.reshape(r.shape[0], width) for r in rows], axis=1)
    return jnp.pad(m, ((0, 0), (0, SUBLANES - m.shape[1]), (0, 0)))


def _stacked_weights(w):
    depth = w["g_mix"].shape[0]
    eye_b = jnp.eye(B_BLOCKS, dtype=F32)
    eye_g = jnp.eye(S5_GROUPS, dtype=F32)
    bd4 = lambda m: jnp.einsum("lnde,nm->lndme", m, eye_b).reshape(depth, B_WIDTH, B_WIDTH)
    zeros_lora = jnp.zeros((depth, LANES - 64, A_WIDTH), F32)
    b_in = lambda m: jnp.einsum("lgpc,gh->lgchp", m, eye_g).reshape(depth, C_WIDTH, S5_W)
    c_out = lambda m: jnp.einsum("lgcp,gh->lgphc", m, eye_g).reshape(depth, S5_W, C_WIDTH)
    g_final = jnp.broadcast_to(w["g_final"][None], (depth, D_MODEL))
    return dict(
        g_mix=w["g_mix"][:, None, :],
        w_in=_bf(w["w_in"]),
        mu=w["mu_a"][:, None, :],
        a_vec=_rows8([w["w0"], w["a0"], w["k_k"], w["k_a"], w["r_k"], w["lnx_w"], w["lnx_b"]], A_WIDTH),
        wdec=_bf(jnp.concatenate([w["w_dec2"], zeros_lora], axis=1)),
        wa=_bf(jnp.concatenate([zeros_lora, w["w_a2"]], axis=1)),
        wg=_bf(w["w_g2"]),
        b_vec=_rows8([w["conv_w"][:, 0], w["conv_w"][:, 1], w["conv_w"][:, 2], w["conv_w"][:, 3],
                      w["conv_b"], w["lru_lambda"], w["g_out_b"]], B_WIDTH),
        b_bias=jnp.concatenate([w["b_rg"], w["b_ig"]], axis=1)[:, None, :],
        b_wgates=_bf(jnp.concatenate([bd4(w["w_rg"]), bd4(w["w_ig"])], axis=2)),
        c_lam=_rows8([w["s5_lam_re"], w["s5_lam_im"], jnp.repeat(w["s5_log_dt"], S5_STATE, axis=1)], S5_W),
        c_vec=_rows8([w["s5_d"], w["b_glu"], w["g_out_c"]], C_WIDTH),
        c_bbd=_bf(jnp.concatenate([b_in(w["s5_b_re"]), b_in(w["s5_b_im"])], axis=2)),
        c_cbd=_bf(jnp.concatenate([c_out(w["s5_c_re"]), -c_out(w["s5_c_im"])], axis=1)),
        c_wglu=_bf(w["w_glu"]),
        w_out=_bf(w["w_out"]),
        p_vec=_rows8([w["g_ffn"], w["g_ple"], g_final], D_MODEL),
        w_up=_bf(w["w_ffn_up"]),
        w_down=_bf(w["w_ffn_down"]),
        w_ple=_bf(w["w_ple"]),
        w_gate=_bf(w["w_ple_gate"]),
    )


def _run_prompt(x, p, wts, depth):
    nb, seq, _ = x.shape
    h = x.reshape(nb * seq, D_MODEL)
    p = p.reshape(depth, nb * seq, PLE_DIM)
    outs = []
    for l in range(depth):
        xs_a, last, cols_b, cols_c = _proj_in_tm(h, wts, l, nb, seq)
        ya, wkv = _rwkv_prompt(xs_a, nb, seq, wts, l)
        yb, conv, lru, yc, s5r, s5i = _scan_tm(cols_b, cols_c, nb, seq, wts, l)
        h = _post(h, ya, yb, yc, p, wts, l, l == depth - 1, nb_tm=nb)
        outs.append((last[:, SUBLANES - 1], wkv, conv, lru, s5r, s5i))
    shift, wkv, conv, lru, s5r, s5i = (jnp.stack([o[j] for o in outs], axis=0) for j in range(6))
    conv = jnp.swapaxes(conv.reshape(depth, CONV_WIDTH - 1, nb, B_WIDTH), 1, 2)
    s5_shape = (depth, nb, S5_GROUPS, S5_STATE)
    return h.reshape(nb, seq, D_MODEL), (shift, wkv, conv, lru, s5r.reshape(s5_shape), s5i.reshape(s5_shape))


def _run_sample(x, p, states, wts, depth):
    nb, steps, _ = x.shape
    st_shift, st_wkv, st_conv, st_lru, st_s5r, st_s5i = states
    h = x.reshape(nb * steps, D_MODEL)
    p = p.reshape(depth, nb * steps, PLE_DIM)
    shift0 = st_shift[:, :, None, :]
    conv0_rows = jnp.pad(st_conv, ((0, 0), (0, 0), (0, steps - (CONV_WIDTH - 1)), (0, 0))
                         ).reshape(depth, nb * steps, B_WIDTH)
    lru0 = st_lru[:, :, None, :]
    s5r0 = st_s5r.reshape(depth, nb, 1, S5_W)
    s5i0 = st_s5i.reshape(depth, nb, 1, S5_W)
    prev_a, prev_b, prev_c = [], [], []
    for l in range(depth):
        last = l == depth - 1
        cols_a, cols_b, cols_c = _proj_in(h, wts["g_mix"], wts["w_in"], l)
        ya, shift, wkv = _rwkv_sample(cols_a, shift0, st_wkv, prev_a if last else [], nb, steps, wts, l)
        yb, conv, lru = _lru_sample(cols_b, conv0_rows, lru0, prev_b if last else [], nb, steps, wts, l)
        yc, s5r, s5i = _s5_sample(cols_c, s5r0, s5i0, prev_c if last else [], nb, steps, wts, l)
        h = _post(h, ya, yb, yc, p, wts, l, last)
        prev_a.append((shift, wkv))
        prev_b.append((conv, lru))
        prev_c.append((s5r, s5i))
    s5_shape = (depth, nb, S5_GROUPS, S5_STATE)
    new_states = (shift.reshape(depth, nb, A_COLS), wkv, conv, lru.reshape(depth, nb, B_WIDTH),
                  s5r.reshape(s5_shape), s5i.reshape(s5_shape))
    return h.reshape(nb, steps, D_MODEL), new_states


def kernel(x_prompt, x_sample, p_prompt, p_sample, state_shift, state_wkv, state_conv, state_lru, state_s5_re, state_s5_im, g_mix, w_in, mu_a, w0, w_dec2, a0, w_a2, w_g2, k_k, k_a, r_k, lnx_w, lnx_b, conv_w, conv_b, w_rg, b_rg, w_ig, b_ig, lru_lambda, g_out_b, s5_lam_re, s5_lam_im, s5_log_dt, s5_b_re, s5_b_im, s5_c_re, s5_c_im, s5_d, w_glu, b_glu, g_out_c, w_out, g_ffn, w_ffn_up, w_ffn_down, g_ple, w_ple, w_ple_gate, g_final):
    w = dict(g_mix=g_mix, w_in=w_in, mu_a=mu_a, w0=w0, w_dec2=w_dec2, a0=a0, w_a2=w_a2, w_g2=w_g2, k_k=k_k,
             k_a=k_a, r_k=r_k, lnx_w=lnx_w, lnx_b=lnx_b, conv_w=conv_w, conv_b=conv_b, w_rg=w_rg, b_rg=b_rg,
             w_ig=w_ig, b_ig=b_ig, lru_lambda=lru_lambda, g_out_b=g_out_b, s5_lam_re=s5_lam_re,
             s5_lam_im=s5_lam_im, s5_log_dt=s5_log_dt, s5_b_re=s5_b_re, s5_b_im=s5_b_im, s5_c_re=s5_c_re,
             s5_c_im=s5_c_im, s5_d=s5_d, w_glu=w_glu, b_glu=b_glu, g_out_c=g_out_c, w_out=w_out, g_ffn=g_ffn,
             w_ffn_up=w_ffn_up, w_ffn_down=w_ffn_down, g_ple=g_ple, w_ple=w_ple, w_ple_gate=w_ple_gate,
             g_final=g_final)
    depth = g_mix.shape[0]
    wts = _stacked_weights(w)
    y_prompt, new_p = _run_prompt(x_prompt, p_prompt, wts, depth)
    y_sample, new_s = _run_sample(x_sample, p_sample,
                                  (state_shift, state_wkv, state_conv, state_lru, state_s5_re, state_s5_im),
                                  wts, depth)
    return (y_prompt, y_sample) + new_p + new_s
```

```python
import functools
import math

import jax
import jax.numpy as jnp
from jax import lax
from jax.experimental import pallas as pl
from jax.experimental.pallas import tpu as pltpu

F32 = jnp.float32
BF16 = jnp.bfloat16

D_MODEL = 1024
A_WIDTH = 512
A_HEADS = 8
A_HEAD_DIM = 64
A_COLS = 1792
B_WIDTH = 256
B_BLOCKS = 4
CONV_WIDTH = 4
C_WIDTH = 256
S5_GROUPS = 16
S5_GROUP_CH = 16
S5_STATE = 64
S5_W = S5_GROUPS * S5_STATE
IN_COLS = A_COLS + 2 * B_WIDTH + C_WIDTH
D_FF = 2816
PLE_DIM = 256
LRU_C = 8.0
RMS_EPS = 1e-6
GN_EPS = 64e-5

LANES = 128
SUBLANES = 8
VMEM_LIMIT_BYTES = 56 * 1024 * 1024

TOKEN_TILE = 512
FFN_CHUNK = 256
RWKV_CHUNK = 64
RWKV_BLOCK = 256
SCAN_STEPS = 128
SAMPLE_SEQS = 16


def _layer_spec(l, shape):
    nd = len(shape)
    return pl.BlockSpec((None,) + tuple(shape), lambda *_: (l,) + (0,) * nd, pipeline_mode=pl.Buffered(1))


def _params(n_axes):
    return pltpu.CompilerParams(dimension_semantics=("arbitrary",) * n_axes,
                                vmem_limit_bytes=VMEM_LIMIT_BYTES)


def _dot(a, b):
    return jnp.dot(a, b, preferred_element_type=F32)


def _dot_nt(a, b):
    return lax.dot_general(a, b, (((1,), (1,)), ((), ())), preferred_element_type=F32)


def _dot_tn(a, b):
    return lax.dot_general(a, b, (((0,), (0,)), ((), ())), preferred_element_type=F32)


def _bf(x):
    return x.astype(BF16)


def _rms(x, g):
    inv = lax.rsqrt(jnp.mean(x * x, axis=-1, keepdims=True) + RMS_EPS)
    return x * inv * g


def _sigmoid(x):
    return 1.0 / (1.0 + jnp.exp(-x))


def _softplus(x):
    return jnp.maximum(x, 0.0) + jnp.log(1.0 + jnp.exp(-jnp.abs(x)))


def _gelu_tanh(x):
    c = math.sqrt(2.0 / math.pi)
    return 0.5 * x * (1.0 + jnp.tanh(c * (x + 0.044715 * (x * x * x))))


def _row_iota(shape):
    return lax.broadcasted_iota(jnp.int32, shape, 0)


def _shift_rows(x, d):
    return pltpu.roll(x, d, axis=0)


def _lane_slabs(x):
    return [x[:, j * LANES:(j + 1) * LANES] for j in range(x.shape[1] // LANES)]


def _last_steps(x, nseq, steps, n):
    return x.reshape(nseq, steps, x.shape[1])[:, steps - n:steps, :]


def _stack_previous(prev_refs, out_ref):
    for j, ref in enumerate(prev_refs):
        out_ref[j] = ref[...]


def _proj_in_kernel(h_ref, g_ref, w_ref, a_ref, b_ref, c_ref):
    xn = _bf(_rms(h_ref[...], g_ref[...]))
    a_ref[...] = _dot(xn, w_ref[:, 0:A_COLS])
    b_ref[...] = _dot(xn, w_ref[:, A_COLS:A_COLS + 2 * B_WIDTH])
    c_ref[...] = _dot(xn, w_ref[:, A_COLS + 2 * B_WIDTH:IN_COLS])


def _proj_in(h, g, w_bf, l):
    n = h.shape[0]
    tm = TOKEN_TILE
    row = lambda i: (i, 0)
    return pl.pallas_call(
        _proj_in_kernel,
        grid=(n // tm,),
        in_specs=[pl.BlockSpec((tm, D_MODEL), row), _layer_spec(l, (1, D_MODEL)),
                  _layer_spec(l, (D_MODEL, IN_COLS))],
        out_specs=[pl.BlockSpec((tm, A_COLS), row), pl.BlockSpec((tm, 2 * B_WIDTH), row),
                   pl.BlockSpec((tm, C_WIDTH), row)],
        out_shape=[jax.ShapeDtypeStruct((n, A_COLS), F32), jax.ShapeDtypeStruct((n, 2 * B_WIDTH), F32),
                   jax.ShapeDtypeStruct((n, C_WIDTH), F32)],
        compiler_params=_params(1),
        name="proj_in",
    )(h, g, w_bf)


def _proj_in_tm_kernel(h_ref, g_ref, w_ref, mu_ref, a_ref, last_ref, b_ref, c_ref, prev_scr, *, nb):
    b = pl.program_id(1)
    tm = h_ref.shape[0]

    @pl.when((pl.program_id(0) == 0) & (b == 0))
    def _():
        prev_scr[...] = jnp.zeros_like(prev_scr)

    xn = _bf(_rms(h_ref[...], g_ref[...]))
    cols = _dot(xn, w_ref[:, 0:A_COLS])
    prev = jnp.where(_row_iota(cols.shape) == 0, prev_scr[b, SUBLANES - 1:SUBLANES, :], _shift_rows(cols, 1))
    a_ref[...] = cols + (prev - cols) * mu_ref[...]
    tail = cols[tm - SUBLANES:tm, :]
    prev_scr[b] = tail
    last_ref[b] = tail
    rows_of_b = pl.ds(b, tm, stride=nb)
    for j, slab in enumerate(_lane_slabs(_dot(xn, w_ref[:, A_COLS:A_COLS + 2 * B_WIDTH]))):
        b_ref[j, rows_of_b, :] = slab
    for j, slab in enumerate(_lane_slabs(_dot(xn, w_ref[:, A_COLS + 2 * B_WIDTH:IN_COLS]))):
        c_ref[j, rows_of_b, :] = slab


def _proj_in_tm(h, wts, l, nb, seq):
    tm = TOKEN_TILE
    nt = seq // tm
    row = lambda i, b: (b * nt + i, 0)
    slab = lambda i, b: (0, i, 0)
    nb_slabs, nc_slabs = 2 * B_WIDTH // LANES, C_WIDTH // LANES
    return pl.pallas_call(
        functools.partial(_proj_in_tm_kernel, nb=nb),
        grid=(nt, nb),
        in_specs=[pl.BlockSpec((tm, D_MODEL), row), _layer_spec(l, (1, D_MODEL)),
                  _layer_spec(l, (D_MODEL, IN_COLS)), _layer_spec(l, (1, A_COLS))],
        out_specs=[pl.BlockSpec((tm, A_COLS), row), pl.BlockSpec((nb, SUBLANES, A_COLS), lambda i, b: (0, 0, 0)),
                   pl.BlockSpec((nb_slabs, tm * nb, LANES), slab),
                   pl.BlockSpec((nc_slabs, tm * nb, LANES), slab)],
        out_shape=[jax.ShapeDtypeStruct((nb * seq, A_COLS), F32),
                   jax.ShapeDtypeStruct((nb, SUBLANES, A_COLS), F32),
                   jax.ShapeDtypeStruct((nb_slabs, nb * seq, LANES), F32),
                   jax.ShapeDtypeStruct((nc_slabs, nb * seq, LANES), F32)],
        scratch_shapes=[pltpu.VMEM((nb, SUBLANES, A_COLS), F32)],
        compiler_params=_params(2),
        name="proj_in_tm",
    )(h, wts["g_mix"], wts["w_in"], wts["mu"])


def _post_kernel(h_ref, ya_ref, yb_ref, yc_ref, p_ref, wo_ref, vec_ref, wup_ref, wdn_ref, wple_ref,
                 wgate_ref, o_ref, act_scr, *, final, nb_tm):
    g_ffn = vec_ref[0:1, :]
    g_ple = vec_ref[1:2, :]
    g_final = vec_ref[2:3, :]
    if nb_tm:
        rows_of_b = pl.ds(pl.program_id(1), h_ref.shape[0], stride=nb_tm)
        yb = jnp.concatenate([yb_ref[j, rows_of_b, :] for j in range(B_WIDTH // LANES)], axis=1)
        yc = jnp.concatenate([yc_ref[j, rows_of_b, :] for j in range(C_WIDTH // LANES)], axis=1)
    else:
        yb, yc = yb_ref[...], yc_ref[...]
    h1 = (h_ref[...] + _dot(_bf(ya_ref[...]), wo_ref[0:A_WIDTH, :])
          + _dot(_bf(yb), wo_ref[A_WIDTH:A_WIDTH + B_WIDTH, :])
          + _dot(_bf(yc), wo_ref[A_WIDTH + B_WIDTH:D_MODEL, :]))
    xf = _bf(_rms(h1, g_ffn))
    for c0 in range(0, D_FF, FFN_CHUNK):
        gate = _dot(xf, wup_ref[:, c0:c0 + FFN_CHUNK])
        up = _dot(xf, wup_ref[:, D_FF + c0:D_FF + c0 + FFN_CHUNK])
        act_scr[:, c0:c0 + FFN_CHUNK] = _bf(gate * _sigmoid(gate) * up)
    h2 = h1 + _dot(act_scr[...], wdn_ref[...])
    ple = _dot(_bf(p_ref[...]), wple_ref[...])
    gate = _sigmoid(_dot(_bf(_rms(h2, g_ple)), wgate_ref[...]))
    h3 = h2 + ple * gate
    if final:
        h3 = _rms(h3, g_final)
    o_ref[...] = h3


def _post(h, ya, yb, yc, p, wts, l, final, nb_tm=0):
    n = h.shape[0]
    tm = TOKEN_TILE
    if nb_tm:
        nt = n // (nb_tm * tm)
        grid = (nt, nb_tm)
        row = lambda i, b: (b * nt + i, 0)
        prow = lambda i, b: (l, b * nt + i, 0)
        slab = lambda i, b: (0, i, 0)
        yb_spec = pl.BlockSpec((B_WIDTH // LANES, tm * nb_tm, LANES), slab, pipeline_mode=pl.Buffered(1))
        yc_spec = pl.BlockSpec((C_WIDTH // LANES, tm * nb_tm, LANES), slab, pipeline_mode=pl.Buffered(1))
    else:
        grid = (n // tm,)
        row = lambda i: (i, 0)
        prow = lambda i: (l, i, 0)
        yb_spec = pl.BlockSpec((tm, B_WIDTH), row)
        yc_spec = pl.BlockSpec((tm, C_WIDTH), row)
    return pl.pallas_call(
        functools.partial(_post_kernel, final=final, nb_tm=nb_tm),
        grid=grid,
        in_specs=[pl.BlockSpec((tm, D_MODEL), row), pl.BlockSpec((tm, A_WIDTH), row),
                  yb_spec, yc_spec,
                  pl.BlockSpec((None, tm, PLE_DIM), prow),
                  _layer_spec(l, (D_MODEL, D_MODEL)), _layer_spec(l, (SUBLANES, D_MODEL)),
                  _layer_spec(l, (D_MODEL, 2 * D_FF)), _layer_spec(l, (D_FF, D_MODEL)),
                  _layer_spec(l, (PLE_DIM, D_MODEL)), _layer_spec(l, (D_MODEL, D_MODEL))],
        out_specs=pl.BlockSpec((tm, D_MODEL), row),
        out_shape=jax.ShapeDtypeStruct((n, D_MODEL), F32),
        scratch_shapes=[pltpu.VMEM((tm, D_FF), BF16)],
        compiler_params=_params(len(grid)),
        name="post",
    )(h, ya, yb, yc, p, wts["w_out"], wts["p_vec"], wts["w_up"], wts["w_down"], wts["w_ple"], wts["w_gate"])


def _head_sum(x):
    first = lax.broadcasted_iota(jnp.int32, (x.shape[0], LANES), 1) < A_HEAD_DIM
    outs = []
    for xp in _lane_slabs(x):
        s0 = jnp.sum(jnp.where(first, xp, 0.0), axis=-1, keepdims=True)
        s1 = jnp.sum(jnp.where(first, 0.0, xp), axis=-1, keepdims=True)
        outs.append(jnp.where(first, s0, s1))
    return jnp.concatenate(outs, axis=1)


def _rwkv_prep(xs, vec, wdec, wa, wg, chunk):
    rows = xs.shape[0]
    r = xs[:, 0:A_WIDTH]
    k = xs[:, A_WIDTH:2 * A_WIDTH]
    v = xs[:, 2 * A_WIDTH:3 * A_WIDTH]
    xwa = xs[:, 3 * A_WIDTH:3 * A_WIDTH + LANES]
    xg = xs[:, 3 * A_WIDTH + LANES:A_COLS]
    w0, a0, k_k, k_a, r_k = (vec[i:i + 1, :] for i in range(5))
    z = w0 + _dot(_bf(jnp.tanh(xwa)), wdec)
    log_decay = -math.exp(-0.5) * _sigmoid(z)
    a = _sigmoid(a0 + _dot(_bf(xwa), wa))
    g = _dot(_bf(_sigmoid(xg)), wg)
    kk_raw = k * k_k
    kk = kk_raw * lax.rsqrt(jnp.maximum(_head_sum(kk_raw * kk_raw), 1e-24))
    k_mod = k * (1.0 + (a - 1.0) * k_a)
    bonus = _head_sum(r * k_mod * r_k) * v
    ri = _row_iota((rows, rows))
    ci = lax.broadcasted_iota(jnp.int32, (rows, rows), 1)
    same_chunk = (ci & (-chunk)) == (ri & (-chunk))
    tri = _bf(jnp.where((ci <= ri) & same_chunk, 1.0, 0.0))
    h1 = _bf(log_decay)
    h2 = _bf(log_decay - h1.astype(F32))
    cum = _dot(tri, h1) + _dot(tri, h2)
    gam = jnp.exp(cum)
    ginv = jnp.exp(-cum)
    gprev = jnp.exp(cum - log_decay)
    return dict(rt=r * gam, kap=kk * gprev, bet=kk * a * ginv, kt=k_mod * ginv, v=v, gam=gam,
                bonus=bonus, g=g)


def _stack_heads(x):
    first = lax.broadcasted_iota(jnp.int32, x.shape, 1) < A_HEAD_DIM
    return _bf(jnp.concatenate([jnp.where(first, x, 0.0), jnp.where(first, 0.0, x)], axis=0))


def _rwkv_local(ops):
    c = ops[0]["kap"].shape[0]
    c2 = 2 * c
    ri = _row_iota((c, c2))
    ci = lax.broadcasted_iota(jnp.int32, (c, c2), 1) & (c - 1)
    strict = ci < ri
    incl = ci <= ri
    eye = jnp.where(ri == ci, 1.0, 0.0)
    same_head = (_row_iota((c2, c2)) & c) == (lax.broadcasted_iota(jnp.int32, (c2, c2), 1) & c)
    merged = c2 % LANES == 0

    def blockdiag(m):
        return _bf(jnp.where(same_head, jnp.concatenate([m, m], axis=0), 0.0))

    st = []
    for o in ops:
        s = {k + "2": _stack_heads(o[k]) for k in ("kap", "bet", "kt", "v")}
        s["rt"] = o["rt"]
        lhs = _bf(jnp.concatenate([o["kap"], o["rt"]], axis=0))
        if merged:
            g = _dot_nt(lhs, jnp.concatenate([s["bet2"], s["kt2"]], axis=0))
            a_b, a_k, a_rb, a_rk = g[0:c, 0:c2], g[0:c, c2:2 * c2], g[c:c2, 0:c2], g[c:c2, c2:2 * c2]
        else:
            gb, gk = _dot_nt(lhs, s["bet2"]), _dot_nt(lhs, s["kt2"])
            a_b, a_k, a_rb, a_rk = gb[0:c], gk[0:c], gb[c:c2], gk[c:c2]
        s["a_b"] = jnp.where(strict, a_b, 0.0)
        s["a_k"] = _bf(jnp.where(strict, a_k, 0.0))
        s["a_rb"] = _bf(jnp.where(incl, a_rb, 0.0))
        s["a_rk"] = _bf(jnp.where(incl, a_rk, 0.0))
        st.append(s)
    for s in st:
        s["t"] = eye - s["a_b"]
        s["lp"] = s["a_b"]
        s["akv"] = _dot(s["a_k"], s["v2"])
    n = 2
    while n < c:
        for s in st:
            s["lp"] = _dot(_bf(s["lp"]), blockdiag(s["lp"]))
        for s in st:
            s["t"] = s["t"] + _dot(_bf(s["t"]), blockdiag(s["lp"]))
        n *= 2
    for s in st:
        wu = _dot(_bf(s["t"]), jnp.concatenate([s["kap2"], _stack_heads(s["akv"])], axis=1))
        s["w"] = -wu[:, 0:LANES]
        s["u0"] = -wu[:, LANES:2 * LANES]
    return st


def _rwkv_state(st, states, g_ends):
    c = st[0]["rt"].shape[0]
    merged = (2 * c) % LANES == 0
    xs = []
    for s, state in zip(st, states):
        xs.append(_dot_nt(_bf(jnp.concatenate([s["w"], s["rt"]], axis=0)), _bf(state)))
    u2s = [_stack_heads(x[0:c] + s["u0"]) for x, s in zip(xs, st)]
    ys, new_states = [], []
    for s, x, u2 in zip(st, xs, u2s):
        if merged:
            y = x[c:2 * c] + _dot(jnp.concatenate([s["a_rb"], s["a_rk"]], axis=1),
                                  jnp.concatenate([u2, s["v2"]], axis=0))
        else:
            y = x[c:2 * c] + _dot(s["a_rb"], u2) + _dot(s["a_rk"], s["v2"])
        ys.append(y)
    for s, u2, state, g_end in zip(st, u2s, states, g_ends):
        ds = _dot_tn(jnp.concatenate([u2, s["v2"]], axis=0), jnp.concatenate([s["bet2"], s["kt2"]], axis=0))
        new_states.append((state + ds) * g_end)
    return ys, new_states


def _rwkv_finish(y, bonus, g, vec):
    lnx_w = vec[5:6, :]
    lnx_b = vec[6:7, :]
    inv_n = 1.0 / A_HEAD_DIM
    mu = _head_sum(y) * inv_n
    d = y - mu
    var = _head_sum(d * d) * inv_n
    yn = d * lax.rsqrt(var + GN_EPS) * lnx_w + lnx_b
    return (yn + bonus) * g


def _pair_state(s_ref, idx, p):
    z = jnp.zeros((A_HEAD_DIM, A_HEAD_DIM), F32)
    top = jnp.concatenate([s_ref[idx, 2 * p], z], axis=1)
    bot = jnp.concatenate([z, s_ref[idx, 2 * p + 1]], axis=1)
    return jnp.concatenate([top, bot], axis=0)


def _rwkv_prompt_kernel(xs_ref, vec_ref, wdec_ref, wa_ref, wg_ref, y_ref, wkv_ref, s_scr):
    t = pl.program_id(1)
    rows = xs_ref.shape[0]
    chunk = RWKV_CHUNK

    @pl.when(t == 0)
    def _():
        s_scr[...] = jnp.zeros_like(s_scr)

    q = _rwkv_prep(xs_ref[...], vec_ref[...], wdec_ref[...], wa_ref[...], wg_ref[...], chunk)
    n_pairs = A_HEADS // 2
    lanes = [slice(p * LANES, (p + 1) * LANES) for p in range(n_pairs)]
    starts = list(range(0, rows, chunk))
    ops = [{k: q[k][c0:c0 + chunk, ln] for k in ("kap", "rt", "bet", "kt", "v")}
           for c0 in starts for ln in lanes]
    st = _rwkv_local(ops)
    states = [s_scr[p] for p in range(n_pairs)]
    y_rows = []
    for i, c0 in enumerate(starts):
        g_ends = [q["gam"][c0 + chunk - 1:c0 + chunk, ln] for ln in lanes]
        ys, states = _rwkv_state(st[i * n_pairs:(i + 1) * n_pairs], states, g_ends)
        y_rows.append(jnp.concatenate(ys, axis=1))
    for p in range(n_pairs):
        s_scr[p] = states[p]
        wkv_ref[0, 2 * p] = states[p][0:A_HEAD_DIM, 0:A_HEAD_DIM]
        wkv_ref[0, 2 * p + 1] = states[p][A_HEAD_DIM:LANES, A_HEAD_DIM:LANES]
    y = jnp.concatenate(y_rows, axis=0) if len(y_rows) > 1 else y_rows[0]
    y_ref[...] = _rwkv_finish(y, q["bonus"], q["g"], vec_ref[...])


def _rwkv_param_specs(l):
    return [_layer_spec(l, (SUBLANES, A_WIDTH)), _layer_spec(l, (LANES, A_WIDTH)),
            _layer_spec(l, (LANES, A_WIDTH)), _layer_spec(l, (LANES, A_WIDTH))]


def _rwkv_param_args(wts):
    return (wts["a_vec"], wts["wdec"], wts["wa"], wts["wg"])


def _rwkv_prompt(xs_a, nb, seq, wts, l):
    tb = RWKV_BLOCK
    nt = seq // tb
    blk = lambda b, t: (b * nt + t, 0)
    return pl.pallas_call(
        _rwkv_prompt_kernel,
        grid=(nb, nt),
        in_specs=[pl.BlockSpec((tb, A_COLS), blk)] + _rwkv_param_specs(l),
        out_specs=[pl.BlockSpec((tb, A_WIDTH), blk),
                   pl.BlockSpec((1, A_HEADS, A_HEAD_DIM, A_HEAD_DIM), lambda b, t: (b, 0, 0, 0))],
        out_shape=[jax.ShapeDtypeStruct((nb * seq, A_WIDTH), F32),
                   jax.ShapeDtypeStruct((nb, A_HEADS, A_HEAD_DIM, A_HEAD_DIM), F32)],
        scratch_shapes=[pltpu.VMEM((A_HEADS // 2, LANES, LANES), F32)],
        compiler_params=_params(2),
        name="rwkv_prompt",
    )(xs_a, *_rwkv_param_args(wts))


def _rwkv_sample_kernel(*refs, steps, n_prev):
    cols_ref, shift_ref, wkv0_ref, mu_ref, vec_ref, wdec_ref, wa_ref, wg_ref = refs[0:8]
    prev_refs = refs[8:8 + 2 * n_prev]
    y_ref, shift_out_ref, wkv_out_ref = refs[8 + 2 * n_prev:]
    rows = cols_ref.shape[0]
    nseq = rows // steps
    cols = cols_ref[...]
    tloc = _row_iota(cols.shape) & (steps - 1)
    shift0 = jnp.broadcast_to(shift_ref[...], (nseq, steps, A_COLS)).reshape(rows, A_COLS)
    prev = jnp.where(tloc == 0, shift0, _shift_rows(cols, 1))
    xs = cols + (prev - cols) * mu_ref[...]
    q = _rwkv_prep(xs, vec_ref[...], wdec_ref[...], wa_ref[...], wg_ref[...], steps)
    n_pairs = A_HEADS // 2
    lanes = [slice(p * LANES, (p + 1) * LANES) for p in range(n_pairs)]
    chains = [(s, p) for s in range(nseq) for p in range(n_pairs)]
    ops = [{k: q[k][s * steps:(s + 1) * steps, lanes[p]] for k in ("kap", "rt", "bet", "kt", "v")}
           for s, p in chains]
    states = [_pair_state(wkv0_ref, s, p) for s, p in chains]
    g_ends = [q["gam"][(s + 1) * steps - 1:(s + 1) * steps, lanes[p]] for s, p in chains]
    ys, new_states = _rwkv_state(_rwkv_local(ops), states, g_ends)
    _stack_previous(prev_refs[0::2], shift_out_ref)
    _stack_previous(prev_refs[1::2], wkv_out_ref)
    for (s, p), s_new in zip(chains, new_states):
        wkv_out_ref[n_prev, s, 2 * p] = s_new[0:A_HEAD_DIM, 0:A_HEAD_DIM]
        wkv_out_ref[n_prev, s, 2 * p + 1] = s_new[A_HEAD_DIM:LANES, A_HEAD_DIM:LANES]
    shift_out_ref[n_prev] = _last_steps(cols, nseq, steps, 1)
    y = jnp.concatenate([jnp.concatenate(ys[s * n_pairs:(s + 1) * n_pairs], axis=1) for s in range(nseq)],
                        axis=0)
    y_ref[...] = _rwkv_finish(y, q["bonus"], q["g"], vec_ref[...])


def _rwkv_sample(cols_a, shift0, wkv0, prevs, nb, steps, wts, l):
    bb = SAMPLE_SEQS
    rows = bb * steps
    n_prev = len(prevs)
    blk = lambda i: (i, 0)
    seq3 = lambda i: (0, i, 0, 0)
    seq4 = lambda i: (0, i, 0, 0, 0)
    prev_specs, prev_args = [], []
    for sh, wk in prevs:
        prev_specs += [pl.BlockSpec((None, bb, 1, A_COLS), seq3),
                       pl.BlockSpec((None, bb, A_HEADS, A_HEAD_DIM, A_HEAD_DIM), seq4)]
        prev_args += [sh, wk]
    return pl.pallas_call(
        functools.partial(_rwkv_sample_kernel, steps=steps, n_prev=n_prev),
        grid=(nb // bb,),
        in_specs=[pl.BlockSpec((rows, A_COLS), blk),
                  pl.BlockSpec((None, bb, 1, A_COLS), lambda i: (l, i, 0, 0)),
                  pl.BlockSpec((None, bb, A_HEADS, A_HEAD_DIM, A_HEAD_DIM), lambda i: (l, i, 0, 0, 0)),
                  _layer_spec(l, (1, A_COLS))]
                 + _rwkv_param_specs(l) + prev_specs,
        out_specs=[pl.BlockSpec((rows, A_WIDTH), blk),
                   pl.BlockSpec((n_prev + 1, bb, 1, A_COLS), seq3),
                   pl.BlockSpec((n_prev + 1, bb, A_HEADS, A_HEAD_DIM, A_HEAD_DIM), seq4)],
        out_shape=[jax.ShapeDtypeStruct((nb * steps, A_WIDTH), F32),
                   jax.ShapeDtypeStruct((n_prev + 1, nb, 1, A_COLS), F32),
                   jax.ShapeDtypeStruct((n_prev + 1, nb, A_HEADS, A_HEAD_DIM, A_HEAD_DIM), F32)],
        compiler_params=_params(1),
        name="rwkv_sample",
    )(cols_a, shift0, wkv0, wts["mu"], *_rwkv_param_args(wts), *prev_args)


def _lru_gates(xc, vec, bias, wgates):
    lam = vec[5:6, :]
    gates = _sigmoid(_dot(_bf(xc), wgates) + bias)
    gate_r = gates[:, 0:B_WIDTH]
    gate_i = gates[:, B_WIDTH:2 * B_WIDTH]
    log_a = (-LRU_C) * gate_r * _softplus(-lam)
    a = jnp.exp(log_a)
    mult = jnp.sqrt(-jnp.tanh(log_a) * (a * a + 1.0))
    return a, mult * gate_i * xc


def _lru_body(gate_br, xb, conv_prev, h0, tloc, tlen, vec, bias, wgates):
    rows = xb.shape[0]
    conv_b = vec[4:5, :]
    g_out = vec[6:7, :]
    xc = conv_b + vec[3:4, :] * xb
    for j in (1, 2, 3):
        tail = conv_prev if j == 3 else _shift_rows(conv_prev, rows - (3 - j))
        xc = xc + vec[3 - j:4 - j, :] * jnp.where(tloc >= j, _shift_rows(xb, j), tail)
    a, b = _lru_gates(xc, vec, bias, wgates)
    b = b + jnp.where(tloc == 0, a * h0, 0.0)
    d = 1
    while d < tlen:
        keep = tloc >= d
        a_s = jnp.where(keep, _shift_rows(a, d), 1.0)
        b_s = jnp.where(keep, _shift_rows(b, d), 0.0)
        b = a * b_s + b
        a = a * a_s
        d *= 2
    hs = b
    y = hs * _gelu_tanh(gate_br)
    return _rms(y, g_out), hs


def _lru_param_specs(l):
    return [_layer_spec(l, (SUBLANES, B_WIDTH)), _layer_spec(l, (1, 2 * B_WIDTH)),
            _layer_spec(l, (B_WIDTH, 2 * B_WIDTH))]


def _lru_param_args(wts):
    return (wts["b_vec"], wts["b_bias"], wts["b_wgates"])


def _lru_sample_kernel(*refs, steps, n_prev):
    cols_ref, conv0_ref, h0_ref, vec_ref, bias_ref, wg_ref = refs[0:6]
    prev_refs = refs[6:6 + 2 * n_prev]
    y_ref, conv_out_ref, h_out_ref = refs[6 + 2 * n_prev:]
    rows = cols_ref.shape[0]
    nseq = rows // steps
    gate_br = cols_ref[:, 0:B_WIDTH]
    xb = cols_ref[:, B_WIDTH:2 * B_WIDTH]
    tloc = _row_iota(xb.shape) & (steps - 1)
    h0 = jnp.broadcast_to(h0_ref[...], (nseq, steps, B_WIDTH)).reshape(rows, B_WIDTH)
    y, hs = _lru_body(gate_br, xb, conv0_ref[...], h0, tloc, steps, vec_ref[...], bias_ref[...], wg_ref[...])
    y_ref[...] = y
    _stack_previous(prev_refs[0::2], conv_out_ref)
    _stack_previous(prev_refs[1::2], h_out_ref)
    conv_out_ref[n_prev] = _last_steps(xb, nseq, steps, CONV_WIDTH - 1)
    h_out_ref[n_prev] = _last_steps(hs, nseq, steps, 1)


def _lru_sample(cols_b, conv0_rows, h0, prevs, nb, steps, wts, l):
    bb = SAMPLE_SEQS
    rows = bb * steps
    n_prev = len(prevs)
    blk = lambda i: (i, 0)
    seq3 = lambda i: (0, i, 0, 0)
    prev_specs, prev_args = [], []
    for cv, hh in prevs:
        prev_specs += [pl.BlockSpec((None, bb, CONV_WIDTH - 1, B_WIDTH), seq3),
                       pl.BlockSpec((None, bb, 1, B_WIDTH), seq3)]
        prev_args += [cv, hh]
    return pl.pallas_call(
        functools.partial(_lru_sample_kernel, steps=steps, n_prev=n_prev),
        grid=(nb // bb,),
        in_specs=[pl.BlockSpec((rows, 2 * B_WIDTH), blk),
                  pl.BlockSpec((None, rows, B_WIDTH), lambda i: (l, i, 0)),
                  pl.BlockSpec((None, bb, 1, B_WIDTH), lambda i: (l, i, 0, 0))]
                 + _lru_param_specs(l) + prev_specs,
        out_specs=[pl.BlockSpec((rows, B_WIDTH), blk),
                   pl.BlockSpec((n_prev + 1, bb, CONV_WIDTH - 1, B_WIDTH), seq3),
                   pl.BlockSpec((n_prev + 1, bb, 1, B_WIDTH), seq3)],
        out_shape=[jax.ShapeDtypeStruct((nb * steps, B_WIDTH), F32),
                   jax.ShapeDtypeStruct((n_prev + 1, nb, CONV_WIDTH - 1, B_WIDTH), F32),
                   jax.ShapeDtypeStruct((n_prev + 1, nb, 1, B_WIDTH), F32)],
        compiler_params=_params(1),
        name="lru_sample",
    )(cols_b, conv0_rows, h0, *_lru_param_args(wts), *prev_args)


def _s5_drive(u, lam, bbd):
    lr = lam[0:1, :]
    li = lam[1:2, :]
    dt = jnp.exp(lam[2:3, :])
    mag = jnp.exp(lr * dt)
    ar = mag * jnp.cos(li * dt)
    ai = mag * jnp.sin(li * dt)
    den = lr * lr + li * li
    cr = ((ar - 1.0) * lr + ai * li) / den
    ci = (ai * lr - (ar - 1.0) * li) / den
    ub = _bf(u)
    pb = _dot(ub, bbd[:, 0:S5_W])
    qb = _dot(ub, bbd[:, S5_W:2 * S5_W])
    return ar, ai, pb * cr - qb * ci, pb * ci + qb * cr


def _s5_readout(hr, hi, u, vec, cbd, wglu):
    s5_d = vec[0:1, :]
    b_glu = vec[1:2, :]
    g_out = vec[2:3, :]
    y = _dot(_bf(hr), cbd[0:S5_W, :]) + _dot(_bf(hi), cbd[S5_W:2 * S5_W, :]) + s5_d * u
    z = _gelu_tanh(y)
    out = z * _sigmoid(_dot(_bf(z), wglu) + b_glu)
    return _rms(out, g_out)


def _s5_body(u, h0r, h0i, tloc, tlen, lam, vec, bbd, cbd, wglu):
    ar, ai, hr, hi = _s5_drive(u, lam, bbd)
    first = tloc == 0
    hr = hr + jnp.where(first, ar * h0r - ai * h0i, 0.0)
    hi = hi + jnp.where(first, ar * h0i + ai * h0r, 0.0)
    pr, pi = ar, ai
    d = 1
    while d < tlen:
        keep = tloc >= d
        sr = jnp.where(keep, _shift_rows(hr, d), 0.0)
        si = jnp.where(keep, _shift_rows(hi, d), 0.0)
        hr, hi = hr + pr * sr - pi * si, hi + pr * si + pi * sr
        pr, pi = pr * pr - pi * pi, 2.0 * pr * pi
        d *= 2
    return _s5_readout(hr, hi, u, vec, cbd, wglu), hr, hi


def _scan_tm_kernel(colsb_ref, u_ref, bvec_ref, bias_ref, wg_ref, lam_ref, cvec_ref, bbd_ref, cbd_ref, wglu_ref,
                    yb_ref, conv_ref, h_ref, yc_ref, hr_ref, hi_ref,
                    x_scr, a_scr, b_scr, h_scr, re_scr, im_scr, sr_scr, si_scr, *, nb):
    rows = a_scr.shape[0]
    hist = (CONV_WIDTH - 1) * nb

    @pl.when(pl.program_id(0) == 0)
    def _():
        x_scr[0:hist, :] = jnp.zeros((hist, B_WIDTH), F32)
        h_scr[...] = jnp.zeros_like(h_scr)
        sr_scr[...] = jnp.zeros_like(sr_scr)
        si_scr[...] = jnp.zeros_like(si_scr)

    u = jnp.concatenate([u_ref[j] for j in range(C_WIDTH // LANES)], axis=1)
    ar, ai, dr, di = _s5_drive(u, lam_ref[...], bbd_ref[...])
    re_scr[...] = dr
    im_scr[...] = di
    ar = jnp.broadcast_to(ar, (nb, S5_W))
    ai = jnp.broadcast_to(ai, (nb, S5_W))

    bvec = bvec_ref[...]
    gate_br = jnp.concatenate([colsb_ref[0], colsb_ref[1]], axis=1)
    xb = jnp.concatenate([colsb_ref[2], colsb_ref[3]], axis=1)
    x_scr[hist:hist + rows, :] = xb
    xc = bvec[4:5, :] + bvec[3:4, :] * xb
    for j in range(1, CONV_WIDTH):
        xc = xc + bvec[3 - j:4 - j, :] * x_scr[hist - j * nb:hist - j * nb + rows, :]
    a, b = _lru_gates(xc, bvec, bias_ref[...], wg_ref[...])
    a_scr[...] = a
    b_scr[...] = b

    def step(i, carry):
        h, hr, hi = carry
        rw = pl.ds(pl.multiple_of(i * nb, nb), nb)
        nr = ar * hr - ai * hi + re_scr[rw, :]
        ni = ar * hi + ai * hr + im_scr[rw, :]
        h = a_scr[rw, :] * h + b_scr[rw, :]
        re_scr[rw, :] = nr
        im_scr[rw, :] = ni
        b_scr[rw, :] = h
        return h, nr, ni

    h, hr, hi = lax.fori_loop(0, rows // nb, step, (h_scr[...], sr_scr[...], si_scr[...]), unroll=4)
    h_scr[...] = h
    sr_scr[...] = hr
    si_scr[...] = hi
    h_ref[...] = h
    hr_ref[...] = hr
    hi_ref[...] = hi
    tail = x_scr[rows:rows + hist, :]
    conv_ref[...] = tail
    x_scr[0:hist, :] = tail
    yc = _s5_readout(re_scr[...], im_scr[...], u, cvec_ref[...], cbd_ref[...], wglu_ref[...])
    for j, slab in enumerate(_lane_slabs(yc)):
        yc_ref[j] = slab
    yb = _rms(b_scr[...] * _gelu_tanh(gate_br), bvec[6:7, :])
    for j, slab in enumerate(_lane_slabs(yb)):
        yb_ref[j] = slab


def _s5_param_specs(l):
    return [_layer_spec(l, (SUBLANES, S5_W)), _layer_spec(l, (SUBLANES, C_WIDTH)),
            _layer_spec(l, (C_WIDTH, 2 * S5_W)), _layer_spec(l, (2 * S5_W, C_WIDTH)),
            _layer_spec(l, (C_WIDTH, C_WIDTH))]


def _s5_param_args(wts):
    return (wts["c_lam"], wts["c_vec"], wts["c_bbd"], wts["c_cbd"], wts["c_wglu"])


def _scan_tm(cols_b, cols_c, nb, seq, wts, l):
    rows = SCAN_STEPS * nb
    hist = (CONV_WIDTH - 1) * nb
    slab = lambda t: (0, t, 0)
    fixed = lambda t: (0, 0)
    nb_slabs, nc_slabs = B_WIDTH // LANES, C_WIDTH // LANES
    return pl.pallas_call(
        functools.partial(_scan_tm_kernel, nb=nb),
        grid=(seq // SCAN_STEPS,),
        in_specs=[pl.BlockSpec((2 * nb_slabs, rows, LANES), slab), pl.BlockSpec((nc_slabs, rows, LANES), slab)]
                 + _lru_param_specs(l) + _s5_param_specs(l),
        out_specs=[pl.BlockSpec((nb_slabs, rows, LANES), slab), pl.BlockSpec((hist, B_WIDTH), fixed),
                   pl.BlockSpec((nb, B_WIDTH), fixed),
                   pl.BlockSpec((nc_slabs, rows, LANES), slab), pl.BlockSpec((nb, S5_W), fixed),
                   pl.BlockSpec((nb, S5_W), fixed)],
        out_shape=[jax.ShapeDtypeStruct((nb_slabs, nb * seq, LANES), F32),
                   jax.ShapeDtypeStruct((hist, B_WIDTH), F32), jax.ShapeDtypeStruct((nb, B_WIDTH), F32),
                   jax.ShapeDtypeStruct((nc_slabs, nb * seq, LANES), F32),
                   jax.ShapeDtypeStruct((nb, S5_W), F32), jax.ShapeDtypeStruct((nb, S5_W), F32)],
        scratch_shapes=[pltpu.VMEM((hist + rows, B_WIDTH), F32), pltpu.VMEM((rows, B_WIDTH), F32),
                        pltpu.VMEM((rows, B_WIDTH), F32), pltpu.VMEM((nb, B_WIDTH), F32),
                        pltpu.VMEM((rows, S5_W), F32), pltpu.VMEM((rows, S5_W), F32),
                        pltpu.VMEM((nb, S5_W), F32), pltpu.VMEM((nb, S5_W), F32)],
        compiler_params=_params(1),
        name="scan_tm",
    )(cols_b, cols_c, *_lru_param_args(wts), *_s5_param_args(wts))


def _s5_sample_kernel(*refs, steps, n_prev):
    u_ref, h0r_ref, h0i_ref, lam_ref, vec_ref, bbd_ref, cbd_ref, wglu_ref = refs[0:8]
    prev_refs = refs[8:8 + 2 * n_prev]
    y_ref, hr_out_ref, hi_out_ref = refs[8 + 2 * n_prev:]
    rows = u_ref.shape[0]
    nseq = rows // steps
    tloc = _row_iota((rows, S5_W)) & (steps - 1)
    h0r = jnp.broadcast_to(h0r_ref[...], (nseq, steps, S5_W)).reshape(rows, S5_W)
    h0i = jnp.broadcast_to(h0i_ref[...], (nseq, steps, S5_W)).reshape(rows, S5_W)
    y, hr, hi = _s5_body(u_ref[...], h0r, h0i, tloc, steps, lam_ref[...], vec_ref[...], bbd_ref[...],
                         cbd_ref[...], wglu_ref[...])
    y_ref[...] = y
    _stack_previous(prev_refs[0::2], hr_out_ref)
    _stack_previous(prev_refs[1::2], hi_out_ref)
    hr_out_ref[n_prev] = _last_steps(hr, nseq, steps, 1)
    hi_out_ref[n_prev] = _last_steps(hi, nseq, steps, 1)


def _s5_sample(cols_c, h0r, h0i, prevs, nb, steps, wts, l):
    bb = SAMPLE_SEQS
    rows = bb * steps
    n_prev = len(prevs)
    blk = lambda i: (i, 0)
    seq3 = lambda i: (0, i, 0, 0)
    st = pl.BlockSpec((None, bb, 1, S5_W), lambda i: (l, i, 0, 0))
    prev_specs, prev_args = [], []
    for pr, pi in prevs:
        prev_specs += [pl.BlockSpec((None, bb, 1, S5_W), seq3)] * 2
        prev_args += [pr, pi]
    stacked = pl.BlockSpec((n_prev + 1, bb, 1, S5_W), seq3)
    return pl.pallas_call(
        functools.partial(_s5_sample_kernel, steps=steps, n_prev=n_prev),
        grid=(nb // bb,),
        in_specs=[pl.BlockSpec((rows, C_WIDTH), blk), st, st] + _s5_param_specs(l) + prev_specs,
        out_specs=[pl.BlockSpec((rows, C_WIDTH), blk), stacked, stacked],
        out_shape=[jax.ShapeDtypeStruct((nb * steps, C_WIDTH), F32),
                   jax.ShapeDtypeStruct((n_prev + 1, nb, 1, S5_W), F32),
                   jax.ShapeDtypeStruct((n_prev + 1, nb, 1, S5_W), F32)],
        compiler_params=_params(1),
        name="s5_sample",
    )(cols_c, h0r, h0i, *_s5_param_args(wts), *prev_args)


def _rows8(rows, width):
    m = jnp.stack([r.reshape(r.shape[0], width) for r in rows], axis=1)
    return jnp.pad(m, ((0, 0), (0, SUBLANES - m.shape[1]), (0, 0)))


def _stacked_weights(w):
    depth = w["g_mix"].shape[0]
    eye_b = jnp.eye(B_BLOCKS, dtype=F32)
    eye_g = jnp.eye(S5_GROUPS, dtype=F32)
    bd4 = lambda m: jnp.einsum("lnde,nm->lndme", m, eye_b).reshape(depth, B_WIDTH, B_WIDTH)
    zeros_lora = jnp.zeros((depth, LANES - 64, A_WIDTH), F32)
    b_in = lambda m: jnp.einsum("lgpc,gh->lgchp", m, eye_g).reshape(depth, C_WIDTH, S5_W)
    c_out = lambda m: jnp.einsum("lgcp,gh->lgphc", m, eye_g).reshape(depth, S5_W, C_WIDTH)
    g_final = jnp.broadcast_to(w["g_final"][None], (depth, D_MODEL))
    return dict(
        g_mix=w["g_mix"][:, None, :],
        w_in=_bf(w["w_in"]),
        mu=w["mu_a"][:, None, :],
        a_vec=_rows8([w["w0"], w["a0"], w["k_k"], w["k_a"], w["r_k"], w["lnx_w"], w["lnx_b"]], A_WIDTH),
        wdec=_bf(jnp.concatenate([w["w_dec2"], zeros_lora], axis=1)),
        wa=_bf(jnp.concatenate([zeros_lora, w["w_a2"]], axis=1)),
        wg=_bf(w["w_g2"]),
        b_vec=_rows8([w["conv_w"][:, 0], w["conv_w"][:, 1], w["conv_w"][:, 2], w["conv_w"][:, 3],
                      w["conv_b"], w["lru_lambda"], w["g_out_b"]], B_WIDTH),
        b_bias=jnp.concatenate([w["b_rg"], w["b_ig"]], axis=1)[:, None, :],
        b_wgates=_bf(jnp.concatenate([bd4(w["w_rg"]), bd4(w["w_ig"])], axis=2)),
        c_lam=_rows8([w["s5_lam_re"], w["s5_lam_im"], jnp.repeat(w["s5_log_dt"], S5_STATE, axis=1)], S5_W),
        c_vec=_rows8([w["s5_d"], w["b_glu"], w["g_out_c"]], C_WIDTH),
        c_bbd=_bf(jnp.concatenate([b_in(w["s5_b_re"]), b_in(w["s5_b_im"])], axis=2)),
        c_cbd=_bf(jnp.concatenate([c_out(w["s5_c_re"]), -c_out(w["s5_c_im"])], axis=1)),
        c_wglu=_bf(w["w_glu"]),
        w_out=_bf(w["w_out"]),
        p_vec=_rows8([w["g_ffn"], w["g_ple"], g_final], D_MODEL),
        w_up=_bf(w["w_ffn_up"]),
        w_down=_bf(w["w_ffn_down"]),
        w_ple=_bf(w["w_ple"]),
        w_gate=_bf(w["w_ple_gate"]),
    )


def _run_prompt(x, p, wts, depth):
    nb, seq, _ = x.shape
    h = x.reshape(nb * seq, D_MODEL)
    p = p.reshape(depth, nb * seq, PLE_DIM)
    outs = []
    for l in range(depth):
        xs_a, last, cols_b, cols_c = _proj_in_tm(h, wts, l, nb, seq)
        ya, wkv = _rwkv_prompt(xs_a, nb, seq, wts, l)
        yb, conv, lru, yc, s5r, s5i = _scan_tm(cols_b, cols_c, nb, seq, wts, l)
        h = _post(h, ya, yb, yc, p, wts, l, l == depth - 1, nb_tm=nb)
        outs.append((last[:, SUBLANES - 1], wkv, conv, lru, s5r, s5i))
    shift, wkv, conv, lru, s5r, s5i = (jnp.stack([o[j] for o in outs], axis=0) for j in range(6))
    conv = jnp.swapaxes(conv.reshape(depth, CONV_WIDTH - 1, nb, B_WIDTH), 1, 2)
    s5_shape = (depth, nb, S5_GROUPS, S5_STATE)
    return h.reshape(nb, seq, D_MODEL), (shift, wkv, conv, lru, s5r.reshape(s5_shape), s5i.reshape(s5_shape))


def _run_sample(x, p, states, wts, depth):
    nb, steps, _ = x.shape
    st_shift, st_wkv, st_conv, st_lru, st_s5r, st_s5i = states
    h = x.reshape(nb * steps, D_MODEL)
    p = p.reshape(depth, nb * steps, PLE_DIM)
    shift0 = st_shift[:, :, None, :]
    conv0_rows = jnp.pad(st_conv, ((0, 0), (0, 0), (0, steps - (CONV_WIDTH - 1)), (0, 0))
                         ).reshape(depth, nb * steps, B_WIDTH)
    lru0 = st_lru[:, :, None, :]
    s5r0 = st_s5r.reshape(depth, nb, 1, S5_W)
    s5i0 = st_s5i.reshape(depth, nb, 1, S5_W)
    prev_a, prev_b, prev_c = [], [], []
    for l in range(depth):
        last = l == depth - 1
        cols_a, cols_b, cols_c = _proj_in(h, wts["g_mix"], wts["w_in"], l)
        ya, shift, wkv = _rwkv_sample(cols_a, shift0, st_wkv, prev_a if last else [], nb, steps, wts, l)
        yb, conv, lru = _lru_sample(cols_b, conv0_rows, lru0, prev_b if last else [], nb, steps, wts, l)
        yc, s5r, s5i = _s5_sample(cols_c, s5r0, s5i0, prev_c if last else [], nb, steps, wts, l)
        h = _post(h, ya, yb, yc, p, wts, l, last)
        prev_a.append((shift, wkv))
        prev_b.append((conv, lru))
        prev_c.append((s5r, s5i))
    s5_shape = (depth, nb, S5_GROUPS, S5_STATE)
    new_states = (shift.reshape(depth, nb, A_COLS), wkv, conv, lru.reshape(depth, nb, B_WIDTH),
                  s5r.reshape(s5_shape), s5i.reshape(s5_shape))
    return h.reshape(nb, steps, D_MODEL), new_states


def kernel(x_prompt, x_sample, p_prompt, p_sample, state_shift, state_wkv, state_conv, state_lru, state_s5_re, state_s5_im, g_mix, w_in, mu_a, w0, w_dec2, a0, w_a2, w_g2, k_k, k_a, r_k, lnx_w, lnx_b, conv_w, conv_b, w_rg, b_rg, w_ig, b_ig, lru_lambda, g_out_b, s5_lam_re, s5_lam_im, s5_log_dt, s5_b_re, s5_b_im, s5_c_re, s5_c_im, s5_d, w_glu, b_glu, g_out_c, w_out, g_ffn, w_ffn_up, w_ffn_down, g_ple, w_ple, w_ple_gate, g_final):
    w = dict(g_mix=g_mix, w_in=w_in, mu_a=mu_a, w0=w0, w_dec2=w_dec2, a0=a0, w_a2=w_a2, w_g2=w_g2, k_k=k_k,
             k_a=k_a, r_k=r_k, lnx_w=lnx_w, lnx_b=lnx_b, conv_w=conv_w, conv_b=conv_b, w_rg=w_rg, b_rg=b_rg,
             w_ig=w_ig, b_ig=b_ig, lru_lambda=lru_lambda, g_out_b=g_out_b, s5_lam_re=s5_lam_re,
             s5_lam_im=s5_lam_im, s5_log_dt=s5_log_dt, s5_b_re=s5_b_re, s5_b_im=s5_b_im, s5_c_re=s5_c_re,
             s5_c_im=s5_c_im, s5_d=s5_d, w_glu=w_glu, b_glu=b_glu, g_out_c=g_out_c, w_out=w_out, g_ffn=g_ffn,
             w_ffn_up=w_ffn_up, w_ffn_down=w_ffn_down, g_ple=g_ple, w_ple=w_ple, w_ple_gate=w_ple_gate,
             g_final=g_final)
    depth = g_mix.shape[0]
    wts = _stacked_weights(w)
    y_prompt, new_p = _run_prompt(x_prompt, p_prompt, wts, depth)
    y_sample, new_s = _run_sample(x_sample, p_sample,
                                  (state_shift, state_wkv, state_conv, state_lru, state_s5_re, state_s5_im),
                                  wts, depth)
    return (y_prompt, y_sample) + new_p + new_s
```

```python
import functools
import math

import jax
import jax.numpy as jnp
from jax import lax
from jax.experimental import pallas as pl
from jax.experimental.pallas import tpu as pltpu

F32 = jnp.float32
BF16 = jnp.bfloat16

D_MODEL = 1024
A_WIDTH = 512
A_HEADS = 8
A_HEAD_DIM = 64
A_COLS = 1792
B_WIDTH = 256
B_BLOCKS = 4
CONV_WIDTH = 4
C_WIDTH = 256
S5_GROUPS = 16
S5_GROUP_CH = 16
S5_STATE = 64
S5_W = S5_GROUPS * S5_STATE
IN_COLS = A_COLS + 2 * B_WIDTH + C_WIDTH
D_FF = 2816
PLE_DIM = 256
LRU_C = 8.0
RMS_EPS = 1e-6
GN_EPS = 64e-5

LANES = 128
SUBLANES = 8
VMEM_LIMIT_BYTES = 56 * 1024 * 1024

TOKEN_TILE = 512
FFN_CHUNK = 256
RWKV_CHUNK = 64
RWKV_BLOCK = 256
SCAN_STEPS = 128
SAMPLE_SEQS = 16


def _layer_spec(l, shape):
    nd = len(shape)
    return pl.BlockSpec((None,) + tuple(shape), lambda *_: (l,) + (0,) * nd, pipeline_mode=pl.Buffered(1))


def _params(n_axes):
    return pltpu.CompilerParams(dimension_semantics=("arbitrary",) * n_axes,
                                vmem_limit_bytes=VMEM_LIMIT_BYTES)


def _dot(a, b):
    return jnp.dot(a, b, preferred_element_type=F32)


def _dot_nt(a, b):
    return lax.dot_general(a, b, (((1,), (1,)), ((), ())), preferred_element_type=F32)


def _dot_tn(a, b):
    return lax.dot_general(a, b, (((0,), (0,)), ((), ())), preferred_element_type=F32)


def _bf(x):
    return x.astype(BF16)


def _rms(x, g):
    inv = lax.rsqrt(jnp.mean(x * x, axis=-1, keepdims=True) + RMS_EPS)
    return x * inv * g


def _sigmoid(x):
    return 1.0 / (1.0 + jnp.exp(-x))


def _softplus(x):
    return jnp.maximum(x, 0.0) + jnp.log(1.0 + jnp.exp(-jnp.abs(x)))


def _gelu_tanh(x):
    c = math.sqrt(2.0 / math.pi)
    return 0.5 * x * (1.0 + jnp.tanh(c * (x + 0.044715 * (x * x * x))))


def _row_iota(shape):
    return lax.broadcasted_iota(jnp.int32, shape, 0)


def _shift_rows(x, d):
    return pltpu.roll(x, d, axis=0)


def _lane_slabs(x):
    return [x[:, j * LANES:(j + 1) * LANES] for j in range(x.shape[1] // LANES)]


def _last_steps(x, nseq, steps, n):
    return x.reshape(nseq, steps, x.shape[1])[:, steps - n:steps, :]


def _stack_previous(prev_refs, out_ref):
    for j, ref in enumerate(prev_refs):
        out_ref[j] = ref[...]


def _proj_in_kernel(h_ref, g_ref, w_ref, a_ref, b_ref, c_ref):
    xn = _bf(_rms(h_ref[...], g_ref[...]))
    a_ref[...] = _dot(xn, w_ref[:, 0:A_COLS])
    b_ref[...] = _dot(xn, w_ref[:, A_COLS:A_COLS + 2 * B_WIDTH])
    c_ref[...] = _dot(xn, w_ref[:, A_COLS + 2 * B_WIDTH:IN_COLS])


def _proj_in(h, g, w_bf, l):
    n = h.shape[0]
    tm = TOKEN_TILE
    row = lambda i: (i, 0)
    return pl.pallas_call(
        _proj_in_kernel,
        grid=(n // tm,),
        in_specs=[pl.BlockSpec((tm, D_MODEL), row), _layer_spec(l, (1, D_MODEL)),
                  _layer_spec(l, (D_MODEL, IN_COLS))],
        out_specs=[pl.BlockSpec((tm, A_COLS), row), pl.BlockSpec((tm, 2 * B_WIDTH), row),
                   pl.BlockSpec((tm, C_WIDTH), row)],
        out_shape=[jax.ShapeDtypeStruct((n, A_COLS), F32), jax.ShapeDtypeStruct((n, 2 * B_WIDTH), F32),
                   jax.ShapeDtypeStruct((n, C_WIDTH), F32)],
        compiler_params=_params(1),
        name="proj_in",
    )(h, g, w_bf)


def _proj_in_tm_kernel(h_ref, g_ref, w_ref, mu_ref, a_ref, last_ref, b_ref, c_ref, prev_scr, *, nb):
    b = pl.program_id(1)
    tm = h_ref.shape[0]

    @pl.when((pl.program_id(0) == 0) & (b == 0))
    def _():
        prev_scr[...] = jnp.zeros_like(prev_scr)

    xn = _bf(_rms(h_ref[...], g_ref[...]))
    cols = _dot(xn, w_ref[:, 0:A_COLS])
    prev = jnp.where(_row_iota(cols.shape) == 0, prev_scr[b, SUBLANES - 1:SUBLANES, :], _shift_rows(cols, 1))
    a_ref[...] = cols + (prev - cols) * mu_ref[...]
    tail = cols[tm - SUBLANES:tm, :]
    prev_scr[b] = tail
    last_ref[b] = tail
    rows_of_b = pl.ds(b, tm, stride=nb)
    for j, slab in enumerate(_lane_slabs(_dot(xn, w_ref[:, A_COLS:A_COLS + 2 * B_WIDTH]))):
        b_ref[j, rows_of_b, :] = slab
    for j, slab in enumerate(_lane_slabs(_dot(xn, w_ref[:, A_COLS + 2 * B_WIDTH:IN_COLS]))):
        c_ref[j, rows_of_b, :] = slab


def _proj_in_tm(h, wts, l, nb, seq):
    tm = TOKEN_TILE
    nt = seq // tm
    row = lambda i, b: (b * nt + i, 0)
    slab = lambda i, b: (0, i, 0)
    nb_slabs, nc_slabs = 2 * B_WIDTH // LANES, C_WIDTH // LANES
    return pl.pallas_call(
        functools.partial(_proj_in_tm_kernel, nb=nb),
        grid=(nt, nb),
        in_specs=[pl.BlockSpec((tm, D_MODEL), row), _layer_spec(l, (1, D_MODEL)),
                  _layer_spec(l, (D_MODEL, IN_COLS)), _layer_spec(l, (1, A_COLS))],
        out_specs=[pl.BlockSpec((tm, A_COLS), row), pl.BlockSpec((nb, SUBLANES, A_COLS), lambda i, b: (0, 0, 0)),
                   pl.BlockSpec((nb_slabs, tm * nb, LANES), slab),
                   pl.BlockSpec((nc_slabs, tm * nb, LANES), slab)],
        out_shape=[jax.ShapeDtypeStruct((nb * seq, A_COLS), F32),
                   jax.ShapeDtypeStruct((nb, SUBLANES, A_COLS), F32),
                   jax.ShapeDtypeStruct((nb_slabs, nb * seq, LANES), F32),
                   jax.ShapeDtypeStruct((nc_slabs, nb * seq, LANES), F32)],
        scratch_shapes=[pltpu.VMEM((nb, SUBLANES, A_COLS), F32)],
        compiler_params=_params(2),
        name="proj_in_tm",
    )(h, wts["g_mix"], wts["w_in"], wts["mu"])


def _post_kernel(h_ref, ya_ref, yb_ref, yc_ref, p_ref, wo_ref, vec_ref, wup_ref, wdn_ref, wple_ref,
                 wgate_ref, o_ref, act_scr, *, final, nb_tm):
    g_ffn = vec_ref[0:1, :]
    g_ple = vec_ref[1:2, :]
    g_final = vec_ref[2:3, :]
    if nb_tm:
        rows_of_b = pl.ds(pl.program_id(1), h_ref.shape[0], stride=nb_tm)
        yb = jnp.concatenate([yb_ref[j, rows_of_b, :] for j in range(B_WIDTH // LANES)], axis=1)
        yc = jnp.concatenate([yc_ref[j, rows_of_b, :] for j in range(C_WIDTH // LANES)], axis=1)
    else:
        yb, yc = yb_ref[...], yc_ref[...]
    h1 = (h_ref[...] + _dot(_bf(ya_ref[...]), wo_ref[0:A_WIDTH, :])
          + _dot(_bf(yb), wo_ref[A_WIDTH:A_WIDTH + B_WIDTH, :])
          + _dot(_bf(yc), wo_ref[A_WIDTH + B_WIDTH:D_MODEL, :]))
    xf = _bf(_rms(h1, g_ffn))
    for c0 in range(0, D_FF, FFN_CHUNK):
        gate = _dot(xf, wup_ref[:, c0:c0 + FFN_CHUNK])
        up = _dot(xf, wup_ref[:, D_FF + c0:D_FF + c0 + FFN_CHUNK])
        act_scr[:, c0:c0 + FFN_CHUNK] = _bf(gate * _sigmoid(gate) * up)
    h2 = h1 + _dot(act_scr[...], wdn_ref[...])
    ple = _dot(_bf(p_ref[...]), wple_ref[...])
    gate = _sigmoid(_dot(_bf(_rms(h2, g_ple)), wgate_ref[...]))
    h3 = h2 + ple * gate
    if final:
        h3 = _rms(h3, g_final)
    o_ref[...] = h3


def _post(h, ya, yb, yc, p, wts, l, final, nb_tm=0):
    n = h.shape[0]
    tm = TOKEN_TILE
    if nb_tm:
        nt = n // (nb_tm * tm)
        grid = (nt, nb_tm)
        row = lambda i, b: (b * nt + i, 0)
        prow = lambda i, b: (l, b * nt + i, 0)
        slab = lambda i, b: (0, i, 0)
        yb_spec = pl.BlockSpec((B_WIDTH // LANES, tm * nb_tm, LANES), slab, pipeline_mode=pl.Buffered(1))
        yc_spec = pl.BlockSpec((C_WIDTH // LANES, tm * nb_tm, LANES), slab, pipeline_mode=pl.Buffered(1))
    else:
        grid = (n // tm,)
        row = lambda i: (i, 0)
        prow = lambda i: (l, i, 0)
        yb_spec = pl.BlockSpec((tm, B_WIDTH), row)
        yc_spec = pl.BlockSpec((tm, C_WIDTH), row)
    return pl.pallas_call(
        functools.partial(_post_kernel, final=final, nb_tm=nb_tm),
        grid=grid,
        in_specs=[pl.BlockSpec((tm, D_MODEL), row), pl.BlockSpec((tm, A_WIDTH), row),
                  yb_spec, yc_spec,
                  pl.BlockSpec((None, tm, PLE_DIM), prow),
                  _layer_spec(l, (D_MODEL, D_MODEL)), _layer_spec(l, (SUBLANES, D_MODEL)),
                  _layer_spec(l, (D_MODEL, 2 * D_FF)), _layer_spec(l, (D_FF, D_MODEL)),
                  _layer_spec(l, (PLE_DIM, D_MODEL)), _layer_spec(l, (D_MODEL, D_MODEL))],
        out_specs=pl.BlockSpec((tm, D_MODEL), row),
        out_shape=jax.ShapeDtypeStruct((n, D_MODEL), F32),
        scratch_shapes=[pltpu.VMEM((tm, D_FF), BF16)],
        compiler_params=_params(len(grid)),
        name="post",
    )(h, ya, yb, yc, p, wts["w_out"], wts["p_vec"], wts["w_up"], wts["w_down"], wts["w_ple"], wts["w_gate"])


def _head_sum(x):
    first = lax.broadcasted_iota(jnp.int32, (x.shape[0], LANES), 1) < A_HEAD_DIM
    outs = []
    for xp in _lane_slabs(x):
        s0 = jnp.sum(jnp.where(first, xp, 0.0), axis=-1, keepdims=True)
        s1 = jnp.sum(jnp.where(first, 0.0, xp), axis=-1, keepdims=True)
        outs.append(jnp.where(first, s0, s1))
    return jnp.concatenate(outs, axis=1)


def _rwkv_prep(xs, vec, wdec, wa, wg, chunk):
    rows = xs.shape[0]
    r = xs[:, 0:A_WIDTH]
    k = xs[:, A_WIDTH:2 * A_WIDTH]
    v = xs[:, 2 * A_WIDTH:3 * A_WIDTH]
    xwa = xs[:, 3 * A_WIDTH:3 * A_WIDTH + LANES]
    xg = xs[:, 3 * A_WIDTH + LANES:A_COLS]
    w0, a0, k_k, k_a, r_k = (vec[i:i + 1, :] for i in range(5))
    z = w0 + _dot(_bf(jnp.tanh(xwa)), wdec)
    log_decay = -math.exp(-0.5) * _sigmoid(z)
    a = _sigmoid(a0 + _dot(_bf(xwa), wa))
    g = _dot(_bf(_sigmoid(xg)), wg)
    kk_raw = k * k_k
    kk = kk_raw * lax.rsqrt(jnp.maximum(_head_sum(kk_raw * kk_raw), 1e-24))
    k_mod = k * (1.0 + (a - 1.0) * k_a)
    bonus = _head_sum(r * k_mod * r_k) * v
    ri = _row_iota((rows, rows))
    ci = lax.broadcasted_iota(jnp.int32, (rows, rows), 1)
    same_chunk = (ci & (-chunk)) == (ri & (-chunk))
    tri = _bf(jnp.where((ci <= ri) & same_chunk, 1.0, 0.0))
    h1 = _bf(log_decay)
    h2 = _bf(log_decay - h1.astype(F32))
    cum = _dot(tri, h1) + _dot(tri, h2)
    gam = jnp.exp(cum)
    ginv = jnp.exp(-cum)
    gprev = jnp.exp(cum - log_decay)
    return dict(rt=r * gam, kap=kk * gprev, bet=kk * a * ginv, kt=k_mod * ginv, v=v, gam=gam,
                bonus=bonus, g=g)


def _stack_heads(x):
    first = lax.broadcasted_iota(jnp.int32, x.shape, 1) < A_HEAD_DIM
    return _bf(jnp.concatenate([jnp.where(first, x, 0.0), jnp.where(first, 0.0, x)], axis=0))


def _rwkv_local(ops):
    c = ops[0]["kap"].shape[0]
    c2 = 2 * c
    ri = _row_iota((c, c2))
    ci = lax.broadcasted_iota(jnp.int32, (c, c2), 1) & (c - 1)
    strict = ci < ri
    incl = ci <= ri
    eye = jnp.where(ri == ci, 1.0, 0.0)
    same_head = (_row_iota((c2, c2)) & c) == (lax.broadcasted_iota(jnp.int32, (c2, c2), 1) & c)
    merged = c2 % LANES == 0

    def blockdiag(m):
        return _bf(jnp.where(same_head, jnp.concatenate([m, m], axis=0), 0.0))

    st = []
    for o in ops:
        s = {k + "2": _stack_heads(o[k]) for k in ("kap", "bet", "kt", "v")}
        s["rt"] = o["rt"]
        lhs = _bf(jnp.concatenate([o["kap"], o["rt"]], axis=0))
        if merged:
            g = _dot_nt(lhs, jnp.concatenate([s["bet2"], s["kt2"]], axis=0))
            a_b, a_k, a_rb, a_rk = g[0:c, 0:c2], g[0:c, c2:2 * c2], g[c:c2, 0:c2], g[c:c2, c2:2 * c2]
        else:
            gb, gk = _dot_nt(lhs, s["bet2"]), _dot_nt(lhs, s["kt2"])
            a_b, a_k, a_rb, a_rk = gb[0:c], gk[0:c], gb[c:c2], gk[c:c2]
        s["a_b"] = jnp.where(strict, a_b, 0.0)
        s["a_k"] = _bf(jnp.where(strict, a_k, 0.0))
        s["a_rb"] = _bf(jnp.where(incl, a_rb, 0.0))
        s["a_rk"] = _bf(jnp.where(incl, a_rk, 0.0))
        st.append(s)
    for s in st:
        s["t"] = eye - s["a_b"]
        s["lp"] = s["a_b"]
        s["akv"] = _dot(s["a_k"], s["v2"])
    n = 2
    while n < c:
        for s in st:
            s["lp"] = _dot(_bf(s["lp"]), blockdiag(s["lp"]))
        for s in st:
            s["t"] = s["t"] + _dot(_bf(s["t"]), blockdiag(s["lp"]))
        n *= 2
    for s in st:
        wu = _dot(_bf(s["t"]), jnp.concatenate([s["kap2"], _stack_heads(s["akv"])], axis=1))
        s["w"] = -wu[:, 0:LANES]
        s["u0"] = -wu[:, LANES:2 * LANES]
    return st


def _rwkv_state(st, states, g_ends):
    c = st[0]["rt"].shape[0]
    merged = (2 * c) % LANES == 0
    xs = []
    for s, state in zip(st, states):
        xs.append(_dot_nt(_bf(jnp.concatenate([s["w"], s["rt"]], axis=0)), _bf(state)))
    u2s = [_stack_heads(x[0:c] + s["u0"]) for x, s in zip(xs, st)]
    ys, new_states = [], []
    for s, x, u2 in zip(st, xs, u2s):
        if merged:
            y = x[c:2 * c] + _dot(jnp.concatenate([s["a_rb"], s["a_rk"]], axis=1),
                                  jnp.concatenate([u2, s["v2"]], axis=0))
        else:
            y = x[c:2 * c] + _dot(s["a_rb"], u2) + _dot(s["a_rk"], s["v2"])
        ys.append(y)
    for s, u2, state, g_end in zip(st, u2s, states, g_ends):
        ds = _dot_tn(jnp.concatenate([u2, s["v2"]], axis=0), jnp.concatenate([s["bet2"], s["kt2"]], axis=0))
        new_states.append((state + ds) * g_end)
    return ys, new_states


def _rwkv_finish(y, bonus, g, vec):
    lnx_w = vec[5:6, :]
    lnx_b = vec[6:7, :]
    inv_n = 1.0 / A_HEAD_DIM
    mu = _head_sum(y) * inv_n
    d = y - mu
    var = _head_sum(d * d) * inv_n
    yn = d * lax.rsqrt(var + GN_EPS) * lnx_w + lnx_b
    return (yn + bonus) * g


def _pair_state(s_ref, idx, p):
    z = jnp.zeros((A_HEAD_DIM, A_HEAD_DIM), F32)
    top = jnp.concatenate([s_ref[idx, 2 * p], z], axis=1)
    bot = jnp.concatenate([z, s_ref[idx, 2 * p + 1]], axis=1)
    return jnp.concatenate([top, bot], axis=0)


def _rwkv_prompt_kernel(xs_ref, vec_ref, wdec_ref, wa_ref, wg_ref, y_ref, wkv_ref, s_scr):
    t = pl.program_id(1)
    rows = xs_ref.shape[0]
    chunk = RWKV_CHUNK

    @pl.when(t == 0)
    def _():
        s_scr[...] = jnp.zeros_like(s_scr)

    q = _rwkv_prep(xs_ref[...], vec_ref[...], wdec_ref[...], wa_ref[...], wg_ref[...], chunk)
    n_pairs = A_HEADS // 2
    lanes = [slice(p * LANES, (p + 1) * LANES) for p in range(n_pairs)]
    starts = list(range(0, rows, chunk))
    ops = [{k: q[k][c0:c0 + chunk, ln] for k in ("kap", "rt", "bet", "kt", "v")}
           for c0 in starts for ln in lanes]
    st = _rwkv_local(ops)
    states = [s_scr[p] for p in range(n_pairs)]
    y_rows = []
    for i, c0 in enumerate(starts):
        g_ends = [q["gam"][c0 + chunk - 1:c0 + chunk, ln] for ln in lanes]
        ys, states = _rwkv_state(st[i * n_pairs:(i + 1) * n_pairs], states, g_ends)
        y_rows.append(jnp.concatenate(ys, axis=1))
    for p in range(n_pairs):
        s_scr[p] = states[p]
        wkv_ref[0, 2 * p] = states[p][0:A_HEAD_DIM, 0:A_HEAD_DIM]
        wkv_ref[0, 2 * p + 1] = states[p][A_HEAD_DIM:LANES, A_HEAD_DIM:LANES]
    y = jnp.concatenate(y_rows, axis=0) if len(y_rows) > 1 else y_rows[0]
    y_ref[...] = _rwkv_finish(y, q["bonus"], q["g"], vec_ref[...])


def _rwkv_param_specs(l):
    return [_layer_spec(l, (SUBLANES, A_WIDTH)), _layer_spec(l, (LANES, A_WIDTH)),
            _layer_spec(l, (LANES, A_WIDTH)), _layer_spec(l, (LANES, A_WIDTH))]


def _rwkv_param_args(wts):
    return (wts["a_vec"], wts["wdec"], wts["wa"], wts["wg"])


def _rwkv_prompt(xs_a, nb, seq, wts, l):
    tb = RWKV_BLOCK
    nt = seq // tb
    blk = lambda b, t: (b * nt + t, 0)
    return pl.pallas_call(
        _rwkv_prompt_kernel,
        grid=(nb, nt),
        in_specs=[pl.BlockSpec((tb, A_COLS), blk)] + _rwkv_param_specs(l),
        out_specs=[pl.BlockSpec((tb, A_WIDTH), blk),
                   pl.BlockSpec((1, A_HEADS, A_HEAD_DIM, A_HEAD_DIM), lambda b, t: (b, 0, 0, 0))],
        out_shape=[jax.ShapeDtypeStruct((nb * seq, A_WIDTH), F32),
                   jax.ShapeDtypeStruct((nb, A_HEADS, A_HEAD_DIM, A_HEAD_DIM), F32)],
        scratch_shapes=[pltpu.VMEM((A_HEADS // 2, LANES, LANES), F32)],
        compiler_params=_params(2),
        name="rwkv_prompt",
    )(xs_a, *_rwkv_param_args(wts))


def _rwkv_sample_kernel(*refs, steps, n_prev):
    cols_ref, shift_ref, wkv0_ref, mu_ref, vec_ref, wdec_ref, wa_ref, wg_ref = refs[0:8]
    prev_refs = refs[8:8 + 2 * n_prev]
    y_ref, shift_out_ref, wkv_out_ref = refs[8 + 2 * n_prev:]
    rows = cols_ref.shape[0]
    nseq = rows // steps
    cols = cols_ref[...]
    tloc = _row_iota(cols.shape) & (steps - 1)
    shift0 = jnp.broadcast_to(shift_ref[...], (nseq, steps, A_COLS)).reshape(rows, A_COLS)
    prev = jnp.where(tloc == 0, shift0, _shift_rows(cols, 1))
    xs = cols + (prev - cols) * mu_ref[...]
    q = _rwkv_prep(xs, vec_ref[...], wdec_ref[...], wa_ref[...], wg_ref[...], steps)
    n_pairs = A_HEADS // 2
    lanes = [slice(p * LANES, (p + 1) * LANES) for p in range(n_pairs)]
    chains = [(s, p) for s in range(nseq) for p in range(n_pairs)]
    ops = [{k: q[k][s * steps:(s + 1) * steps, lanes[p]] for k in ("kap", "rt", "bet", "kt", "v")}
           for s, p in chains]
    states = [_pair_state(wkv0_ref, s, p) for s, p in chains]
    g_ends = [q["gam"][(s + 1) * steps - 1:(s + 1) * steps, lanes[p]] for s, p in chains]
    ys, new_states = _rwkv_state(_rwkv_local(ops), states, g_ends)
    _stack_previous(prev_refs[0::2], shift_out_ref)
    _stack_previous(prev_refs[1::2], wkv_out_ref)
    for (s, p), s_new in zip(chains, new_states):
        wkv_out_ref[n_prev, s, 2 * p] = s_new[0:A_HEAD_DIM, 0:A_HEAD_DIM]
        wkv_out_ref[n_prev, s, 2 * p + 1] = s_new[A_HEAD_DIM:LANES, A_HEAD_DIM:LANES]
    shift_out_ref[n_prev] = _last_steps(cols, nseq, steps, 1)
    y = jnp.concatenate([jnp.concatenate(ys[s * n_pairs:(s + 1) * n_pairs], axis=1) for s in range(nseq)],
                        axis=0)
    y_ref[...] = _rwkv_finish(y, q["bonus"], q["g"], vec_ref[...])


def _rwkv_sample(cols_a, shift0, wkv0, prevs, nb, steps, wts, l):
    bb = SAMPLE_SEQS
    rows = bb * steps
    n_prev = len(prevs)
    blk = lambda i: (i, 0)
    seq3 = lambda i: (0, i, 0, 0)
    seq4 = lambda i: (0, i, 0, 0, 0)
    prev_specs, prev_args = [], []
    for sh, wk in prevs:
        prev_specs += [pl.BlockSpec((None, bb, 1, A_COLS), seq3),
                       pl.BlockSpec((None, bb, A_HEADS, A_HEAD_DIM, A_HEAD_DIM), seq4)]
        prev_args += [sh, wk]
    return pl.pallas_call(
        functools.partial(_rwkv_sample_kernel, steps=steps, n_prev=n_prev),
        grid=(nb // bb,),
        in_specs=[pl.BlockSpec((rows, A_COLS), blk),
                  pl.BlockSpec((None, bb, 1, A_COLS), lambda i: (l, i, 0, 0)),
                  pl.BlockSpec((None, bb, A_HEADS, A_HEAD_DIM, A_HEAD_DIM), lambda i: (l, i, 0, 0, 0)),
                  _layer_spec(l, (1, A_COLS))]
                 + _rwkv_param_specs(l) + prev_specs,
        out_specs=[pl.BlockSpec((rows, A_WIDTH), blk),
                   pl.BlockSpec((n_prev + 1, bb, 1, A_COLS), seq3),
                   pl.BlockSpec((n_prev + 1, bb, A_HEADS, A_HEAD_DIM, A_HEAD_DIM), seq4)],
        out_shape=[jax.ShapeDtypeStruct((nb * steps, A_WIDTH), F32),
                   jax.ShapeDtypeStruct((n_prev + 1, nb, 1, A_COLS), F32),
                   jax.ShapeDtypeStruct((n_prev + 1, nb, A_HEADS, A_HEAD_DIM, A_HEAD_DIM), F32)],
        compiler_params=_params(1),
        name="rwkv_sample",
    )(cols_a, shift0, wkv0, wts["mu"], *_rwkv_param_args(wts), *prev_args)


def _lru_gates(xc, vec, bias, wgates):
    lam = vec[5:6, :]
    gates = _sigmoid(_dot(_bf(xc), wgates) + bias)
    gate_r = gates[:, 0:B_WIDTH]
    gate_i = gates[:, B_WIDTH:2 * B_WIDTH]
    log_a = (-LRU_C) * gate_r * _softplus(-lam)
    a = jnp.exp(log_a)
    mult = jnp.sqrt(-jnp.tanh(log_a) * (a * a + 1.0))
    return a, mult * gate_i * xc


def _lru_body(gate_br, xb, conv_prev, h0, tloc, tlen, vec, bias, wgates):
    rows = xb.shape[0]
    conv_b = vec[4:5, :]
    g_out = vec[6:7, :]
    xc = conv_b + vec[3:4, :] * xb
    for j in (1, 2, 3):
        tail = conv_prev if j == 3 else _shift_rows(conv_prev, rows - (3 - j))
        xc = xc + vec[3 - j:4 - j, :] * jnp.where(tloc >= j, _shift_rows(xb, j), tail)
    a, b = _lru_gates(xc, vec, bias, wgates)
    b = b + jnp.where(tloc == 0, a * h0, 0.0)
    d = 1
    while d < tlen:
        keep = tloc >= d
        a_s = jnp.where(keep, _shift_rows(a, d), 1.0)
        b_s = jnp.where(keep, _shift_rows(b, d), 0.0)
        b = a * b_s + b
        a = a * a_s
        d *= 2
    hs = b
    y = hs * _gelu_tanh(gate_br)
    return _rms(y, g_out), hs


def _lru_param_specs(l):
    return [_layer_spec(l, (SUBLANES, B_WIDTH)), _layer_spec(l, (1, 2 * B_WIDTH)),
            _layer_spec(l, (B_WIDTH, 2 * B_WIDTH))]


def _lru_param_args(wts):
    return (wts["b_vec"], wts["b_bias"], wts["b_wgates"])


def _lru_sample_kernel(*refs, steps, n_prev):
    cols_ref, conv0_ref, h0_ref, vec_ref, bias_ref, wg_ref = refs[0:6]
    prev_refs = refs[6:6 + 2 * n_prev]
    y_ref, conv_out_ref, h_out_ref = refs[6 + 2 * n_prev:]
    rows = cols_ref.shape[0]
    nseq = rows // steps
    gate_br = cols_ref[:, 0:B_WIDTH]
    xb = cols_ref[:, B_WIDTH:2 * B_WIDTH]
    tloc = _row_iota(xb.shape) & (steps - 1)
    h0 = jnp.broadcast_to(h0_ref[...], (nseq, steps, B_WIDTH)).reshape(rows, B_WIDTH)
    y, hs = _lru_body(gate_br, xb, conv0_ref[...], h0, tloc, steps, vec_ref[...], bias_ref[...], wg_ref[...])
    y_ref[...] = y
    _stack_previous(prev_refs[0::2], conv_out_ref)
    _stack_previous(prev_refs[1::2], h_out_ref)
    conv_out_ref[n_prev] = _last_steps(xb, nseq, steps, CONV_WIDTH - 1)
    h_out_ref[n_prev] = _last_steps(hs, nseq, steps, 1)


def _lru_sample(cols_b, conv0_rows, h0, prevs, nb, steps, wts, l):
    bb = SAMPLE_SEQS
    rows = bb * steps
    n_prev = len(prevs)
    blk = lambda i: (i, 0)
    seq3 = lambda i: (0, i, 0, 0)
    prev_specs, prev_args = [], []
    for cv, hh in prevs:
        prev_specs += [pl.BlockSpec((None, bb, CONV_WIDTH - 1, B_WIDTH), seq3),
                       pl.BlockSpec((None, bb, 1, B_WIDTH), seq3)]
        prev_args += [cv, hh]
    return pl.pallas_call(
        functools.partial(_lru_sample_kernel, steps=steps, n_prev=n_prev),
        grid=(nb // bb,),
        in_specs=[pl.BlockSpec((rows, 2 * B_WIDTH), blk),
                  pl.BlockSpec((None, rows, B_WIDTH), lambda i: (l, i, 0)),
                  pl.BlockSpec((None, bb, 1, B_WIDTH), lambda i: (l, i, 0, 0))]
                 + _lru_param_specs(l) + prev_specs,
        out_specs=[pl.BlockSpec((rows, B_WIDTH), blk),
                   pl.BlockSpec((n_prev + 1, bb, CONV_WIDTH - 1, B_WIDTH), seq3),
                   pl.BlockSpec((n_prev + 1, bb, 1, B_WIDTH), seq3)],
        out_shape=[jax.ShapeDtypeStruct((nb * steps, B_WIDTH), F32),
                   jax.ShapeDtypeStruct((n_prev + 1, nb, CONV_WIDTH - 1, B_WIDTH), F32),
                   jax.ShapeDtypeStruct((n_prev + 1, nb, 1, B_WIDTH), F32)],
        compiler_params=_params(1),
        name="lru_sample",
    )(cols_b, conv0_rows, h0, *_lru_param_args(wts), *prev_args)


def _s5_discretise(lam, b_ref, bbar_scr, abar_scr):
    lr = lam[0:1, :]
    li = lam[1:2, :]
    dt = jnp.exp(lam[2:3, :])
    mag = jnp.exp(lr * dt)
    ar = mag * jnp.cos(li * dt)
    ai = mag * jnp.sin(li * dt)
    den = lr * lr + li * li
    cr = ((ar - 1.0) * lr + ai * li) / den
    ci = (ai * lr - (ar - 1.0) * li) / den
    b_re = b_ref[:, 0:S5_W]
    b_im = b_ref[:, S5_W:2 * S5_W]
    bbar_scr[:, 0:S5_W] = _bf(cr * b_re - ci * b_im)
    bbar_scr[:, S5_W:2 * S5_W] = _bf(cr * b_im + ci * b_re)
    abar_scr[...] = jnp.concatenate([ar, ai, jnp.zeros((SUBLANES - 2, S5_W), F32)], axis=0)


def _s5_drive(u, bbar_scr):
    ub = _bf(u)
    return _dot(ub, bbar_scr[:, 0:S5_W]), _dot(ub, bbar_scr[:, S5_W:2 * S5_W])


def _s5_readout(hr, hi, u, vec, cbd, wglu):
    s5_d = vec[0:1, :]
    b_glu = vec[1:2, :]
    g_out = vec[2:3, :]
    y = _dot(_bf(hr), cbd[0:S5_W, :]) + _dot(_bf(hi), cbd[S5_W:2 * S5_W, :]) + s5_d * u
    z = _gelu_tanh(y)
    out = z * _sigmoid(_dot(_bf(z), wglu) + b_glu)
    return _rms(out, g_out)


def _s5_body(u, h0r, h0i, tloc, tlen, abar_scr, bbar_scr, vec, cbd, wglu):
    ar, ai = abar_scr[0:1, :], abar_scr[1:2, :]
    hr, hi = _s5_drive(u, bbar_scr)
    first = tloc == 0
    hr = hr + jnp.where(first, ar * h0r - ai * h0i, 0.0)
    hi = hi + jnp.where(first, ar * h0i + ai * h0r, 0.0)
    pr, pi = ar, ai
    d = 1
    while d < tlen:
        keep = tloc >= d
        sr = jnp.where(keep, _shift_rows(hr, d), 0.0)
        si = jnp.where(keep, _shift_rows(hi, d), 0.0)
        hr, hi = hr + pr * sr - pi * si, hi + pr * si + pi * sr
        pr, pi = pr * pr - pi * pi, 2.0 * pr * pi
        d *= 2
    return _s5_readout(hr, hi, u, vec, cbd, wglu), hr, hi


def _scan_tm_kernel(colsb_ref, u_ref, bvec_ref, bias_ref, wg_ref, lam_ref, cvec_ref, bbd_ref, cbd_ref, wglu_ref,
                    yb_ref, conv_ref, h_ref, yc_ref, hr_ref, hi_ref,
                    x_scr, a_scr, b_scr, h_scr, re_scr, im_scr, sr_scr, si_scr, abar_scr, bbar_scr, *, nb):
    rows = a_scr.shape[0]
    hist = (CONV_WIDTH - 1) * nb

    @pl.when(pl.program_id(0) == 0)
    def _():
        x_scr[0:hist, :] = jnp.zeros((hist, B_WIDTH), F32)
        h_scr[...] = jnp.zeros_like(h_scr)
        sr_scr[...] = jnp.zeros_like(sr_scr)
        si_scr[...] = jnp.zeros_like(si_scr)
        _s5_discretise(lam_ref[...], bbd_ref, bbar_scr, abar_scr)

    u = jnp.concatenate([u_ref[j] for j in range(C_WIDTH // LANES)], axis=1)
    dr, di = _s5_drive(u, bbar_scr)
    re_scr[...] = dr
    im_scr[...] = di
    ar = jnp.broadcast_to(abar_scr[0:1, :], (nb, S5_W))
    ai = jnp.broadcast_to(abar_scr[1:2, :], (nb, S5_W))

    bvec = bvec_ref[...]
    gate_br = jnp.concatenate([colsb_ref[0], colsb_ref[1]], axis=1)
    xb = jnp.concatenate([colsb_ref[2], colsb_ref[3]], axis=1)
    x_scr[hist:hist + rows, :] = xb
    xc = bvec[4:5, :] + bvec[3:4, :] * xb
    for j in range(1, CONV_WIDTH):
        xc = xc + bvec[3 - j:4 - j, :] * x_scr[hist - j * nb:hist - j * nb + rows, :]
    a, b = _lru_gates(xc, bvec, bias_ref[...], wg_ref[...])
    a_scr[...] = a
    b_scr[...] = b

    def step(i, carry):
        h, hr, hi = carry
        rw = pl.ds(pl.multiple_of(i * nb, nb), nb)
        nr = ar * hr - ai * hi + re_scr[rw, :]
        ni = ar * hi + ai * hr + im_scr[rw, :]
        h = a_scr[rw, :] * h + b_scr[rw, :]
        re_scr[rw, :] = nr
        im_scr[rw, :] = ni
        b_scr[rw, :] = h
        return h, nr, ni

    h, hr, hi = lax.fori_loop(0, rows // nb, step, (h_scr[...], sr_scr[...], si_scr[...]), unroll=4)
    h_scr[...] = h
    sr_scr[...] = hr
    si_scr[...] = hi
    h_ref[...] = h
    hr_ref[...] = hr
    hi_ref[...] = hi
    tail = x_scr[rows:rows + hist, :]
    conv_ref[...] = tail
    x_scr[0:hist, :] = tail
    yc = _s5_readout(re_scr[...], im_scr[...], u, cvec_ref[...], cbd_ref[...], wglu_ref[...])
    for j, slab in enumerate(_lane_slabs(yc)):
        yc_ref[j] = slab
    yb = _rms(b_scr[...] * _gelu_tanh(gate_br), bvec[6:7, :])
    for j, slab in enumerate(_lane_slabs(yb)):
        yb_ref[j] = slab


def _s5_param_specs(l):
    return [_layer_spec(l, (SUBLANES, S5_W)), _layer_spec(l, (SUBLANES, C_WIDTH)),
            _layer_spec(l, (C_WIDTH, 2 * S5_W)), _layer_spec(l, (2 * S5_W, C_WIDTH)),
            _layer_spec(l, (C_WIDTH, C_WIDTH))]


def _s5_scratch():
    return [pltpu.VMEM((SUBLANES, S5_W), F32), pltpu.VMEM((C_WIDTH, 2 * S5_W), BF16)]


def _s5_param_args(wts):
    return (wts["c_lam"], wts["c_vec"], wts["c_bbd"], wts["c_cbd"], wts["c_wglu"])


def _scan_tm(cols_b, cols_c, nb, seq, wts, l):
    rows = SCAN_STEPS * nb
    hist = (CONV_WIDTH - 1) * nb
    slab = lambda t: (0, t, 0)
    fixed = lambda t: (0, 0)
    nb_slabs, nc_slabs = B_WIDTH // LANES, C_WIDTH // LANES
    return pl.pallas_call(
        functools.partial(_scan_tm_kernel, nb=nb),
        grid=(seq // SCAN_STEPS,),
        in_specs=[pl.BlockSpec((2 * nb_slabs, rows, LANES), slab), pl.BlockSpec((nc_slabs, rows, LANES), slab)]
                 + _lru_param_specs(l) + _s5_param_specs(l),
        out_specs=[pl.BlockSpec((nb_slabs, rows, LANES), slab), pl.BlockSpec((hist, B_WIDTH), fixed),
                   pl.BlockSpec((nb, B_WIDTH), fixed),
                   pl.BlockSpec((nc_slabs, rows, LANES), slab), pl.BlockSpec((nb, S5_W), fixed),
                   pl.BlockSpec((nb, S5_W), fixed)],
        out_shape=[jax.ShapeDtypeStruct((nb_slabs, nb * seq, LANES), F32),
                   jax.ShapeDtypeStruct((hist, B_WIDTH), F32), jax.ShapeDtypeStruct((nb, B_WIDTH), F32),
                   jax.ShapeDtypeStruct((nc_slabs, nb * seq, LANES), F32),
                   jax.ShapeDtypeStruct((nb, S5_W), F32), jax.ShapeDtypeStruct((nb, S5_W), F32)],
        scratch_shapes=[pltpu.VMEM((hist + rows, B_WIDTH), F32), pltpu.VMEM((rows, B_WIDTH), F32),
                        pltpu.VMEM((rows, B_WIDTH), F32), pltpu.VMEM((nb, B_WIDTH), F32),
                        pltpu.VMEM((rows, S5_W), F32), pltpu.VMEM((rows, S5_W), F32),
                        pltpu.VMEM((nb, S5_W), F32), pltpu.VMEM((nb, S5_W), F32)] + _s5_scratch(),
        compiler_params=_params(1),
        name="scan_tm",
    )(cols_b, cols_c, *_lru_param_args(wts), *_s5_param_args(wts))


def _s5_sample_kernel(*refs, steps, n_prev):
    u_ref, h0r_ref, h0i_ref, lam_ref, vec_ref, bbd_ref, cbd_ref, wglu_ref = refs[0:8]
    prev_refs = refs[8:8 + 2 * n_prev]
    y_ref, hr_out_ref, hi_out_ref, abar_scr, bbar_scr = refs[8 + 2 * n_prev:]
    rows = u_ref.shape[0]
    nseq = rows // steps

    @pl.when(pl.program_id(0) == 0)
    def _():
        _s5_discretise(lam_ref[...], bbd_ref, bbar_scr, abar_scr)

    tloc = _row_iota((rows, S5_W)) & (steps - 1)
    h0r = jnp.broadcast_to(h0r_ref[...], (nseq, steps, S5_W)).reshape(rows, S5_W)
    h0i = jnp.broadcast_to(h0i_ref[...], (nseq, steps, S5_W)).reshape(rows, S5_W)
    y, hr, hi = _s5_body(u_ref[...], h0r, h0i, tloc, steps, abar_scr, bbar_scr, vec_ref[...],
                         cbd_ref[...], wglu_ref[...])
    y_ref[...] = y
    _stack_previous(prev_refs[0::2], hr_out_ref)
    _stack_previous(prev_refs[1::2], hi_out_ref)
    hr_out_ref[n_prev] = _last_steps(hr, nseq, steps, 1)
    hi_out_ref[n_prev] = _last_steps(hi, nseq, steps, 1)


def _s5_sample(cols_c, h0r, h0i, prevs, nb, steps, wts, l):
    bb = SAMPLE_SEQS
    rows = bb * steps
    n_prev = len(prevs)
    blk = lambda i: (i, 0)
    seq3 = lambda i: (0, i, 0, 0)
    st = pl.BlockSpec((None, bb, 1, S5_W), lambda i: (l, i, 0, 0))
    prev_specs, prev_args = [], []
    for pr, pi in prevs:
        prev_specs += [pl.BlockSpec((None, bb, 1, S5_W), seq3)] * 2
        prev_args += [pr, pi]
    stacked = pl.BlockSpec((n_prev + 1, bb, 1, S5_W), seq3)
    return pl.pallas_call(
        functools.partial(_s5_sample_kernel, steps=steps, n_prev=n_prev),
        grid=(nb // bb,),
        in_specs=[pl.BlockSpec((rows, C_WIDTH), blk), st, st] + _s5_param_specs(l) + prev_specs,
        out_specs=[pl.BlockSpec((rows, C_WIDTH), blk), stacked, stacked],
        out_shape=[jax.ShapeDtypeStruct((nb * steps, C_WIDTH), F32),
                   jax.ShapeDtypeStruct((n_prev + 1, nb, 1, S5_W), F32),
                   jax.ShapeDtypeStruct((n_prev + 1, nb, 1, S5_W), F32)],
        scratch_shapes=_s5_scratch(),
        compiler_params=_params(1),
        name="s5_sample",
    )(cols_c, h0r, h0i, *_s5_param_args(wts), *prev_args)


def _rows8(rows, width):
    m = jnp.stack([r.reshape(r.shape[0], width) for r in rows], axis=1)
    return jnp.pad(m, ((0, 0), (0, SUBLANES - m.shape[1]), (0, 0)))


def _stacked_weights(w):
    depth = w["g_mix"].shape[0]
    eye_b = jnp.eye(B_BLOCKS, dtype=F32)
    eye_g = jnp.eye(S5_GROUPS, dtype=F32)
    bd4 = lambda m: jnp.einsum("lnde,nm->lndme", m, eye_b).reshape(depth, B_WIDTH, B_WIDTH)
    zeros_lora = jnp.zeros((depth, LANES - 64, A_WIDTH), F32)
    b_in = lambda m: jnp.einsum("lgpc,gh->lgchp", m, eye_g).reshape(depth, C_WIDTH, S5_W)
    c_out = lambda m: jnp.einsum("lgcp,gh->lgphc", m, eye_g).reshape(depth, S5_W, C_WIDTH)
    g_final = jnp.broadcast_to(w["g_final"][None], (depth, D_MODEL))
    return dict(
        g_mix=w["g_mix"][:, None, :],
        w_in=_bf(w["w_in"]),
        mu=w["mu_a"][:, None, :],
        a_vec=_rows8([w["w0"], w["a0"], w["k_k"], w["k_a"], w["r_k"], w["lnx_w"], w["lnx_b"]], A_WIDTH),
        wdec=_bf(jnp.concatenate([w["w_dec2"], zeros_lora], axis=1)),
        wa=_bf(jnp.concatenate([zeros_lora, w["w_a2"]], axis=1)),
        wg=_bf(w["w_g2"]),
        b_vec=_rows8([w["conv_w"][:, 0], w["conv_w"][:, 1], w["conv_w"][:, 2], w["conv_w"][:, 3],
                      w["conv_b"], w["lru_lambda"], w["g_out_b"]], B_WIDTH),
        b_bias=jnp.concatenate([w["b_rg"], w["b_ig"]], axis=1)[:, None, :],
        b_wgates=_bf(jnp.concatenate([bd4(w["w_rg"]), bd4(w["w_ig"])], axis=2)),
        c_lam=_rows8([w["s5_lam_re"], w["s5_lam_im"], jnp.repeat(w["s5_log_dt"], S5_STATE, axis=1)], S5_W),
        c_vec=_rows8([w["s5_d"], w["b_glu"], w["g_out_c"]], C_WIDTH),
        c_bbd=jnp.concatenate([b_in(w["s5_b_re"]), b_in(w["s5_b_im"])], axis=2),
        c_cbd=_bf(jnp.concatenate([c_out(w["s5_c_re"]), -c_out(w["s5_c_im"])], axis=1)),
        c_wglu=_bf(w["w_glu"]),
        w_out=_bf(w["w_out"]),
        p_vec=_rows8([w["g_ffn"], w["g_ple"], g_final], D_MODEL),
        w_up=_bf(w["w_ffn_up"]),
        w_down=_bf(w["w_ffn_down"]),
        w_ple=_bf(w["w_ple"]),
        w_gate=_bf(w["w_ple_gate"]),
    )


def _run_prompt(x, p, wts, depth):
    nb, seq, _ = x.shape
    h = x.reshape(nb * seq, D_MODEL)
    p = p.reshape(depth, nb * seq, PLE_DIM)
    outs = []
    for l in range(depth):
        xs_a, last, cols_b, cols_c = _proj_in_tm(h, wts, l, nb, seq)
        ya, wkv = _rwkv_prompt(xs_a, nb, seq, wts, l)
        yb, conv, lru, yc, s5r, s5i = _scan_tm(cols_b, cols_c, nb, seq, wts, l)
        h = _post(h, ya, yb, yc, p, wts, l, l == depth - 1, nb_tm=nb)
        outs.append((last[:, SUBLANES - 1], wkv, conv, lru, s5r, s5i))
    shift, wkv, conv, lru, s5r, s5i = (jnp.stack([o[j] for o in outs], axis=0) for j in range(6))
    conv = jnp.swapaxes(conv.reshape(depth, CONV_WIDTH - 1, nb, B_WIDTH), 1, 2)
    s5_shape = (depth, nb, S5_GROUPS, S5_STATE)
    return h.reshape(nb, seq, D_MODEL), (shift, wkv, conv, lru, s5r.reshape(s5_shape), s5i.reshape(s5_shape))


def _run_sample(x, p, states, wts, depth):
    nb, steps, _ = x.shape
    st_shift, st_wkv, st_conv, st_lru, st_s5r, st_s5i = states
    h = x.reshape(nb * steps, D_MODEL)
    p = p.reshape(depth, nb * steps, PLE_DIM)
    shift0 = st_shift[:, :, None, :]
    conv0_rows = jnp.pad(st_conv, ((0, 0), (0, 0), (0, steps - (CONV_WIDTH - 1)), (0, 0))
                         ).reshape(depth, nb * steps, B_WIDTH)
    lru0 = st_lru[:, :, None, :]
    s5r0 = st_s5r.reshape(depth, nb, 1, S5_W)
    s5i0 = st_s5i.reshape(depth, nb, 1, S5_W)
    prev_a, prev_b, prev_c = [], [], []
    for l in range(depth):
        last = l == depth - 1
        cols_a, cols_b, cols_c = _proj_in(h, wts["g_mix"], wts["w_in"], l)
        ya, shift, wkv = _rwkv_sample(cols_a, shift0, st_wkv, prev_a if last else [], nb, steps, wts, l)
        yb, conv, lru = _lru_sample(cols_b, conv0_rows, lru0, prev_b if last else [], nb, steps, wts, l)
        yc, s5r, s5i = _s5_sample(cols_c, s5r0, s5i0, prev_c if last else [], nb, steps, wts, l)
        h = _post(h, ya, yb, yc, p, wts, l, last)
        prev_a.append((shift, wkv))
        prev_b.append((conv, lru))
        prev_c.append((s5r, s5i))
    s5_shape = (depth, nb, S5_GROUPS, S5_STATE)
    new_states = (shift.reshape(depth, nb, A_COLS), wkv, conv, lru.reshape(depth, nb, B_WIDTH),
                  s5r.reshape(s5_shape), s5i.reshape(s5_shape))
    return h.reshape(nb, steps, D_MODEL), new_states


def kernel(x_prompt, x_sample, p_prompt, p_sample, state_shift, state_wkv, state_conv, state_lru, state_s5_re, state_s5_im, g_mix, w_in, mu_a, w0, w_dec2, a0, w_a2, w_g2, k_k, k_a, r_k, lnx_w, lnx_b, conv_w, conv_b, w_rg, b_rg, w_ig, b_ig, lru_lambda, g_out_b, s5_lam_re, s5_lam_im, s5_log_dt, s5_b_re, s5_b_im, s5_c_re, s5_c_im, s5_d, w_glu, b_glu, g_out_c, w_out, g_ffn, w_ffn_up, w_ffn_down, g_ple, w_ple, w_ple_gate, g_final):
    w = dict(g_mix=g_mix, w_in=w_in, mu_a=mu_a, w0=w0, w_dec2=w_dec2, a0=a0, w_a2=w_a2, w_g2=w_g2, k_k=k_k,
             k_a=k_a, r_k=r_k, lnx_w=lnx_w, lnx_b=lnx_b, conv_w=conv_w, conv_b=conv_b, w_rg=w_rg, b_rg=b_rg,
             w_ig=w_ig, b_ig=b_ig, lru_lambda=lru_lambda, g_out_b=g_out_b, s5_lam_re=s5_lam_re,
             s5_lam_im=s5_lam_im, s5_log_dt=s5_log_dt, s5_b_re=s5_b_re, s5_b_im=s5_b_im, s5_c_re=s5_c_re,
             s5_c_im=s5_c_im, s5_d=s5_d, w_glu=w_glu, b_glu=b_glu, g_out_c=g_out_c, w_out=w_out, g_ffn=g_ffn,
             w_ffn_up=w_ffn_up, w_ffn_down=w_ffn_down, g_ple=g_ple, w_ple=w_ple, w_ple_gate=w_ple_gate,
             g_final=g_final)
    depth = g_mix.shape[0]
    wts = _stacked_weights(w)
    y_prompt, new_p = _run_prompt(x_prompt, p_prompt, wts, depth)
    y_sample, new_s = _run_sample(x_sample, p_sample,
                                  (state_shift, state_wkv, state_conv, state_lru, state_s5_re, state_s5_im),
                                  wts, depth)
    return (y_prompt, y_sample) + new_p + new_s
```

```python
import functools
import math

import jax
import jax.numpy as jnp
from jax import lax
from jax.experimental import pallas as pl
from jax.experimental.pallas import tpu as pltpu

F32 = jnp.float32
BF16 = jnp.bfloat16

D_MODEL = 1024
A_WIDTH = 512
A_HEADS = 8
A_HEAD_DIM = 64
A_COLS = 1792
B_WIDTH = 256
B_BLOCKS = 4
CONV_WIDTH = 4
C_WIDTH = 256
S5_GROUPS = 16
S5_STATE = 64
S5_W = S5_GROUPS * S5_STATE
IN_COLS = A_COLS + 2 * B_WIDTH + C_WIDTH
D_FF = 2816
PLE_DIM = 256
LRU_C = 8.0
RMS_EPS = 1e-6
GN_EPS = 64e-5

LANES = 128
SUBLANES = 8
VMEM_LIMIT_BYTES = 56 * 1024 * 1024

TOKEN_TILE = 512
FFN_CHUNK = 256
RWKV_CHUNK = 64
RWKV_BLOCK = 256
SCAN_STEPS = 256
SAMPLE_SEQS = 16


def _layer_spec(l, shape):
    nd = len(shape)
    return pl.BlockSpec((None,) + tuple(shape), lambda *_: (l,) + (0,) * nd, pipeline_mode=pl.Buffered(1))


def _params(n_axes):
    return pltpu.CompilerParams(dimension_semantics=("arbitrary",) * n_axes,
                                vmem_limit_bytes=VMEM_LIMIT_BYTES)


def _dot(a, b):
    return jnp.dot(a, b, preferred_element_type=F32)


def _dot_nt(a, b):
    return lax.dot_general(a, b, (((1,), (1,)), ((), ())), preferred_element_type=F32)


def _dot_tn(a, b):
    return lax.dot_general(a, b, (((0,), (0,)), ((), ())), preferred_element_type=F32)


def _bf(x):
    return x.astype(BF16)


def _rms(x, g):
    inv = lax.rsqrt(jnp.mean(x * x, axis=-1, keepdims=True) + RMS_EPS)
    return x * inv * g


def _sigmoid(x):
    return 1.0 / (1.0 + jnp.exp(-x))


def _softplus(x):
    return jnp.maximum(x, 0.0) + jnp.log(1.0 + jnp.exp(-jnp.abs(x)))


def _gelu_tanh(x):
    c = math.sqrt(2.0 / math.pi)
    return 0.5 * x * (1.0 + jnp.tanh(c * (x + 0.044715 * (x * x * x))))


def _row_iota(shape):
    return lax.broadcasted_iota(jnp.int32, shape, 0)


def _shift_rows(x, d):
    return pltpu.roll(x, d, axis=0)


def _lane_slabs(x):
    return [x[:, j * LANES:(j + 1) * LANES] for j in range(x.shape[1] // LANES)]


def _last_steps(x, nseq, steps, n):
    return x.reshape(nseq, steps, x.shape[1])[:, steps - n:steps, :]


def _stack_previous(prev_refs, out_ref):
    for j, ref in enumerate(prev_refs):
        out_ref[j] = ref[...]


def _proj_in_kernel(h_ref, g_ref, w_ref, a_ref, b_ref, c_ref):
    xn = _bf(_rms(h_ref[...], g_ref[...]))
    a_ref[...] = _dot(xn, w_ref[:, 0:A_COLS])
    b_ref[...] = _dot(xn, w_ref[:, A_COLS:A_COLS + 2 * B_WIDTH])
    c_ref[...] = _dot(xn, w_ref[:, A_COLS + 2 * B_WIDTH:IN_COLS])


def _proj_in(h, g, w_bf, l):
    n = h.shape[0]
    tm = TOKEN_TILE
    row = lambda i: (i, 0)
    return pl.pallas_call(
        _proj_in_kernel,
        grid=(n // tm,),
        in_specs=[pl.BlockSpec((tm, D_MODEL), row), _layer_spec(l, (1, D_MODEL)),
                  _layer_spec(l, (D_MODEL, IN_COLS))],
        out_specs=[pl.BlockSpec((tm, A_COLS), row), pl.BlockSpec((tm, 2 * B_WIDTH), row),
                   pl.BlockSpec((tm, C_WIDTH), row)],
        out_shape=[jax.ShapeDtypeStruct((n, A_COLS), F32), jax.ShapeDtypeStruct((n, 2 * B_WIDTH), F32),
                   jax.ShapeDtypeStruct((n, C_WIDTH), F32)],
        compiler_params=_params(1),
        name="proj_in",
    )(h, g, w_bf)


def _proj_in_tm_kernel(h_ref, g_ref, w_ref, mu_ref, a_ref, last_ref, b_ref, c_ref, prev_scr, *, nb):
    b = pl.program_id(1)
    tm = h_ref.shape[0]

    @pl.when((pl.program_id(0) == 0) & (b == 0))
    def _():
        prev_scr[...] = jnp.zeros_like(prev_scr)

    xn = _bf(_rms(h_ref[...], g_ref[...]))
    cols = _dot(xn, w_ref[:, 0:A_COLS])
    prev = jnp.where(_row_iota(cols.shape) == 0, prev_scr[b, SUBLANES - 1:SUBLANES, :], _shift_rows(cols, 1))
    a_ref[...] = cols + (prev - cols) * mu_ref[...]
    tail = cols[tm - SUBLANES:tm, :]
    prev_scr[b] = tail
    last_ref[b] = tail
    rows_of_b = pl.ds(b, tm, stride=nb)
    for j, slab in enumerate(_lane_slabs(_dot(xn, w_ref[:, A_COLS:A_COLS + 2 * B_WIDTH]))):
        b_ref[j, rows_of_b, :] = slab
    for j, slab in enumerate(_lane_slabs(_dot(xn, w_ref[:, A_COLS + 2 * B_WIDTH:IN_COLS]))):
        c_ref[j, rows_of_b, :] = slab


def _proj_in_tm(h, wts, l, nb, seq):
    tm = TOKEN_TILE
    nt = seq // tm
    row = lambda i, b: (b * nt + i, 0)
    slab = lambda i, b: (0, i, 0)
    nb_slabs, nc_slabs = 2 * B_WIDTH // LANES, C_WIDTH // LANES
    return pl.pallas_call(
        functools.partial(_proj_in_tm_kernel, nb=nb),
        grid=(nt, nb),
        in_specs=[pl.BlockSpec((tm, D_MODEL), row), _layer_spec(l, (1, D_MODEL)),
                  _layer_spec(l, (D_MODEL, IN_COLS)), _layer_spec(l, (1, A_COLS))],
        out_specs=[pl.BlockSpec((tm, A_COLS), row), pl.BlockSpec((nb, SUBLANES, A_COLS), lambda i, b: (0, 0, 0)),
                   pl.BlockSpec((nb_slabs, tm * nb, LANES), slab),
                   pl.BlockSpec((nc_slabs, tm * nb, LANES), slab)],
        out_shape=[jax.ShapeDtypeStruct((nb * seq, A_COLS), F32),
                   jax.ShapeDtypeStruct((nb, SUBLANES, A_COLS), F32),
                   jax.ShapeDtypeStruct((nb_slabs, nb * seq, LANES), F32),
                   jax.ShapeDtypeStruct((nc_slabs, nb * seq, LANES), F32)],
        scratch_shapes=[pltpu.VMEM((nb, SUBLANES, A_COLS), F32)],
        compiler_params=_params(2),
        name="proj_in_tm",
    )(h, wts["g_mix"], wts["w_in"], wts["mu"])


def _post_kernel(h_ref, ya_ref, yb_ref, yc_ref, p_ref, wo_ref, vec_ref, wup_ref, wdn_ref, wple_ref,
                 wgate_ref, o_ref, act_scr, *, final, nb_tm):
    g_ffn = vec_ref[0:1, :]
    g_ple = vec_ref[1:2, :]
    g_final = vec_ref[2:3, :]
    if nb_tm:
        rows_of_b = pl.ds(pl.program_id(1), h_ref.shape[0], stride=nb_tm)
        yb = jnp.concatenate([yb_ref[j, rows_of_b, :] for j in range(B_WIDTH // LANES)], axis=1)
        yc = jnp.concatenate([yc_ref[j, rows_of_b, :] for j in range(C_WIDTH // LANES)], axis=1)
    else:
        yb, yc = yb_ref[...], yc_ref[...]
    h1 = (h_ref[...] + _dot(_bf(ya_ref[...]), wo_ref[0:A_WIDTH, :])
          + _dot(_bf(yb), wo_ref[A_WIDTH:A_WIDTH + B_WIDTH, :])
          + _dot(_bf(yc), wo_ref[A_WIDTH + B_WIDTH:D_MODEL, :]))
    xf = _bf(_rms(h1, g_ffn))
    for c0 in range(0, D_FF, FFN_CHUNK):
        gate = _dot(xf, wup_ref[:, c0:c0 + FFN_CHUNK])
        up = _dot(xf, wup_ref[:, D_FF + c0:D_FF + c0 + FFN_CHUNK])
        act_scr[:, c0:c0 + FFN_CHUNK] = _bf(gate * _sigmoid(gate) * up)
    h2 = h1 + _dot(act_scr[...], wdn_ref[...])
    ple = _dot(_bf(p_ref[...]), wple_ref[...])
    gate = _sigmoid(_dot(_bf(_rms(h2, g_ple)), wgate_ref[...]))
    h3 = h2 + ple * gate
    if final:
        h3 = _rms(h3, g_final)
    o_ref[...] = h3


def _post(h, ya, yb, yc, p, wts, l, final, nb_tm=0):
    n = h.shape[0]
    tm = TOKEN_TILE
    if nb_tm:
        nt = n // (nb_tm * tm)
        grid = (nt, nb_tm)
        row = lambda i, b: (b * nt + i, 0)
        prow = lambda i, b: (l, b * nt + i, 0)
        slab = lambda i, b: (0, i, 0)
        yb_spec = pl.BlockSpec((B_WIDTH // LANES, tm * nb_tm, LANES), slab, pipeline_mode=pl.Buffered(1))
        yc_spec = pl.BlockSpec((C_WIDTH // LANES, tm * nb_tm, LANES), slab, pipeline_mode=pl.Buffered(1))
    else:
        grid = (n // tm,)
        row = lambda i: (i, 0)
        prow = lambda i: (l, i, 0)
        yb_spec = pl.BlockSpec((tm, B_WIDTH), row)
        yc_spec = pl.BlockSpec((tm, C_WIDTH), row)
    return pl.pallas_call(
        functools.partial(_post_kernel, final=final, nb_tm=nb_tm),
        grid=grid,
        in_specs=[pl.BlockSpec((tm, D_MODEL), row), pl.BlockSpec((tm, A_WIDTH), row),
                  yb_spec, yc_spec,
                  pl.BlockSpec((None, tm, PLE_DIM), prow),
                  _layer_spec(l, (D_MODEL, D_MODEL)), _layer_spec(l, (SUBLANES, D_MODEL)),
                  _layer_spec(l, (D_MODEL, 2 * D_FF)), _layer_spec(l, (D_FF, D_MODEL)),
                  _layer_spec(l, (PLE_DIM, D_MODEL)), _layer_spec(l, (D_MODEL, D_MODEL))],
        out_specs=pl.BlockSpec((tm, D_MODEL), row),
        out_shape=jax.ShapeDtypeStruct((n, D_MODEL), F32),
        scratch_shapes=[pltpu.VMEM((tm, D_FF), BF16)],
        compiler_params=_params(len(grid)),
        name="post",
    )(h, ya, yb, yc, p, wts["w_out"], wts["p_vec"], wts["w_up"], wts["w_down"], wts["w_ple"], wts["w_gate"])


def _head_sum(x):
    first = lax.broadcasted_iota(jnp.int32, (x.shape[0], LANES), 1) < A_HEAD_DIM
    outs = []
    for xp in _lane_slabs(x):
        s0 = jnp.sum(jnp.where(first, xp, 0.0), axis=-1, keepdims=True)
        s1 = jnp.sum(jnp.where(first, 0.0, xp), axis=-1, keepdims=True)
        outs.append(jnp.where(first, s0, s1))
    return jnp.concatenate(outs, axis=1)


def _rwkv_prep(xs, vec, wdec, wa, wg, chunk):
    rows = xs.shape[0]
    r = xs[:, 0:A_WIDTH]
    k = xs[:, A_WIDTH:2 * A_WIDTH]
    v = xs[:, 2 * A_WIDTH:3 * A_WIDTH]
    xwa = xs[:, 3 * A_WIDTH:3 * A_WIDTH + LANES]
    xg = xs[:, 3 * A_WIDTH + LANES:A_COLS]
    w0, a0, k_k, k_a, r_k = (vec[i:i + 1, :] for i in range(5))
    z = w0 + _dot(_bf(jnp.tanh(xwa)), wdec)
    log_decay = -math.exp(-0.5) * _sigmoid(z)
    a = _sigmoid(a0 + _dot(_bf(xwa), wa))
    g = _dot(_bf(_sigmoid(xg)), wg)
    kk_raw = k * k_k
    kk = kk_raw * lax.rsqrt(jnp.maximum(_head_sum(kk_raw * kk_raw), 1e-24))
    k_mod = k * (1.0 + (a - 1.0) * k_a)
    bonus = _head_sum(r * k_mod * r_k) * v
    ri = _row_iota((rows, rows))
    ci = lax.broadcasted_iota(jnp.int32, (rows, rows), 1)
    same_chunk = (ci & (-chunk)) == (ri & (-chunk))
    tri = _bf(jnp.where((ci <= ri) & same_chunk, 1.0, 0.0))
    h1 = _bf(log_decay)
    h2 = _bf(log_decay - h1.astype(F32))
    cum = _dot(tri, h1) + _dot(tri, h2)
    gam = jnp.exp(cum)
    ginv = jnp.exp(-cum)
    gprev = jnp.exp(cum - log_decay)
    return dict(rt=r * gam, kap=kk * gprev, bet=kk * a * ginv, kt=k_mod * ginv, v=v, gam=gam,
                bonus=bonus, g=g)


def _stack_heads(x):
    first = lax.broadcasted_iota(jnp.int32, x.shape, 1) < A_HEAD_DIM
    return _bf(jnp.concatenate([jnp.where(first, x, 0.0), jnp.where(first, 0.0, x)], axis=0))


def _rwkv_local(ops):
    c = ops[0]["kap"].shape[0]
    c2 = 2 * c
    ri = _row_iota((c, c2))
    ci = lax.broadcasted_iota(jnp.int32, (c, c2), 1) & (c - 1)
    strict = ci < ri
    incl = ci <= ri
    eye = jnp.where(ri == ci, 1.0, 0.0)
    same_head = (_row_iota((c2, c2)) & c) == (lax.broadcasted_iota(jnp.int32, (c2, c2), 1) & c)
    merged = c2 % LANES == 0

    def blockdiag(m):
        return _bf(jnp.where(same_head, jnp.concatenate([m, m], axis=0), 0.0))

    st = []
    for o in ops:
        s = {k + "2": _stack_heads(o[k]) for k in ("kap", "bet", "kt", "v")}
        s["rt"] = o["rt"]
        lhs = _bf(jnp.concatenate([o["kap"], o["rt"]], axis=0))
        if merged:
            g = _dot_nt(lhs, jnp.concatenate([s["bet2"], s["kt2"]], axis=0))
            a_b, a_k, a_rb, a_rk = g[0:c, 0:c2], g[0:c, c2:2 * c2], g[c:c2, 0:c2], g[c:c2, c2:2 * c2]
        else:
            gb, gk = _dot_nt(lhs, s["bet2"]), _dot_nt(lhs, s["kt2"])
            a_b, a_k, a_rb, a_rk = gb[0:c], gk[0:c], gb[c:c2], gk[c:c2]
        s["a_b"] = jnp.where(strict, a_b, 0.0)
        s["a_k"] = _bf(jnp.where(strict, a_k, 0.0))
        s["a_rb"] = _bf(jnp.where(incl, a_rb, 0.0))
        s["a_rk"] = _bf(jnp.where(incl, a_rk, 0.0))
        st.append(s)
    for s in st:
        s["t"] = eye - s["a_b"]
        s["lp"] = s["a_b"]
        s["akv"] = _dot(s["a_k"], s["v2"])
    n = 2
    while n < c:
        for s in st:
            s["lp"] = _dot(_bf(s["lp"]), blockdiag(s["lp"]))
        for s in st:
            s["t"] = s["t"] + _dot(_bf(s["t"]), blockdiag(s["lp"]))
        n *= 2
    for s in st:
        wu = _dot(_bf(s["t"]), jnp.concatenate([s["kap2"], _stack_heads(s["akv"])], axis=1))
        s["w"] = -wu[:, 0:LANES]
        s["u0"] = -wu[:, LANES:2 * LANES]
    return st


def _rwkv_state(st, states, g_ends):
    c = st[0]["rt"].shape[0]
    merged = (2 * c) % LANES == 0
    xs = []
    for s, state in zip(st, states):
        xs.append(_dot_nt(_bf(jnp.concatenate([s["w"], s["rt"]], axis=0)), _bf(state)))
    u2s = [_stack_heads(x[0:c] + s["u0"]) for x, s in zip(xs, st)]
    ys, new_states = [], []
    for s, x, u2 in zip(st, xs, u2s):
        if merged:
            y = x[c:2 * c] + _dot(jnp.concatenate([s["a_rb"], s["a_rk"]], axis=1),
                                  jnp.concatenate([u2, s["v2"]], axis=0))
        else:
            y = x[c:2 * c] + _dot(s["a_rb"], u2) + _dot(s["a_rk"], s["v2"])
        ys.append(y)
    for s, u2, state, g_end in zip(st, u2s, states, g_ends):
        ds = _dot_tn(jnp.concatenate([u2, s["v2"]], axis=0), jnp.concatenate([s["bet2"], s["kt2"]], axis=0))
        new_states.append((state + ds) * g_end)
    return ys, new_states


def _rwkv_finish(y, bonus, g, vec):
    lnx_w = vec[5:6, :]
    lnx_b = vec[6:7, :]
    inv_n = 1.0 / A_HEAD_DIM
    mu = _head_sum(y) * inv_n
    d = y - mu
    var = _head_sum(d * d) * inv_n
    yn = d * lax.rsqrt(var + GN_EPS) * lnx_w + lnx_b
    return (yn + bonus) * g


def _pair_state(s_ref, idx, p):
    z = jnp.zeros((A_HEAD_DIM, A_HEAD_DIM), F32)
    top = jnp.concatenate([s_ref[idx, 2 * p], z], axis=1)
    bot = jnp.concatenate([z, s_ref[idx, 2 * p + 1]], axis=1)
    return jnp.concatenate([top, bot], axis=0)


def _rwkv_prompt_kernel(xs_ref, vec_ref, wdec_ref, wa_ref, wg_ref, y_ref, wkv_ref, s_scr):
    t = pl.program_id(1)
    rows = xs_ref.shape[0]
    chunk = RWKV_CHUNK

    @pl.when(t == 0)
    def _():
        s_scr[...] = jnp.zeros_like(s_scr)

    q = _rwkv_prep(xs_ref[...], vec_ref[...], wdec_ref[...], wa_ref[...], wg_ref[...], chunk)
    n_pairs = A_HEADS // 2
    lanes = [slice(p * LANES, (p + 1) * LANES) for p in range(n_pairs)]
    starts = list(range(0, rows, chunk))
    ops = [{k: q[k][c0:c0 + chunk, ln] for k in ("kap", "rt", "bet", "kt", "v")}
           for c0 in starts for ln in lanes]
    st = _rwkv_local(ops)
    states = [s_scr[p] for p in range(n_pairs)]
    y_rows = []
    for i, c0 in enumerate(starts):
        g_ends = [q["gam"][c0 + chunk - 1:c0 + chunk, ln] for ln in lanes]
        ys, states = _rwkv_state(st[i * n_pairs:(i + 1) * n_pairs], states, g_ends)
        y_rows.append(jnp.concatenate(ys, axis=1))
    for p in range(n_pairs):
        s_scr[p] = states[p]
        wkv_ref[0, 2 * p] = states[p][0:A_HEAD_DIM, 0:A_HEAD_DIM]
        wkv_ref[0, 2 * p + 1] = states[p][A_HEAD_DIM:LANES, A_HEAD_DIM:LANES]
    y = jnp.concatenate(y_rows, axis=0) if len(y_rows) > 1 else y_rows[0]
    y_ref[...] = _rwkv_finish(y, q["bonus"], q["g"], vec_ref[...])


def _rwkv_param_specs(l):
    return [_layer_spec(l, (SUBLANES, A_WIDTH)), _layer_spec(l, (LANES, A_WIDTH)),
            _layer_spec(l, (LANES, A_WIDTH)), _layer_spec(l, (LANES, A_WIDTH))]


def _rwkv_param_args(wts):
    return (wts["a_vec"], wts["wdec"], wts["wa"], wts["wg"])


def _rwkv_prompt(xs_a, nb, seq, wts, l):
    tb = RWKV_BLOCK
    nt = seq // tb
    blk = lambda b, t: (b * nt + t, 0)
    return pl.pallas_call(
        _rwkv_prompt_kernel,
        grid=(nb, nt),
        in_specs=[pl.BlockSpec((tb, A_COLS), blk)] + _rwkv_param_specs(l),
        out_specs=[pl.BlockSpec((tb, A_WIDTH), blk),
                   pl.BlockSpec((1, A_HEADS, A_HEAD_DIM, A_HEAD_DIM), lambda b, t: (b, 0, 0, 0))],
        out_shape=[jax.ShapeDtypeStruct((nb * seq, A_WIDTH), F32),
                   jax.ShapeDtypeStruct((nb, A_HEADS, A_HEAD_DIM, A_HEAD_DIM), F32)],
        scratch_shapes=[pltpu.VMEM((A_HEADS // 2, LANES, LANES), F32)],
        compiler_params=_params(2),
        name="rwkv_prompt",
    )(xs_a, *_rwkv_param_args(wts))


def _rwkv_sample_kernel(*refs, steps, n_prev):
    cols_ref, shift_ref, wkv0_ref, mu_ref, vec_ref, wdec_ref, wa_ref, wg_ref = refs[0:8]
    prev_refs = refs[8:8 + 2 * n_prev]
    y_ref, shift_out_ref, wkv_out_ref = refs[8 + 2 * n_prev:]
    rows = cols_ref.shape[0]
    nseq = rows // steps
    cols = cols_ref[...]
    tloc = _row_iota(cols.shape) & (steps - 1)
    shift0 = jnp.broadcast_to(shift_ref[...], (nseq, steps, A_COLS)).reshape(rows, A_COLS)
    prev = jnp.where(tloc == 0, shift0, _shift_rows(cols, 1))
    xs = cols + (prev - cols) * mu_ref[...]
    q = _rwkv_prep(xs, vec_ref[...], wdec_ref[...], wa_ref[...], wg_ref[...], steps)
    n_pairs = A_HEADS // 2
    lanes = [slice(p * LANES, (p + 1) * LANES) for p in range(n_pairs)]
    chains = [(s, p) for s in range(nseq) for p in range(n_pairs)]
    ops = [{k: q[k][s * steps:(s + 1) * steps, lanes[p]] for k in ("kap", "rt", "bet", "kt", "v")}
           for s, p in chains]
    states = [_pair_state(wkv0_ref, s, p) for s, p in chains]
    g_ends = [q["gam"][(s + 1) * steps - 1:(s + 1) * steps, lanes[p]] for s, p in chains]
    ys, new_states = _rwkv_state(_rwkv_local(ops), states, g_ends)
    _stack_previous(prev_refs[0::2], shift_out_ref)
    _stack_previous(prev_refs[1::2], wkv_out_ref)
    for (s, p), s_new in zip(chains, new_states):
        wkv_out_ref[n_prev, s, 2 * p] = s_new[0:A_HEAD_DIM, 0:A_HEAD_DIM]
        wkv_out_ref[n_prev, s, 2 * p + 1] = s_new[A_HEAD_DIM:LANES, A_HEAD_DIM:LANES]
    shift_out_ref[n_prev] = _last_steps(cols, nseq, steps, 1)
    y = jnp.concatenate([jnp.concatenate(ys[s * n_pairs:(s + 1) * n_pairs], axis=1) for s in range(nseq)],
                        axis=0)
    y_ref[...] = _rwkv_finish(y, q["bonus"], q["g"], vec_ref[...])


def _rwkv_sample(cols_a, shift0, wkv0, prevs, nb, steps, wts, l):
    bb = SAMPLE_SEQS
    rows = bb * steps
    n_prev = len(prevs)
    blk = lambda i: (i, 0)
    seq3 = lambda i: (0, i, 0, 0)
    seq4 = lambda i: (0, i, 0, 0, 0)
    prev_specs, prev_args = [], []
    for sh, wk in prevs:
        prev_specs += [pl.BlockSpec((None, bb, 1, A_COLS), seq3),
                       pl.BlockSpec((None, bb, A_HEADS, A_HEAD_DIM, A_HEAD_DIM), seq4)]
        prev_args += [sh, wk]
    return pl.pallas_call(
        functools.partial(_rwkv_sample_kernel, steps=steps, n_prev=n_prev),
        grid=(nb // bb,),
        in_specs=[pl.BlockSpec((rows, A_COLS), blk),
                  pl.BlockSpec((None, bb, 1, A_COLS), lambda i: (l, i, 0, 0)),
                  pl.BlockSpec((None, bb, A_HEADS, A_HEAD_DIM, A_HEAD_DIM), lambda i: (l, i, 0, 0, 0)),
                  _layer_spec(l, (1, A_COLS))]
                 + _rwkv_param_specs(l) + prev_specs,
        out_specs=[pl.BlockSpec((rows, A_WIDTH), blk),
                   pl.BlockSpec((n_prev + 1, bb, 1, A_COLS), seq3),
                   pl.BlockSpec((n_prev + 1, bb, A_HEADS, A_HEAD_DIM, A_HEAD_DIM), seq4)],
        out_shape=[jax.ShapeDtypeStruct((nb * steps, A_WIDTH), F32),
                   jax.ShapeDtypeStruct((n_prev + 1, nb, 1, A_COLS), F32),
                   jax.ShapeDtypeStruct((n_prev + 1, nb, A_HEADS, A_HEAD_DIM, A_HEAD_DIM), F32)],
        compiler_params=_params(1),
        name="rwkv_sample",
    )(cols_a, shift0, wkv0, wts["mu"], *_rwkv_param_args(wts), *prev_args)


def _lru_gates(xc, vec, bias, wgates):
    lam = vec[5:6, :]
    gates = _sigmoid(_dot(_bf(xc), wgates) + bias)
    gate_r = gates[:, 0:B_WIDTH]
    gate_i = gates[:, B_WIDTH:2 * B_WIDTH]
    log_a = (-LRU_C) * gate_r * _softplus(-lam)
    a = jnp.exp(log_a)
    mult = jnp.sqrt(-jnp.tanh(log_a) * (a * a + 1.0))
    return a, mult * gate_i * xc


def _lru_body(gate_br, xb, conv_prev, h0, tloc, tlen, vec, bias, wgates):
    rows = xb.shape[0]
    conv_b = vec[4:5, :]
    g_out = vec[6:7, :]
    xc = conv_b + vec[3:4, :] * xb
    for j in (1, 2, 3):
        tail = conv_prev if j == 3 else _shift_rows(conv_prev, rows - (3 - j))
        xc = xc + vec[3 - j:4 - j, :] * jnp.where(tloc >= j, _shift_rows(xb, j), tail)
    a, b = _lru_gates(xc, vec, bias, wgates)
    b = b + jnp.where(tloc == 0, a * h0, 0.0)
    d = 1
    while d < tlen:
        keep = tloc >= d
        a_s = jnp.where(keep, _shift_rows(a, d), 1.0)
        b_s = jnp.where(keep, _shift_rows(b, d), 0.0)
        b = a * b_s + b
        a = a * a_s
        d *= 2
    hs = b
    y = hs * _gelu_tanh(gate_br)
    return _rms(y, g_out), hs


def _lru_param_specs(l):
    return [_layer_spec(l, (SUBLANES, B_WIDTH)), _layer_spec(l, (1, 2 * B_WIDTH)),
            _layer_spec(l, (B_WIDTH, 2 * B_WIDTH))]


def _lru_param_args(wts):
    return (wts["b_vec"], wts["b_bias"], wts["b_wgates"])


def _lru_sample_kernel(*refs, steps, n_prev):
    cols_ref, conv0_ref, h0_ref, vec_ref, bias_ref, wg_ref = refs[0:6]
    prev_refs = refs[6:6 + 2 * n_prev]
    y_ref, conv_out_ref, h_out_ref = refs[6 + 2 * n_prev:]
    rows = cols_ref.shape[0]
    nseq = rows // steps
    gate_br = cols_ref[:, 0:B_WIDTH]
    xb = cols_ref[:, B_WIDTH:2 * B_WIDTH]
    tloc = _row_iota(xb.shape) & (steps - 1)
    h0 = jnp.broadcast_to(h0_ref[...], (nseq, steps, B_WIDTH)).reshape(rows, B_WIDTH)
    y, hs = _lru_body(gate_br, xb, conv0_ref[...], h0, tloc, steps, vec_ref[...], bias_ref[...], wg_ref[...])
    y_ref[...] = y
    _stack_previous(prev_refs[0::2], conv_out_ref)
    _stack_previous(prev_refs[1::2], h_out_ref)
    conv_out_ref[n_prev] = _last_steps(xb, nseq, steps, CONV_WIDTH - 1)
    h_out_ref[n_prev] = _last_steps(hs, nseq, steps, 1)


def _lru_sample(cols_b, conv0_rows, h0, prevs, nb, steps, wts, l):
    bb = SAMPLE_SEQS
    rows = bb * steps
    n_prev = len(prevs)
    blk = lambda i: (i, 0)
    seq3 = lambda i: (0, i, 0, 0)
    prev_specs, prev_args = [], []
    for cv, hh in prevs:
        prev_specs += [pl.BlockSpec((None, bb, CONV_WIDTH - 1, B_WIDTH), seq3),
                       pl.BlockSpec((None, bb, 1, B_WIDTH), seq3)]
        prev_args += [cv, hh]
    return pl.pallas_call(
        functools.partial(_lru_sample_kernel, steps=steps, n_prev=n_prev),
        grid=(nb // bb,),
        in_specs=[pl.BlockSpec((rows, 2 * B_WIDTH), blk),
                  pl.BlockSpec((None, rows, B_WIDTH), lambda i: (l, i, 0)),
                  pl.BlockSpec((None, bb, 1, B_WIDTH), lambda i: (l, i, 0, 0))]
                 + _lru_param_specs(l) + prev_specs,
        out_specs=[pl.BlockSpec((rows, B_WIDTH), blk),
                   pl.BlockSpec((n_prev + 1, bb, CONV_WIDTH - 1, B_WIDTH), seq3),
                   pl.BlockSpec((n_prev + 1, bb, 1, B_WIDTH), seq3)],
        out_shape=[jax.ShapeDtypeStruct((nb * steps, B_WIDTH), F32),
                   jax.ShapeDtypeStruct((n_prev + 1, nb, CONV_WIDTH - 1, B_WIDTH), F32),
                   jax.ShapeDtypeStruct((n_prev + 1, nb, 1, B_WIDTH), F32)],
        compiler_params=_params(1),
        name="lru_sample",
    )(cols_b, conv0_rows, h0, *_lru_param_args(wts), *prev_args)


def _s5_discretise(lam, b_ref, bbar_scr, abar_scr):
    lr = lam[0:1, :]
    li = lam[1:2, :]
    dt = jnp.exp(lam[2:3, :])
    mag = jnp.exp(lr * dt)
    ar = mag * jnp.cos(li * dt)
    ai = mag * jnp.sin(li * dt)
    den = lr * lr + li * li
    cr = ((ar - 1.0) * lr + ai * li) / den
    ci = (ai * lr - (ar - 1.0) * li) / den
    b_re = b_ref[:, 0:S5_W]
    b_im = b_ref[:, S5_W:2 * S5_W]
    bbar_scr[:, 0:S5_W] = _bf(cr * b_re - ci * b_im)
    bbar_scr[:, S5_W:2 * S5_W] = _bf(cr * b_im + ci * b_re)
    abar_scr[...] = jnp.concatenate([ar, ai, jnp.zeros((SUBLANES - 2, S5_W), F32)], axis=0)


def _s5_drive(u, bbar_scr):
    ub = _bf(u)
    return _dot(ub, bbar_scr[:, 0:S5_W]), _dot(ub, bbar_scr[:, S5_W:2 * S5_W])


def _s5_readout(hr, hi, u, vec, cbd, wglu):
    s5_d = vec[0:1, :]
    b_glu = vec[1:2, :]
    g_out = vec[2:3, :]
    y = _dot(_bf(hr), cbd[0:S5_W, :]) + _dot(_bf(hi), cbd[S5_W:2 * S5_W, :]) + s5_d * u
    z = _gelu_tanh(y)
    out = z * _sigmoid(_dot(_bf(z), wglu) + b_glu)
    return _rms(out, g_out)


def _s5_body(u, h0r, h0i, tloc, tlen, abar_scr, bbar_scr, vec, cbd, wglu):
    ar, ai = abar_scr[0:1, :], abar_scr[1:2, :]
    hr, hi = _s5_drive(u, bbar_scr)
    first = tloc == 0
    hr = hr + jnp.where(first, ar * h0r - ai * h0i, 0.0)
    hi = hi + jnp.where(first, ar * h0i + ai * h0r, 0.0)
    pr, pi = ar, ai
    d = 1
    while d < tlen:
        keep = tloc >= d
        sr = jnp.where(keep, _shift_rows(hr, d), 0.0)
        si = jnp.where(keep, _shift_rows(hi, d), 0.0)
        hr, hi = hr + pr * sr - pi * si, hi + pr * si + pi * sr
        pr, pi = pr * pr - pi * pi, 2.0 * pr * pi
        d *= 2
    return _s5_readout(hr, hi, u, vec, cbd, wglu), hr, hi


def _scan_tm_kernel(colsb_ref, u_ref, bvec_ref, bias_ref, wg_ref, lam_ref, cvec_ref, bbd_ref, cbd_ref, wglu_ref,
                    yb_ref, conv_ref, h_ref, yc_ref, hr_ref, hi_ref,
                    x_scr, a_scr, b_scr, h_scr, re_scr, im_scr, sr_scr, si_scr, abar_scr, bbar_scr, *, nb):
    rows = a_scr.shape[0]
    hist = (CONV_WIDTH - 1) * nb

    @pl.when(pl.program_id(0) == 0)
    def _():
        x_scr[0:hist, :] = jnp.zeros((hist, B_WIDTH), F32)
        h_scr[...] = jnp.zeros_like(h_scr)
        sr_scr[...] = jnp.zeros_like(sr_scr)
        si_scr[...] = jnp.zeros_like(si_scr)
        _s5_discretise(lam_ref[...], bbd_ref, bbar_scr, abar_scr)

    u = jnp.concatenate([u_ref[j] for j in range(C_WIDTH // LANES)], axis=1)
    dr, di = _s5_drive(u, bbar_scr)
    re_scr[...] = dr
    im_scr[...] = di
    ar = jnp.broadcast_to(abar_scr[0:1, :], (nb, S5_W))
    ai = jnp.broadcast_to(abar_scr[1:2, :], (nb, S5_W))

    bvec = bvec_ref[...]
    gate_br = jnp.concatenate([colsb_ref[0], colsb_ref[1]], axis=1)
    xb = jnp.concatenate([colsb_ref[2], colsb_ref[3]], axis=1)
    x_scr[hist:hist + rows, :] = xb
    xc = bvec[4:5, :] + bvec[3:4, :] * xb
    for j in range(1, CONV_WIDTH):
        xc = xc + bvec[3 - j:4 - j, :] * x_scr[hist - j * nb:hist - j * nb + rows, :]
    a, b = _lru_gates(xc, bvec, bias_ref[...], wg_ref[...])
    a_scr[...] = a
    b_scr[...] = b

    def step(i, carry):
        h, hr, hi = carry
        rw = pl.ds(pl.multiple_of(i * nb, nb), nb)
        nr = ar * hr - ai * hi + re_scr[rw, :]
        ni = ar * hi + ai * hr + im_scr[rw, :]
        h = a_scr[rw, :] * h + b_scr[rw, :]
        re_scr[rw, :] = nr
        im_scr[rw, :] = ni
        b_scr[rw, :] = h
        return h, nr, ni

    h, hr, hi = lax.fori_loop(0, rows // nb, step, (h_scr[...], sr_scr[...], si_scr[...]), unroll=4)
    h_scr[...] = h
    sr_scr[...] = hr
    si_scr[...] = hi
    h_ref[...] = h
    hr_ref[...] = hr
    hi_ref[...] = hi
    tail = x_scr[rows:rows + hist, :]
    conv_ref[...] = tail
    x_scr[0:hist, :] = tail
    yc = _s5_readout(re_scr[...], im_scr[...], u, cvec_ref[...], cbd_ref[...], wglu_ref[...])
    for j, slab in enumerate(_lane_slabs(yc)):
        yc_ref[j] = slab
    yb = _rms(b_scr[...] * _gelu_tanh(gate_br), bvec[6:7, :])
    for j, slab in enumerate(_lane_slabs(yb)):
        yb_ref[j] = slab


def _s5_param_specs(l):
    return [_layer_spec(l, (SUBLANES, S5_W)), _layer_spec(l, (SUBLANES, C_WIDTH)),
            _layer_spec(l, (C_WIDTH, 2 * S5_W)), _layer_spec(l, (2 * S5_W, C_WIDTH)),
            _layer_spec(l, (C_WIDTH, C_WIDTH))]


def _s5_scratch():
    return [pltpu.VMEM((SUBLANES, S5_W), F32), pltpu.VMEM((C_WIDTH, 2 * S5_W), BF16)]


def _s5_param_args(wts):
    return (wts["c_lam"], wts["c_vec"], wts["c_bbd"], wts["c_cbd"], wts["c_wglu"])


def _scan_tm(cols_b, cols_c, nb, seq, wts, l):
    rows = SCAN_STEPS * nb
    hist = (CONV_WIDTH - 1) * nb
    slab = lambda t: (0, t, 0)
    fixed = lambda t: (0, 0)
    nb_slabs, nc_slabs = B_WIDTH // LANES, C_WIDTH // LANES
    return pl.pallas_call(
        functools.partial(_scan_tm_kernel, nb=nb),
        grid=(seq // SCAN_STEPS,),
        in_specs=[pl.BlockSpec((2 * nb_slabs, rows, LANES), slab), pl.BlockSpec((nc_slabs, rows, LANES), slab)]
                 + _lru_param_specs(l) + _s5_param_specs(l),
        out_specs=[pl.BlockSpec((nb_slabs, rows, LANES), slab), pl.BlockSpec((hist, B_WIDTH), fixed),
                   pl.BlockSpec((nb, B_WIDTH), fixed),
                   pl.BlockSpec((nc_slabs, rows, LANES), slab), pl.BlockSpec((nb, S5_W), fixed),
                   pl.BlockSpec((nb, S5_W), fixed)],
        out_shape=[jax.ShapeDtypeStruct((nb_slabs, nb * seq, LANES), F32),
                   jax.ShapeDtypeStruct((hist, B_WIDTH), F32), jax.ShapeDtypeStruct((nb, B_WIDTH), F32),
                   jax.ShapeDtypeStruct((nc_slabs, nb * seq, LANES), F32),
                   jax.ShapeDtypeStruct((nb, S5_W), F32), jax.ShapeDtypeStruct((nb, S5_W), F32)],
        scratch_shapes=[pltpu.VMEM((hist + rows, B_WIDTH), F32), pltpu.VMEM((rows, B_WIDTH), F32),
                        pltpu.VMEM((rows, B_WIDTH), F32), pltpu.VMEM((nb, B_WIDTH), F32),
                        pltpu.VMEM((rows, S5_W), F32), pltpu.VMEM((rows, S5_W), F32),
                        pltpu.VMEM((nb, S5_W), F32), pltpu.VMEM((nb, S5_W), F32)] + _s5_scratch(),
        compiler_params=_params(1),
        name="scan_tm",
    )(cols_b, cols_c, *_lru_param_args(wts), *_s5_param_args(wts))


def _s5_sample_kernel(*refs, steps, n_prev):
    u_ref, h0r_ref, h0i_ref, lam_ref, vec_ref, bbd_ref, cbd_ref, wglu_ref = refs[0:8]
    prev_refs = refs[8:8 + 2 * n_prev]
    y_ref, hr_out_ref, hi_out_ref, abar_scr, bbar_scr = refs[8 + 2 * n_prev:]
    rows = u_ref.shape[0]
    nseq = rows // steps

    @pl.when(pl.program_id(0) == 0)
    def _():
        _s5_discretise(lam_ref[...], bbd_ref, bbar_scr, abar_scr)

    tloc = _row_iota((rows, S5_W)) & (steps - 1)
    h0r = jnp.broadcast_to(h0r_ref[...], (nseq, steps, S5_W)).reshape(rows, S5_W)
    h0i = jnp.broadcast_to(h0i_ref[...], (nseq, steps, S5_W)).reshape(rows, S5_W)
    y, hr, hi = _s5_body(u_ref[...], h0r, h0i, tloc, steps, abar_scr, bbar_scr, vec_ref[...],
                         cbd_ref[...], wglu_ref[...])
    y_ref[...] = y
    _stack_previous(prev_refs[0::2], hr_out_ref)
    _stack_previous(prev_refs[1::2], hi_out_ref)
    hr_out_ref[n_prev] = _last_steps(hr, nseq, steps, 1)
    hi_out_ref[n_prev] = _last_steps(hi, nseq, steps, 1)


def _s5_sample(cols_c, h0r, h0i, prevs, nb, steps, wts, l):
    bb = SAMPLE_SEQS
    rows = bb * steps
    n_prev = len(prevs)
    blk = lambda i: (i, 0)
    seq3 = lambda i: (0, i, 0, 0)
    st = pl.BlockSpec((None, bb, 1, S5_W), lambda i: (l, i, 0, 0))
    prev_specs, prev_args = [], []
    for pr, pi in prevs:
        prev_specs += [pl.BlockSpec((None, bb, 1, S5_W), seq3)] * 2
        prev_args += [pr, pi]
    stacked = pl.BlockSpec((n_prev + 1, bb, 1, S5_W), seq3)
    return pl.pallas_call(
        functools.partial(_s5_sample_kernel, steps=steps, n_prev=n_prev),
        grid=(nb // bb,),
        in_specs=[pl.BlockSpec((rows, C_WIDTH), blk), st, st] + _s5_param_specs(l) + prev_specs,
        out_specs=[pl.BlockSpec((rows, C_WIDTH), blk), stacked, stacked],
        out_shape=[jax.ShapeDtypeStruct((nb * steps, C_WIDTH), F32),
                   jax.ShapeDtypeStruct((n_prev + 1, nb, 1, S5_W), F32),
                   jax.ShapeDtypeStruct((n_prev + 1, nb, 1, S5_W), F32)],
        scratch_shapes=_s5_scratch(),
        compiler_params=_params(1),
        name="s5_sample",
    )(cols_c, h0r, h0i, *_s5_param_args(wts), *prev_args)


def _rows8(rows, width):
    m = jnp.stack([r.reshape(r.shape[0], width) for r in rows], axis=1)
    return jnp.pad(m, ((0, 0), (0, SUBLANES - m.shape[1]), (0, 0)))


def _stacked_weights(w):
    depth = w["g_mix"].shape[0]
    eye_b = jnp.eye(B_BLOCKS, dtype=F32)
    eye_g = jnp.eye(S5_GROUPS, dtype=F32)
    bd4 = lambda m: jnp.einsum("lnde,nm->lndme", m, eye_b).reshape(depth, B_WIDTH, B_WIDTH)
    zeros_lora = jnp.zeros((depth, LANES - 64, A_WIDTH), F32)
    b_in = lambda m: jnp.einsum("lgpc,gh->lgchp", m, eye_g).reshape(depth, C_WIDTH, S5_W)
    c_out = lambda m: jnp.einsum("lgcp,gh->lgphc", m, eye_g).reshape(depth, S5_W, C_WIDTH)
    g_final = jnp.broadcast_to(w["g_final"][None], (depth, D_MODEL))
    return dict(
        g_mix=w["g_mix"][:, None, :],
        w_in=_bf(w["w_in"]),
        mu=w["mu_a"][:, None, :],
        a_vec=_rows8([w["w0"], w["a0"], w["k_k"], w["k_a"], w["r_k"], w["lnx_w"], w["lnx_b"]], A_WIDTH),
        wdec=_bf(jnp.concatenate([w["w_dec2"], zeros_lora], axis=1)),
        wa=_bf(jnp.concatenate([zeros_lora, w["w_a2"]], axis=1)),
        wg=_bf(w["w_g2"]),
        b_vec=_rows8([w["conv_w"][:, 0], w["conv_w"][:, 1], w["conv_w"][:, 2], w["conv_w"][:, 3],
                      w["conv_b"], w["lru_lambda"], w["g_out_b"]], B_WIDTH),
        b_bias=jnp.concatenate([w["b_rg"], w["b_ig"]], axis=1)[:, None, :],
        b_wgates=_bf(jnp.concatenate([bd4(w["w_rg"]), bd4(w["w_ig"])], axis=2)),
        c_lam=_rows8([w["s5_lam_re"], w["s5_lam_im"], jnp.repeat(w["s5_log_dt"], S5_STATE, axis=1)], S5_W),
        c_vec=_rows8([w["s5_d"], w["b_glu"], w["g_out_c"]], C_WIDTH),
        c_bbd=jnp.concatenate([b_in(w["s5_b_re"]), b_in(w["s5_b_im"])], axis=2),
        c_cbd=_bf(jnp.concatenate([c_out(w["s5_c_re"]), -c_out(w["s5_c_im"])], axis=1)),
        c_wglu=_bf(w["w_glu"]),
        w_out=_bf(w["w_out"]),
        p_vec=_rows8([w["g_ffn"], w["g_ple"], g_final], D_MODEL),
        w_up=_bf(w["w_ffn_up"]),
        w_down=_bf(w["w_ffn_down"]),
        w_ple=_bf(w["w_ple"]),
        w_gate=_bf(w["w_ple_gate"]),
    )


def _run_prompt(x, p, wts, depth):
    nb, seq, _ = x.shape
    h = x.reshape(nb * seq, D_MODEL)
    p = p.reshape(depth, nb * seq, PLE_DIM)
    outs = []
    for l in range(depth):
        xs_a, last, cols_b, cols_c = _proj_in_tm(h, wts, l, nb, seq)
        ya, wkv = _rwkv_prompt(xs_a, nb, seq, wts, l)
        yb, conv, lru, yc, s5r, s5i = _scan_tm(cols_b, cols_c, nb, seq, wts, l)
        h = _post(h, ya, yb, yc, p, wts, l, l == depth - 1, nb_tm=nb)
        outs.append((last[:, SUBLANES - 1], wkv, conv, lru, s5r, s5i))
    shift, wkv, conv, lru, s5r, s5i = (jnp.stack([o[j] for o in outs], axis=0) for j in range(6))
    conv = jnp.swapaxes(conv.reshape(depth, CONV_WIDTH - 1, nb, B_WIDTH), 1, 2)
    s5_shape = (depth, nb, S5_GROUPS, S5_STATE)
    return h.reshape(nb, seq, D_MODEL), (shift, wkv, conv, lru, s5r.reshape(s5_shape), s5i.reshape(s5_shape))


def _run_sample(x, p, states, wts, depth):
    nb, steps, _ = x.shape
    st_shift, st_wkv, st_conv, st_lru, st_s5r, st_s5i = states
    h = x.reshape(nb * steps, D_MODEL)
    p = p.reshape(depth, nb * steps, PLE_DIM)
    shift0 = st_shift[:, :, None, :]
    conv0_rows = jnp.pad(st_conv, ((0, 0), (0, 0), (0, steps - (CONV_WIDTH - 1)), (0, 0))
                         ).reshape(depth, nb * steps, B_WIDTH)
    lru0 = st_lru[:, :, None, :]
    s5r0 = st_s5r.reshape(depth, nb, 1, S5_W)
    s5i0 = st_s5i.reshape(depth, nb, 1, S5_W)
    prev_a, prev_b, prev_c = [], [], []
    for l in range(depth):
        last = l == depth - 1
        cols_a, cols_b, cols_c = _proj_in(h, wts["g_mix"], wts["w_in"], l)
        ya, shift, wkv = _rwkv_sample(cols_a, shift0, st_wkv, prev_a if last else [], nb, steps, wts, l)
        yb, conv, lru = _lru_sample(cols_b, conv0_rows, lru0, prev_b if last else [], nb, steps, wts, l)
        yc, s5r, s5i = _s5_sample(cols_c, s5r0, s5i0, prev_c if last else [], nb, steps, wts, l)
        h = _post(h, ya, yb, yc, p, wts, l, last)
        prev_a.append((shift, wkv))
        prev_b.append((conv, lru))
        prev_c.append((s5r, s5i))
    s5_shape = (depth, nb, S5_GROUPS, S5_STATE)
    new_states = (shift.reshape(depth, nb, A_COLS), wkv, conv, lru.reshape(depth, nb, B_WIDTH),
                  s5r.reshape(s5_shape), s5i.reshape(s5_shape))
    return h.reshape(nb, steps, D_MODEL), new_states


def kernel(x_prompt, x_sample, p_prompt, p_sample, state_shift, state_wkv, state_conv, state_lru, state_s5_re, state_s5_im, g_mix, w_in, mu_a, w0, w_dec2, a0, w_a2, w_g2, k_k, k_a, r_k, lnx_w, lnx_b, conv_w, conv_b, w_rg, b_rg, w_ig, b_ig, lru_lambda, g_out_b, s5_lam_re, s5_lam_im, s5_log_dt, s5_b_re, s5_b_im, s5_c_re, s5_c_im, s5_d, w_glu, b_glu, g_out_c, w_out, g_ffn, w_ffn_up, w_ffn_down, g_ple, w_ple, w_ple_gate, g_final):
    w = dict(g_mix=g_mix, w_in=w_in, mu_a=mu_a, w0=w0, w_dec2=w_dec2, a0=a0, w_a2=w_a2, w_g2=w_g2, k_k=k_k,
             k_a=k_a, r_k=r_k, lnx_w=lnx_w, lnx_b=lnx_b, conv_w=conv_w, conv_b=conv_b, w_rg=w_rg, b_rg=b_rg,
             w_ig=w_ig, b_ig=b_ig, lru_lambda=lru_lambda, g_out_b=g_out_b, s5_lam_re=s5_lam_re,
             s5_lam_im=s5_lam_im, s5_log_dt=s5_log_dt, s5_b_re=s5_b_re, s5_b_im=s5_b_im, s5_c_re=s5_c_re,
             s5_c_im=s5_c_im, s5_d=s5_d, w_glu=w_glu, b_glu=b_glu, g_out_c=g_out_c, w_out=w_out, g_ffn=g_ffn,
             w_ffn_up=w_ffn_up, w_ffn_down=w_ffn_down, g_ple=g_ple, w_ple=w_ple, w_ple_gate=w_ple_gate,
             g_final=g_final)
    depth = g_mix.shape[0]
    wts = _stacked_weights(w)
    y_prompt, new_p = _run_prompt(x_prompt, p_prompt, wts, depth)
    y_sample, new_s = _run_sample(x_sample, p_sample,
                                  (state_shift, state_wkv, state_conv, state_lru, state_s5_re, state_s5_im),
                                  wts, depth)
    return (y_prompt, y_sample) + new_p + new_s
```

```python
import functools
import math

import jax
import jax.numpy as jnp
from jax import lax
from jax.experimental import pallas as pl
from jax.experimental.pallas import tpu as pltpu

F32 = jnp.float32
BF16 = jnp.bfloat16

D_MODEL = 1024
A_WIDTH = 512
A_HEADS = 8
A_HEAD_DIM = 64
A_COLS = 1792
B_WIDTH = 256
B_BLOCKS = 4
CONV_WIDTH = 4
C_WIDTH = 256
S5_GROUPS = 16
S5_STATE = 64
S5_W = S5_GROUPS * S5_STATE
IN_COLS = A_COLS + 2 * B_WIDTH + C_WIDTH
D_FF = 2816
PLE_DIM = 256
LRU_C = 8.0
RMS_EPS = 1e-6
GN_EPS = 64e-5

LANES = 128
SUBLANES = 8
VMEM_LIMIT_BYTES = 56 * 1024 * 1024

TOKEN_TILE = 512
FFN_CHUNK = 256
RWKV_CHUNK = 64
RWKV_BLOCK = 256
RWKV_SEQS = 2
SCAN_STEPS = 256
SAMPLE_SEQS = 16


def _layer_spec(l, shape):
    nd = len(shape)
    return pl.BlockSpec((None,) + tuple(shape), lambda *_: (l,) + (0,) * nd, pipeline_mode=pl.Buffered(1))


def _params(n_axes):
    return pltpu.CompilerParams(dimension_semantics=("arbitrary",) * n_axes,
                                vmem_limit_bytes=VMEM_LIMIT_BYTES)


def _dot(a, b):
    return jnp.dot(a, b, preferred_element_type=F32)


def _dot_nt(a, b):
    return lax.dot_general(a, b, (((1,), (1,)), ((), ())), preferred_element_type=F32)


def _dot_tn(a, b):
    return lax.dot_general(a, b, (((0,), (0,)), ((), ())), preferred_element_type=F32)


def _bf(x):
    return x.astype(BF16)


def _rms(x, g):
    inv = lax.rsqrt(jnp.mean(x * x, axis=-1, keepdims=True) + RMS_EPS)
    return x * inv * g


def _sigmoid(x):
    return 1.0 / (1.0 + jnp.exp(-x))


def _softplus(x):
    return jnp.maximum(x, 0.0) + jnp.log(1.0 + jnp.exp(-jnp.abs(x)))


def _gelu_tanh(x):
    c = math.sqrt(2.0 / math.pi)
    return 0.5 * x * (1.0 + jnp.tanh(c * (x + 0.044715 * (x * x * x))))


def _row_iota(shape):
    return lax.broadcasted_iota(jnp.int32, shape, 0)


def _shift_rows(x, d):
    return pltpu.roll(x, d, axis=0)


def _lane_slabs(x):
    return [x[:, j * LANES:(j + 1) * LANES] for j in range(x.shape[1] // LANES)]


def _last_steps(x, nseq, steps, n):
    return x.reshape(nseq, steps, x.shape[1])[:, steps - n:steps, :]


def _stack_previous(prev_refs, out_ref):
    for j, ref in enumerate(prev_refs):
        out_ref[j] = ref[...]


def _proj_in_kernel(h_ref, g_ref, w_ref, a_ref, b_ref, c_ref):
    xn = _bf(_rms(h_ref[...], g_ref[...]))
    a_ref[...] = _dot(xn, w_ref[:, 0:A_COLS])
    b_ref[...] = _dot(xn, w_ref[:, A_COLS:A_COLS + 2 * B_WIDTH])
    c_ref[...] = _dot(xn, w_ref[:, A_COLS + 2 * B_WIDTH:IN_COLS])


def _proj_in(h, g, w_bf, l):
    n = h.shape[0]
    tm = TOKEN_TILE
    row = lambda i: (i, 0)
    return pl.pallas_call(
        _proj_in_kernel,
        grid=(n // tm,),
        in_specs=[pl.BlockSpec((tm, D_MODEL), row), _layer_spec(l, (1, D_MODEL)),
                  _layer_spec(l, (D_MODEL, IN_COLS))],
        out_specs=[pl.BlockSpec((tm, A_COLS), row), pl.BlockSpec((tm, 2 * B_WIDTH), row),
                   pl.BlockSpec((tm, C_WIDTH), row)],
        out_shape=[jax.ShapeDtypeStruct((n, A_COLS), F32), jax.ShapeDtypeStruct((n, 2 * B_WIDTH), F32),
                   jax.ShapeDtypeStruct((n, C_WIDTH), F32)],
        compiler_params=_params(1),
        name="proj_in",
    )(h, g, w_bf)


def _proj_in_tm_kernel(h_ref, g_ref, w_ref, mu_ref, a_ref, last_ref, b_ref, c_ref, prev_scr, *, nb):
    b = pl.program_id(1)
    tm = h_ref.shape[0]

    @pl.when((pl.program_id(0) == 0) & (b == 0))
    def _():
        prev_scr[...] = jnp.zeros_like(prev_scr)

    xn = _bf(_rms(h_ref[...], g_ref[...]))
    cols = _dot(xn, w_ref[:, 0:A_COLS])
    prev = jnp.where(_row_iota(cols.shape) == 0, prev_scr[b, SUBLANES - 1:SUBLANES, :], _shift_rows(cols, 1))
    a_ref[...] = cols + (prev - cols) * mu_ref[...]
    tail = cols[tm - SUBLANES:tm, :]
    prev_scr[b] = tail
    last_ref[b] = tail
    rows_of_b = pl.ds(b, tm, stride=nb)
    for j, slab in enumerate(_lane_slabs(_dot(xn, w_ref[:, A_COLS:A_COLS + 2 * B_WIDTH]))):
        b_ref[j, rows_of_b, :] = slab
    for j, slab in enumerate(_lane_slabs(_dot(xn, w_ref[:, A_COLS + 2 * B_WIDTH:IN_COLS]))):
        c_ref[j, rows_of_b, :] = slab


def _proj_in_tm(h, wts, l, nb, seq):
    tm = TOKEN_TILE
    nt = seq // tm
    row = lambda i, b: (b * nt + i, 0)
    slab = lambda i, b: (0, i, 0)
    nb_slabs, nc_slabs = 2 * B_WIDTH // LANES, C_WIDTH // LANES
    return pl.pallas_call(
        functools.partial(_proj_in_tm_kernel, nb=nb),
        grid=(nt, nb),
        in_specs=[pl.BlockSpec((tm, D_MODEL), row), _layer_spec(l, (1, D_MODEL)),
                  _layer_spec(l, (D_MODEL, IN_COLS)), _layer_spec(l, (1, A_COLS))],
        out_specs=[pl.BlockSpec((tm, A_COLS), row), pl.BlockSpec((nb, SUBLANES, A_COLS), lambda i, b: (0, 0, 0)),
                   pl.BlockSpec((nb_slabs, tm * nb, LANES), slab),
                   pl.BlockSpec((nc_slabs, tm * nb, LANES), slab)],
        out_shape=[jax.ShapeDtypeStruct((nb * seq, A_COLS), F32),
                   jax.ShapeDtypeStruct((nb, SUBLANES, A_COLS), F32),
                   jax.ShapeDtypeStruct((nb_slabs, nb * seq, LANES), F32),
                   jax.ShapeDtypeStruct((nc_slabs, nb * seq, LANES), F32)],
        scratch_shapes=[pltpu.VMEM((nb, SUBLANES, A_COLS), F32)],
        compiler_params=_params(2),
        name="proj_in_tm",
    )(h, wts["g_mix"], wts["w_in"], wts["mu"])


def _post_kernel(h_ref, ya_ref, yb_ref, yc_ref, p_ref, wo_ref, vec_ref, wup_ref, wdn_ref, wple_ref,
                 wgate_ref, o_ref, act_scr, *, final, nb_tm):
    g_ffn = vec_ref[0:1, :]
    g_ple = vec_ref[1:2, :]
    g_final = vec_ref[2:3, :]
    if nb_tm:
        rows_of_b = pl.ds(pl.program_id(1), h_ref.shape[0], stride=nb_tm)
        yb = jnp.concatenate([yb_ref[j, rows_of_b, :] for j in range(B_WIDTH // LANES)], axis=1)
        yc = jnp.concatenate([yc_ref[j, rows_of_b, :] for j in range(C_WIDTH // LANES)], axis=1)
    else:
        yb, yc = yb_ref[...], yc_ref[...]
    h1 = (h_ref[...] + _dot(_bf(ya_ref[...]), wo_ref[0:A_WIDTH, :])
          + _dot(_bf(yb), wo_ref[A_WIDTH:A_WIDTH + B_WIDTH, :])
          + _dot(_bf(yc), wo_ref[A_WIDTH + B_WIDTH:D_MODEL, :]))
    xf = _bf(_rms(h1, g_ffn))
    for c0 in range(0, D_FF, FFN_CHUNK):
        gate = _dot(xf, wup_ref[:, c0:c0 + FFN_CHUNK])
        up = _dot(xf, wup_ref[:, D_FF + c0:D_FF + c0 + FFN_CHUNK])
        act_scr[:, c0:c0 + FFN_CHUNK] = _bf(gate * _sigmoid(gate) * up)
    h2 = h1 + _dot(act_scr[...], wdn_ref[...])
    ple = _dot(_bf(p_ref[...]), wple_ref[...])
    gate = _sigmoid(_dot(_bf(_rms(h2, g_ple)), wgate_ref[...]))
    h3 = h2 + ple * gate
    if final:
        h3 = _rms(h3, g_final)
    o_ref[...] = h3


def _post(h, ya, yb, yc, p, wts, l, final, nb_tm=0):
    n = h.shape[0]
    tm = TOKEN_TILE
    if nb_tm:
        nt = n // (nb_tm * tm)
        grid = (nt, nb_tm)
        row = lambda i, b: (b * nt + i, 0)
        prow = lambda i, b: (l, b * nt + i, 0)
        slab = lambda i, b: (0, i, 0)
        yb_spec = pl.BlockSpec((B_WIDTH // LANES, tm * nb_tm, LANES), slab, pipeline_mode=pl.Buffered(1))
        yc_spec = pl.BlockSpec((C_WIDTH // LANES, tm * nb_tm, LANES), slab, pipeline_mode=pl.Buffered(1))
    else:
        grid = (n // tm,)
        row = lambda i: (i, 0)
        prow = lambda i: (l, i, 0)
        yb_spec = pl.BlockSpec((tm, B_WIDTH), row)
        yc_spec = pl.BlockSpec((tm, C_WIDTH), row)
    return pl.pallas_call(
        functools.partial(_post_kernel, final=final, nb_tm=nb_tm),
        grid=grid,
        in_specs=[pl.BlockSpec((tm, D_MODEL), row), pl.BlockSpec((tm, A_WIDTH), row),
                  yb_spec, yc_spec,
                  pl.BlockSpec((None, tm, PLE_DIM), prow),
                  _layer_spec(l, (D_MODEL, D_MODEL)), _layer_spec(l, (SUBLANES, D_MODEL)),
                  _layer_spec(l, (D_MODEL, 2 * D_FF)), _layer_spec(l, (D_FF, D_MODEL)),
                  _layer_spec(l, (PLE_DIM, D_MODEL)), _layer_spec(l, (D_MODEL, D_MODEL))],
        out_specs=pl.BlockSpec((tm, D_MODEL), row),
        out_shape=jax.ShapeDtypeStruct((n, D_MODEL), F32),
        scratch_shapes=[pltpu.VMEM((tm, D_FF), BF16)],
        compiler_params=_params(len(grid)),
        name="post",
    )(h, ya, yb, yc, p, wts["w_out"], wts["p_vec"], wts["w_up"], wts["w_down"], wts["w_ple"], wts["w_gate"])


def _head_sum(x):
    first = lax.broadcasted_iota(jnp.int32, (x.shape[0], LANES), 1) < A_HEAD_DIM
    outs = []
    for xp in _lane_slabs(x):
        s0 = jnp.sum(jnp.where(first, xp, 0.0), axis=-1, keepdims=True)
        s1 = jnp.sum(jnp.where(first, 0.0, xp), axis=-1, keepdims=True)
        outs.append(jnp.where(first, s0, s1))
    return jnp.concatenate(outs, axis=1)


def _rwkv_prep(xs, vec, wdec, wa, wg, chunk):
    rows = xs.shape[0]
    r = xs[:, 0:A_WIDTH]
    k = xs[:, A_WIDTH:2 * A_WIDTH]
    v = xs[:, 2 * A_WIDTH:3 * A_WIDTH]
    xwa = xs[:, 3 * A_WIDTH:3 * A_WIDTH + LANES]
    xg = xs[:, 3 * A_WIDTH + LANES:A_COLS]
    w0, a0, k_k, k_a, r_k = (vec[i:i + 1, :] for i in range(5))
    z = w0 + _dot(_bf(jnp.tanh(xwa)), wdec)
    log_decay = -math.exp(-0.5) * _sigmoid(z)
    a = _sigmoid(a0 + _dot(_bf(xwa), wa))
    g = _dot(_bf(_sigmoid(xg)), wg)
    kk_raw = k * k_k
    kk = kk_raw * lax.rsqrt(jnp.maximum(_head_sum(kk_raw * kk_raw), 1e-24))
    k_mod = k * (1.0 + (a - 1.0) * k_a)
    bonus = _head_sum(r * k_mod * r_k) * v
    ri = _row_iota((rows, rows))
    ci = lax.broadcasted_iota(jnp.int32, (rows, rows), 1)
    same_chunk = (ci & (-chunk)) == (ri & (-chunk))
    tri = _bf(jnp.where((ci <= ri) & same_chunk, 1.0, 0.0))
    h1 = _bf(log_decay)
    h2 = _bf(log_decay - h1.astype(F32))
    cum = _dot(tri, h1) + _dot(tri, h2)
    gam = jnp.exp(cum)
    ginv = jnp.exp(-cum)
    gprev = jnp.exp(cum - log_decay)
    return dict(rt=r * gam, kap=kk * gprev, bet=kk * a * ginv, kt=k_mod * ginv, v=v, gam=gam,
                bonus=bonus, g=g)


def _stack_heads(x):
    first = lax.broadcasted_iota(jnp.int32, x.shape, 1) < A_HEAD_DIM
    return _bf(jnp.concatenate([jnp.where(first, x, 0.0), jnp.where(first, 0.0, x)], axis=0))


def _rwkv_local(ops):
    c = ops[0]["kap"].shape[0]
    c2 = 2 * c
    ri = _row_iota((c, c2))
    ci = lax.broadcasted_iota(jnp.int32, (c, c2), 1) & (c - 1)
    strict = ci < ri
    incl = ci <= ri
    eye = jnp.where(ri == ci, 1.0, 0.0)
    same_head = (_row_iota((c2, c2)) & c) == (lax.broadcasted_iota(jnp.int32, (c2, c2), 1) & c)
    merged = c2 % LANES == 0

    def blockdiag(m):
        return _bf(jnp.where(same_head, jnp.concatenate([m, m], axis=0), 0.0))

    st = []
    for o in ops:
        s = {k + "2": _stack_heads(o[k]) for k in ("kap", "bet", "kt", "v")}
        s["rt"] = o["rt"]
        lhs = _bf(jnp.concatenate([o["kap"], o["rt"]], axis=0))
        if merged:
            g = _dot_nt(lhs, jnp.concatenate([s["bet2"], s["kt2"]], axis=0))
            a_b, a_k, a_rb, a_rk = g[0:c, 0:c2], g[0:c, c2:2 * c2], g[c:c2, 0:c2], g[c:c2, c2:2 * c2]
        else:
            gb, gk = _dot_nt(lhs, s["bet2"]), _dot_nt(lhs, s["kt2"])
            a_b, a_k, a_rb, a_rk = gb[0:c], gk[0:c], gb[c:c2], gk[c:c2]
        s["a_b"] = jnp.where(strict, a_b, 0.0)
        s["a_k"] = _bf(jnp.where(strict, a_k, 0.0))
        s["a_rb"] = _bf(jnp.where(incl, a_rb, 0.0))
        s["a_rk"] = _bf(jnp.where(incl, a_rk, 0.0))
        st.append(s)
    for s in st:
        s["t"] = eye - s["a_b"]
        s["lp"] = s["a_b"]
        s["akv"] = _dot(s["a_k"], s["v2"])
    n = 2
    while n < c:
        for s in st:
            s["lp"] = _dot(_bf(s["lp"]), blockdiag(s["lp"]))
        for s in st:
            s["t"] = s["t"] + _dot(_bf(s["t"]), blockdiag(s["lp"]))
        n *= 2
    for s in st:
        wu = _dot(_bf(s["t"]), jnp.concatenate([s["kap2"], _stack_heads(s["akv"])], axis=1))
        s["w"] = -wu[:, 0:LANES]
        s["u0"] = -wu[:, LANES:2 * LANES]
    return st


def _rwkv_state(st, states, g_ends):
    c = st[0]["rt"].shape[0]
    merged = (2 * c) % LANES == 0
    xs = []
    for s, state in zip(st, states):
        xs.append(_dot_nt(_bf(jnp.concatenate([s["w"], s["rt"]], axis=0)), _bf(state)))
    u2s = [_stack_heads(x[0:c] + s["u0"]) for x, s in zip(xs, st)]
    ys, new_states = [], []
    for s, x, u2 in zip(st, xs, u2s):
        if merged:
            y = x[c:2 * c] + _dot(jnp.concatenate([s["a_rb"], s["a_rk"]], axis=1),
                                  jnp.concatenate([u2, s["v2"]], axis=0))
        else:
            y = x[c:2 * c] + _dot(s["a_rb"], u2) + _dot(s["a_rk"], s["v2"])
        ys.append(y)
    for s, u2, state, g_end in zip(st, u2s, states, g_ends):
        ds = _dot_tn(jnp.concatenate([u2, s["v2"]], axis=0), jnp.concatenate([s["bet2"], s["kt2"]], axis=0))
        new_states.append((state + ds) * g_end)
    return ys, new_states


def _rwkv_finish(y, bonus, g, vec):
    lnx_w = vec[5:6, :]
    lnx_b = vec[6:7, :]
    inv_n = 1.0 / A_HEAD_DIM
    mu = _head_sum(y) * inv_n
    d = y - mu
    var = _head_sum(d * d) * inv_n
    yn = d * lax.rsqrt(var + GN_EPS) * lnx_w + lnx_b
    return (yn + bonus) * g


def _pair_state(s_ref, idx, p):
    z = jnp.zeros((A_HEAD_DIM, A_HEAD_DIM), F32)
    top = jnp.concatenate([s_ref[idx, 2 * p], z], axis=1)
    bot = jnp.concatenate([z, s_ref[idx, 2 * p + 1]], axis=1)
    return jnp.concatenate([top, bot], axis=0)


def _rwkv_prompt_kernel(xs_ref, vec_ref, wdec_ref, wa_ref, wg_ref, y_ref, wkv_ref, s_scr):
    t = pl.program_id(1)
    nseq, rows = xs_ref.shape[0], xs_ref.shape[1]
    chunk = RWKV_CHUNK

    @pl.when(t == 0)
    def _():
        s_scr[...] = jnp.zeros_like(s_scr)

    qs = [_rwkv_prep(xs_ref[s], vec_ref[...], wdec_ref[...], wa_ref[...], wg_ref[...], chunk)
          for s in range(nseq)]
    n_pairs = A_HEADS // 2
    lanes = [slice(p * LANES, (p + 1) * LANES) for p in range(n_pairs)]
    starts = list(range(0, rows, chunk))
    ops = [{k: q[k][c0:c0 + chunk, ln] for k in ("kap", "rt", "bet", "kt", "v")}
           for c0 in starts for q in qs for ln in lanes]
    st = _rwkv_local(ops)
    per_chunk = nseq * n_pairs
    states = [s_scr[s, p] for s in range(nseq) for p in range(n_pairs)]
    y_rows = [[] for _ in range(nseq)]
    for i, c0 in enumerate(starts):
        g_ends = [q["gam"][c0 + chunk - 1:c0 + chunk, ln] for q in qs for ln in lanes]
        ys, states = _rwkv_state(st[i * per_chunk:(i + 1) * per_chunk], states, g_ends)
        for s in range(nseq):
            y_rows[s].append(jnp.concatenate(ys[s * n_pairs:(s + 1) * n_pairs], axis=1))
    for s in range(nseq):
        for p in range(n_pairs):
            state = states[s * n_pairs + p]
            s_scr[s, p] = state
            wkv_ref[s, 2 * p] = state[0:A_HEAD_DIM, 0:A_HEAD_DIM]
            wkv_ref[s, 2 * p + 1] = state[A_HEAD_DIM:LANES, A_HEAD_DIM:LANES]
        y = jnp.concatenate(y_rows[s], axis=0) if len(y_rows[s]) > 1 else y_rows[s][0]
        y_ref[s] = _rwkv_finish(y, qs[s]["bonus"], qs[s]["g"], vec_ref[...])


def _rwkv_param_specs(l):
    return [_layer_spec(l, (SUBLANES, A_WIDTH)), _layer_spec(l, (LANES, A_WIDTH)),
            _layer_spec(l, (LANES, A_WIDTH)), _layer_spec(l, (LANES, A_WIDTH))]


def _rwkv_param_args(wts):
    return (wts["a_vec"], wts["wdec"], wts["wa"], wts["wg"])


def _rwkv_prompt(xs_a, nb, seq, wts, l):
    tb = RWKV_BLOCK
    ns = RWKV_SEQS
    blk = lambda b, t: (b, t, 0)
    return pl.pallas_call(
        _rwkv_prompt_kernel,
        grid=(nb // ns, seq // tb),
        in_specs=[pl.BlockSpec((ns, tb, A_COLS), blk)] + _rwkv_param_specs(l),
        out_specs=[pl.BlockSpec((ns, tb, A_WIDTH), blk),
                   pl.BlockSpec((ns, A_HEADS, A_HEAD_DIM, A_HEAD_DIM), lambda b, t: (b, 0, 0, 0))],
        out_shape=[jax.ShapeDtypeStruct((nb, seq, A_WIDTH), F32),
                   jax.ShapeDtypeStruct((nb, A_HEADS, A_HEAD_DIM, A_HEAD_DIM), F32)],
        scratch_shapes=[pltpu.VMEM((ns, A_HEADS // 2, LANES, LANES), F32)],
        compiler_params=_params(2),
        name="rwkv_prompt",
    )(xs_a, *_rwkv_param_args(wts))


def _rwkv_sample_kernel(*refs, steps, n_prev):
    cols_ref, shift_ref, wkv0_ref, mu_ref, vec_ref, wdec_ref, wa_ref, wg_ref = refs[0:8]
    prev_refs = refs[8:8 + 2 * n_prev]
    y_ref, shift_out_ref, wkv_out_ref = refs[8 + 2 * n_prev:]
    rows = cols_ref.shape[0]
    nseq = rows // steps
    cols = cols_ref[...]
    tloc = _row_iota(cols.shape) & (steps - 1)
    shift0 = jnp.broadcast_to(shift_ref[...], (nseq, steps, A_COLS)).reshape(rows, A_COLS)
    prev = jnp.where(tloc == 0, shift0, _shift_rows(cols, 1))
    xs = cols + (prev - cols) * mu_ref[...]
    q = _rwkv_prep(xs, vec_ref[...], wdec_ref[...], wa_ref[...], wg_ref[...], steps)
    n_pairs = A_HEADS // 2
    lanes = [slice(p * LANES, (p + 1) * LANES) for p in range(n_pairs)]
    chains = [(s, p) for s in range(nseq) for p in range(n_pairs)]
    ops = [{k: q[k][s * steps:(s + 1) * steps, lanes[p]] for k in ("kap", "rt", "bet", "kt", "v")}
           for s, p in chains]
    states = [_pair_state(wkv0_ref, s, p) for s, p in chains]
    g_ends = [q["gam"][(s + 1) * steps - 1:(s + 1) * steps, lanes[p]] for s, p in chains]
    ys, new_states = _rwkv_state(_rwkv_local(ops), states, g_ends)
    _stack_previous(prev_refs[0::2], shift_out_ref)
    _stack_previous(prev_refs[1::2], wkv_out_ref)
    for (s, p), s_new in zip(chains, new_states):
        wkv_out_ref[n_prev, s, 2 * p] = s_new[0:A_HEAD_DIM, 0:A_HEAD_DIM]
        wkv_out_ref[n_prev, s, 2 * p + 1] = s_new[A_HEAD_DIM:LANES, A_HEAD_DIM:LANES]
    shift_out_ref[n_prev] = _last_steps(cols, nseq, steps, 1)
    y = jnp.concatenate([jnp.concatenate(ys[s * n_pairs:(s + 1) * n_pairs], axis=1) for s in range(nseq)],
                        axis=0)
    y_ref[...] = _rwkv_finish(y, q["bonus"], q["g"], vec_ref[...])


def _rwkv_sample(cols_a, shift0, wkv0, prevs, nb, steps, wts, l):
    bb = SAMPLE_SEQS
    rows = bb * steps
    n_prev = len(prevs)
    blk = lambda i: (i, 0)
    seq3 = lambda i: (0, i, 0, 0)
    seq4 = lambda i: (0, i, 0, 0, 0)
    prev_specs, prev_args = [], []
    for sh, wk in prevs:
        prev_specs += [pl.BlockSpec((None, bb, 1, A_COLS), seq3),
                       pl.BlockSpec((None, bb, A_HEADS, A_HEAD_DIM, A_HEAD_DIM), seq4)]
        prev_args += [sh, wk]
    return pl.pallas_call(
        functools.partial(_rwkv_sample_kernel, steps=steps, n_prev=n_prev),
        grid=(nb // bb,),
        in_specs=[pl.BlockSpec((rows, A_COLS), blk),
                  pl.BlockSpec((None, bb, 1, A_COLS), lambda i: (l, i, 0, 0)),
                  pl.BlockSpec((None, bb, A_HEADS, A_HEAD_DIM, A_HEAD_DIM), lambda i: (l, i, 0, 0, 0)),
                  _layer_spec(l, (1, A_COLS))]
                 + _rwkv_param_specs(l) + prev_specs,
        out_specs=[pl.BlockSpec((rows, A_WIDTH), blk),
                   pl.BlockSpec((n_prev + 1, bb, 1, A_COLS), seq3),
                   pl.BlockSpec((n_prev + 1, bb, A_HEADS, A_HEAD_DIM, A_HEAD_DIM), seq4)],
        out_shape=[jax.ShapeDtypeStruct((nb * steps, A_WIDTH), F32),
                   jax.ShapeDtypeStruct((n_prev + 1, nb, 1, A_COLS), F32),
                   jax.ShapeDtypeStruct((n_prev + 1, nb, A_HEADS, A_HEAD_DIM, A_HEAD_DIM), F32)],
        compiler_params=_params(1),
        name="rwkv_sample",
    )(cols_a, shift0, wkv0, wts["mu"], *_rwkv_param_args(wts), *prev_args)


def _lru_gates(xc, vec, bias, wgates):
    lam = vec[5:6, :]
    gates = _sigmoid(_dot(_bf(xc), wgates) + bias)
    gate_r = gates[:, 0:B_WIDTH]
    gate_i = gates[:, B_WIDTH:2 * B_WIDTH]
    log_a = (-LRU_C) * gate_r * _softplus(-lam)
    a = jnp.exp(log_a)
    mult = jnp.sqrt(-jnp.tanh(log_a) * (a * a + 1.0))
    return a, mult * gate_i * xc


def _lru_body(gate_br, xb, conv_prev, h0, tloc, tlen, vec, bias, wgates):
    rows = xb.shape[0]
    conv_b = vec[4:5, :]
    g_out = vec[6:7, :]
    xc = conv_b + vec[3:4, :] * xb
    for j in (1, 2, 3):
        tail = conv_prev if j == 3 else _shift_rows(conv_prev, rows - (3 - j))
        xc = xc + vec[3 - j:4 - j, :] * jnp.where(tloc >= j, _shift_rows(xb, j), tail)
    a, b = _lru_gates(xc, vec, bias, wgates)
    b = b + jnp.where(tloc == 0, a * h0, 0.0)
    d = 1
    while d < tlen:
        keep = tloc >= d
        a_s = jnp.where(keep, _shift_rows(a, d), 1.0)
        b_s = jnp.where(keep, _shift_rows(b, d), 0.0)
        b = a * b_s + b
        a = a * a_s
        d *= 2
    hs = b
    y = hs * _gelu_tanh(gate_br)
    return _rms(y, g_out), hs


def _lru_param_specs(l):
    return [_layer_spec(l, (SUBLANES, B_WIDTH)), _layer_spec(l, (1, 2 * B_WIDTH)),
            _layer_spec(l, (B_WIDTH, 2 * B_WIDTH))]


def _lru_param_args(wts):
    return (wts["b_vec"], wts["b_bias"], wts["b_wgates"])


def _lru_sample_kernel(*refs, steps, n_prev):
    cols_ref, conv0_ref, h0_ref, vec_ref, bias_ref, wg_ref = refs[0:6]
    prev_refs = refs[6:6 + 2 * n_prev]
    y_ref, conv_out_ref, h_out_ref = refs[6 + 2 * n_prev:]
    rows = cols_ref.shape[0]
    nseq = rows // steps
    gate_br = cols_ref[:, 0:B_WIDTH]
    xb = cols_ref[:, B_WIDTH:2 * B_WIDTH]
    tloc = _row_iota(xb.shape) & (steps - 1)
    h0 = jnp.broadcast_to(h0_ref[...], (nseq, steps, B_WIDTH)).reshape(rows, B_WIDTH)
    y, hs = _lru_body(gate_br, xb, conv0_ref[...], h0, tloc, steps, vec_ref[...], bias_ref[...], wg_ref[...])
    y_ref[...] = y
    _stack_previous(prev_refs[0::2], conv_out_ref)
    _stack_previous(prev_refs[1::2], h_out_ref)
    conv_out_ref[n_prev] = _last_steps(xb, nseq, steps, CONV_WIDTH - 1)
    h_out_ref[n_prev] = _last_steps(hs, nseq, steps, 1)


def _lru_sample(cols_b, conv0_rows, h0, prevs, nb, steps, wts, l):
    bb = SAMPLE_SEQS
    rows = bb * steps
    n_prev = len(prevs)
    blk = lambda i: (i, 0)
    seq3 = lambda i: (0, i, 0, 0)
    prev_specs, prev_args = [], []
    for cv, hh in prevs:
        prev_specs += [pl.BlockSpec((None, bb, CONV_WIDTH - 1, B_WIDTH), seq3),
                       pl.BlockSpec((None, bb, 1, B_WIDTH), seq3)]
        prev_args += [cv, hh]
    return pl.pallas_call(
        functools.partial(_lru_sample_kernel, steps=steps, n_prev=n_prev),
        grid=(nb // bb,),
        in_specs=[pl.BlockSpec((rows, 2 * B_WIDTH), blk),
                  pl.BlockSpec((None, rows, B_WIDTH), lambda i: (l, i, 0)),
                  pl.BlockSpec((None, bb, 1, B_WIDTH), lambda i: (l, i, 0, 0))]
                 + _lru_param_specs(l) + prev_specs,
        out_specs=[pl.BlockSpec((rows, B_WIDTH), blk),
                   pl.BlockSpec((n_prev + 1, bb, CONV_WIDTH - 1, B_WIDTH), seq3),
                   pl.BlockSpec((n_prev + 1, bb, 1, B_WIDTH), seq3)],
        out_shape=[jax.ShapeDtypeStruct((nb * steps, B_WIDTH), F32),
                   jax.ShapeDtypeStruct((n_prev + 1, nb, CONV_WIDTH - 1, B_WIDTH), F32),
                   jax.ShapeDtypeStruct((n_prev + 1, nb, 1, B_WIDTH), F32)],
        compiler_params=_params(1),
        name="lru_sample",
    )(cols_b, conv0_rows, h0, *_lru_param_args(wts), *prev_args)


def _s5_discretise(lam, b_ref, bbar_scr, abar_scr):
    lr = lam[0:1, :]
    li = lam[1:2, :]
    dt = jnp.exp(lam[2:3, :])
    mag = jnp.exp(lr * dt)
    ar = mag * jnp.cos(li * dt)
    ai = mag * jnp.sin(li * dt)
    den = lr * lr + li * li
    cr = ((ar - 1.0) * lr + ai * li) / den
    ci = (ai * lr - (ar - 1.0) * li) / den
    b_re = b_ref[:, 0:S5_W]
    b_im = b_ref[:, S5_W:2 * S5_W]
    bbar_scr[:, 0:S5_W] = _bf(cr * b_re - ci * b_im)
    bbar_scr[:, S5_W:2 * S5_W] = _bf(cr * b_im + ci * b_re)
    abar_scr[...] = jnp.concatenate([ar, ai, jnp.zeros((SUBLANES - 2, S5_W), F32)], axis=0)


def _s5_drive(u, bbar_scr):
    ub = _bf(u)
    return _dot(ub, bbar_scr[:, 0:S5_W]), _dot(ub, bbar_scr[:, S5_W:2 * S5_W])


def _s5_readout(hr, hi, u, vec, cbd, wglu):
    s5_d = vec[0:1, :]
    b_glu = vec[1:2, :]
    g_out = vec[2:3, :]
    y = _dot(_bf(hr), cbd[0:S5_W, :]) + _dot(_bf(hi), cbd[S5_W:2 * S5_W, :]) + s5_d * u
    z = _gelu_tanh(y)
    out = z * _sigmoid(_dot(_bf(z), wglu) + b_glu)
    return _rms(out, g_out)


def _s5_body(u, h0r, h0i, tloc, tlen, abar_scr, bbar_scr, vec, cbd, wglu):
    ar, ai = abar_scr[0:1, :], abar_scr[1:2, :]
    hr, hi = _s5_drive(u, bbar_scr)
    first = tloc == 0
    hr = hr + jnp.where(first, ar * h0r - ai * h0i, 0.0)
    hi = hi + jnp.where(first, ar * h0i + ai * h0r, 0.0)
    pr, pi = ar, ai
    d = 1
    while d < tlen:
        keep = tloc >= d
        sr = jnp.where(keep, _shift_rows(hr, d), 0.0)
        si = jnp.where(keep, _shift_rows(hi, d), 0.0)
        hr, hi = hr + pr * sr - pi * si, hi + pr * si + pi * sr
        pr, pi = pr * pr - pi * pi, 2.0 * pr * pi
        d *= 2
    return _s5_readout(hr, hi, u, vec, cbd, wglu), hr, hi


def _scan_tm_kernel(colsb_ref, u_ref, bvec_ref, bias_ref, wg_ref, lam_ref, cvec_ref, bbd_ref, cbd_ref, wglu_ref,
                    yb_ref, conv_ref, h_ref, yc_ref, hr_ref, hi_ref,
                    x_scr, a_scr, b_scr, h_scr, re_scr, im_scr, sr_scr, si_scr, abar_scr, bbar_scr, *, nb):
    rows = a_scr.shape[0]
    hist = (CONV_WIDTH - 1) * nb

    @pl.when(pl.program_id(0) == 0)
    def _():
        x_scr[0:hist, :] = jnp.zeros((hist, B_WIDTH), F32)
        h_scr[...] = jnp.zeros_like(h_scr)
        sr_scr[...] = jnp.zeros_like(sr_scr)
        si_scr[...] = jnp.zeros_like(si_scr)
        _s5_discretise(lam_ref[...], bbd_ref, bbar_scr, abar_scr)

    u = jnp.concatenate([u_ref[j] for j in range(C_WIDTH // LANES)], axis=1)
    dr, di = _s5_drive(u, bbar_scr)
    re_scr[...] = dr
    im_scr[...] = di
    ar = jnp.broadcast_to(abar_scr[0:1, :], (nb, S5_W))
    ai = jnp.broadcast_to(abar_scr[1:2, :], (nb, S5_W))

    bvec = bvec_ref[...]
    gate_br = jnp.concatenate([colsb_ref[0], colsb_ref[1]], axis=1)
    xb = jnp.concatenate([colsb_ref[2], colsb_ref[3]], axis=1)
    x_scr[hist:hist + rows, :] = xb
    xc = bvec[4:5, :] + bvec[3:4, :] * xb
    for j in range(1, CONV_WIDTH):
        xc = xc + bvec[3 - j:4 - j, :] * x_scr[hist - j * nb:hist - j * nb + rows, :]
    a, b = _lru_gates(xc, bvec, bias_ref[...], wg_ref[...])
    a_scr[...] = a
    b_scr[...] = b

    def step(i, carry):
        h, hr, hi = carry
        rw = pl.ds(pl.multiple_of(i * nb, nb), nb)
        nr = ar * hr - ai * hi + re_scr[rw, :]
        ni = ar * hi + ai * hr + im_scr[rw, :]
        h = a_scr[rw, :] * h + b_scr[rw, :]
        re_scr[rw, :] = nr
        im_scr[rw, :] = ni
        b_scr[rw, :] = h
        return h, nr, ni

    h, hr, hi = lax.fori_loop(0, rows // nb, step, (h_scr[...], sr_scr[...], si_scr[...]), unroll=4)
    h_scr[...] = h
    sr_scr[...] = hr
    si_scr[...] = hi
    h_ref[...] = h
    hr_ref[...] = hr
    hi_ref[...] = hi
    tail = x_scr[rows:rows + hist, :]
    conv_ref[...] = tail
    x_scr[0:hist, :] = tail
    yc = _s5_readout(re_scr[...], im_scr[...], u, cvec_ref[...], cbd_ref[...], wglu_ref[...])
    for j, slab in enumerate(_lane_slabs(yc)):
        yc_ref[j] = slab
    yb = _rms(b_scr[...] * _gelu_tanh(gate_br), bvec[6:7, :])
    for j, slab in enumerate(_lane_slabs(yb)):
        yb_ref[j] = slab


def _s5_param_specs(l):
    return [_layer_spec(l, (SUBLANES, S5_W)), _layer_spec(l, (SUBLANES, C_WIDTH)),
            _layer_spec(l, (C_WIDTH, 2 * S5_W)), _layer_spec(l, (2 * S5_W, C_WIDTH)),
            _layer_spec(l, (C_WIDTH, C_WIDTH))]


def _s5_scratch():
    return [pltpu.VMEM((SUBLANES, S5_W), F32), pltpu.VMEM((C_WIDTH, 2 * S5_W), BF16)]


def _s5_param_args(wts):
    return (wts["c_lam"], wts["c_vec"], wts["c_bbd"], wts["c_cbd"], wts["c_wglu"])


def _scan_tm(cols_b, cols_c, nb, seq, wts, l):
    rows = SCAN_STEPS * nb
    hist = (CONV_WIDTH - 1) * nb
    slab = lambda t: (0, t, 0)
    fixed = lambda t: (0, 0)
    nb_slabs, nc_slabs = B_WIDTH // LANES, C_WIDTH // LANES
    return pl.pallas_call(
        functools.partial(_scan_tm_kernel, nb=nb),
        grid=(seq // SCAN_STEPS,),
        in_specs=[pl.BlockSpec((2 * nb_slabs, rows, LANES), slab), pl.BlockSpec((nc_slabs, rows, LANES), slab)]
                 + _lru_param_specs(l) + _s5_param_specs(l),
        out_specs=[pl.BlockSpec((nb_slabs, rows, LANES), slab), pl.BlockSpec((hist, B_WIDTH), fixed),
                   pl.BlockSpec((nb, B_WIDTH), fixed),
                   pl.BlockSpec((nc_slabs, rows, LANES), slab), pl.BlockSpec((nb, S5_W), fixed),
                   pl.BlockSpec((nb, S5_W), fixed)],
        out_shape=[jax.ShapeDtypeStruct((nb_slabs, nb * seq, LANES), F32),
                   jax.ShapeDtypeStruct((hist, B_WIDTH), F32), jax.ShapeDtypeStruct((nb, B_WIDTH), F32),
                   jax.ShapeDtypeStruct((nc_slabs, nb * seq, LANES), F32),
                   jax.ShapeDtypeStruct((nb, S5_W), F32), jax.ShapeDtypeStruct((nb, S5_W), F32)],
        scratch_shapes=[pltpu.VMEM((hist + rows, B_WIDTH), F32), pltpu.VMEM((rows, B_WIDTH), F32),
                        pltpu.VMEM((rows, B_WIDTH), F32), pltpu.VMEM((nb, B_WIDTH), F32),
                        pltpu.VMEM((rows, S5_W), F32), pltpu.VMEM((rows, S5_W), F32),
                        pltpu.VMEM((nb, S5_W), F32), pltpu.VMEM((nb, S5_W), F32)] + _s5_scratch(),
        compiler_params=_params(1),
        name="scan_tm",
    )(cols_b, cols_c, *_lru_param_args(wts), *_s5_param_args(wts))


def _s5_sample_kernel(*refs, steps, n_prev):
    u_ref, h0r_ref, h0i_ref, lam_ref, vec_ref, bbd_ref, cbd_ref, wglu_ref = refs[0:8]
    prev_refs = refs[8:8 + 2 * n_prev]
    y_ref, hr_out_ref, hi_out_ref, abar_scr, bbar_scr = refs[8 + 2 * n_prev:]
    rows = u_ref.shape[0]
    nseq = rows // steps

    @pl.when(pl.program_id(0) == 0)
    def _():
        _s5_discretise(lam_ref[...], bbd_ref, bbar_scr, abar_scr)

    tloc = _row_iota((rows, S5_W)) & (steps - 1)
    h0r = jnp.broadcast_to(h0r_ref[...], (nseq, steps, S5_W)).reshape(rows, S5_W)
    h0i = jnp.broadcast_to(h0i_ref[...], (nseq, steps, S5_W)).reshape(rows, S5_W)
    y, hr, hi = _s5_body(u_ref[...], h0r, h0i, tloc, steps, abar_scr, bbar_scr, vec_ref[...],
                         cbd_ref[...], wglu_ref[...])
    y_ref[...] = y
    _stack_previous(prev_refs[0::2], hr_out_ref)
    _stack_previous(prev_refs[1::2], hi_out_ref)
    hr_out_ref[n_prev] = _last_steps(hr, nseq, steps, 1)
    hi_out_ref[n_prev] = _last_steps(hi, nseq, steps, 1)


def _s5_sample(cols_c, h0r, h0i, prevs, nb, steps, wts, l):
    bb = SAMPLE_SEQS
    rows = bb * steps
    n_prev = len(prevs)
    blk = lambda i: (i, 0)
    seq3 = lambda i: (0, i, 0, 0)
    st = pl.BlockSpec((None, bb, 1, S5_W), lambda i: (l, i, 0, 0))
    prev_specs, prev_args = [], []
    for pr, pi in prevs:
        prev_specs += [pl.BlockSpec((None, bb, 1, S5_W), seq3)] * 2
        prev_args += [pr, pi]
    stacked = pl.BlockSpec((n_prev + 1, bb, 1, S5_W), seq3)
    return pl.pallas_call(
        functools.partial(_s5_sample_kernel, steps=steps, n_prev=n_prev),
        grid=(nb // bb,),
        in_specs=[pl.BlockSpec((rows, C_WIDTH), blk), st, st] + _s5_param_specs(l) + prev_specs,
        out_specs=[pl.BlockSpec((rows, C_WIDTH), blk), stacked, stacked],
        out_shape=[jax.ShapeDtypeStruct((nb * steps, C_WIDTH), F32),
                   jax.ShapeDtypeStruct((n_prev + 1, nb, 1, S5_W), F32),
                   jax.ShapeDtypeStruct((n_prev + 1, nb, 1, S5_W), F32)],
        scratch_shapes=_s5_scratch(),
        compiler_params=_params(1),
        name="s5_sample",
    )(cols_c, h0r, h0i, *_s5_param_args(wts), *prev_args)


def _rows8(rows, width):
    m = jnp.stack([r.reshape(r.shape[0], width) for r in rows], axis=1)
    return jnp.pad(m, ((0, 0), (0, SUBLANES - m.shape[1]), (0, 0)))


def _stacked_weights(w):
    depth = w["g_mix"].shape[0]
    eye_b = jnp.eye(B_BLOCKS, dtype=F32)
    eye_g = jnp.eye(S5_GROUPS, dtype=F32)
    bd4 = lambda m: jnp.einsum("lnde,nm->lndme", m, eye_b).reshape(depth, B_WIDTH, B_WIDTH)
    zeros_lora = jnp.zeros((depth, LANES - 64, A_WIDTH), F32)
    b_in = lambda m: jnp.einsum("lgpc,gh->lgchp", m, eye_g).reshape(depth, C_WIDTH, S5_W)
    c_out = lambda m: jnp.einsum("lgcp,gh->lgphc", m, eye_g).reshape(depth, S5_W, C_WIDTH)
    g_final = jnp.broadcast_to(w["g_final"][None], (depth, D_MODEL))
    return dict(
        g_mix=w["g_mix"][:, None, :],
        w_in=_bf(w["w_in"]),
        mu=w["mu_a"][:, None, :],
        a_vec=_rows8([w["w0"], w["a0"], w["k_k"], w["k_a"], w["r_k"], w["lnx_w"], w["lnx_b"]], A_WIDTH),
        wdec=_bf(jnp.concatenate([w["w_dec2"], zeros_lora], axis=1)),
        wa=_bf(jnp.concatenate([zeros_lora, w["w_a2"]], axis=1)),
        wg=_bf(w["w_g2"]),
        b_vec=_rows8([w["conv_w"][:, 0], w["conv_w"][:, 1], w["conv_w"][:, 2], w["conv_w"][:, 3],
                      w["conv_b"], w["lru_lambda"], w["g_out_b"]], B_WIDTH),
        b_bias=jnp.concatenate([w["b_rg"], w["b_ig"]], axis=1)[:, None, :],
        b_wgates=_bf(jnp.concatenate([bd4(w["w_rg"]), bd4(w["w_ig"])], axis=2)),
        c_lam=_rows8([w["s5_lam_re"], w["s5_lam_im"], jnp.repeat(w["s5_log_dt"], S5_STATE, axis=1)], S5_W),
        c_vec=_rows8([w["s5_d"], w["b_glu"], w["g_out_c"]], C_WIDTH),
        c_bbd=jnp.concatenate([b_in(w["s5_b_re"]), b_in(w["s5_b_im"])], axis=2),
        c_cbd=_bf(jnp.concatenate([c_out(w["s5_c_re"]), -c_out(w["s5_c_im"])], axis=1)),
        c_wglu=_bf(w["w_glu"]),
        w_out=_bf(w["w_out"]),
        p_vec=_rows8([w["g_ffn"], w["g_ple"], g_final], D_MODEL),
        w_up=_bf(w["w_ffn_up"]),
        w_down=_bf(w["w_ffn_down"]),
        w_ple=_bf(w["w_ple"]),
        w_gate=_bf(w["w_ple_gate"]),
    )


def _run_prompt(x, p, wts, depth):
    nb, seq, _ = x.shape
    h = x.reshape(nb * seq, D_MODEL)
    p = p.reshape(depth, nb * seq, PLE_DIM)
    outs = []
    for l in range(depth):
        xs_a, last, cols_b, cols_c = _proj_in_tm(h, wts, l, nb, seq)
        ya, wkv = _rwkv_prompt(xs_a.reshape(nb, seq, A_COLS), nb, seq, wts, l)
        ya = ya.reshape(nb * seq, A_WIDTH)
        yb, conv, lru, yc, s5r, s5i = _scan_tm(cols_b, cols_c, nb, seq, wts, l)
        h = _post(h, ya, yb, yc, p, wts, l, l == depth - 1, nb_tm=nb)
        outs.append((last[:, SUBLANES - 1], wkv, conv, lru, s5r, s5i))
    shift, wkv, conv, lru, s5r, s5i = (jnp.stack([o[j] for o in outs], axis=0) for j in range(6))
    conv = jnp.swapaxes(conv.reshape(depth, CONV_WIDTH - 1, nb, B_WIDTH), 1, 2)
    s5_shape = (depth, nb, S5_GROUPS, S5_STATE)
    return h.reshape(nb, seq, D_MODEL), (shift, wkv, conv, lru, s5r.reshape(s5_shape), s5i.reshape(s5_shape))


def _run_sample(x, p, states, wts, depth):
    nb, steps, _ = x.shape
    st_shift, st_wkv, st_conv, st_lru, st_s5r, st_s5i = states
    h = x.reshape(nb * steps, D_MODEL)
    p = p.reshape(depth, nb * steps, PLE_DIM)
    shift0 = st_shift[:, :, None, :]
    conv0_rows = jnp.pad(st_conv, ((0, 0), (0, 0), (0, steps - (CONV_WIDTH - 1)), (0, 0))
                         ).reshape(depth, nb * steps, B_WIDTH)
    lru0 = st_lru[:, :, None, :]
    s5r0 = st_s5r.reshape(depth, nb, 1, S5_W)
    s5i0 = st_s5i.reshape(depth, nb, 1, S5_W)
    prev_a, prev_b, prev_c = [], [], []
    for l in range(depth):
        last = l == depth - 1
        cols_a, cols_b, cols_c = _proj_in(h, wts["g_mix"], wts["w_in"], l)
        ya, shift, wkv = _rwkv_sample(cols_a, shift0, st_wkv, prev_a if last else [], nb, steps, wts, l)
        yb, conv, lru = _lru_sample(cols_b, conv0_rows, lru0, prev_b if last else [], nb, steps, wts, l)
        yc, s5r, s5i = _s5_sample(cols_c, s5r0, s5i0, prev_c if last else [], nb, steps, wts, l)
        h = _post(h, ya, yb, yc, p, wts, l, last)
        prev_a.append((shift, wkv))
        prev_b.append((conv, lru))
        prev_c.append((s5r, s5i))
    s5_shape = (depth, nb, S5_GROUPS, S5_STATE)
    new_states = (shift.reshape(depth, nb, A_COLS), wkv, conv, lru.reshape(depth, nb, B_WIDTH),
                  s5r.reshape(s5_shape), s5i.reshape(s5_shape))
    return h.reshape(nb, steps, D_MODEL), new_states


def kernel(x_prompt, x_sample, p_prompt, p_sample, state_shift, state_wkv, state_conv, state_lru, state_s5_re, state_s5_im, g_mix, w_in, mu_a, w0, w_dec2, a0, w_a2, w_g2, k_k, k_a, r_k, lnx_w, lnx_b, conv_w, conv_b, w_rg, b_rg, w_ig, b_ig, lru_lambda, g_out_b, s5_lam_re, s5_lam_im, s5_log_dt, s5_b_re, s5_b_im, s5_c_re, s5_c_im, s5_d, w_glu, b_glu, g_out_c, w_out, g_ffn, w_ffn_up, w_ffn_down, g_ple, w_ple, w_ple_gate, g_final):
    w = dict(g_mix=g_mix, w_in=w_in, mu_a=mu_a, w0=w0, w_dec2=w_dec2, a0=a0, w_a2=w_a2, w_g2=w_g2, k_k=k_k,
             k_a=k_a, r_k=r_k, lnx_w=lnx_w, lnx_b=lnx_b, conv_w=conv_w, conv_b=conv_b, w_rg=w_rg, b_rg=b_rg,
             w_ig=w_ig, b_ig=b_ig, lru_lambda=lru_lambda, g_out_b=g_out_b, s5_lam_re=s5_lam_re,
             s5_lam_im=s5_lam_im, s5_log_dt=s5_log_dt, s5_b_re=s5_b_re, s5_b_im=s5_b_im, s5_c_re=s5_c_re,
             s5_c_im=s5_c_im, s5_d=s5_d, w_glu=w_glu, b_glu=b_glu, g_out_c=g_out_c, w_out=w_out, g_ffn=g_ffn,
             w_ffn_up=w_ffn_up, w_ffn_down=w_ffn_down, g_ple=g_ple, w_ple=w_ple, w_ple_gate=w_ple_gate,
             g_final=g_final)
    depth = g_mix.shape[0]
    wts = _stacked_weights(w)
    y_prompt, new_p = _run_prompt(x_prompt, p_prompt, wts, depth)
    y_sample, new_s = _run_sample(x_sample, p_sample,
                                  (state_shift, state_wkv, state_conv, state_lru, state_s5_re, state_s5_im),
                                  wts, depth)
    return (y_prompt, y_sample) + new_p + new_s
```
